```python
import math
import jax, jax.numpy as jnp
from jax import lax
import numpy as np

D_MODEL = 1024
BATCH = 8
SEQ = 4096
DEPTH = 1
DEC_BATCH = 128
DEC_SEQ = 1
PAST_LEN = 16384
PAGE_SIZE = 128

ATT_HEADS = 8
ATT_KV_HEADS = 2
ATT_GROUP = ATT_HEADS // ATT_KV_HEADS
HEAD_DIM = 64
WINDOW = 128
ATT_BLOCK = WINDOW
ROPE_DIMS = HEAD_DIM // 4
ROPE_THETA = 500000.0
RET_HEADS = 4
RET_DK = 128
RET_DV = 128
RET_CHUNK = 128
RET_THETA = 10000.0
ATT_WIDTH = ATT_HEADS * HEAD_DIM
KV_WIDTH = ATT_KV_HEADS * HEAD_DIM
RET_QK_WIDTH = RET_HEADS * RET_DK
RET_WIDTH = RET_HEADS * RET_DV
MIX_WIDTH = ATT_WIDTH + RET_WIDTH
IN_SPLITS = (ATT_WIDTH, KV_WIDTH, KV_WIDTH, RET_QK_WIDTH, RET_QK_WIDTH, RET_WIDTH, RET_WIDTH)
IN_WIDTH = sum(IN_SPLITS)
D_FF = -(-8 * D_MODEL // (3 * 256)) * 256
DEEPNORM_ALPHA = (2 * DEPTH) ** 0.25
DEEPNORM_BETA = (8 * DEPTH) ** -0.25
LN_EPS = 1e-5
GN_EPS = 1e-6

kernel_name = 'hymba_swa_sink_retention_deepnorm_adaln_step'


def layer_norm(x, w, b):
    xf = x.astype(jnp.float32)
    mu = jnp.mean(xf, -1, keepdims=True)
    var = jnp.mean(jnp.square(xf - mu), -1, keepdims=True)
    return ((xf - mu) * lax.rsqrt(var + LN_EPS) * w.astype(jnp.float32) + b.astype(jnp.float32)).astype(x.dtype)


def rope(x, pos, n_rot, theta):
    half = n_rot // 2
    inv = theta ** (-jnp.arange(half, dtype=jnp.float32) / half)
    ang = pos.astype(jnp.float32)[:, None] * inv[None, :]
    cos = jnp.cos(ang)[:, None, :]
    sin = jnp.sin(ang)[:, None, :]
    xf = x.astype(jnp.float32)
    x1 = xf[..., :half]
    x2 = xf[..., half:n_rot]
    out = jnp.concatenate([x1 * cos - x2 * sin, x2 * cos + x1 * sin, xf[..., n_rot:]], -1)
    return out.astype(x.dtype)


def ada_mod(c, w_ada, b_ada):
    m = jax.nn.silu(c) @ w_ada + b_ada
    shift, scale, gate = jnp.split(m, 3, axis=-1)
    return shift[:, None, :], scale[:, None, :], gate[:, None, :]


def project_in(h, w_in):
    z = h @ w_in
    return jnp.split(z, np.cumsum(IN_SPLITS)[:-1].tolist(), axis=-1)


def attend(q, k, v, qpos, kpos, sinks):
    s = jnp.einsum('bnqhgd,bnkhd->bnhgqk', q, k).astype(jnp.float32) * (HEAD_DIM ** -0.5)
    dist = qpos[:, :, None] - kpos[:, None, :]
    mask = (dist >= 0) & (dist <= WINDOW) & (kpos[:, None, :] >= 0)
    s = jnp.where(mask[None, :, None, None], s, -jnp.inf)
    sink = sinks.astype(jnp.float32).reshape(ATT_KV_HEADS, ATT_GROUP)[None, None, :, :, None, None]
    m = jnp.maximum(jnp.max(s, -1, keepdims=True), sink)
    p = jnp.exp(s - m)
    w = p / (jnp.sum(p, -1, keepdims=True) + jnp.exp(sink - m))
    o = jnp.einsum('bnhgqk,bnkhd->bnqhgd', w.astype(v.dtype), v)
    B, N, Lq = o.shape[:3]
    return o.reshape(B, N * Lq, ATT_WIDTH)


def retention_log_decay():
    return jnp.log1p(-jnp.exp(jnp.linspace(math.log(1.0 / 32), math.log(1.0 / 512), RET_HEADS))).astype(jnp.float32)


def retention_chunk(S, q, k, v, log_gamma):
    q = q.astype(jnp.float32)
    k = k.astype(jnp.float32)
    v = v.astype(jnp.float32)
    Sf = S.astype(jnp.float32)
    L = q.shape[1]
    idx = jnp.arange(L, dtype=jnp.float32)
    diff = idx[:, None] - idx[None, :]
    decay = jnp.where(diff[None] >= 0, jnp.exp(log_gamma[:, None, None] * jnp.maximum(diff, 0.0)[None]), 0.0)
    scores = jnp.einsum('bihd,bjhd->bhij', q, k) * decay[None]
    o = jnp.einsum('bhij,bjhe->bihe', scores, v)
    q_dec = q * jnp.exp(log_gamma[None, :] * (idx[:, None] + 1.0))[None, :, :, None]
    o = o + jnp.einsum('bihd,bhde->bihe', q_dec, Sf)
    k_dec = k * jnp.exp(log_gamma[None, :] * (L - 1.0 - idx[:, None]))[None, :, :, None]
    S_new = jnp.exp(log_gamma * L)[None, :, None, None] * Sf + jnp.einsum('bjhd,bjhe->bhde', k_dec, v)
    return S_new.astype(S.dtype), o


def retention_output(o, g, gn_w):
    mu = jnp.mean(o, -1, keepdims=True)
    var = jnp.mean(jnp.square(o - mu), -1, keepdims=True)
    n = ((o - mu) * lax.rsqrt(var + GN_EPS)).reshape(o.shape[0], o.shape[1], RET_WIDTH)
    n = n * gn_w.astype(jnp.float32)
    return (jax.nn.silu(g.astype(jnp.float32)) * n).astype(g.dtype)


def split_heads(q, k, v, rq, rk, rv, pos):
    B, L, _ = q.shape
    q = rope(q.reshape(B, L, ATT_HEADS, HEAD_DIM), pos, ROPE_DIMS, ROPE_THETA)
    k = rope(k.reshape(B, L, ATT_KV_HEADS, HEAD_DIM), pos, ROPE_DIMS, ROPE_THETA)
    v = v.reshape(B, L, ATT_KV_HEADS, HEAD_DIM)
    rq = rope(rq.reshape(B, L, RET_HEADS, RET_DK), pos, RET_DK, RET_THETA)
    rk = rope(rk.reshape(B, L, RET_HEADS, RET_DK), pos, RET_DK, RET_THETA) * (RET_DK ** -0.5)
    rv = rv.reshape(B, L, RET_HEADS, RET_DV)
    return q, k, v, rq, rk, rv


def mixer_prompt(h, w_in, sinks, gn_w, w_out):
    B, S, _ = h.shape
    q, k, v, rq, rk, rv, rg = project_in(h, w_in)
    pos = jnp.arange(S, dtype=jnp.int32)
    q, k, v, rq, rk, rv = split_heads(q, k, v, rq, rk, rv, pos)
    nb = S // ATT_BLOCK
    qb = q.reshape(B, nb, ATT_BLOCK, ATT_KV_HEADS, ATT_GROUP, HEAD_DIM)
    kb = k.reshape(B, nb, ATT_BLOCK, ATT_KV_HEADS, HEAD_DIM)
    vb = v.reshape(B, nb, ATT_BLOCK, ATT_KV_HEADS, HEAD_DIM)
    pad = ((0, 0), (1, 0), (0, 0), (0, 0), (0, 0))
    k_band = jnp.concatenate([jnp.pad(kb, pad)[:, :-1], kb], axis=2)
    v_band = jnp.concatenate([jnp.pad(vb, pad)[:, :-1], vb], axis=2)
    qpos = pos.reshape(nb, ATT_BLOCK)
    kpos = (jnp.arange(nb, dtype=jnp.int32)[:, None] - 1) * ATT_BLOCK + jnp.arange(2 * ATT_BLOCK, dtype=jnp.int32)[None, :]
    att = attend(qb, k_band, v_band, qpos, kpos, sinks)
    nc = S // RET_CHUNK
    to_chunks = lambda t: jnp.moveaxis(t.reshape(B, nc, RET_CHUNK, RET_HEADS, t.shape[-1]), 1, 0)
    log_gamma = retention_log_decay()
    S0 = jnp.zeros((B, RET_HEADS, RET_DK, RET_DV), jnp.float32)
    S_fin, o = lax.scan(lambda st, xs: retention_chunk(st, xs[0], xs[1], xs[2], log_gamma), S0,
                        (to_chunks(rq), to_chunks(rk), to_chunks(rv)))
    o = jnp.moveaxis(o, 0, 1).reshape(B, S, RET_HEADS, RET_DV)
    ret = retention_output(o, rg, gn_w)
    y = jnp.concatenate([att, ret], -1) @ w_out
    return y, k[:, S - WINDOW:], v[:, S - WINDOW:], S_fin


def mixer_sample(h, cache_k, cache_v, state, w_in, sinks, gn_w, w_out):
    B, L, _ = h.shape
    q, k, v, rq, rk, rv, rg = project_in(h, w_in)
    pos = PAST_LEN + jnp.arange(L, dtype=jnp.int32)
    q, k, v, rq, rk, rv = split_heads(q, k, v, rq, rk, rv, pos)
    k_all = jnp.concatenate([cache_k.astype(k.dtype), k], axis=1)
    v_all = jnp.concatenate([cache_v.astype(v.dtype), v], axis=1)
    kpos = PAST_LEN - WINDOW + jnp.arange(WINDOW + L, dtype=jnp.int32)
    att = attend(q.reshape(B, 1, L, ATT_KV_HEADS, ATT_GROUP, HEAD_DIM), k_all[:, None], v_all[:, None],
                 pos[None], kpos[None], sinks)
    S_new, o = retention_chunk(state, rq, rk, rv, retention_log_decay())
    ret = retention_output(o, rg, gn_w)
    y = jnp.concatenate([att, ret], -1) @ w_out
    return y, k_all[:, L:], v_all[:, L:], S_new


def swiglu(h, w_up, w_down):
    g, u = jnp.split(h @ w_up, 2, axis=-1)
    return (jax.nn.silu(g) * u) @ w_down


def setup_inputs(seed: int = 0) -> dict:
    key = jax.random.key(seed)
    ks = jax.random.split(key, 24)
    nrm = lambda k, shape, s: jax.random.normal(k, shape, jnp.float32) * s
    D = D_MODEL
    return {
        'x_prompt': nrm(ks[0], (BATCH, SEQ, D), 1.0),
        'x_sample': nrm(ks[1], (DEC_BATCH, DEC_SEQ, D), 1.0),
        'c_prompt': nrm(ks[2], (BATCH, D), 1.0),
        'c_sample': nrm(ks[3], (DEC_BATCH, D), 1.0),
        'cache_k': nrm(ks[4], (DEPTH, DEC_BATCH, WINDOW, ATT_KV_HEADS, HEAD_DIM), 1.0),
        'cache_v': nrm(ks[5], (DEPTH, DEC_BATCH, WINDOW, ATT_KV_HEADS, HEAD_DIM), 1.0),
        'state_ret': nrm(ks[6], (DEPTH, DEC_BATCH, RET_HEADS, RET_DK, RET_DV), 0.5),
        'w_ada_mix': nrm(ks[7], (DEPTH, D, 3 * D), 0.1 * D ** -0.5),
        'b_ada_mix': nrm(ks[8], (DEPTH, 3 * D), 0.01),
        'w_in': nrm(ks[9], (DEPTH, D, IN_WIDTH), D ** -0.5),
        'att_sinks': nrm(ks[10], (DEPTH, ATT_HEADS), 1.0),
        'ret_gn_w': 1.0 + nrm(ks[11], (DEPTH, RET_WIDTH), 0.01),
        'w_out': nrm(ks[12], (DEPTH, MIX_WIDTH, D), DEEPNORM_BETA * MIX_WIDTH ** -0.5),
        'ln1_w': 1.0 + nrm(ks[13], (DEPTH, D), 0.01),
        'ln1_b': nrm(ks[14], (DEPTH, D), 0.01),
        'w_ada_ffn': nrm(ks[15], (DEPTH, D, 3 * D), 0.1 * D ** -0.5),
        'b_ada_ffn': nrm(ks[16], (DEPTH, 3 * D), 0.01),
        'w_up': nrm(ks[17], (DEPTH, D, 2 * D_FF), D ** -0.5),
        'w_down': nrm(ks[18], (DEPTH, D_FF, D), DEEPNORM_BETA * D_FF ** -0.5),
        'ln2_w': 1.0 + nrm(ks[19], (DEPTH, D), 0.01),
        'ln2_b': nrm(ks[20], (DEPTH, D), 0.01),
    }


def reference(x_prompt, x_sample, c_prompt, c_sample, cache_k, cache_v, state_ret,
              w_ada_mix, b_ada_mix, w_in, att_sinks, ret_gn_w, w_out, ln1_w, ln1_b,
              w_ada_ffn, b_ada_ffn, w_up, w_down, ln2_w, ln2_b):
    yp, ys = x_prompt, x_sample
    kp_l, vp_l, sp_l, ks_l, vs_l, ss_l = [], [], [], [], [], []
    for l in range(DEPTH):
        shp, scp, gtp = ada_mod(c_prompt, w_ada_mix[l], b_ada_mix[l])
        shs, scs, gts = ada_mod(c_sample, w_ada_mix[l], b_ada_mix[l])
        mp, kp, vp, sp = mixer_prompt(yp * (1.0 + scp) + shp, w_in[l], att_sinks[l], ret_gn_w[l], w_out[l])
        ms, ks, vs, ss = mixer_sample(ys * (1.0 + scs) + shs, cache_k[l], cache_v[l], state_ret[l],
                                      w_in[l], att_sinks[l], ret_gn_w[l], w_out[l])
        yp = layer_norm(DEEPNORM_ALPHA * yp + (1.0 + gtp) * mp, ln1_w[l], ln1_b[l])
        ys = layer_norm(DEEPNORM_ALPHA * ys + (1.0 + gts) * ms, ln1_w[l], ln1_b[l])
        shp, scp, gtp = ada_mod(c_prompt, w_ada_ffn[l], b_ada_ffn[l])
        shs, scs, gts = ada_mod(c_sample, w_ada_ffn[l], b_ada_ffn[l])
        fp = swiglu(yp * (1.0 + scp) + shp, w_up[l], w_down[l])
        fs = swiglu(ys * (1.0 + scs) + shs, w_up[l], w_down[l])
        yp = layer_norm(DEEPNORM_ALPHA * yp + (1.0 + gtp) * fp, ln2_w[l], ln2_b[l])
        ys = layer_norm(DEEPNORM_ALPHA * ys + (1.0 + gts) * fs, ln2_w[l], ln2_b[l])
        kp_l.append(kp); vp_l.append(vp); sp_l.append(sp)
        ks_l.append(ks); vs_l.append(vs); ss_l.append(ss)
    new_k_prompt = jnp.stack(kp_l, 0)
    new_v_prompt = jnp.stack(vp_l, 0)
    new_ret_prompt = jnp.stack(sp_l, 0)
    new_k_sample = jnp.stack(ks_l, 0)
    new_v_sample = jnp.stack(vs_l, 0)
    new_ret_sample = jnp.stack(ss_l, 0)
    return (yp, ys, new_k_prompt, new_v_prompt, new_ret_prompt, new_k_sample, new_v_sample, new_ret_sample)
```

```python
import functools
import math

import numpy as np
import jax
import jax.numpy as jnp
from jax.experimental import pallas as pl
from jax.experimental.pallas import tpu as pltpu

D_MODEL = 1024
WINDOW = 128
CHUNK = 128
ATT_HEADS = 8
ATT_KV_HEADS = 2
ATT_GROUP = ATT_HEADS // ATT_KV_HEADS
HEAD_DIM = 64
ROPE_DIMS = HEAD_DIM // 4
ROPE_THETA = 500000.0
RET_HEADS = 4
RET_DK = 128
RET_THETA = 10000.0
ATT_WIDTH = ATT_HEADS * HEAD_DIM
KV_WIDTH = ATT_KV_HEADS * HEAD_DIM
RET_WIDTH = RET_HEADS * RET_DK
IN_WIDTH = ATT_WIDTH + 2 * KV_WIDTH + 4 * RET_WIDTH
D_FF = 2816
DEPTH = 1
DEEPNORM_ALPHA = (2 * DEPTH) ** 0.25
LN_EPS = 1e-5
GN_EPS = 1e-6
PAST_LEN = 16384

OFF_Q = 0
OFF_KV = ATT_WIDTH
OFF_RQ = OFF_KV + 2 * KV_WIDTH
OFF_RK = OFF_RQ + RET_WIDTH
OFF_RV = OFF_RK + RET_WIDTH
OFF_RG = OFF_RV + RET_WIDTH

LANES = 128
VMEM_LIMIT_BYTES = 56 * 1024 * 1024

PROMPT_TILE = 512
FFN_TILE = 512
SAMPLE_BLOCK = 8

BF16 = jnp.bfloat16
F32 = jnp.float32


def _dot(a, b):
    return jnp.dot(a, b, preferred_element_type=F32)


def _dot_nt(a, b):
    return jax.lax.dot_general(a, b, (((1,), (1,)), ((), ())), preferred_element_type=F32)


def _dot_tn(a, b):
    return jax.lax.dot_general(a, b, (((0,), (0,)), ((), ())), preferred_element_type=F32)


def _sigmoid(x):
    return 1.0 / (1.0 + jnp.exp(-x))


def _silu(x):
    return x * _sigmoid(x)


def _layer_norm(r, w, b):
    mu = jnp.mean(r, axis=-1, keepdims=True)
    d = r - mu
    var = jnp.mean(d * d, axis=-1, keepdims=True)
    return d * jax.lax.rsqrt(var + LN_EPS) * w + b


def _rope_att(slab, c, s1, s2):
    return slab * c + pltpu.roll(slab, LANES - 8, axis=1) * s1 + pltpu.roll(slab, 8, axis=1) * s2


def _rope_ret(slab, c, s):
    return slab * c + pltpu.roll(slab, LANES // 2, axis=1) * s


def _rope_tables(pos):
    pos = np.asarray(pos, np.float32)[:, None]
    half = ROPE_DIMS // 2
    inv = (np.float32(ROPE_THETA) ** (-np.arange(half, dtype=np.float32) / np.float32(half))).astype(np.float32)
    ang = (pos * inv[None, :]).astype(np.float32)
    cos, sin = np.cos(ang.astype(np.float64)), np.sin(ang.astype(np.float64))
    n = pos.shape[0]
    head_c = np.ones((n, HEAD_DIM)); head_s1 = np.zeros((n, HEAD_DIM)); head_s2 = np.zeros((n, HEAD_DIM))
    head_c[:, :half] = cos; head_c[:, half:2 * half] = cos
    head_s1[:, :half] = -sin
    head_s2[:, half:2 * half] = sin
    att = [np.tile(t, (1, LANES // HEAD_DIM)) for t in (head_c, head_s1, head_s2)]
    rhalf = RET_DK // 2
    rinv = (np.float32(RET_THETA) ** (-np.arange(rhalf, dtype=np.float32) / np.float32(rhalf))).astype(np.float32)
    rang = (pos * rinv[None, :]).astype(np.float32).astype(np.float64)
    rc = np.concatenate([np.cos(rang), np.cos(rang)], axis=1)
    rs = np.concatenate([-np.sin(rang), np.sin(rang)], axis=1)
    return np.stack(att + [rc, rs], axis=0).astype(np.float32)


def _log_gamma():
    lin = np.linspace(math.log(1.0 / 32), math.log(1.0 / 512), RET_HEADS)
    return np.log1p(-np.exp(lin)).astype(np.float32).astype(np.float64)


def _decay_tables():
    lg = _log_gamma()
    idx = np.arange(CHUNK, dtype=np.float64)
    diff = idx[:, None] - idx[None, :]
    dmat = np.where(diff[None] >= 0, np.exp(lg[:, None, None] * np.maximum(diff, 0.0)[None]), 0.0)
    dq = np.exp(lg[:, None] * (idx[None, :] + 1.0))[:, :, None] * np.ones((1, 1, LANES))
    dk = np.exp(lg[:, None] * (CHUNK - 1.0 - idx[None, :]))[:, :, None] * np.ones((1, 1, LANES))
    return np.concatenate([dmat, dq, dk], axis=0).astype(np.float32)


def _band_bias():
    i = np.arange(CHUNK)[:, None]
    j = np.arange(2 * CHUNK)[None, :]
    valid = (j >= i) & (j <= i + WINDOW)
    neg = np.float32(-1e30)
    b0 = np.where(valid, 0.0, neg)
    b1 = np.where(valid & (j >= CHUNK), 0.0, neg)
    return np.stack([b0, b1], axis=0).astype(np.float32)


def _ada_kernel(c_ref, w_ref, b_ref, o_ref):
    a = _silu(c_ref[...]).astype(BF16)
    o_ref[...] = _dot(a, w_ref[...].astype(BF16)) + b_ref[...]


def _ada_mod(c_all, w_ada, b_ada):
    rows = c_all.shape[0]
    n = w_ada.shape[1]
    bn = 512
    return pl.pallas_call(
        _ada_kernel,
        out_shape=jax.ShapeDtypeStruct((rows, n), F32),
        grid=(n // bn,),
        in_specs=[
            pl.BlockSpec((rows, D_MODEL), lambda j: (0, 0)),
            pl.BlockSpec((D_MODEL, bn), lambda j: (0, j)),
            pl.BlockSpec((1, bn), lambda j: (0, j)),
        ],
        out_specs=pl.BlockSpec((rows, bn), lambda j: (0, j)),
        compiler_params=pltpu.CompilerParams(dimension_semantics=("arbitrary",)),
        name="ada_mod",
    )(c_all, w_ada, b_ada.reshape(1, n))


def _mixer_kernel(sinks_ref, x_ref, mod_ref, tab_ref, dec_ref, bias_ref, win_ref, wout_ref,
                  gnw_ref, lnw_ref, lnb_ref,
                  y_ref, kout_ref, vout_ref, s_ref,
                  q_s, kbuf, vbuf, rqb, rqd, rkb, rkd, rvb, rgs, mix_s, *, tile, gl):
    s_idx = pl.program_id(1)
    n_chunks = tile // CHUNK

    @pl.when(s_idx == 0)
    def _():
        s_ref[...] = jnp.zeros_like(s_ref)
        kbuf[:, 0:CHUNK, :] = jnp.zeros((ATT_KV_HEADS, CHUNK, LANES), BF16)
        vbuf[:, 0:CHUNK, :] = jnp.zeros((ATT_KV_HEADS, CHUNK, LANES), BF16)

    x = x_ref[...]
    m = mod_ref[...]
    shift, scale, gate = m[0], m[1], m[2]
    h = (x * (1.0 + scale) + shift).astype(BF16)

    att_c, att_s1, att_s2 = tab_ref[0], tab_ref[1], tab_ref[2]
    ret_c, ret_s = tab_ref[3], tab_ref[4]

    zq = _dot(h, win_ref[:, OFF_Q:OFF_Q + ATT_WIDTH])
    for j in range(4):
        slab = zq[:, j * LANES:(j + 1) * LANES]
        q_s[:, j * LANES:(j + 1) * LANES] = (
            _rope_att(slab, att_c, att_s1, att_s2) * (HEAD_DIM ** -0.5)).astype(BF16)

    zkv = _dot(h, win_ref[:, OFF_KV:OFF_KV + 2 * KV_WIDTH])
    k = _rope_att(zkv[:, :LANES], att_c, att_s1, att_s2)
    v = zkv[:, LANES:]
    lo = jax.lax.broadcasted_iota(jnp.int32, (tile, LANES), 1) < HEAD_DIM
    zero = jnp.zeros_like(k)
    kbuf[0, CHUNK:CHUNK + tile, :] = jnp.where(lo, k, zero).astype(BF16)
    kbuf[1, CHUNK:CHUNK + tile, :] = jnp.where(lo, zero, k).astype(BF16)
    vbuf[0, CHUNK:CHUNK + tile, :] = jnp.where(lo, v, zero).astype(BF16)
    vbuf[1, CHUNK:CHUNK + tile, :] = jnp.where(lo, zero, v).astype(BF16)
    kout_ref[...] = k[tile - WINDOW:, :]
    vout_ref[...] = v[tile - WINDOW:, :]

    zrq = _dot(h, win_ref[:, OFF_RQ:OFF_RQ + RET_WIDTH])
    zrk = _dot(h, win_ref[:, OFF_RK:OFF_RK + RET_WIDTH])
    zrv = _dot(h, win_ref[:, OFF_RV:OFF_RV + RET_WIDTH])
    zrg = _dot(h, win_ref[:, OFF_RG:OFF_RG + RET_WIDTH])
    rvb[...] = zrv.astype(BF16)
    rgs[...] = _silu(zrg)
    for hh in range(RET_HEADS):
        cols = slice(hh * LANES, (hh + 1) * LANES)
        rq = _rope_ret(zrq[:, cols], ret_c, ret_s)
        rk = _rope_ret(zrk[:, cols], ret_c, ret_s) * (RET_DK ** -0.5)
        rqb[:, cols] = rq.astype(BF16)
        rkb[:, cols] = rk.astype(BF16)
        dq = dec_ref[RET_HEADS + hh]
        dk = dec_ref[2 * RET_HEADS + hh]
        for c in range(n_chunks):
            rows = slice(c * CHUNK, (c + 1) * CHUNK)
            rqd[rows, cols] = (rq[rows] * dq).astype(BF16)
            rkd[rows, cols] = (rk[rows] * dk).astype(BF16)

    def chunk_body(c, carry):
        r0 = pl.multiple_of(c * CHUNK, CHUNK)
        rows = pl.ds(r0, CHUNK)
        band = pl.ds(r0, 2 * CHUNK)
        first = jnp.logical_and(s_idx == 0, c == 0).astype(jnp.int32)
        bias = bias_ref[first]
        qs = q_s[rows, :]
        q_stack = jnp.concatenate([qs[:, j * LANES:(j + 1) * LANES] for j in range(4)], axis=0)
        att = None
        for g in range(ATT_KV_HEADS):
            sc = _dot_nt(q_stack, kbuf[g, band, :])
            ws = []
            for j in range(4):
                sj = sc[j * CHUNK:(j + 1) * CHUNK, :] + bias
                sink = sinks_ref[g * ATT_GROUP + j]
                mx = jnp.maximum(jnp.max(sj, axis=-1, keepdims=True), sink)
                p = jnp.exp(sj - mx)
                den = jnp.sum(p, axis=-1, keepdims=True) + jnp.exp(sink - mx)
                ws.append((p * (1.0 / den)).astype(BF16))
            o = _dot(jnp.concatenate(ws, axis=0), vbuf[g, band, :])
            att = o if att is None else att + o
        for j in range(4):
            mix_s[rows, j * LANES:(j + 1) * LANES] = att[j * CHUNK:(j + 1) * CHUNK, :].astype(BF16)

        for hh in range(RET_HEADS):
            cols = slice(hh * LANES, (hh + 1) * LANES)
            qb = rqb[rows, cols]
            kb_ = rkb[rows, cols]
            vb_ = rvb[rows, cols]
            state = s_ref[hh]
            sc = _dot_nt(qb, kb_) * dec_ref[hh]
            o = _dot(sc.astype(BF16), vb_) + _dot(rqd[rows, cols], state.astype(BF16))
            s_ref[hh] = gl[hh] * state + _dot_tn(rkd[rows, cols], vb_)
            mu = jnp.mean(o, axis=-1, keepdims=True)
            d = o - mu
            var = jnp.mean(d * d, axis=-1, keepdims=True)
            nrm = d * jax.lax.rsqrt(var + GN_EPS) * gnw_ref[:, cols]
            mix_s[rows, ATT_WIDTH + hh * LANES:ATT_WIDTH + (hh + 1) * LANES] = (
                rgs[rows, cols] * nrm).astype(BF16)
        return carry

    jax.lax.fori_loop(0, n_chunks, chunk_body, 0)

    kbuf[:, 0:CHUNK, :] = kbuf[:, tile:tile + CHUNK, :]
    vbuf[:, 0:CHUNK, :] = vbuf[:, tile:tile + CHUNK, :]

    y = _dot(mix_s[...], wout_ref[...])
    r = DEEPNORM_ALPHA * x + (1.0 + gate) * y
    y_ref[...] = _layer_norm(r, lnw_ref[...], lnb_ref[...])


def _mixer_prompt(x, modp, sinks, tab, dec, bias, w_in_p, w_out_p, gn_w, ln_w, ln_b):
    batch, seq, _ = x.shape
    tile = PROMPT_TILE
    ns = seq // tile
    gl = tuple(float(v) for v in np.exp(_log_gamma() * CHUNK).astype(np.float32))
    const = lambda shape: pl.BlockSpec(shape, lambda b, s, *_: (0,) * len(shape))
    grid_spec = pltpu.PrefetchScalarGridSpec(
        num_scalar_prefetch=1,
        grid=(batch, ns),
        in_specs=[
            pl.BlockSpec((None, tile, D_MODEL), lambda b, s, *_: (b, s, 0)),
            pl.BlockSpec((None, 3, 1, D_MODEL), lambda b, s, *_: (b, 0, 0, 0)),
            pl.BlockSpec((5, tile, LANES), lambda b, s, *_: (0, s, 0)),
            const((3 * RET_HEADS, CHUNK, LANES)),
            const((2, CHUNK, 2 * CHUNK)),
            const((D_MODEL, IN_WIDTH)),
            const((D_MODEL, D_MODEL)),
            const((1, RET_WIDTH)),
            const((1, D_MODEL)),
            const((1, D_MODEL)),
        ],
        out_specs=[
            pl.BlockSpec((None, tile, D_MODEL), lambda b, s, *_: (b, s, 0)),
            pl.BlockSpec((None, WINDOW, LANES), lambda b, s, *_: (b, 0, 0)),
            pl.BlockSpec((None, WINDOW, LANES), lambda b, s, *_: (b, 0, 0)),
            pl.BlockSpec((None, RET_HEADS, RET_DK, LANES), lambda b, s, *_: (b, 0, 0, 0)),
        ],
        scratch_shapes=[
            pltpu.VMEM((tile, ATT_WIDTH), BF16),
            pltpu.VMEM((ATT_KV_HEADS, tile + CHUNK, LANES), BF16),
            pltpu.VMEM((ATT_KV_HEADS, tile + CHUNK, LANES), BF16),
            pltpu.VMEM((tile, RET_WIDTH), BF16),
            pltpu.VMEM((tile, RET_WIDTH), BF16),
            pltpu.VMEM((tile, RET_WIDTH), BF16),
            pltpu.VMEM((tile, RET_WIDTH), BF16),
            pltpu.VMEM((tile, RET_WIDTH), BF16),
            pltpu.VMEM((tile, RET_WIDTH), F32),
            pltpu.VMEM((tile, D_MODEL), BF16),
        ],
    )
    return pl.pallas_call(
        functools.partial(_mixer_kernel, tile=tile, gl=gl),
        out_shape=[
            jax.ShapeDtypeStruct((batch, seq, D_MODEL), F32),
            jax.ShapeDtypeStruct((batch, WINDOW, LANES), F32),
            jax.ShapeDtypeStruct((batch, WINDOW, LANES), F32),
            jax.ShapeDtypeStruct((batch, RET_HEADS, RET_DK, LANES), F32),
        ],
        grid_spec=grid_spec,
        compiler_params=pltpu.CompilerParams(
            dimension_semantics=("arbitrary", "arbitrary"), vmem_limit_bytes=VMEM_LIMIT_BYTES),
        name="mixer_prompt",
    )(sinks, x, modp, tab, dec, bias, w_in_p, w_out_p, gn_w, ln_w, ln_b)


def _ffn_kernel(y_ref, mod_ref, wup_ref, wdown_ref, lnw_ref, lnb_ref, o_ref):
    y = y_ref[...]
    m = mod_ref[...]
    shift, scale, gate = m[0], m[1], m[2]
    h = (y * (1.0 + scale) + shift).astype(BF16)
    g = _dot(h, wup_ref[:, :D_FF])
    u = _dot(h, wup_ref[:, D_FF:])
    a = (_silu(g) * u).astype(BF16)
    f = _dot(a, wdown_ref[...])
    r = DEEPNORM_ALPHA * y + (1.0 + gate) * f
    o_ref[...] = _layer_norm(r, lnw_ref[...], lnb_ref[...])


def _ffn(y, mod, mod_rows, w_up, w_down, ln_w, ln_b, tile):
    groups, rows, _ = y.shape
    nt = rows // tile
    const = lambda shape: pl.BlockSpec(shape, lambda g, t: (0,) * len(shape))
    mod_map = (lambda g, t: (g, 0, 0, 0)) if mod_rows == 1 else (lambda g, t: (g, 0, t, 0))
    return pl.pallas_call(
        _ffn_kernel,
        out_shape=jax.ShapeDtypeStruct(y.shape, F32),
        grid=(groups, nt),
        in_specs=[
            pl.BlockSpec((None, tile, D_MODEL), lambda g, t: (g, t, 0)),
            pl.BlockSpec((None, 3, mod_rows, D_MODEL), mod_map),
            const((D_MODEL, 2 * D_FF)),
            const((D_FF, D_MODEL)),
            const((1, D_MODEL)),
            const((1, D_MODEL)),
        ],
        out_specs=pl.BlockSpec((None, tile, D_MODEL), lambda g, t: (g, t, 0)),
        compiler_params=pltpu.CompilerParams(
            dimension_semantics=("arbitrary", "arbitrary"), vmem_limit_bytes=VMEM_LIMIT_BYTES),
        name="ffn",
    )(y, mod, w_up, w_down, ln_w, ln_b)


def _sample_proj_kernel(x_ref, mod_ref, tab_ref, win_ref,
                        q_ref, k_ref, v_ref, rq_ref, rk_ref, rv_ref, rg_ref):
    x = x_ref[...]
    m = mod_ref[...]
    h = (x * (1.0 + m[1]) + m[0]).astype(BF16)
    z = _dot(h, win_ref[...])
    att_c, att_s1, att_s2 = tab_ref[0], tab_ref[1], tab_ref[2]
    ret_c, ret_s = tab_ref[3], tab_ref[4]
    for j in range(4):
        cols = slice(j * LANES, (j + 1) * LANES)
        q_ref[:, cols] = _rope_att(z[:, OFF_Q + j * LANES:OFF_Q + (j + 1) * LANES],
                                   att_c, att_s1, att_s2) * (HEAD_DIM ** -0.5)
        rq_ref[:, cols] = _rope_ret(z[:, OFF_RQ + j * LANES:OFF_RQ + (j + 1) * LANES], ret_c, ret_s)
        rk_ref[:, cols] = _rope_ret(z[:, OFF_RK + j * LANES:OFF_RK + (j + 1) * LANES],
                                    ret_c, ret_s) * (RET_DK ** -0.5)
    k_ref[...] = _rope_att(z[:, OFF_KV:OFF_KV + LANES], att_c, att_s1, att_s2)
    v_ref[...] = z[:, OFF_KV + LANES:OFF_KV + 2 * LANES]
    rv_ref[...] = z[:, OFF_RV:OFF_RV + RET_WIDTH]
    rg_ref[...] = z[:, OFF_RG:OFF_RG + RET_WIDTH]


def _sample_proj(x, mod, tab, w_in_p):
    n = x.shape[0]
    full = lambda shape: pl.BlockSpec(shape, lambda i: (0,) * len(shape))
    wide = jax.ShapeDtypeStruct((n, RET_WIDTH), F32)
    narrow = jax.ShapeDtypeStruct((n, LANES), F32)
    return pl.pallas_call(
        _sample_proj_kernel,
        out_shape=[wide, narrow, narrow, wide, wide, wide, wide],
        grid=(1,),
        in_specs=[full((n, D_MODEL)), full((3, n, D_MODEL)), full((5, 1, LANES)),
                  full((D_MODEL, IN_WIDTH))],
        out_specs=[full((n, RET_WIDTH)), full((n, LANES)), full((n, LANES)),
                   full((n, RET_WIDTH)), full((n, RET_WIDTH)), full((n, RET_WIDTH)),
                   full((n, RET_WIDTH))],
        compiler_params=pltpu.CompilerParams(
            dimension_semantics=("arbitrary",), vmem_limit_bytes=VMEM_LIMIT_BYTES),
        name="sample_proj",
    )(x, mod, tab, w_in_p)


def _sample_mix_kernel(sinks_ref, q_ref, kn_ref, vn_ref, rq_ref, rk_ref, rv_ref, rg_ref,
                       ck_ref, cv_ref, st_ref, gnw_ref,
                       mix_ref, ko_ref, vo_ref, so_ref, *, g1):
    lane_w = jax.lax.broadcasted_iota(jnp.int32, (WINDOW, LANES), 1)
    row_w = jax.lax.broadcasted_iota(jnp.int32, (WINDOW, LANES), 0)
    lo = lane_w < HEAD_DIM
    lo_row = lo[0:1, :]
    eye = row_w == lane_w
    last_row = row_w == WINDOW - 1
    row4 = jax.lax.broadcasted_iota(jnp.int32, (RET_HEADS, LANES), 0)
    r4 = jax.lax.broadcasted_iota(jnp.int32, (ATT_GROUP, 1), 0)

    def sink_col(base):
        s = [sinks_ref[base + j] for j in range(ATT_GROUP)]
        return jnp.where(r4 == 0, s[0], jnp.where(r4 == 1, s[1], jnp.where(r4 == 2, s[2], s[3])))

    sink_lo, sink_hi = sink_col(0), sink_col(ATT_GROUP)
    zero_w = jnp.zeros((WINDOW, LANES), F32)
    pad = jnp.zeros((16 - RET_HEADS, LANES), F32)

    def rows16(a):
        return jnp.concatenate([a, pad], axis=0).astype(BF16)

    for b in range(SAMPLE_BLOCK):
        ck = ck_ref[b]
        cv = cv_ref[b]
        kn = kn_ref[b]
        vn = vn_ref[b]
        ko_ref[b] = jnp.where(last_row, kn, pltpu.roll(ck, WINDOW - 1, axis=0))
        vo_ref[b] = jnp.where(last_row, vn, pltpu.roll(cv, WINDOW - 1, axis=0))
        q4 = q_ref[b]
        kcat = jnp.concatenate([jnp.where(lo, ck, zero_w), jnp.where(lo, zero_w, ck)], axis=0).astype(BF16)
        vcat = jnp.concatenate([jnp.where(lo, cv, zero_w), jnp.where(lo, zero_w, cv)], axis=0).astype(BF16)
        s01 = _dot_nt(rows16(q4), kcat)[0:ATT_GROUP]
        qk = q4 * kn
        zero4 = jnp.zeros_like(qk)
        sn_lo = jnp.sum(jnp.where(lo_row, qk, zero4), axis=-1, keepdims=True)
        sn_hi = jnp.sum(jnp.where(lo_row, zero4, qk), axis=-1, keepdims=True)
        ws = []
        wn = []
        for s_half, sn, sink in ((s01[:, :WINDOW], sn_lo, sink_lo), (s01[:, WINDOW:], sn_hi, sink_hi)):
            mx = jnp.maximum(jnp.maximum(jnp.max(s_half, axis=-1, keepdims=True), sn), sink)
            p = jnp.exp(s_half - mx)
            pn = jnp.exp(sn - mx)
            den = jnp.sum(p, axis=-1, keepdims=True) + pn + jnp.exp(sink - mx)
            inv = 1.0 / den
            ws.append(rows16(p * inv))
            wn.append(pn * inv)
        w01 = jnp.concatenate(ws, axis=1)
        vn_lo = jnp.where(lo_row, vn, jnp.zeros_like(vn))
        vn_hi = vn - vn_lo
        att = _dot(w01, vcat)[0:ATT_GROUP] + wn[0] * vn_lo + wn[1] * vn_hi
        mix_ref[b, 0:4, :] = att

        rq = rq_ref[b]
        rk = rk_ref[b]
        rv = rv_ref[b]
        qkr = jnp.sum(rq * rk, axis=-1, keepdims=True)
        o = qkr * rv
        for hh in range(RET_HEADS):
            state = st_ref[b, hh]
            oh = _dot(rows16(rq * g1[hh]), state.astype(BF16))[0:RET_HEADS]
            o = o + jnp.where(row4 == hh, oh, jnp.zeros_like(oh))
            kdiag = jnp.where(eye, jnp.broadcast_to(rk[hh:hh + 1, :], (WINDOW, LANES)), zero_w).astype(BF16)
            vrows = jnp.broadcast_to(rv[hh:hh + 1, :], (WINDOW, LANES)).astype(BF16)
            so_ref[b, hh] = g1[hh] * state + _dot(kdiag, vrows)
        mu = jnp.mean(o, axis=-1, keepdims=True)
        d = o - mu
        var = jnp.mean(d * d, axis=-1, keepdims=True)
        nrm = d * jax.lax.rsqrt(var + GN_EPS) * gnw_ref[...]
        mix_ref[b, 4:8, :] = _silu(rg_ref[b]) * nrm


def _sample_mix(sinks, q, kn, vn, rq, rk, rv, rg, cache_k, cache_v, state, gn_w4):
    n = q.shape[0]
    bb = SAMPLE_BLOCK
    g1 = tuple(float(v) for v in np.exp(_log_gamma()).astype(np.float32))
    blk = lambda *shape: pl.BlockSpec((bb,) + shape, lambda i, *_: (i,) + (0,) * len(shape))
    grid_spec = pltpu.PrefetchScalarGridSpec(
        num_scalar_prefetch=1,
        grid=(n // bb,),
        in_specs=[
            blk(4, LANES), blk(1, LANES), blk(1, LANES),
            blk(4, LANES), blk(4, LANES), blk(4, LANES), blk(4, LANES),
            blk(WINDOW, LANES), blk(WINDOW, LANES), blk(RET_HEADS, RET_DK, LANES),
            pl.BlockSpec((RET_HEADS, LANES), lambda i, *_: (0, 0)),
        ],
        out_specs=[blk(8, LANES), blk(WINDOW, LANES), blk(WINDOW, LANES),
                   blk(RET_HEADS, RET_DK, LANES)],
    )
    return pl.pallas_call(
        functools.partial(_sample_mix_kernel, g1=g1),
        out_shape=[
            jax.ShapeDtypeStruct((n, 8, LANES), F32),
            jax.ShapeDtypeStruct((n, WINDOW, LANES), F32),
            jax.ShapeDtypeStruct((n, WINDOW, LANES), F32),
            jax.ShapeDtypeStruct((n, RET_HEADS, RET_DK, LANES), F32),
        ],
        grid_spec=grid_spec,
        compiler_params=pltpu.CompilerParams(
            dimension_semantics=("arbitrary",), vmem_limit_bytes=VMEM_LIMIT_BYTES),
        name="sample_mix",
    )(sinks, q.reshape(n, 4, LANES), kn.reshape(n, 1, LANES), vn.reshape(n, 1, LANES),
      rq.reshape(n, 4, LANES), rk.reshape(n, 4, LANES), rv.reshape(n, 4, LANES),
      rg.reshape(n, 4, LANES), cache_k, cache_v, state, gn_w4)


def _sample_out_kernel(mix_ref, x_ref, mod_ref, wout_ref, lnw_ref, lnb_ref, o_ref):
    y = _dot(mix_ref[...].astype(BF16), wout_ref[...])
    r = DEEPNORM_ALPHA * x_ref[...] + (1.0 + mod_ref[2]) * y
    o_ref[...] = _layer_norm(r, lnw_ref[...], lnb_ref[...])


def _sample_out(mix, x, mod, w_out_p, ln_w, ln_b):
    n = x.shape[0]
    full = lambda shape: pl.BlockSpec(shape, lambda i: (0,) * len(shape))
    return pl.pallas_call(
        _sample_out_kernel,
        out_shape=jax.ShapeDtypeStruct((n, D_MODEL), F32),
        grid=(1,),
        in_specs=[full((n, D_MODEL)), full((n, D_MODEL)), full((3, n, D_MODEL)),
                  full((D_MODEL, D_MODEL)), full((1, D_MODEL)), full((1, D_MODEL))],
        out_specs=full((n, D_MODEL)),
        compiler_params=pltpu.CompilerParams(dimension_semantics=("arbitrary",)),
        name="sample_out",
    )(mix, x, mod, w_out_p, ln_w, ln_b)


def _slab_order(w, axis):
    shape = w.shape
    pre, post = shape[:axis], shape[axis + 1:]
    w = w.reshape(pre + (ATT_KV_HEADS, ATT_GROUP, HEAD_DIM) + post)
    w = jnp.swapaxes(w, axis, axis + 1)
    return w.reshape(shape)


def kernel(x_prompt, x_sample, c_prompt, c_sample, cache_k, cache_v, state_ret, w_ada_mix, b_ada_mix, w_in, att_sinks, ret_gn_w, w_out, ln1_w, ln1_b, w_ada_ffn, b_ada_ffn, w_up, w_down, ln2_w, ln2_b):
    batch, seq, _ = x_prompt.shape
    n_s = x_sample.shape[0]
    l = 0

    w_in_p = jnp.concatenate(
        [_slab_order(w_in[l][:, :ATT_WIDTH], 1), w_in[l][:, ATT_WIDTH:]], axis=1).astype(BF16)
    w_out_p = jnp.concatenate(
        [_slab_order(w_out[l][:ATT_WIDTH], 0), w_out[l][ATT_WIDTH:]], axis=0).astype(BF16)
    w_up_b = w_up[l].astype(BF16)
    w_down_b = w_down[l].astype(BF16)
    sinks = att_sinks[l]
    gn_w = ret_gn_w[l].reshape(1, RET_WIDTH)
    ln1w, ln1b = ln1_w[l].reshape(1, D_MODEL), ln1_b[l].reshape(1, D_MODEL)
    ln2w, ln2b = ln2_w[l].reshape(1, D_MODEL), ln2_b[l].reshape(1, D_MODEL)

    tab_p = jnp.asarray(_rope_tables(np.arange(seq)))
    tab_s = jnp.asarray(_rope_tables(np.array([PAST_LEN])))
    dec = jnp.asarray(_decay_tables())
    bias = jnp.asarray(_band_bias())

    c_all = jnp.concatenate([c_prompt, c_sample], axis=0)
    mod_mix = _ada_mod(c_all, w_ada_mix[l], b_ada_mix[l])
    mod_ffn = _ada_mod(c_all, w_ada_ffn[l], b_ada_ffn[l])
    modp_mix = mod_mix[:batch].reshape(batch, 3, 1, D_MODEL)
    modp_ffn = mod_ffn[:batch].reshape(batch, 3, 1, D_MODEL)
    mods_mix = jnp.swapaxes(mod_mix[batch:].reshape(n_s, 3, D_MODEL), 0, 1)
    mods_ffn = jnp.swapaxes(mod_ffn[batch:].reshape(n_s, 3, D_MODEL), 0, 1)

    y1p, kp, vp, sp = _mixer_prompt(x_prompt, modp_mix, sinks, tab_p, dec, bias, w_in_p, w_out_p,
                                    gn_w, ln1w, ln1b)
    yp = _ffn(y1p, modp_ffn, 1, w_up_b, w_down_b, ln2w, ln2b, FFN_TILE)

    xs = x_sample.reshape(n_s, D_MODEL)
    q, kn, vn, rq, rk, rv, rg = _sample_proj(xs, mods_mix, tab_s, w_in_p)
    mix, ks, vs, ss = _sample_mix(
        sinks, q, kn, vn, rq, rk, rv, rg,
        cache_k[l].reshape(n_s, WINDOW, LANES), cache_v[l].reshape(n_s, WINDOW, LANES),
        state_ret[l], ret_gn_w[l].reshape(RET_HEADS, LANES))
    y1s = _sample_out(mix.reshape(n_s, D_MODEL), xs, mods_mix, w_out_p, ln1w, ln1b)
    ys = _ffn(y1s.reshape(1, n_s, D_MODEL), mods_ffn.reshape(1, 3, n_s, D_MODEL), n_s,
              w_up_b, w_down_b, ln2w, ln2b, n_s)

    kv_shape = (1, batch, WINDOW, ATT_KV_HEADS, HEAD_DIM)
    kvs_shape = (1, n_s, WINDOW, ATT_KV_HEADS, HEAD_DIM)
    return (yp, ys.reshape(n_s, 1, D_MODEL),
            kp.reshape(kv_shape), vp.reshape(kv_shape), sp[None],
            ks.reshape(kvs_shape), vs.reshape(kvs_shape), ss[None])
```

```python
import functools
import math

import numpy as np
import jax
import jax.numpy as jnp
from jax.experimental import pallas as pl
from jax.experimental.pallas import tpu as pltpu

D_MODEL = 1024
WINDOW = 128
CHUNK = 128
ATT_HEADS = 8
ATT_KV_HEADS = 2
ATT_GROUP = ATT_HEADS // ATT_KV_HEADS
HEAD_DIM = 64
ROPE_DIMS = HEAD_DIM // 4
ROPE_THETA = 500000.0
RET_HEADS = 4
RET_DK = 128
RET_THETA = 10000.0
ATT_WIDTH = ATT_HEADS * HEAD_DIM
KV_WIDTH = ATT_KV_HEADS * HEAD_DIM
RET_WIDTH = RET_HEADS * RET_DK
IN_WIDTH = ATT_WIDTH + 2 * KV_WIDTH + 4 * RET_WIDTH
D_FF = 2816
DEPTH = 1
DEEPNORM_ALPHA = (2 * DEPTH) ** 0.25
LN_EPS = 1e-5
GN_EPS = 1e-6
PAST_LEN = 16384

OFF_Q = 0
OFF_KV = ATT_WIDTH
OFF_RQ = OFF_KV + 2 * KV_WIDTH
OFF_RK = OFF_RQ + RET_WIDTH
OFF_RV = OFF_RK + RET_WIDTH
OFF_RG = OFF_RV + RET_WIDTH

LANES = 128
VMEM_LIMIT_BYTES = 56 * 1024 * 1024

PROMPT_TILE = 512
FILL_WIDTH = 256
FILLS_UNDER_SOFTMAX = 2
FILLS_UNDER_NORM = 1
FILLS_UNDER_LAYER_NORM = 1
FFN_TILE = 512
SAMPLE_BLOCK = 8

BF16 = jnp.bfloat16
F32 = jnp.float32


def _dot(a, b):
    return jnp.dot(a, b, preferred_element_type=F32)


def _dot_nt(a, b):
    return jax.lax.dot_general(a, b, (((1,), (1,)), ((), ())), preferred_element_type=F32)


def _dot_tn(a, b):
    return jax.lax.dot_general(a, b, (((0,), (0,)), ((), ())), preferred_element_type=F32)


def _sigmoid(x):
    return 1.0 / (1.0 + jnp.exp(-x))


def _silu(x):
    return x * _sigmoid(x)


def _layer_norm(r, w, b):
    mu = jnp.mean(r, axis=-1, keepdims=True)
    d = r - mu
    var = jnp.mean(d * d, axis=-1, keepdims=True)
    return d * jax.lax.rsqrt(var + LN_EPS) * w + b


def _rope_att(slab, c, s1, s2):
    return slab * c + pltpu.roll(slab, LANES - 8, axis=1) * s1 + pltpu.roll(slab, 8, axis=1) * s2


def _rope_ret(slab, c, s):
    return slab * c + pltpu.roll(slab, LANES // 2, axis=1) * s


def _rope_tables(pos):
    pos = np.asarray(pos, np.float64)[:, None]
    half = ROPE_DIMS // 2
    inv = ROPE_THETA ** (-np.arange(half, dtype=np.float64) / half)
    ang = pos * inv[None, :]
    cos, sin = np.cos(ang), np.sin(ang)
    n = pos.shape[0]
    head_c = np.ones((n, HEAD_DIM)); head_s1 = np.zeros((n, HEAD_DIM)); head_s2 = np.zeros((n, HEAD_DIM))
    head_c[:, :half] = cos; head_c[:, half:2 * half] = cos
    head_s1[:, :half] = -sin
    head_s2[:, half:2 * half] = sin
    att = [np.tile(t, (1, LANES // HEAD_DIM)) for t in (head_c, head_s1, head_s2)]
    rhalf = RET_DK // 2
    rinv = RET_THETA ** (-np.arange(rhalf, dtype=np.float64) / rhalf)
    rang = pos * rinv[None, :]
    rc = np.concatenate([np.cos(rang), np.cos(rang)], axis=1)
    rs = np.concatenate([-np.sin(rang), np.sin(rang)], axis=1)
    return np.stack(att + [rc, rs], axis=0).astype(np.float32)


def _log_gamma():
    lin = np.linspace(math.log(1.0 / 32), math.log(1.0 / 512), RET_HEADS)
    return np.log1p(-np.exp(lin))


def _decay_tables():
    lg = _log_gamma()
    idx = np.arange(CHUNK, dtype=np.float64)
    diff = idx[:, None] - idx[None, :]
    dmat = np.where(diff[None] >= 0, np.exp(lg[:, None, None] * np.maximum(diff, 0.0)[None]), 0.0)
    dq = np.exp(lg[:, None] * (idx[None, :] + 1.0))[:, :, None] * np.ones((1, 1, LANES))
    dk = np.exp(lg[:, None] * (CHUNK - 1.0 - idx[None, :]))[:, :, None] * np.ones((1, 1, LANES))
    return np.concatenate([dmat, dq, dk], axis=0).astype(np.float32)


def _band_bias():
    i = np.arange(CHUNK)[:, None]
    j = np.arange(2 * CHUNK)[None, :]
    valid = (j >= i) & (j <= i + WINDOW)
    neg = np.float32(-1e30)
    b0 = np.where(valid, 0.0, neg)
    b1 = np.where(valid & (j >= CHUNK), 0.0, neg)
    return np.stack([b0, b1], axis=0).astype(np.float32)


def _ada_kernel(c_ref, w_ref, b_ref, o_ref):
    a = _silu(c_ref[...]).astype(BF16)
    o_ref[...] = _dot(a, w_ref[...].astype(BF16)) + b_ref[...]


def _ada_mod(c_all, w_ada, b_ada):
    rows = c_all.shape[0]
    n = w_ada.shape[1]
    bn = 512
    return pl.pallas_call(
        _ada_kernel,
        out_shape=jax.ShapeDtypeStruct((rows, n), F32),
        grid=(n // bn,),
        in_specs=[
            pl.BlockSpec((rows, D_MODEL), lambda j: (0, 0)),
            pl.BlockSpec((D_MODEL, bn), lambda j: (0, j)),
            pl.BlockSpec((1, bn), lambda j: (0, j)),
        ],
        out_specs=pl.BlockSpec((rows, bn), lambda j: (0, j)),
        compiler_params=pltpu.CompilerParams(dimension_semantics=("arbitrary",)),
        name="ada_mod",
    )(c_all, w_ada, b_ada.reshape(1, n))


def _mixer_step(z_prev, z_cur, sinks_ref, x_ref, mod_ref, tab_ref, xp_ref, modp_ref, dec_ref, bias_ref,
                win_ref, wout_ref, gnw_ref, lnw_ref, lnb_ref,
                y_ref, kout_ref, vout_ref, s_ref, kcarry, vcarry, mix_s, *, seq_start, tile, gl):
    n_chunks = tile // CHUNK
    x = x_ref[...]
    m = mod_ref[...]
    h = (x * (1.0 + m[1]) + m[0]).astype(BF16)
    n_fill = IN_WIDTH // FILL_WIDTH
    pending = list(range(n_fill))

    def fill(count, keep=FILLS_UNDER_LAYER_NORM):
        for _ in range(count):
            if len(pending) > keep:
                i = pending.pop(0)
                cols = slice(i * FILL_WIDTH, (i + 1) * FILL_WIDTH)
                z_cur[:, cols] = _dot(h, win_ref[:, cols])

    gate = modp_ref[2]
    lo =jax.lax.broadcasted_iota(jnp.int32, (CHUNK, LANES), 1) < HEAD_DIM
    zero = jnp.zeros((CHUNK, LANES), F32)
    k_prev = [kcarry[g] for g in range(ATT_KV_HEADS)]
    v_prev = [vcarry[g] for g in range(ATT_KV_HEADS)]

    for c in range(n_chunks):
        rows = slice(c * CHUNK, (c + 1) * CHUNK)
        bias = bias_ref[seq_start] if c == 0 else bias_ref[0]
        att_c, att_s1, att_s2 = tab_ref[0, rows, :], tab_ref[1, rows, :], tab_ref[2, rows, :]
        ret_c, ret_s = tab_ref[3, rows, :], tab_ref[4, rows, :]

        def zs(col):
            return z_prev[rows, col:col + LANES]

        q_stack = jnp.concatenate(
            [(_rope_att(zs(OFF_Q + j * LANES), att_c, att_s1, att_s2) * (HEAD_DIM ** -0.5)).astype(BF16)
             for j in range(4)], axis=0)
        k = _rope_att(zs(OFF_KV), att_c, att_s1, att_s2)
        v = zs(OFF_KV + LANES)
        k_cur = [jnp.where(lo, k, zero).astype(BF16), jnp.where(lo, zero, k).astype(BF16)]
        v_cur = [jnp.where(lo, v, zero).astype(BF16), jnp.where(lo, zero, v).astype(BF16)]
        k_band = [jnp.concatenate([k_prev[g], k_cur[g]], axis=0) for g in range(ATT_KV_HEADS)]
        v_band = [jnp.concatenate([v_prev[g], v_cur[g]], axis=0) for g in range(ATT_KV_HEADS)]
        k_prev, v_prev = k_cur, v_cur
        if c == n_chunks - 1:
            kout_ref[...] = k
            vout_ref[...] = v
            for g in range(ATT_KV_HEADS):
                kcarry[g] = k_cur[g]
                vcarry[g] = v_cur[g]

        ret = []
        for hh in range(RET_HEADS):
            rq = _rope_ret(zs(OFF_RQ + hh * LANES), ret_c, ret_s)
            rk = _rope_ret(zs(OFF_RK + hh * LANES), ret_c, ret_s) * (RET_DK ** -0.5)
            ret.append(dict(
                q=rq.astype(BF16), qd=(rq * dec_ref[RET_HEADS + hh]).astype(BF16),
                k=rk.astype(BF16), kd=(rk * dec_ref[2 * RET_HEADS + hh]).astype(BF16),
                v=zs(OFF_RV + hh * LANES).astype(BF16)))

        sc_att = [_dot_nt(q_stack, k_band[g]) for g in range(ATT_KV_HEADS)]
        sc_ret = [_dot_nt(r["q"], r["k"]) for r in ret]
        fill(FILLS_UNDER_SOFTMAX)

        att = None
        for g in range(ATT_KV_HEADS):
            ws = []
            for j in range(4):
                sj = sc_att[g][j * CHUNK:(j + 1) * CHUNK, :] + bias
                sink = sinks_ref[g * ATT_GROUP + j]
                mx = jnp.maximum(jnp.max(sj, axis=-1, keepdims=True), sink)
                p = jnp.exp(sj - mx)
                den = jnp.sum(p, axis=-1, keepdims=True) + jnp.exp(sink - mx)
                ws.append((p * (1.0 / den)).astype(BF16))
            o = _dot(jnp.concatenate(ws, axis=0), v_band[g])
            att = o if att is None else att + o
        for j in range(4):
            mix_s[rows, j * LANES:(j + 1) * LANES] = att[j * CHUNK:(j + 1) * CHUNK, :].astype(BF16)

        outs = []
        for hh, r in enumerate(ret):
            state = s_ref[hh]
            sc = sc_ret[hh] * dec_ref[hh]
            outs.append(_dot(sc.astype(BF16), r["v"]) + _dot(r["qd"], state.astype(BF16)))
            s_ref[hh] = gl[hh] * state + _dot_tn(r["kd"], r["v"])
        if c < n_chunks - 1:
            fill(FILLS_UNDER_NORM)
        for hh in range(RET_HEADS):
            cols = slice(hh * LANES, (hh + 1) * LANES)
            o = outs[hh]
            mu = jnp.mean(o, axis=-1, keepdims=True)
            d = o - mu
            var = jnp.mean(d * d, axis=-1, keepdims=True)
            nrm = d * jax.lax.rsqrt(var + GN_EPS) * gnw_ref[:, cols]
            mix_s[rows, ATT_WIDTH + hh * LANES:ATT_WIDTH + (hh + 1) * LANES] = (
                _silu(zs(OFF_RG + hh * LANES)) * nrm).astype(BF16)

    y = _dot(mix_s[...], wout_ref[...])
    fill(n_fill, keep=0)
    r = DEEPNORM_ALPHA * xp_ref[...] + (1.0 + gate) * y
    y_ref[...] = _layer_norm(r, lnw_ref[...], lnb_ref[...])


def _mixer_kernel(sinks_ref, x_ref, mod_ref, tab_ref, xp_ref, modp_ref, dec_ref, bias_ref,
                  win_ref, wout_ref, gnw_ref, lnw_ref, lnb_ref,
                  y_ref, kout_ref, vout_ref, s_ref,
                  z_a, z_b, kcarry, vcarry, mix_s, *, tile, ns, gl):
    t = pl.program_id(0)
    sa = jax.lax.rem(jnp.maximum(t - 1, 0), ns)
    seq_start = (sa == 0).astype(jnp.int32)
    parity = jax.lax.rem(t, 2)

    @pl.when(t == 0)
    def _():
        z_b[...] = jnp.zeros(z_b.shape, z_b.dtype)
        kcarry[...] = jnp.zeros(kcarry.shape, kcarry.dtype)
        vcarry[...] = jnp.zeros(vcarry.shape, vcarry.dtype)

    @pl.when(sa == 0)
    def _():
        s_ref[...] = jnp.zeros_like(s_ref)

    step = functools.partial(
        _mixer_step, sinks_ref=sinks_ref, x_ref=x_ref, mod_ref=mod_ref, tab_ref=tab_ref, xp_ref=xp_ref,
        modp_ref=modp_ref, dec_ref=dec_ref, bias_ref=bias_ref, win_ref=win_ref, wout_ref=wout_ref,
        gnw_ref=gnw_ref, lnw_ref=lnw_ref, lnb_ref=lnb_ref, y_ref=y_ref, kout_ref=kout_ref,
        vout_ref=vout_ref, s_ref=s_ref, kcarry=kcarry, vcarry=vcarry, mix_s=mix_s,
        seq_start=seq_start, tile=tile, gl=gl)

    @pl.when(parity == 0)
    def _():
        step(z_b, z_a)

    @pl.when(parity == 1)
    def _():
        step(z_a, z_b)


def _mixer_prompt(x, modp, sinks, tab, dec, bias, w_in_p, w_out_p, gn_w, ln_w, ln_b):
    batch, seq, _ = x.shape
    tile = PROMPT_TILE
    ns = seq // tile
    gl = tuple(float(v) for v in np.exp(_log_gamma() * CHUNK).astype(np.float32))
    nt = batch * ns
    const = lambda shape: pl.BlockSpec(shape, lambda t, *_: (0,) * len(shape),
                                       pipeline_mode=pl.Buffered(1))
    cur_b = lambda t: jnp.minimum(t, nt - 1) // ns
    cur_s = lambda t: jnp.minimum(t, nt - 1) % ns
    prev_b = lambda t: jnp.maximum(t - 1, 0) // ns
    prev_s = lambda t: jnp.maximum(t - 1, 0) % ns
    grid_spec = pltpu.PrefetchScalarGridSpec(
        num_scalar_prefetch=1,
        grid=(nt + 1,),
        in_specs=[
            pl.BlockSpec((None, tile, D_MODEL), lambda t, *_: (cur_b(t), cur_s(t), 0)),
            pl.BlockSpec((None, 3, 1, D_MODEL), lambda t, *_: (cur_b(t), 0, 0, 0)),
            pl.BlockSpec((5, tile, LANES), lambda t, *_: (0, prev_s(t), 0)),
            pl.BlockSpec((None, tile, D_MODEL), lambda t, *_: (prev_b(t), prev_s(t), 0)),
            pl.BlockSpec((None, 3, 1, D_MODEL), lambda t, *_: (prev_b(t), 0, 0, 0)),
            const((3 * RET_HEADS, CHUNK, LANES)),
            const((2, CHUNK, 2 * CHUNK)),
            const((D_MODEL, IN_WIDTH)),
            const((D_MODEL, D_MODEL)),
            const((1, RET_WIDTH)),
            const((1, D_MODEL)),
            const((1, D_MODEL)),
        ],
        out_specs=[
            pl.BlockSpec((None, tile, D_MODEL), lambda t, *_: (prev_b(t), prev_s(t), 0)),
            pl.BlockSpec((None, WINDOW, LANES), lambda t, *_: (prev_b(t), 0, 0)),
            pl.BlockSpec((None, WINDOW, LANES), lambda t, *_: (prev_b(t), 0, 0)),
            pl.BlockSpec((None, RET_HEADS, RET_DK, LANES), lambda t, *_: (prev_b(t), 0, 0, 0)),
        ],
        scratch_shapes=[
            pltpu.VMEM((tile, IN_WIDTH), F32),
            pltpu.VMEM((tile, IN_WIDTH), F32),
            pltpu.VMEM((ATT_KV_HEADS, CHUNK, LANES), BF16),
            pltpu.VMEM((ATT_KV_HEADS, CHUNK, LANES), BF16),
            pltpu.VMEM((tile, D_MODEL), BF16),
        ],
    )
    return pl.pallas_call(
        functools.partial(_mixer_kernel, tile=tile, ns=ns, gl=gl),
        out_shape=[
            jax.ShapeDtypeStruct((batch, seq, D_MODEL), F32),
            jax.ShapeDtypeStruct((batch, WINDOW, LANES), F32),
            jax.ShapeDtypeStruct((batch, WINDOW, LANES), F32),
            jax.ShapeDtypeStruct((batch, RET_HEADS, RET_DK, LANES), F32),
        ],
        grid_spec=grid_spec,
        compiler_params=pltpu.CompilerParams(
            dimension_semantics=("arbitrary",), vmem_limit_bytes=VMEM_LIMIT_BYTES),
        name="mixer_prompt",
    )(sinks, x, modp, tab, x, modp, dec, bias, w_in_p, w_out_p, gn_w, ln_w, ln_b)


def _ffn_kernel(y_ref, mod_ref, wup_ref, wdown_ref, lnw_ref, lnb_ref, o_ref):
    y = y_ref[...]
    m = mod_ref[...]
    shift, scale, gate = m[0], m[1], m[2]
    h = (y * (1.0 + scale) + shift).astype(BF16)
    g = _dot(h, wup_ref[:, :D_FF])
    u = _dot(h, wup_ref[:, D_FF:])
    a = (_silu(g) * u).astype(BF16)
    f = _dot(a, wdown_ref[...])
    r = DEEPNORM_ALPHA * y + (1.0 + gate) * f
    o_ref[...] = _layer_norm(r, lnw_ref[...], lnb_ref[...])


def _ffn(y, mod, mod_rows, w_up, w_down, ln_w, ln_b, tile):
    groups, rows, _ = y.shape
    nt = rows // tile
    const = lambda shape: pl.BlockSpec(shape, lambda g, t: (0,) * len(shape))
    mod_map = (lambda g, t: (g, 0, 0, 0)) if mod_rows == 1 else (lambda g, t: (g, 0, t, 0))
    return pl.pallas_call(
        _ffn_kernel,
        out_shape=jax.ShapeDtypeStruct(y.shape, F32),
        grid=(groups, nt),
        in_specs=[
            pl.BlockSpec((None, tile, D_MODEL), lambda g, t: (g, t, 0)),
            pl.BlockSpec((None, 3, mod_rows, D_MODEL), mod_map),
            const((D_MODEL, 2 * D_FF)),
            const((D_FF, D_MODEL)),
            const((1, D_MODEL)),
            const((1, D_MODEL)),
        ],
        out_specs=pl.BlockSpec((None, tile, D_MODEL), lambda g, t: (g, t, 0)),
        compiler_params=pltpu.CompilerParams(
            dimension_semantics=("arbitrary", "arbitrary"), vmem_limit_bytes=VMEM_LIMIT_BYTES),
        name="ffn",
    )(y, mod, w_up, w_down, ln_w, ln_b)


def _sample_proj_kernel(x_ref, mod_ref, tab_ref, win_ref,
                        q_ref, k_ref, v_ref, rq_ref, rk_ref, rv_ref, rg_ref):
    x = x_ref[...]
    m = mod_ref[...]
    h = (x * (1.0 + m[1]) + m[0]).astype(BF16)
    z = _dot(h, win_ref[...])
    att_c, att_s1, att_s2 = tab_ref[0], tab_ref[1], tab_ref[2]
    ret_c, ret_s = tab_ref[3], tab_ref[4]
    for j in range(4):
        cols = slice(j * LANES, (j + 1) * LANES)
        q_ref[:, cols] = _rope_att(z[:, OFF_Q + j * LANES:OFF_Q + (j + 1) * LANES],
                                   att_c, att_s1, att_s2) * (HEAD_DIM ** -0.5)
        rq_ref[:, cols] = _rope_ret(z[:, OFF_RQ + j * LANES:OFF_RQ + (j + 1) * LANES], ret_c, ret_s)
        rk_ref[:, cols] = _rope_ret(z[:, OFF_RK + j * LANES:OFF_RK + (j + 1) * LANES],
                                    ret_c, ret_s) * (RET_DK ** -0.5)
    k_ref[...] = _rope_att(z[:, OFF_KV:OFF_KV + LANES], att_c, att_s1, att_s2)
    v_ref[...] = z[:, OFF_KV + LANES:OFF_KV + 2 * LANES]
    rv_ref[...] = z[:, OFF_RV:OFF_RV + RET_WIDTH]
    rg_ref[...] = z[:, OFF_RG:OFF_RG + RET_WIDTH]


def _sample_proj(x, mod, tab, w_in_p):
    n = x.shape[0]
    full = lambda shape: pl.BlockSpec(shape, lambda i: (0,) * len(shape))
    wide = jax.ShapeDtypeStruct((n, RET_WIDTH), F32)
    narrow = jax.ShapeDtypeStruct((n, LANES), F32)
    return pl.pallas_call(
        _sample_proj_kernel,
        out_shape=[wide, narrow, narrow, wide, wide, wide, wide],
        grid=(1,),
        in_specs=[full((n, D_MODEL)), full((3, n, D_MODEL)), full((5, 1, LANES)),
                  full((D_MODEL, IN_WIDTH))],
        out_specs=[full((n, RET_WIDTH)), full((n, LANES)), full((n, LANES)),
                   full((n, RET_WIDTH)), full((n, RET_WIDTH)), full((n, RET_WIDTH)),
                   full((n, RET_WIDTH))],
        compiler_params=pltpu.CompilerParams(
            dimension_semantics=("arbitrary",), vmem_limit_bytes=VMEM_LIMIT_BYTES),
        name="sample_proj",
    )(x, mod, tab, w_in_p)


def _sample_mix_kernel(sinks_ref, q_ref, kn_ref, vn_ref, rq_ref, rk_ref, rv_ref, rg_ref,
                       ck_ref, cv_ref, st_ref, gnw_ref,
                       mix_ref, ko_ref, vo_ref, so_ref, *, g1):
    lane_w = jax.lax.broadcasted_iota(jnp.int32, (WINDOW, LANES), 1)
    row_w = jax.lax.broadcasted_iota(jnp.int32, (WINDOW, LANES), 0)
    lo = lane_w < HEAD_DIM
    lo_row = lo[0:1, :]
    eye = row_w == lane_w
    last_row = row_w == WINDOW - 1
    row4 = jax.lax.broadcasted_iota(jnp.int32, (RET_HEADS, LANES), 0)
    r4 = jax.lax.broadcasted_iota(jnp.int32, (ATT_GROUP, 1), 0)

    def sink_col(base):
        s = [sinks_ref[base + j] for j in range(ATT_GROUP)]
        return jnp.where(r4 == 0, s[0], jnp.where(r4 == 1, s[1], jnp.where(r4 == 2, s[2], s[3])))

    sink_lo, sink_hi = sink_col(0), sink_col(ATT_GROUP)
    zero_w = jnp.zeros((WINDOW, LANES), F32)
    pad = jnp.zeros((16 - RET_HEADS, LANES), F32)

    def rows16(a):
        return jnp.concatenate([a, pad], axis=0).astype(BF16)

    for b in range(SAMPLE_BLOCK):
        ck = ck_ref[b]
        cv = cv_ref[b]
        kn = kn_ref[b]
        vn = vn_ref[b]
        ko_ref[b] = jnp.where(last_row, kn, pltpu.roll(ck, WINDOW - 1, axis=0))
        vo_ref[b] = jnp.where(last_row, vn, pltpu.roll(cv, WINDOW - 1, axis=0))
        q4 = q_ref[b]
        kcat = jnp.concatenate([jnp.where(lo, ck, zero_w), jnp.where(lo, zero_w, ck)], axis=0).astype(BF16)
        vcat = jnp.concatenate([jnp.where(lo, cv, zero_w), jnp.where(lo, zero_w, cv)], axis=0).astype(BF16)
        s01 = _dot_nt(rows16(q4), kcat)[0:ATT_GROUP]
        qk = q4 * kn
        zero4 = jnp.zeros_like(qk)
        sn_lo = jnp.sum(jnp.where(lo_row, qk, zero4), axis=-1, keepdims=True)
        sn_hi = jnp.sum(jnp.where(lo_row, zero4, qk), axis=-1, keepdims=True)
        ws = []
        wn = []
        for s_half, sn, sink in ((s01[:, :WINDOW], sn_lo, sink_lo), (s01[:, WINDOW:], sn_hi, sink_hi)):
            mx = jnp.maximum(jnp.maximum(jnp.max(s_half, axis=-1, keepdims=True), sn), sink)
            p = jnp.exp(s_half - mx)
            pn = jnp.exp(sn - mx)
            den = jnp.sum(p, axis=-1, keepdims=True) + pn + jnp.exp(sink - mx)
            inv = 1.0 / den
            ws.append(rows16(p * inv))
            wn.append(pn * inv)
        w01 = jnp.concatenate(ws, axis=1)
        vn_lo = jnp.where(lo_row, vn, jnp.zeros_like(vn))
        vn_hi = vn - vn_lo
        att = _dot(w01, vcat)[0:ATT_GROUP] + wn[0] * vn_lo + wn[1] * vn_hi
        mix_ref[b, 0:4, :] = att

        rq = rq_ref[b]
        rk = rk_ref[b]
        rv = rv_ref[b]
        qkr = jnp.sum(rq * rk, axis=-1, keepdims=True)
        o = qkr * rv
        for hh in range(RET_HEADS):
            state = st_ref[b, hh]
            oh = _dot(rows16(rq * g1[hh]), state.astype(BF16))[0:RET_HEADS]
            o = o + jnp.where(row4 == hh, oh, jnp.zeros_like(oh))
            kdiag = jnp.where(eye, jnp.broadcast_to(rk[hh:hh + 1, :], (WINDOW, LANES)), zero_w).astype(BF16)
            vrows = jnp.broadcast_to(rv[hh:hh + 1, :], (WINDOW, LANES)).astype(BF16)
            so_ref[b, hh] = g1[hh] * state + _dot(kdiag, vrows)
        mu = jnp.mean(o, axis=-1, keepdims=True)
        d = o - mu
        var = jnp.mean(d * d, axis=-1, keepdims=True)
        nrm = d * jax.lax.rsqrt(var + GN_EPS) * gnw_ref[...]
        mix_ref[b, 4:8, :] = _silu(rg_ref[b]) * nrm


def _sample_mix(sinks, q, kn, vn, rq, rk, rv, rg, cache_k, cache_v, state, gn_w4):
    n = q.shape[0]
    bb = SAMPLE_BLOCK
    g1 = tuple(float(v) for v in np.exp(_log_gamma()).astype(np.float32))
    blk = lambda *shape: pl.BlockSpec((bb,) + shape, lambda i, *_: (i,) + (0,) * len(shape))
    grid_spec = pltpu.PrefetchScalarGridSpec(
        num_scalar_prefetch=1,
        grid=(n // bb,),
        in_specs=[
            blk(4, LANES), blk(1, LANES), blk(1, LANES),
            blk(4, LANES), blk(4, LANES), blk(4, LANES), blk(4, LANES),
            blk(WINDOW, LANES), blk(WINDOW, LANES), blk(RET_HEADS, RET_DK, LANES),
            pl.BlockSpec((RET_HEADS, LANES), lambda i, *_: (0, 0)),
        ],
        out_specs=[blk(8, LANES), blk(WINDOW, LANES), blk(WINDOW, LANES),
                   blk(RET_HEADS, RET_DK, LANES)],
    )
    return pl.pallas_call(
        functools.partial(_sample_mix_kernel, g1=g1),
        out_shape=[
            jax.ShapeDtypeStruct((n, 8, LANES), F32),
            jax.ShapeDtypeStruct((n, WINDOW, LANES), F32),
            jax.ShapeDtypeStruct((n, WINDOW, LANES), F32),
            jax.ShapeDtypeStruct((n, RET_HEADS, RET_DK, LANES), F32),
        ],
        grid_spec=grid_spec,
        compiler_params=pltpu.CompilerParams(
            dimension_semantics=("arbitrary",), vmem_limit_bytes=VMEM_LIMIT_BYTES),
        name="sample_mix",
    )(sinks, q.reshape(n, 4, LANES), kn.reshape(n, 1, LANES), vn.reshape(n, 1, LANES),
      rq.reshape(n, 4, LANES), rk.reshape(n, 4, LANES), rv.reshape(n, 4, LANES),
      rg.reshape(n, 4, LANES), cache_k, cache_v, state, gn_w4)


def _sample_out_kernel(mix_ref, x_ref, mod_ref, wout_ref, lnw_ref, lnb_ref, o_ref):
    y = _dot(mix_ref[...].astype(BF16), wout_ref[...])
    r = DEEPNORM_ALPHA * x_ref[...] + (1.0 + mod_ref[2]) * y
    o_ref[...] = _layer_norm(r, lnw_ref[...], lnb_ref[...])


def _sample_out(mix, x, mod, w_out_p, ln_w, ln_b):
    n = x.shape[0]
    full = lambda shape: pl.BlockSpec(shape, lambda i: (0,) * len(shape))
    return pl.pallas_call(
        _sample_out_kernel,
        out_shape=jax.ShapeDtypeStruct((n, D_MODEL), F32),
        grid=(1,),
        in_specs=[full((n, D_MODEL)), full((n, D_MODEL)), full((3, n, D_MODEL)),
                  full((D_MODEL, D_MODEL)), full((1, D_MODEL)), full((1, D_MODEL))],
        out_specs=full((n, D_MODEL)),
        compiler_params=pltpu.CompilerParams(dimension_semantics=("arbitrary",)),
        name="sample_out",
    )(mix, x, mod, w_out_p, ln_w, ln_b)


def _slab_order(w, axis):
    shape = w.shape
    pre, post = shape[:axis], shape[axis + 1:]
    w = w.reshape(pre + (ATT_KV_HEADS, ATT_GROUP, HEAD_DIM) + post)
    w = jnp.swapaxes(w, axis, axis + 1)
    return w.reshape(shape)


def kernel(x_prompt, x_sample, c_prompt, c_sample, cache_k, cache_v, state_ret, w_ada_mix, b_ada_mix, w_in, att_sinks, ret_gn_w, w_out, ln1_w, ln1_b, w_ada_ffn, b_ada_ffn, w_up, w_down, ln2_w, ln2_b):
    batch, seq, _ = x_prompt.shape
    n_s = x_sample.shape[0]
    l = 0
    assert w_in.shape[0] == DEPTH == 1

    w_in_b = w_in.reshape(D_MODEL, IN_WIDTH).astype(BF16)
    w_out_b = w_out.reshape(D_MODEL, D_MODEL).astype(BF16)
    w_in_p = jnp.concatenate(
        [_slab_order(w_in_b[:, :ATT_WIDTH], 1), w_in_b[:, ATT_WIDTH:]], axis=1)
    w_out_p = jnp.concatenate(
        [_slab_order(w_out_b[:ATT_WIDTH], 0), w_out_b[ATT_WIDTH:]], axis=0)
    w_up_b = w_up.reshape(D_MODEL, 2 * D_FF).astype(BF16)
    w_down_b = w_down.reshape(D_FF, D_MODEL).astype(BF16)
    sinks = att_sinks[l]
    gn_w = ret_gn_w[l].reshape(1, RET_WIDTH)
    ln1w, ln1b = ln1_w[l].reshape(1, D_MODEL), ln1_b[l].reshape(1, D_MODEL)
    ln2w, ln2b = ln2_w[l].reshape(1, D_MODEL), ln2_b[l].reshape(1, D_MODEL)

    tab_p = jnp.asarray(_rope_tables(np.arange(seq)))
    tab_s = jnp.asarray(_rope_tables(np.array([PAST_LEN])))
    dec = jnp.asarray(_decay_tables())
    bias = jnp.asarray(_band_bias())

    c_all = jnp.concatenate([c_prompt, c_sample], axis=0)
    mod_mix = _ada_mod(c_all, w_ada_mix[l], b_ada_mix[l])
    mod_ffn = _ada_mod(c_all, w_ada_ffn[l], b_ada_ffn[l])
    modp_mix = mod_mix[:batch].reshape(batch, 3, 1, D_MODEL)
    modp_ffn = mod_ffn[:batch].reshape(batch, 3, 1, D_MODEL)
    mods_mix = jnp.swapaxes(mod_mix[batch:].reshape(n_s, 3, D_MODEL), 0, 1)
    mods_ffn = jnp.swapaxes(mod_ffn[batch:].reshape(n_s, 3, D_MODEL), 0, 1)

    y1p, kp, vp, sp = _mixer_prompt(x_prompt, modp_mix, sinks, tab_p, dec, bias, w_in_p, w_out_p,
                                    gn_w, ln1w, ln1b)
    yp = _ffn(y1p, modp_ffn, 1, w_up_b, w_down_b, ln2w, ln2b, FFN_TILE)

    xs = x_sample.reshape(n_s, D_MODEL)
    q, kn, vn, rq, rk, rv, rg = _sample_proj(xs, mods_mix, tab_s, w_in_p)
    mix, ks, vs, ss = _sample_mix(
        sinks, q, kn, vn, rq, rk, rv, rg,
        cache_k.reshape(n_s, WINDOW, LANES), cache_v.reshape(n_s, WINDOW, LANES),
        state_ret.reshape(n_s, RET_HEADS, RET_DK, LANES), ret_gn_w.reshape(RET_HEADS, LANES))
    y1s = _sample_out(mix.reshape(n_s, D_MODEL), xs, mods_mix, w_out_p, ln1w, ln1b)
    ys = _ffn(y1s.reshape(1, n_s, D_MODEL), mods_ffn.reshape(1, 3, n_s, D_MODEL), n_s,
              w_up_b, w_down_b, ln2w, ln2b, n_s)

    kv_shape = (1, batch, WINDOW, ATT_KV_HEADS, HEAD_DIM)
    kvs_shape = (1, n_s, WINDOW, ATT_KV_HEADS, HEAD_DIM)
    return (yp, ys.reshape(n_s, 1, D_MODEL),
            kp.reshape(kv_shape), vp.reshape(kv_shape), sp[None],
            ks.reshape(kvs_shape), vs.reshape(kvs_shape), ss[None])
```

```python
import functools
import math

import numpy as np
import jax
import jax.numpy as jnp
from jax.experimental import pallas as pl
from jax.experimental.pallas import tpu as pltpu

D_MODEL = 1024
WINDOW = 128
CHUNK = 128
ATT_HEADS = 8
ATT_KV_HEADS = 2
ATT_GROUP = ATT_HEADS // ATT_KV_HEADS
HEAD_DIM = 64
ROPE_DIMS = HEAD_DIM // 4
ROPE_THETA = 500000.0
RET_HEADS = 4
RET_DK = 128
RET_THETA = 10000.0
ATT_WIDTH = ATT_HEADS * HEAD_DIM
KV_WIDTH = ATT_KV_HEADS * HEAD_DIM
RET_WIDTH = RET_HEADS * RET_DK
IN_WIDTH = ATT_WIDTH + 2 * KV_WIDTH + 4 * RET_WIDTH
D_FF = 2816
DEPTH = 1
DEEPNORM_ALPHA = (2 * DEPTH) ** 0.25
LN_EPS = 1e-5
GN_EPS = 1e-6
PAST_LEN = 16384

OFF_Q = 0
OFF_KV = ATT_WIDTH
OFF_RQ = OFF_KV + 2 * KV_WIDTH
OFF_RK = OFF_RQ + RET_WIDTH
OFF_RV = OFF_RK + RET_WIDTH
OFF_RG = OFF_RV + RET_WIDTH

LANES = 128
VMEM_LIMIT_BYTES = 56 * 1024 * 1024

PROMPT_TILE = 512
FILL_WIDTH = 256
FILLS_UNDER_SOFTMAX = 2
FILLS_UNDER_NORM = 1
FILLS_UNDER_LAYER_NORM = 1
FFN_TILE = 512
SAMPLE_BLOCK = 8
ADA_BLOCK = 1024

BF16 = jnp.bfloat16
F32 = jnp.float32


def _dot(a, b):
    return jnp.dot(a, b, preferred_element_type=F32)


def _dot_nt(a, b):
    return jax.lax.dot_general(a, b, (((1,), (1,)), ((), ())), preferred_element_type=F32)


def _dot_tn(a, b):
    return jax.lax.dot_general(a, b, (((0,), (0,)), ((), ())), preferred_element_type=F32)


def _sigmoid(x):
    return 1.0 / (1.0 + jnp.exp(-x))


def _silu(x):
    return x * _sigmoid(x)


def _layer_norm(r, w, b):
    mu = jnp.mean(r, axis=-1, keepdims=True)
    d = r - mu
    var = jnp.mean(d * d, axis=-1, keepdims=True)
    return d * jax.lax.rsqrt(var + LN_EPS) * w + b


def _rope_att(slab, c, s1, s2):
    return slab * c + pltpu.roll(slab, LANES - 8, axis=1) * s1 + pltpu.roll(slab, 8, axis=1) * s2


def _rope_ret(slab, c, s):
    return slab * c + pltpu.roll(slab, LANES // 2, axis=1) * s


def _rope_tables(pos):
    pos = np.asarray(pos, np.float64)[:, None]
    half = ROPE_DIMS // 2
    inv = ROPE_THETA ** (-np.arange(half, dtype=np.float64) / half)
    ang = pos * inv[None, :]
    cos, sin = np.cos(ang), np.sin(ang)
    n = pos.shape[0]
    head_c = np.ones((n, HEAD_DIM)); head_s1 = np.zeros((n, HEAD_DIM)); head_s2 = np.zeros((n, HEAD_DIM))
    head_c[:, :half] = cos; head_c[:, half:2 * half] = cos
    head_s1[:, :half] = -sin
    head_s2[:, half:2 * half] = sin
    att = [np.tile(t, (1, LANES // HEAD_DIM)) for t in (head_c, head_s1, head_s2)]
    rhalf = RET_DK // 2
    rinv = RET_THETA ** (-np.arange(rhalf, dtype=np.float64) / rhalf)
    rang = pos * rinv[None, :]
    rc = np.concatenate([np.cos(rang), np.cos(rang)], axis=1)
    rs = np.concatenate([-np.sin(rang), np.sin(rang)], axis=1)
    return np.stack(att + [rc, rs], axis=0).astype(np.float32)


def _log_gamma():
    lin = np.linspace(math.log(1.0 / 32), math.log(1.0 / 512), RET_HEADS)
    return np.log1p(-np.exp(lin))


def _decay_tables():
    lg = _log_gamma()
    idx = np.arange(CHUNK, dtype=np.float64)
    diff = idx[:, None] - idx[None, :]
    dmat = np.where(diff[None] >= 0, np.exp(lg[:, None, None] * np.maximum(diff, 0.0)[None]), 0.0)
    dq = np.exp(lg[:, None] * (idx[None, :] + 1.0))[:, :, None] * np.ones((1, 1, LANES))
    dk = np.exp(lg[:, None] * (CHUNK - 1.0 - idx[None, :]))[:, :, None] * np.ones((1, 1, LANES))
    return np.concatenate([dmat, dq, dk], axis=0).astype(np.float32)


def _band_bias():
    i = np.arange(CHUNK)[:, None]
    j = np.arange(2 * CHUNK)[None, :]
    valid = (j >= i) & (j <= i + WINDOW)
    neg = np.float32(-1e30)
    b0 = np.where(valid, 0.0, neg)
    b1 = np.where(valid & (j >= CHUNK), 0.0, neg)
    return np.stack([b0, b1], axis=0).astype(np.float32)


def _ada_kernel(c_ref, w1_ref, w2_ref, b_ref, o_ref, *, steps_per_weight):
    a = _silu(c_ref[...]).astype(BF16)
    j = pl.program_id(0)

    @pl.when(j < steps_per_weight)
    def _():
        o_ref[...] = _dot(a, w1_ref[...].astype(BF16)) + b_ref[...]

    @pl.when(j >= steps_per_weight)
    def _():
        o_ref[...] = _dot(a, w2_ref[...].astype(BF16)) + b_ref[...]


def _ada_mod(c_all, w_mix, w_ffn, b_all):
    rows = c_all.shape[0]
    n = w_mix.shape[1]
    bn = ADA_BLOCK
    spw = n // bn
    return pl.pallas_call(
        functools.partial(_ada_kernel, steps_per_weight=spw),
        out_shape=jax.ShapeDtypeStruct((rows, 2 * n), F32),
        grid=(2 * spw,),
        in_specs=[
            pl.BlockSpec((rows, D_MODEL), lambda j: (0, 0)),
            pl.BlockSpec((D_MODEL, bn), lambda j: (0, jnp.minimum(j, spw - 1))),
            pl.BlockSpec((D_MODEL, bn), lambda j: (0, jnp.maximum(j - spw, 0))),
            pl.BlockSpec((1, bn), lambda j: (0, j)),
        ],
        out_specs=pl.BlockSpec((rows, bn), lambda j: (0, j)),
        compiler_params=pltpu.CompilerParams(
            dimension_semantics=("arbitrary",), vmem_limit_bytes=VMEM_LIMIT_BYTES),
        name="ada_mod",
    )(c_all, w_mix, w_ffn, b_all)


def _mixer_step(z_prev, z_cur, sinks_ref, x_ref, mod_ref, tab_ref, xp_ref, modp_ref, dec_ref, bias_ref,
                win_ref, wout_ref, gnw_ref, lnw_ref, lnb_ref,
                y_ref, kout_ref, vout_ref, s_ref, kcarry, vcarry, mix_s, *, seq_start, tile, gl):
    n_chunks = tile // CHUNK
    x = x_ref[...]
    m = mod_ref[...]
    h = (x * (1.0 + m[1]) + m[0]).astype(BF16)
    n_fill = IN_WIDTH // FILL_WIDTH
    pending = list(range(n_fill))

    def fill(count, keep=FILLS_UNDER_LAYER_NORM):
        for _ in range(count):
            if len(pending) > keep:
                i = pending.pop(0)
                cols = slice(i * FILL_WIDTH, (i + 1) * FILL_WIDTH)
                z_cur[:, cols] = _dot(h, win_ref[:, cols])

    gate = modp_ref[2]
    lo =jax.lax.broadcasted_iota(jnp.int32, (CHUNK, LANES), 1) < HEAD_DIM
    zero = jnp.zeros((CHUNK, LANES), F32)
    k_prev = [kcarry[g] for g in range(ATT_KV_HEADS)]
    v_prev = [vcarry[g] for g in range(ATT_KV_HEADS)]

    for c in range(n_chunks):
        rows = slice(c * CHUNK, (c + 1) * CHUNK)
        bias = bias_ref[seq_start] if c == 0 else bias_ref[0]
        att_c, att_s1, att_s2 = tab_ref[0, rows, :], tab_ref[1, rows, :], tab_ref[2, rows, :]
        ret_c, ret_s = tab_ref[3, rows, :], tab_ref[4, rows, :]

        def zs(col):
            return z_prev[rows, col:col + LANES]

        q_stack = jnp.concatenate(
            [(_rope_att(zs(OFF_Q + j * LANES), att_c, att_s1, att_s2) * (HEAD_DIM ** -0.5)).astype(BF16)
             for j in range(4)], axis=0)
        k = _rope_att(zs(OFF_KV), att_c, att_s1, att_s2)
        v = zs(OFF_KV + LANES)
        k_cur = [jnp.where(lo, k, zero).astype(BF16), jnp.where(lo, zero, k).astype(BF16)]
        v_cur = [jnp.where(lo, v, zero).astype(BF16), jnp.where(lo, zero, v).astype(BF16)]
        k_band = [jnp.concatenate([k_prev[g], k_cur[g]], axis=0) for g in range(ATT_KV_HEADS)]
        v_band = [jnp.concatenate([v_prev[g], v_cur[g]], axis=0) for g in range(ATT_KV_HEADS)]
        k_prev, v_prev = k_cur, v_cur
        if c == n_chunks - 1:
            kout_ref[...] = k
            vout_ref[...] = v
            for g in range(ATT_KV_HEADS):
                kcarry[g] = k_cur[g]
                vcarry[g] = v_cur[g]

        ret = []
        for hh in range(RET_HEADS):
            rq = _rope_ret(zs(OFF_RQ + hh * LANES), ret_c, ret_s)
            rk = _rope_ret(zs(OFF_RK + hh * LANES), ret_c, ret_s) * (RET_DK ** -0.5)
            ret.append(dict(
                q=rq.astype(BF16), qd=(rq * dec_ref[RET_HEADS + hh]).astype(BF16),
                k=rk.astype(BF16), kd=(rk * dec_ref[2 * RET_HEADS + hh]).astype(BF16),
                v=zs(OFF_RV + hh * LANES).astype(BF16)))

        sc_att = [_dot_nt(q_stack, k_band[g]) for g in range(ATT_KV_HEADS)]
        sc_ret = [_dot_nt(r["q"], r["k"]) for r in ret]
        fill(FILLS_UNDER_SOFTMAX)

        att = None
        for g in range(ATT_KV_HEADS):
            ws = []
            for j in range(4):
                sj = sc_att[g][j * CHUNK:(j + 1) * CHUNK, :] + bias
                sink = sinks_ref[g * ATT_GROUP + j]
                mx = jnp.maximum(jnp.max(sj, axis=-1, keepdims=True), sink)
                p = jnp.exp(sj - mx)
                den = jnp.sum(p, axis=-1, keepdims=True) + jnp.exp(sink - mx)
                ws.append((p * (1.0 / den)).astype(BF16))
            o = _dot(jnp.concatenate(ws, axis=0), v_band[g])
            att = o if att is None else att + o
        for j in range(4):
            mix_s[rows, j * LANES:(j + 1) * LANES] = att[j * CHUNK:(j + 1) * CHUNK, :].astype(BF16)

        outs = []
        for hh, r in enumerate(ret):
            state = s_ref[hh]
            sc = sc_ret[hh] * dec_ref[hh]
            outs.append(_dot(sc.astype(BF16), r["v"]) + _dot(r["qd"], state.astype(BF16)))
            s_ref[hh] = gl[hh] * state + _dot_tn(r["kd"], r["v"])
        if c < n_chunks - 1:
            fill(FILLS_UNDER_NORM)
        for hh in range(RET_HEADS):
            cols = slice(hh * LANES, (hh + 1) * LANES)
            o = outs[hh]
            mu = jnp.mean(o, axis=-1, keepdims=True)
            d = o - mu
            var = jnp.mean(d * d, axis=-1, keepdims=True)
            nrm = d * jax.lax.rsqrt(var + GN_EPS) * gnw_ref[:, cols]
            mix_s[rows, ATT_WIDTH + hh * LANES:ATT_WIDTH + (hh + 1) * LANES] = (
                _silu(zs(OFF_RG + hh * LANES)) * nrm).astype(BF16)

    y = _dot(mix_s[...], wout_ref[...])
    fill(n_fill, keep=0)
    r = DEEPNORM_ALPHA * xp_ref[...] + (1.0 + gate) * y
    y_ref[...] = _layer_norm(r, lnw_ref[...], lnb_ref[...])


def _mixer_kernel(sinks_ref, x_ref, mod_ref, tab_ref, xp_ref, modp_ref, dec_ref, bias_ref,
                  win_ref, wout_ref, gnw_ref, lnw_ref, lnb_ref,
                  y_ref, kout_ref, vout_ref, s_ref,
                  z_a, z_b, kcarry, vcarry, mix_s, *, tile, ns, gl):
    t = pl.program_id(0)
    sa = jax.lax.rem(jnp.maximum(t - 1, 0), ns)
    seq_start = (sa == 0).astype(jnp.int32)
    parity = jax.lax.rem(t, 2)

    @pl.when(t == 0)
    def _():
        z_b[...] = jnp.zeros(z_b.shape, z_b.dtype)
        kcarry[...] = jnp.zeros(kcarry.shape, kcarry.dtype)
        vcarry[...] = jnp.zeros(vcarry.shape, vcarry.dtype)

    @pl.when(sa == 0)
    def _():
        s_ref[...] = jnp.zeros_like(s_ref)

    step = functools.partial(
        _mixer_step, sinks_ref=sinks_ref, x_ref=x_ref, mod_ref=mod_ref, tab_ref=tab_ref, xp_ref=xp_ref,
        modp_ref=modp_ref, dec_ref=dec_ref, bias_ref=bias_ref, win_ref=win_ref, wout_ref=wout_ref,
        gnw_ref=gnw_ref, lnw_ref=lnw_ref, lnb_ref=lnb_ref, y_ref=y_ref, kout_ref=kout_ref,
        vout_ref=vout_ref, s_ref=s_ref, kcarry=kcarry, vcarry=vcarry, mix_s=mix_s,
        seq_start=seq_start, tile=tile, gl=gl)

    @pl.when(parity == 0)
    def _():
        step(z_b, z_a)

    @pl.when(parity == 1)
    def _():
        step(z_a, z_b)


def _mixer_prompt(x, modp, sinks, tab, dec, bias, w_in_p, w_out_p, gn_w, ln_w, ln_b):
    batch, seq, _ = x.shape
    tile = PROMPT_TILE
    ns = seq // tile
    gl = tuple(float(v) for v in np.exp(_log_gamma() * CHUNK).astype(np.float32))
    nt = batch * ns
    const = lambda shape: pl.BlockSpec(shape, lambda t, *_: (0,) * len(shape),
                                       pipeline_mode=pl.Buffered(1))
    cur_b = lambda t: jnp.minimum(t, nt - 1) // ns
    cur_s = lambda t: jnp.minimum(t, nt - 1) % ns
    prev_b = lambda t: jnp.maximum(t - 1, 0) // ns
    prev_s = lambda t: jnp.maximum(t - 1, 0) % ns
    grid_spec = pltpu.PrefetchScalarGridSpec(
        num_scalar_prefetch=1,
        grid=(nt + 1,),
        in_specs=[
            pl.BlockSpec((None, tile, D_MODEL), lambda t, *_: (cur_b(t), cur_s(t), 0)),
            pl.BlockSpec((None, 3, 1, D_MODEL), lambda t, *_: (cur_b(t), 0, 0, 0)),
            pl.BlockSpec((5, tile, LANES), lambda t, *_: (0, prev_s(t), 0)),
            pl.BlockSpec((None, tile, D_MODEL), lambda t, *_: (prev_b(t), prev_s(t), 0)),
            pl.BlockSpec((None, 3, 1, D_MODEL), lambda t, *_: (prev_b(t), 0, 0, 0)),
            const((3 * RET_HEADS, CHUNK, LANES)),
            const((2, CHUNK, 2 * CHUNK)),
            const((D_MODEL, IN_WIDTH)),
            const((D_MODEL, D_MODEL)),
            const((1, RET_WIDTH)),
            const((1, D_MODEL)),
            const((1, D_MODEL)),
        ],
        out_specs=[
            pl.BlockSpec((None, tile, D_MODEL), lambda t, *_: (prev_b(t), prev_s(t), 0)),
            pl.BlockSpec((None, WINDOW, LANES), lambda t, *_: (prev_b(t), 0, 0)),
            pl.BlockSpec((None, WINDOW, LANES), lambda t, *_: (prev_b(t), 0, 0)),
            pl.BlockSpec((None, RET_HEADS, RET_DK, LANES), lambda t, *_: (prev_b(t), 0, 0, 0)),
        ],
        scratch_shapes=[
            pltpu.VMEM((tile, IN_WIDTH), F32),
            pltpu.VMEM((tile, IN_WIDTH), F32),
            pltpu.VMEM((ATT_KV_HEADS, CHUNK, LANES), BF16),
            pltpu.VMEM((ATT_KV_HEADS, CHUNK, LANES), BF16),
            pltpu.VMEM((tile, D_MODEL), BF16),
        ],
    )
    return pl.pallas_call(
        functools.partial(_mixer_kernel, tile=tile, ns=ns, gl=gl),
        out_shape=[
            jax.ShapeDtypeStruct((batch, seq, D_MODEL), F32),
            jax.ShapeDtypeStruct((batch, WINDOW, LANES), F32),
            jax.ShapeDtypeStruct((batch, WINDOW, LANES), F32),
            jax.ShapeDtypeStruct((batch, RET_HEADS, RET_DK, LANES), F32),
        ],
        grid_spec=grid_spec,
        compiler_params=pltpu.CompilerParams(
            dimension_semantics=("arbitrary",), vmem_limit_bytes=VMEM_LIMIT_BYTES),
        name="mixer_prompt",
    )(sinks, x, modp, tab, x, modp, dec, bias, w_in_p, w_out_p, gn_w, ln_w, ln_b)


def _ffn_kernel(y_ref, mod_ref, wup_ref, wdown_ref, lnw_ref, lnb_ref, o_ref):
    y = y_ref[...]
    m = mod_ref[...]
    shift, scale, gate = m[0], m[1], m[2]
    h = (y * (1.0 + scale) + shift).astype(BF16)
    g = _dot(h, wup_ref[:, :D_FF])
    u = _dot(h, wup_ref[:, D_FF:])
    a = (_silu(g) * u).astype(BF16)
    f = _dot(a, wdown_ref[...])
    r = DEEPNORM_ALPHA * y + (1.0 + gate) * f
    o_ref[...] = _layer_norm(r, lnw_ref[...], lnb_ref[...])


def _ffn(y, mod, mod_rows, w_up, w_down, ln_w, ln_b, tile):
    groups, rows, _ = y.shape
    nt = rows // tile
    const = lambda shape: pl.BlockSpec(shape, lambda g, t: (0,) * len(shape))
    mod_map = (lambda g, t: (g, 0, 0, 0)) if mod_rows == 1 else (lambda g, t: (g, 0, t, 0))
    return pl.pallas_call(
        _ffn_kernel,
        out_shape=jax.ShapeDtypeStruct(y.shape, F32),
        grid=(groups, nt),
        in_specs=[
            pl.BlockSpec((None, tile, D_MODEL), lambda g, t: (g, t, 0)),
            pl.BlockSpec((None, 3, mod_rows, D_MODEL), mod_map),
            const((D_MODEL, 2 * D_FF)),
            const((D_FF, D_MODEL)),
            const((1, D_MODEL)),
            const((1, D_MODEL)),
        ],
        out_specs=pl.BlockSpec((None, tile, D_MODEL), lambda g, t: (g, t, 0)),
        compiler_params=pltpu.CompilerParams(
            dimension_semantics=("arbitrary", "arbitrary"), vmem_limit_bytes=VMEM_LIMIT_BYTES),
        name="ffn",
    )(y, mod, w_up, w_down, ln_w, ln_b)


def _sample_kernel(sinks_ref, x_ref, mod_ref, tab_ref, win_ref, wout_ref, gnw_ref, lnw_ref, lnb_ref,
                   ck_ref, cv_ref, st_ref,
                   y_ref, ko_ref, vo_ref, so_ref,
                   z_s, mix_s, *, g1, n_blocks):
    i = pl.program_id(0)
    bb = SAMPLE_BLOCK

    @pl.when(i == 0)
    def _():
        m = mod_ref[...]
        h = (x_ref[...] * (1.0 + m[1]) + m[0]).astype(BF16)
        z = _dot(h, win_ref[...])
        att_c, att_s1, att_s2 = tab_ref[0], tab_ref[1], tab_ref[2]
        ret_c, ret_s = tab_ref[3], tab_ref[4]
        for j in range(4):
            col = OFF_Q + j * LANES
            z_s[:, col:col + LANES] = _rope_att(z[:, col:col + LANES], att_c, att_s1, att_s2) * (HEAD_DIM ** -0.5)
            col = OFF_RQ + j * LANES
            z_s[:, col:col + LANES] = _rope_ret(z[:, col:col + LANES], ret_c, ret_s)
            col = OFF_RK + j * LANES
            z_s[:, col:col + LANES] = _rope_ret(z[:, col:col + LANES], ret_c, ret_s) * (RET_DK ** -0.5)
        z_s[:, OFF_KV:OFF_KV + LANES] = _rope_att(z[:, OFF_KV:OFF_KV + LANES], att_c, att_s1, att_s2)
        z_s[:, OFF_KV + LANES:OFF_RQ] = z[:, OFF_KV + LANES:OFF_RQ]
        z_s[:, OFF_RV:OFF_RG] = z[:, OFF_RV:OFF_RG]
        z_s[:, OFF_RG:] = _silu(z[:, OFF_RG:])

    lane_w = jax.lax.broadcasted_iota(jnp.int32, (WINDOW, LANES), 1)
    row_w = jax.lax.broadcasted_iota(jnp.int32, (WINDOW, LANES), 0)
    lo = lane_w < HEAD_DIM
    lo_row = lo[0:1, :]
    last_row = row_w == WINDOW - 1
    r4 = jax.lax.broadcasted_iota(jnp.int32, (ATT_GROUP, 1), 0)
    head_blk = (jax.lax.broadcasted_iota(jnp.int32, (16, RET_WIDTH), 1) // LANES
                == jax.lax.broadcasted_iota(jnp.int32, (16, RET_WIDTH), 0))
    g1_col = jnp.where(r4 == 0, g1[0], jnp.where(r4 == 1, g1[1], jnp.where(r4 == 2, g1[2], g1[3])))

    def sink_col(base):
        s = [sinks_ref[base + j] for j in range(ATT_GROUP)]
        return jnp.where(r4 == 0, s[0], jnp.where(r4 == 1, s[1], jnp.where(r4 == 2, s[2], s[3])))

    sink_lo, sink_hi = sink_col(0), sink_col(ATT_GROUP)
    zero_w = jnp.zeros((WINDOW, LANES), F32)

    def rows16(a):
        return jnp.concatenate([a, jnp.zeros((16 - a.shape[0], a.shape[1]), F32)], axis=0)

    def heads(row, off):
        return jnp.concatenate([row[:, off + j * LANES:off + (j + 1) * LANES] for j in range(4)], axis=0)

    def block_rows(a4):
        wide = jnp.concatenate([rows16(a4)] * RET_HEADS, axis=1)
        return jnp.where(head_blk, wide, jnp.zeros_like(wide))

    r0 = pl.multiple_of(i * bb, bb)
    zblk = z_s[pl.ds(r0, bb), :]
    rows = [zblk[b:b + 1, :] for b in range(bb)]

    q4 = [heads(r, OFF_Q) for r in rows]
    kn = [r[:, OFF_KV:OFF_KV + LANES] for r in rows]
    vn = [r[:, OFF_KV + LANES:OFF_RQ] for r in rows]
    s01, oh = [], []
    for b in range(bb):
        ck = ck_ref[b]
        ko_ref[b] = jnp.where(last_row, kn[b], pltpu.roll(ck, WINDOW - 1, axis=0))
        kcat = jnp.concatenate([jnp.where(lo, ck, zero_w), jnp.where(lo, zero_w, ck)], axis=0).astype(BF16)
        s01.append(_dot_nt(rows16(q4[b]).astype(BF16), kcat)[0:ATT_GROUP])
    for b in range(bb):
        rq = heads(rows[b], OFF_RQ)
        rk = heads(rows[b], OFF_RK)
        rv = heads(rows[b], OFF_RV)
        state = st_ref[b].reshape(RET_WIDTH, LANES)
        o2 = _dot(block_rows(rq * g1_col).astype(BF16), state.astype(BF16))[0:RET_HEADS]
        outer = _dot_tn(block_rows(rk).astype(BF16), rows16(rv).astype(BF16))
        for hh in range(RET_HEADS):
            so_ref[b, hh] = g1[hh] * st_ref[b, hh] + outer[hh * RET_DK:(hh + 1) * RET_DK, :]
        o = jnp.sum(rq * rk, axis=-1, keepdims=True) * rv + o2
        mu = jnp.mean(o, axis=-1, keepdims=True)
        d = o - mu
        var = jnp.mean(d * d, axis=-1, keepdims=True)
        oh.append(heads(rows[b], OFF_RG) * (d * jax.lax.rsqrt(var + GN_EPS) * gnw_ref[...]))

    w01, wn = [], []
    for b in range(bb):
        qk = q4[b] * kn[b]
        zero4 = jnp.zeros_like(qk)
        sn_lo = jnp.sum(jnp.where(lo_row, qk, zero4), axis=-1, keepdims=True)
        sn_hi = jnp.sum(jnp.where(lo_row, zero4, qk), axis=-1, keepdims=True)
        ws, wb = [], []
        for s_half, sn, sink in ((s01[b][:, :WINDOW], sn_lo, sink_lo), (s01[b][:, WINDOW:], sn_hi, sink_hi)):
            mx = jnp.maximum(jnp.maximum(jnp.max(s_half, axis=-1, keepdims=True), sn), sink)
            p = jnp.exp(s_half - mx)
            pn = jnp.exp(sn - mx)
            den = jnp.sum(p, axis=-1, keepdims=True) + pn + jnp.exp(sink - mx)
            inv = 1.0 / den
            ws.append(rows16(p * inv).astype(BF16))
            wb.append(pn * inv)
        w01.append(jnp.concatenate(ws, axis=1))
        wn.append(wb)

    att = []
    for b in range(bb):
        cv = cv_ref[b]
        vo_ref[b] = jnp.where(last_row, vn[b], pltpu.roll(cv, WINDOW - 1, axis=0))
        vcat = jnp.concatenate([jnp.where(lo, cv, zero_w), jnp.where(lo, zero_w, cv)], axis=0).astype(BF16)
        vn_lo = jnp.where(lo_row, vn[b], jnp.zeros_like(vn[b]))
        vn_hi = vn[b] - vn_lo
        att.append(_dot(w01[b], vcat)[0:ATT_GROUP] + wn[b][0] * vn_lo + wn[b][1] * vn_hi)
    blk_rows = pl.ds(r0, bb)
    for j in range(4):
        mix_s[blk_rows, j * LANES:(j + 1) * LANES] = jnp.concatenate(
            [a[j:j + 1, :] for a in att], axis=0)
        mix_s[blk_rows, ATT_WIDTH + j * LANES:ATT_WIDTH + (j + 1) * LANES] = jnp.concatenate(
            [o[j:j + 1, :] for o in oh], axis=0)

    @pl.when(i == n_blocks - 1)
    def _():
        y = _dot(mix_s[...].astype(BF16), wout_ref[...])
        r = DEEPNORM_ALPHA * x_ref[...] + (1.0 + mod_ref[2]) * y
        y_ref[...] = _layer_norm(r, lnw_ref[...], lnb_ref[...])


def _sample_step(sinks, x, mod, tab, w_in_p, w_out_p, gn_w4, ln_w, ln_b, cache_k, cache_v, state):
    n = x.shape[0]
    bb = SAMPLE_BLOCK
    n_blocks = n // bb
    g1 = tuple(float(v) for v in np.exp(_log_gamma()).astype(np.float32))
    full = lambda *shape: pl.BlockSpec(shape, lambda i, *_: (0,) * len(shape))
    blk = lambda *shape: pl.BlockSpec((bb,) + shape, lambda i, *_: (i,) + (0,) * len(shape))
    grid_spec = pltpu.PrefetchScalarGridSpec(
        num_scalar_prefetch=1,
        grid=(n_blocks,),
        in_specs=[
            full(n, D_MODEL), full(3, n, D_MODEL), full(5, 1, LANES),
            full(D_MODEL, IN_WIDTH), full(D_MODEL, D_MODEL),
            full(RET_HEADS, LANES), full(1, D_MODEL), full(1, D_MODEL),
            blk(WINDOW, LANES), blk(WINDOW, LANES), blk(RET_HEADS, RET_DK, LANES),
        ],
        out_specs=[full(n, D_MODEL), blk(WINDOW, LANES), blk(WINDOW, LANES),
                   blk(RET_HEADS, RET_DK, LANES)],
        scratch_shapes=[pltpu.VMEM((n, IN_WIDTH), F32), pltpu.VMEM((n, D_MODEL), F32)],
    )
    return pl.pallas_call(
        functools.partial(_sample_kernel, g1=g1, n_blocks=n_blocks),
        out_shape=[
            jax.ShapeDtypeStruct((n, D_MODEL), F32),
            jax.ShapeDtypeStruct((n, WINDOW, LANES), F32),
            jax.ShapeDtypeStruct((n, WINDOW, LANES), F32),
            jax.ShapeDtypeStruct((n, RET_HEADS, RET_DK, LANES), F32),
        ],
        grid_spec=grid_spec,
        compiler_params=pltpu.CompilerParams(
            dimension_semantics=("arbitrary",), vmem_limit_bytes=VMEM_LIMIT_BYTES),
        name="sample_step",
    )(sinks, x, mod, tab, w_in_p, w_out_p, gn_w4, ln_w, ln_b, cache_k, cache_v, state)


def _slab_order(w, axis):
    shape = w.shape
    pre, post = shape[:axis], shape[axis + 1:]
    w = w.reshape(pre + (ATT_KV_HEADS, ATT_GROUP, HEAD_DIM) + post)
    w = jnp.swapaxes(w, axis, axis + 1)
    return w.reshape(shape)


def kernel(x_prompt, x_sample, c_prompt, c_sample, cache_k, cache_v, state_ret, w_ada_mix, b_ada_mix, w_in, att_sinks, ret_gn_w, w_out, ln1_w, ln1_b, w_ada_ffn, b_ada_ffn, w_up, w_down, ln2_w, ln2_b):
    batch, seq, _ = x_prompt.shape
    n_s = x_sample.shape[0]
    l = 0
    assert w_in.shape[0] == DEPTH == 1

    w_in_b = w_in.reshape(D_MODEL, IN_WIDTH).astype(BF16)
    w_out_b = w_out.reshape(D_MODEL, D_MODEL).astype(BF16)
    w_in_p = jnp.concatenate(
        [_slab_order(w_in_b[:, :ATT_WIDTH], 1), w_in_b[:, ATT_WIDTH:]], axis=1)
    w_out_p = jnp.concatenate(
        [_slab_order(w_out_b[:ATT_WIDTH], 0), w_out_b[ATT_WIDTH:]], axis=0)
    w_up_b = w_up.reshape(D_MODEL, 2 * D_FF).astype(BF16)
    w_down_b = w_down.reshape(D_FF, D_MODEL).astype(BF16)
    sinks = att_sinks[l]
    gn_w = ret_gn_w[l].reshape(1, RET_WIDTH)
    ln1w, ln1b = ln1_w[l].reshape(1, D_MODEL), ln1_b[l].reshape(1, D_MODEL)
    ln2w, ln2b = ln2_w[l].reshape(1, D_MODEL), ln2_b[l].reshape(1, D_MODEL)

    tab_p = jnp.asarray(_rope_tables(np.arange(seq)))
    tab_s = jnp.asarray(_rope_tables(np.array([PAST_LEN])))
    dec = jnp.asarray(_decay_tables())
    bias = jnp.asarray(_band_bias())

    c_all = jnp.concatenate([c_prompt, c_sample], axis=0)
    mod_all = _ada_mod(c_all, w_ada_mix.reshape(D_MODEL, 3 * D_MODEL), w_ada_ffn.reshape(D_MODEL, 3 * D_MODEL),
                       jnp.concatenate([b_ada_mix, b_ada_ffn], axis=1))
    mod_mix, mod_ffn = mod_all[:, :3 * D_MODEL], mod_all[:, 3 * D_MODEL:]
    modp_mix = mod_mix[:batch].reshape(batch, 3, 1, D_MODEL)
    modp_ffn = mod_ffn[:batch].reshape(batch, 3, 1, D_MODEL)
    mods_mix = jnp.swapaxes(mod_mix[batch:].reshape(n_s, 3, D_MODEL), 0, 1)
    mods_ffn = jnp.swapaxes(mod_ffn[batch:].reshape(n_s, 3, D_MODEL), 0, 1)

    y1p, kp, vp, sp = _mixer_prompt(x_prompt, modp_mix, sinks, tab_p, dec, bias, w_in_p, w_out_p,
                                    gn_w, ln1w, ln1b)
    yp = _ffn(y1p, modp_ffn, 1, w_up_b, w_down_b, ln2w, ln2b, FFN_TILE)

    xs = x_sample.reshape(n_s, D_MODEL)
    y1s, ks, vs, ss = _sample_step(
        sinks, xs, mods_mix, tab_s, w_in_p, w_out_p, ret_gn_w.reshape(RET_HEADS, LANES), ln1w, ln1b,
        cache_k.reshape(n_s, WINDOW, LANES), cache_v.reshape(n_s, WINDOW, LANES),
        state_ret.reshape(n_s, RET_HEADS, RET_DK, LANES))
    ys = _ffn(y1s.reshape(1, n_s, D_MODEL), mods_ffn.reshape(1, 3, n_s, D_MODEL), n_s,
              w_up_b, w_down_b, ln2w, ln2b, n_s)

    kv_shape = (1, batch, WINDOW, ATT_KV_HEADS, HEAD_DIM)
    kvs_shape = (1, n_s, WINDOW, ATT_KV_HEADS, HEAD_DIM)
    return (yp, ys.reshape(n_s, 1, D_MODEL),
            kp.reshape(kv_shape), vp.reshape(kv_shape), sp[None],
            ks.reshape(kvs_shape), vs.reshape(kvs_shape), ss[None])
```

```python
import functools
import math

import numpy as np
import jax
import jax.numpy as jnp
from jax.experimental import pallas as pl
from jax.experimental.pallas import tpu as pltpu

D_MODEL = 1024
WINDOW = 128
CHUNK = 128
ATT_HEADS = 8
ATT_KV_HEADS = 2
ATT_GROUP = ATT_HEADS // ATT_KV_HEADS
HEAD_DIM = 64
ROPE_DIMS = HEAD_DIM // 4
ROPE_THETA = 500000.0
RET_HEADS = 4
RET_DK = 128
RET_THETA = 10000.0
ATT_WIDTH = ATT_HEADS * HEAD_DIM
KV_WIDTH = ATT_KV_HEADS * HEAD_DIM
RET_WIDTH = RET_HEADS * RET_DK
IN_WIDTH = ATT_WIDTH + 2 * KV_WIDTH + 4 * RET_WIDTH
D_FF = 2816
DEPTH = 1
DEEPNORM_ALPHA = (2 * DEPTH) ** 0.25
LN_EPS = 1e-5
GN_EPS = 1e-6
PAST_LEN = 16384

OFF_Q = 0
OFF_KV = ATT_WIDTH
OFF_RQ = OFF_KV + 2 * KV_WIDTH
OFF_RK = OFF_RQ + RET_WIDTH
OFF_RV = OFF_RK + RET_WIDTH
OFF_RG = OFF_RV + RET_WIDTH

LANES = 128
VMEM_LIMIT_BYTES = 56 * 1024 * 1024

PROMPT_TILE = 512
FILL_WIDTH = 256
FILLS_UNDER_SOFTMAX = 2
FILLS_UNDER_NORM = 1
FILLS_UNDER_LAYER_NORM = 1
FFN_TILE = 1024
FFN_PARTS = 4
SAMPLE_BLOCK = 8
ADA_BLOCK = 1024

BF16 = jnp.bfloat16
F32 = jnp.float32


def _dot(a, b):
    return jnp.dot(a, b, preferred_element_type=F32)


def _dot_nt(a, b):
    return jax.lax.dot_general(a, b, (((1,), (1,)), ((), ())), preferred_element_type=F32)


def _dot_tn(a, b):
    return jax.lax.dot_general(a, b, (((0,), (0,)), ((), ())), preferred_element_type=F32)


def _sigmoid(x):
    return 1.0 / (1.0 + jnp.exp(-x))


def _silu(x):
    return x * _sigmoid(x)


def _layer_norm(r, w, b):
    mu = jnp.mean(r, axis=-1, keepdims=True)
    d = r - mu
    var = jnp.mean(d * d, axis=-1, keepdims=True)
    return d * jax.lax.rsqrt(var + LN_EPS) * w + b


def _rope_att(slab, c, s1, s2):
    return slab * c + pltpu.roll(slab, LANES - 8, axis=1) * s1 + pltpu.roll(slab, 8, axis=1) * s2


def _rope_ret(slab, c, s):
    return slab * c + pltpu.roll(slab, LANES // 2, axis=1) * s


def _rope_tables(pos):
    pos = np.asarray(pos, np.float64)[:, None]
    half = ROPE_DIMS // 2
    inv = ROPE_THETA ** (-np.arange(half, dtype=np.float64) / half)
    ang = pos * inv[None, :]
    cos, sin = np.cos(ang), np.sin(ang)
    n = pos.shape[0]
    head_c = np.ones((n, HEAD_DIM)); head_s1 = np.zeros((n, HEAD_DIM)); head_s2 = np.zeros((n, HEAD_DIM))
    head_c[:, :half] = cos; head_c[:, half:2 * half] = cos
    head_s1[:, :half] = -sin
    head_s2[:, half:2 * half] = sin
    att = [np.tile(t, (1, LANES // HEAD_DIM)) for t in (head_c, head_s1, head_s2)]
    rhalf = RET_DK // 2
    rinv = RET_THETA ** (-np.arange(rhalf, dtype=np.float64) / rhalf)
    rang = pos * rinv[None, :]
    rc = np.concatenate([np.cos(rang), np.cos(rang)], axis=1)
    rs = np.concatenate([-np.sin(rang), np.sin(rang)], axis=1)
    return np.stack(att + [rc, rs], axis=0).astype(np.float32)


def _log_gamma():
    lin = np.linspace(math.log(1.0 / 32), math.log(1.0 / 512), RET_HEADS)
    return np.log1p(-np.exp(lin))


def _decay_tables():
    lg = _log_gamma()
    idx = np.arange(CHUNK, dtype=np.float64)
    diff = idx[:, None] - idx[None, :]
    dmat = np.where(diff[None] >= 0, np.exp(lg[:, None, None] * np.maximum(diff, 0.0)[None]), 0.0)
    dq = np.exp(lg[:, None] * (idx[None, :] + 1.0))[:, :, None] * np.ones((1, 1, LANES))
    dk = np.exp(lg[:, None] * (CHUNK - 1.0 - idx[None, :]))[:, :, None] * np.ones((1, 1, LANES))
    return np.concatenate([dmat, dq, dk], axis=0).astype(np.float32)


def _band_bias():
    i = np.arange(CHUNK)[:, None]
    j = np.arange(2 * CHUNK)[None, :]
    valid = (j >= i) & (j <= i + WINDOW)
    neg = np.float32(-1e30)
    b0 = np.where(valid, 0.0, neg)
    b1 = np.where(valid & (j >= CHUNK), 0.0, neg)
    return np.stack([b0, b1], axis=0).astype(np.float32)


def _ada_kernel(c_ref, w1_ref, w2_ref, b_ref, o_ref, *, steps_per_weight):
    a = _silu(c_ref[...]).astype(BF16)
    j = pl.program_id(0)

    @pl.when(j < steps_per_weight)
    def _():
        o_ref[...] = _dot(a, w1_ref[...].astype(BF16)) + b_ref[...]

    @pl.when(j >= steps_per_weight)
    def _():
        o_ref[...] = _dot(a, w2_ref[...].astype(BF16)) + b_ref[...]


def _ada_mod(c_all, w_mix, w_ffn, b_all):
    rows = c_all.shape[0]
    n = w_mix.shape[1]
    bn = ADA_BLOCK
    spw = n // bn
    return pl.pallas_call(
        functools.partial(_ada_kernel, steps_per_weight=spw),
        out_shape=jax.ShapeDtypeStruct((rows, 2 * n), F32),
        grid=(2 * spw,),
        in_specs=[
            pl.BlockSpec((rows, D_MODEL), lambda j: (0, 0)),
            pl.BlockSpec((D_MODEL, bn), lambda j: (0, jnp.minimum(j, spw - 1))),
            pl.BlockSpec((D_MODEL, bn), lambda j: (0, jnp.maximum(j - spw, 0))),
            pl.BlockSpec((1, bn), lambda j: (0, j)),
        ],
        out_specs=pl.BlockSpec((rows, bn), lambda j: (0, j)),
        compiler_params=pltpu.CompilerParams(
            dimension_semantics=("arbitrary",), vmem_limit_bytes=VMEM_LIMIT_BYTES),
        name="ada_mod",
    )(c_all, w_mix, w_ffn, b_all)


def _mixer_step(z_prev, z_cur, sinks_ref, x_ref, mod_ref, tab_ref, xp_ref, modp_ref, dec_ref, bias_ref,
                win_ref, wout_ref, gnw_ref, lnw_ref, lnb_ref,
                y_ref, kout_ref, vout_ref, s_ref, kcarry, vcarry, mix_s, *, seq_start, tile, gl):
    n_chunks = tile // CHUNK
    x = x_ref[...]
    m = mod_ref[...]
    h = (x * (1.0 + m[1]) + m[0]).astype(BF16)
    n_fill = IN_WIDTH // FILL_WIDTH
    pending = list(range(n_fill))

    def fill(count, keep=FILLS_UNDER_LAYER_NORM):
        for _ in range(count):
            if len(pending) > keep:
                i = pending.pop(0)
                cols = slice(i * FILL_WIDTH, (i + 1) * FILL_WIDTH)
                z_cur[:, cols] = _dot(h, win_ref[:, cols])

    gate = modp_ref[2]
    lo =jax.lax.broadcasted_iota(jnp.int32, (CHUNK, LANES), 1) < HEAD_DIM
    zero = jnp.zeros((CHUNK, LANES), F32)
    k_prev = [kcarry[g] for g in range(ATT_KV_HEADS)]
    v_prev = [vcarry[g] for g in range(ATT_KV_HEADS)]

    for c in range(n_chunks):
        rows = slice(c * CHUNK, (c + 1) * CHUNK)
        bias = bias_ref[seq_start] if c == 0 else bias_ref[0]
        att_c, att_s1, att_s2 = tab_ref[0, rows, :], tab_ref[1, rows, :], tab_ref[2, rows, :]
        ret_c, ret_s = tab_ref[3, rows, :], tab_ref[4, rows, :]

        def zs(col):
            return z_prev[rows, col:col + LANES]

        q_stack = jnp.concatenate(
            [(_rope_att(zs(OFF_Q + j * LANES), att_c, att_s1, att_s2) * (HEAD_DIM ** -0.5)).astype(BF16)
             for j in range(4)], axis=0)
        k = _rope_att(zs(OFF_KV), att_c, att_s1, att_s2)
        v = zs(OFF_KV + LANES)
        k_cur = [jnp.where(lo, k, zero).astype(BF16), jnp.where(lo, zero, k).astype(BF16)]
        v_cur = [jnp.where(lo, v, zero).astype(BF16), jnp.where(lo, zero, v).astype(BF16)]
        k_band = [jnp.concatenate([k_prev[g], k_cur[g]], axis=0) for g in range(ATT_KV_HEADS)]
        v_band = [jnp.concatenate([v_prev[g], v_cur[g]], axis=0) for g in range(ATT_KV_HEADS)]
        k_prev, v_prev = k_cur, v_cur
        if c == n_chunks - 1:
            kout_ref[...] = k
            vout_ref[...] = v
            for g in range(ATT_KV_HEADS):
                kcarry[g] = k_cur[g]
                vcarry[g] = v_cur[g]

        ret = []
        for hh in range(RET_HEADS):
            rq = _rope_ret(zs(OFF_RQ + hh * LANES), ret_c, ret_s)
            rk = _rope_ret(zs(OFF_RK + hh * LANES), ret_c, ret_s) * (RET_DK ** -0.5)
            ret.append(dict(
                q=rq.astype(BF16), qd=(rq * dec_ref[RET_HEADS + hh]).astype(BF16),
                k=rk.astype(BF16), kd=(rk * dec_ref[2 * RET_HEADS + hh]).astype(BF16),
                v=zs(OFF_RV + hh * LANES).astype(BF16)))

        sc_att = [_dot_nt(q_stack, k_band[g]) for g in range(ATT_KV_HEADS)]
        sc_ret = [_dot_nt(r["q"], r["k"]) for r in ret]
        fill(FILLS_UNDER_SOFTMAX)

        att = None
        for g in range(ATT_KV_HEADS):
            ws = []
            for j in range(4):
                sj = sc_att[g][j * CHUNK:(j + 1) * CHUNK, :] + bias
                sink = sinks_ref[g * ATT_GROUP + j]
                mx = jnp.maximum(jnp.max(sj, axis=-1, keepdims=True), sink)
                p = jnp.exp(sj - mx)
                den = jnp.sum(p, axis=-1, keepdims=True) + jnp.exp(sink - mx)
                ws.append((p * (1.0 / den)).astype(BF16))
            o = _dot(jnp.concatenate(ws, axis=0), v_band[g])
            att = o if att is None else att + o
        for j in range(4):
            mix_s[rows, j * LANES:(j + 1) * LANES] = att[j * CHUNK:(j + 1) * CHUNK, :].astype(BF16)

        outs = []
        for hh, r in enumerate(ret):
            state = s_ref[hh]
            sc = sc_ret[hh] * dec_ref[hh]
            outs.append(_dot(sc.astype(BF16), r["v"]) + _dot(r["qd"], state.astype(BF16)))
            s_ref[hh] = gl[hh] * state + _dot_tn(r["kd"], r["v"])
        if c < n_chunks - 1:
            fill(FILLS_UNDER_NORM)
        for hh in range(RET_HEADS):
            cols = slice(hh * LANES, (hh + 1) * LANES)
            o = outs[hh]
            mu = jnp.mean(o, axis=-1, keepdims=True)
            d = o - mu
            var = jnp.mean(d * d, axis=-1, keepdims=True)
            nrm = d * jax.lax.rsqrt(var + GN_EPS) * gnw_ref[:, cols]
            mix_s[rows, ATT_WIDTH + hh * LANES:ATT_WIDTH + (hh + 1) * LANES] = (
                _silu(zs(OFF_RG + hh * LANES)) * nrm).astype(BF16)

    y = _dot(mix_s[...], wout_ref[...])
    fill(n_fill, keep=0)
    r = DEEPNORM_ALPHA * xp_ref[...] + (1.0 + gate) * y
    y_ref[...] = _layer_norm(r, lnw_ref[...], lnb_ref[...])


def _mixer_kernel(sinks_ref, x_ref, mod_ref, tab_ref, xp_ref, modp_ref, dec_ref, bias_ref,
                  win_ref, wout_ref, gnw_ref, lnw_ref, lnb_ref,
                  y_ref, kout_ref, vout_ref, s_ref,
                  z_a, z_b, kcarry, vcarry, mix_s, *, tile, ns, gl):
    t = pl.program_id(0)
    sa = jax.lax.rem(jnp.maximum(t - 1, 0), ns)
    seq_start = (sa == 0).astype(jnp.int32)
    parity = jax.lax.rem(t, 2)

    @pl.when(t == 0)
    def _():
        z_b[...] = jnp.zeros(z_b.shape, z_b.dtype)
        kcarry[...] = jnp.zeros(kcarry.shape, kcarry.dtype)
        vcarry[...] = jnp.zeros(vcarry.shape, vcarry.dtype)

    @pl.when(sa == 0)
    def _():
        s_ref[...] = jnp.zeros_like(s_ref)

    step = functools.partial(
        _mixer_step, sinks_ref=sinks_ref, x_ref=x_ref, mod_ref=mod_ref, tab_ref=tab_ref, xp_ref=xp_ref,
        modp_ref=modp_ref, dec_ref=dec_ref, bias_ref=bias_ref, win_ref=win_ref, wout_ref=wout_ref,
        gnw_ref=gnw_ref, lnw_ref=lnw_ref, lnb_ref=lnb_ref, y_ref=y_ref, kout_ref=kout_ref,
        vout_ref=vout_ref, s_ref=s_ref, kcarry=kcarry, vcarry=vcarry, mix_s=mix_s,
        seq_start=seq_start, tile=tile, gl=gl)

    @pl.when(parity == 0)
    def _():
        step(z_b, z_a)

    @pl.when(parity == 1)
    def _():
        step(z_a, z_b)


def _mixer_prompt(x, modp, sinks, tab, dec, bias, w_in_p, w_out_p, gn_w, ln_w, ln_b):
    batch, seq, _ = x.shape
    tile = PROMPT_TILE
    ns = seq // tile
    gl = tuple(float(v) for v in np.exp(_log_gamma() * CHUNK).astype(np.float32))
    nt = batch * ns
    const = lambda shape: pl.BlockSpec(shape, lambda t, *_: (0,) * len(shape),
                                       pipeline_mode=pl.Buffered(1))
    cur_b = lambda t: jnp.minimum(t, nt - 1) // ns
    cur_s = lambda t: jnp.minimum(t, nt - 1) % ns
    prev_b = lambda t: jnp.maximum(t - 1, 0) // ns
    prev_s = lambda t: jnp.maximum(t - 1, 0) % ns
    grid_spec = pltpu.PrefetchScalarGridSpec(
        num_scalar_prefetch=1,
        grid=(nt + 1,),
        in_specs=[
            pl.BlockSpec((None, tile, D_MODEL), lambda t, *_: (cur_b(t), cur_s(t), 0)),
            pl.BlockSpec((None, 3, 1, D_MODEL), lambda t, *_: (cur_b(t), 0, 0, 0)),
            pl.BlockSpec((5, tile, LANES), lambda t, *_: (0, prev_s(t), 0)),
            pl.BlockSpec((None, tile, D_MODEL), lambda t, *_: (prev_b(t), prev_s(t), 0)),
            pl.BlockSpec((None, 3, 1, D_MODEL), lambda t, *_: (prev_b(t), 0, 0, 0)),
            const((3 * RET_HEADS, CHUNK, LANES)),
            const((2, CHUNK, 2 * CHUNK)),
            const((D_MODEL, IN_WIDTH)),
            const((D_MODEL, D_MODEL)),
            const((1, RET_WIDTH)),
            const((1, D_MODEL)),
            const((1, D_MODEL)),
        ],
        out_specs=[
            pl.BlockSpec((None, tile, D_MODEL), lambda t, *_: (prev_b(t), prev_s(t), 0)),
            pl.BlockSpec((None, WINDOW, LANES), lambda t, *_: (prev_b(t), 0, 0)),
            pl.BlockSpec((None, WINDOW, LANES), lambda t, *_: (prev_b(t), 0, 0)),
            pl.BlockSpec((None, RET_HEADS, RET_DK, LANES), lambda t, *_: (prev_b(t), 0, 0, 0)),
        ],
        scratch_shapes=[
            pltpu.VMEM((tile, IN_WIDTH), F32),
            pltpu.VMEM((tile, IN_WIDTH), F32),
            pltpu.VMEM((ATT_KV_HEADS, CHUNK, LANES), BF16),
            pltpu.VMEM((ATT_KV_HEADS, CHUNK, LANES), BF16),
            pltpu.VMEM((tile, D_MODEL), BF16),
        ],
    )
    return pl.pallas_call(
        functools.partial(_mixer_kernel, tile=tile, ns=ns, gl=gl),
        out_shape=[
            jax.ShapeDtypeStruct((batch, seq, D_MODEL), F32),
            jax.ShapeDtypeStruct((batch, WINDOW, LANES), F32),
            jax.ShapeDtypeStruct((batch, WINDOW, LANES), F32),
            jax.ShapeDtypeStruct((batch, RET_HEADS, RET_DK, LANES), F32),
        ],
        grid_spec=grid_spec,
        compiler_params=pltpu.CompilerParams(
            dimension_semantics=("arbitrary",), vmem_limit_bytes=VMEM_LIMIT_BYTES),
        name="mixer_prompt",
    )(sinks, x, modp, tab, x, modp, dec, bias, w_in_p, w_out_p, gn_w, ln_w, ln_b)


def _ffn_kernel(y_ref, mod_ref, wup_ref, wdown_ref, lnw_ref, lnb_ref, o_ref, *, parts):
    m = mod_ref[...]
    rows = y_ref.shape[0] // parts
    sl = [slice(p * rows, (p + 1) * rows) for p in range(parts)]
    pick = lambda a, p: a if a.shape[0] == 1 else a[sl[p]]
    ys = [y_ref[sl[p], :] for p in range(parts)]
    hs = [(ys[p] * (1.0 + pick(m[1], p)) + pick(m[0], p)).astype(BF16) for p in range(parts)]
    acts = [(_silu(_dot(h, wup_ref[:, :D_FF])) * _dot(h, wup_ref[:, D_FF:])).astype(BF16) for h in hs]
    fs = [_dot(a, wdown_ref[...]) for a in acts]
    for p in range(parts):
        r = DEEPNORM_ALPHA * ys[p] + (1.0 + pick(m[2], p)) * fs[p]
        o_ref[sl[p], :] = _layer_norm(r, lnw_ref[...], lnb_ref[...])


def _ffn(y, mod, tiles_per_mod, w_up, w_down, ln_w, ln_b, tile, parts):
    rows, _ = y.shape
    nt = rows // tile
    mod_rows = mod.shape[2]
    assert mod_rows in (1, tile) and (mod_rows == 1 or tiles_per_mod == 1)
    const = lambda shape: pl.BlockSpec(shape, lambda t: (0,) * len(shape), pipeline_mode=pl.Buffered(1))
    return pl.pallas_call(
        functools.partial(_ffn_kernel, parts=parts),
        out_shape=jax.ShapeDtypeStruct(y.shape, F32),
        grid=(nt,),
        in_specs=[
            pl.BlockSpec((tile, D_MODEL), lambda t: (t, 0)),
            pl.BlockSpec((None, 3, mod_rows, D_MODEL), lambda t: (t // tiles_per_mod, 0, 0, 0)),
            const((D_MODEL, 2 * D_FF)),
            const((D_FF, D_MODEL)),
            const((1, D_MODEL)),
            const((1, D_MODEL)),
        ],
        out_specs=pl.BlockSpec((tile, D_MODEL), lambda t: (t, 0)),
        compiler_params=pltpu.CompilerParams(
            dimension_semantics=("arbitrary",), vmem_limit_bytes=VMEM_LIMIT_BYTES),
        name="ffn",
    )(y, mod, w_up, w_down, ln_w, ln_b)


def _sample_kernel(sinks_ref, x_ref, mod_ref, tab_ref, win_ref, wout_ref, gnw_ref, lnw_ref, lnb_ref,
                   ck_ref, cv_ref, st_ref,
                   y_ref, ko_ref, vo_ref, so_ref,
                   z_s, mix_s, *, g1, n_blocks):
    i = pl.program_id(0)
    bb = SAMPLE_BLOCK

    @pl.when(i == 0)
    def _():
        m = mod_ref[...]
        h = (x_ref[...] * (1.0 + m[1]) + m[0]).astype(BF16)
        z = _dot(h, win_ref[...])
        att_c, att_s1, att_s2 = tab_ref[0], tab_ref[1], tab_ref[2]
        ret_c, ret_s = tab_ref[3], tab_ref[4]
        for j in range(4):
            col = OFF_Q + j * LANES
            z_s[:, col:col + LANES] = _rope_att(z[:, col:col + LANES], att_c, att_s1, att_s2) * (HEAD_DIM ** -0.5)
            col = OFF_RQ + j * LANES
            z_s[:, col:col + LANES] = _rope_ret(z[:, col:col + LANES], ret_c, ret_s)
            col = OFF_RK + j * LANES
            z_s[:, col:col + LANES] = _rope_ret(z[:, col:col + LANES], ret_c, ret_s) * (RET_DK ** -0.5)
        z_s[:, OFF_KV:OFF_KV + LANES] = _rope_att(z[:, OFF_KV:OFF_KV + LANES], att_c, att_s1, att_s2)
        z_s[:, OFF_KV + LANES:OFF_RQ] = z[:, OFF_KV + LANES:OFF_RQ]
        z_s[:, OFF_RV:OFF_RG] = z[:, OFF_RV:OFF_RG]
        z_s[:, OFF_RG:] = _silu(z[:, OFF_RG:])

    lane_w = jax.lax.broadcasted_iota(jnp.int32, (WINDOW, LANES), 1)
    row_w = jax.lax.broadcasted_iota(jnp.int32, (WINDOW, LANES), 0)
    lo = lane_w < HEAD_DIM
    lo_row = lo[0:1, :]
    last_row = row_w == WINDOW - 1
    r4 = jax.lax.broadcasted_iota(jnp.int32, (ATT_GROUP, 1), 0)
    head_blk = (jax.lax.broadcasted_iota(jnp.int32, (16, RET_WIDTH), 1) // LANES
                == jax.lax.broadcasted_iota(jnp.int32, (16, RET_WIDTH), 0))
    g1_col = jnp.where(r4 == 0, g1[0], jnp.where(r4 == 1, g1[1], jnp.where(r4 == 2, g1[2], g1[3])))

    def sink_col(base):
        s = [sinks_ref[base + j] for j in range(ATT_GROUP)]
        return jnp.where(r4 == 0, s[0], jnp.where(r4 == 1, s[1], jnp.where(r4 == 2, s[2], s[3])))

    sink_lo, sink_hi = sink_col(0), sink_col(ATT_GROUP)
    zero_w = jnp.zeros((WINDOW, LANES), F32)

    def rows16(a):
        return jnp.concatenate([a, jnp.zeros((16 - a.shape[0], a.shape[1]), F32)], axis=0)

    def heads(row, off):
        return jnp.concatenate([row[:, off + j * LANES:off + (j + 1) * LANES] for j in range(4)], axis=0)

    def block_rows(a4):
        wide = jnp.concatenate([rows16(a4)] * RET_HEADS, axis=1)
        return jnp.where(head_blk, wide, jnp.zeros_like(wide))

    r0 = pl.multiple_of(i * bb, bb)
    zblk = z_s[pl.ds(r0, bb), :]
    rows = [zblk[b:b + 1, :] for b in range(bb)]

    q4 = [heads(r, OFF_Q) for r in rows]
    kn = [r[:, OFF_KV:OFF_KV + LANES] for r in rows]
    vn = [r[:, OFF_KV + LANES:OFF_RQ] for r in rows]
    s01, oh = [], []
    for b in range(bb):
        ck = ck_ref[b]
        ko_ref[b] = jnp.where(last_row, kn[b], pltpu.roll(ck, WINDOW - 1, axis=0))
        kcat = jnp.concatenate([jnp.where(lo, ck, zero_w), jnp.where(lo, zero_w, ck)], axis=0).astype(BF16)
        s01.append(_dot_nt(rows16(q4[b]).astype(BF16), kcat)[0:ATT_GROUP])
    for b in range(bb):
        rq = heads(rows[b], OFF_RQ)
        rk = heads(rows[b], OFF_RK)
        rv = heads(rows[b], OFF_RV)
        state = st_ref[b].reshape(RET_WIDTH, LANES)
        o2 = _dot(block_rows(rq * g1_col).astype(BF16), state.astype(BF16))[0:RET_HEADS]
        outer = _dot_tn(block_rows(rk).astype(BF16), rows16(rv).astype(BF16))
        for hh in range(RET_HEADS):
            so_ref[b, hh] = g1[hh] * st_ref[b, hh] + outer[hh * RET_DK:(hh + 1) * RET_DK, :]
        o = jnp.sum(rq * rk, axis=-1, keepdims=True) * rv + o2
        mu = jnp.mean(o, axis=-1, keepdims=True)
        d = o - mu
        var = jnp.mean(d * d, axis=-1, keepdims=True)
        oh.append(heads(rows[b], OFF_RG) * (d * jax.lax.rsqrt(var + GN_EPS) * gnw_ref[...]))

    w01, wn = [], []
    for b in range(bb):
        qk = q4[b] * kn[b]
        zero4 = jnp.zeros_like(qk)
        sn_lo = jnp.sum(jnp.where(lo_row, qk, zero4), axis=-1, keepdims=True)
        sn_hi = jnp.sum(jnp.where(lo_row, zero4, qk), axis=-1, keepdims=True)
        ws, wb = [], []
        for s_half, sn, sink in ((s01[b][:, :WINDOW], sn_lo, sink_lo), (s01[b][:, WINDOW:], sn_hi, sink_hi)):
            mx = jnp.maximum(jnp.maximum(jnp.max(s_half, axis=-1, keepdims=True), sn), sink)
            p = jnp.exp(s_half - mx)
            pn = jnp.exp(sn - mx)
            den = jnp.sum(p, axis=-1, keepdims=True) + pn + jnp.exp(sink - mx)
            inv = 1.0 / den
            ws.append(rows16(p * inv).astype(BF16))
            wb.append(pn * inv)
        w01.append(jnp.concatenate(ws, axis=1))
        wn.append(wb)

    att = []
    for b in range(bb):
        cv = cv_ref[b]
        vo_ref[b] = jnp.where(last_row, vn[b], pltpu.roll(cv, WINDOW - 1, axis=0))
        vcat = jnp.concatenate([jnp.where(lo, cv, zero_w), jnp.where(lo, zero_w, cv)], axis=0).astype(BF16)
        vn_lo = jnp.where(lo_row, vn[b], jnp.zeros_like(vn[b]))
        vn_hi = vn[b] - vn_lo
        att.append(_dot(w01[b], vcat)[0:ATT_GROUP] + wn[b][0] * vn_lo + wn[b][1] * vn_hi)
    blk_rows = pl.ds(r0, bb)
    for j in range(4):
        mix_s[blk_rows, j * LANES:(j + 1) * LANES] = jnp.concatenate(
            [a[j:j + 1, :] for a in att], axis=0)
        mix_s[blk_rows, ATT_WIDTH + j * LANES:ATT_WIDTH + (j + 1) * LANES] = jnp.concatenate(
            [o[j:j + 1, :] for o in oh], axis=0)

    @pl.when(i == n_blocks - 1)
    def _():
        y = _dot(mix_s[...].astype(BF16), wout_ref[...])
        r = DEEPNORM_ALPHA * x_ref[...] + (1.0 + mod_ref[2]) * y
        y_ref[...] = _layer_norm(r, lnw_ref[...], lnb_ref[...])


def _sample_step(sinks, x, mod, tab, w_in_p, w_out_p, gn_w4, ln_w, ln_b, cache_k, cache_v, state):
    n = x.shape[0]
    bb = SAMPLE_BLOCK
    n_blocks = n // bb
    g1 = tuple(float(v) for v in np.exp(_log_gamma()).astype(np.float32))
    full = lambda *shape: pl.BlockSpec(shape, lambda i, *_: (0,) * len(shape))
    blk = lambda *shape: pl.BlockSpec((bb,) + shape, lambda i, *_: (i,) + (0,) * len(shape))
    grid_spec = pltpu.PrefetchScalarGridSpec(
        num_scalar_prefetch=1,
        grid=(n_blocks,),
        in_specs=[
            full(n, D_MODEL), full(3, n, D_MODEL), full(5, 1, LANES),
            full(D_MODEL, IN_WIDTH), full(D_MODEL, D_MODEL),
            full(RET_HEADS, LANES), full(1, D_MODEL), full(1, D_MODEL),
            blk(WINDOW, LANES), blk(WINDOW, LANES), blk(RET_HEADS, RET_DK, LANES),
        ],
        out_specs=[full(n, D_MODEL), blk(WINDOW, LANES), blk(WINDOW, LANES),
                   blk(RET_HEADS, RET_DK, LANES)],
        scratch_shapes=[pltpu.VMEM((n, IN_WIDTH), F32), pltpu.VMEM((n, D_MODEL), F32)],
    )
    return pl.pallas_call(
        functools.partial(_sample_kernel, g1=g1, n_blocks=n_blocks),
        out_shape=[
            jax.ShapeDtypeStruct((n, D_MODEL), F32),
            jax.ShapeDtypeStruct((n, WINDOW, LANES), F32),
            jax.ShapeDtypeStruct((n, WINDOW, LANES), F32),
            jax.ShapeDtypeStruct((n, RET_HEADS, RET_DK, LANES), F32),
        ],
        grid_spec=grid_spec,
        compiler_params=pltpu.CompilerParams(
            dimension_semantics=("arbitrary",), vmem_limit_bytes=VMEM_LIMIT_BYTES),
        name="sample_step",
    )(sinks, x, mod, tab, w_in_p, w_out_p, gn_w4, ln_w, ln_b, cache_k, cache_v, state)


def _slab_order(w, axis):
    shape = w.shape
    pre, post = shape[:axis], shape[axis + 1:]
    w = w.reshape(pre + (ATT_KV_HEADS, ATT_GROUP, HEAD_DIM) + post)
    w = jnp.swapaxes(w, axis, axis + 1)
    return w.reshape(shape)


def kernel(x_prompt, x_sample, c_prompt, c_sample, cache_k, cache_v, state_ret, w_ada_mix, b_ada_mix, w_in, att_sinks, ret_gn_w, w_out, ln1_w, ln1_b, w_ada_ffn, b_ada_ffn, w_up, w_down, ln2_w, ln2_b):
    batch, seq, _ = x_prompt.shape
    n_s = x_sample.shape[0]
    l = 0
    assert w_in.shape[0] == DEPTH == 1

    w_in_b = w_in.reshape(D_MODEL, IN_WIDTH).astype(BF16)
    w_out_b = w_out.reshape(D_MODEL, D_MODEL).astype(BF16)
    w_in_p = jnp.concatenate(
        [_slab_order(w_in_b[:, :ATT_WIDTH], 1), w_in_b[:, ATT_WIDTH:]], axis=1)
    w_out_p = jnp.concatenate(
        [_slab_order(w_out_b[:ATT_WIDTH], 0), w_out_b[ATT_WIDTH:]], axis=0)
    w_up_b = w_up.reshape(D_MODEL, 2 * D_FF).astype(BF16)
    w_down_b = w_down.reshape(D_FF, D_MODEL).astype(BF16)
    sinks = att_sinks[l]
    gn_w = ret_gn_w[l].reshape(1, RET_WIDTH)
    ln1w, ln1b = ln1_w[l].reshape(1, D_MODEL), ln1_b[l].reshape(1, D_MODEL)
    ln2w, ln2b = ln2_w[l].reshape(1, D_MODEL), ln2_b[l].reshape(1, D_MODEL)

    tab_p = jnp.asarray(_rope_tables(np.arange(seq)))
    tab_s = jnp.asarray(_rope_tables(np.array([PAST_LEN])))
    dec = jnp.asarray(_decay_tables())
    bias = jnp.asarray(_band_bias())

    c_all = jnp.concatenate([c_prompt, c_sample], axis=0)
    mod_all = _ada_mod(c_all, w_ada_mix.reshape(D_MODEL, 3 * D_MODEL), w_ada_ffn.reshape(D_MODEL, 3 * D_MODEL),
                       jnp.concatenate([b_ada_mix, b_ada_ffn], axis=1))
    mod_mix, mod_ffn = mod_all[:, :3 * D_MODEL], mod_all[:, 3 * D_MODEL:]
    modp_mix = mod_mix[:batch].reshape(batch, 3, 1, D_MODEL)
    modp_ffn = mod_ffn[:batch].reshape(batch, 3, 1, D_MODEL)
    mods_mix = jnp.swapaxes(mod_mix[batch:].reshape(n_s, 3, D_MODEL), 0, 1)
    mods_ffn = jnp.swapaxes(mod_ffn[batch:].reshape(n_s, 3, D_MODEL), 0, 1)

    y1p, kp, vp, sp = _mixer_prompt(x_prompt, modp_mix, sinks, tab_p, dec, bias, w_in_p, w_out_p,
                                    gn_w, ln1w, ln1b)
    yp = _ffn(y1p.reshape(batch * seq, D_MODEL), modp_ffn, seq // FFN_TILE, w_up_b, w_down_b, ln2w, ln2b,
              FFN_TILE, FFN_PARTS).reshape(batch, seq, D_MODEL)

    xs = x_sample.reshape(n_s, D_MODEL)
    y1s, ks, vs, ss = _sample_step(
        sinks, xs, mods_mix, tab_s, w_in_p, w_out_p, ret_gn_w.reshape(RET_HEADS, LANES), ln1w, ln1b,
        cache_k.reshape(n_s, WINDOW, LANES), cache_v.reshape(n_s, WINDOW, LANES),
        state_ret.reshape(n_s, RET_HEADS, RET_DK, LANES))
    ys = _ffn(y1s, mods_ffn.reshape(1, 3, n_s, D_MODEL), 1, w_up_b, w_down_b, ln2w, ln2b, n_s, 1)

    kv_shape = (1, batch, WINDOW, ATT_KV_HEADS, HEAD_DIM)
    kvs_shape = (1, n_s, WINDOW, ATT_KV_HEADS, HEAD_DIM)
    return (yp, ys.reshape(n_s, 1, D_MODEL),
            kp.reshape(kv_shape), vp.reshape(kv_shape), sp[None],
            ks.reshape(kvs_shape), vs.reshape(kvs_shape), ss[None])
```

```python
import functools
import math

import numpy as np
import jax
import jax.numpy as jnp
from jax.experimental import pallas as pl
from jax.experimental.pallas import tpu as pltpu

D_MODEL = 1024
WINDOW = 128
CHUNK = 128
ATT_HEADS = 8
ATT_KV_HEADS = 2
ATT_GROUP = ATT_HEADS // ATT_KV_HEADS
HEAD_DIM = 64
ROPE_DIMS = HEAD_DIM // 4
ROPE_THETA = 500000.0
RET_HEADS = 4
RET_DK = 128
RET_THETA = 10000.0
ATT_WIDTH = ATT_HEADS * HEAD_DIM
KV_WIDTH = ATT_KV_HEADS * HEAD_DIM
RET_WIDTH = RET_HEADS * RET_DK
IN_WIDTH = ATT_WIDTH + 2 * KV_WIDTH + 4 * RET_WIDTH
D_FF = 2816
DEPTH = 1
DEEPNORM_ALPHA = (2 * DEPTH) ** 0.25
LN_EPS = 1e-5
GN_EPS = 1e-6
PAST_LEN = 16384

OFF_Q = 0
OFF_KV = ATT_WIDTH
OFF_RQ = OFF_KV + 2 * KV_WIDTH
OFF_RK = OFF_RQ + RET_WIDTH
OFF_RV = OFF_RK + RET_WIDTH
OFF_RG = OFF_RV + RET_WIDTH

LANES = 128
VMEM_LIMIT_BYTES = 56 * 1024 * 1024

PROMPT_TILE = 512
FILL_WIDTH = 256
FILL_PLAN = ((2, 1), (2, 1), (2, 1), (2, 0))
FFN_TILE = 1024
FFN_PARTS = 4
SAMPLE_BLOCK = 8

BF16 = jnp.bfloat16
F32 = jnp.float32


def _dot(a, b):
    return jnp.dot(a, b, preferred_element_type=F32)


def _dot_nt(a, b):
    return jax.lax.dot_general(a, b, (((1,), (1,)), ((), ())), preferred_element_type=F32)


def _dot_tn(a, b):
    return jax.lax.dot_general(a, b, (((0,), (0,)), ((), ())), preferred_element_type=F32)


def _sigmoid(x):
    return 1.0 / (1.0 + jnp.exp(-x))


def _silu(x):
    return x * _sigmoid(x)


def _layer_norm(r, w, b):
    mu = jnp.mean(r, axis=-1, keepdims=True)
    d = r - mu
    var = jnp.mean(d * d, axis=-1, keepdims=True)
    return d * jax.lax.rsqrt(var + LN_EPS) * w + b


def _rope_att(slab, c, s1, s2):
    return slab * c + pltpu.roll(slab, LANES - 8, axis=1) * s1 + pltpu.roll(slab, 8, axis=1) * s2


def _rope_ret(slab, c, s):
    return slab * c + pltpu.roll(slab, LANES // 2, axis=1) * s


def _rope_tables(pos):
    pos = np.asarray(pos, np.float64)[:, None]
    half = ROPE_DIMS // 2
    inv = ROPE_THETA ** (-np.arange(half, dtype=np.float64) / half)
    ang = pos * inv[None, :]
    cos, sin = np.cos(ang), np.sin(ang)
    n = pos.shape[0]
    head_c = np.ones((n, HEAD_DIM)); head_s1 = np.zeros((n, HEAD_DIM)); head_s2 = np.zeros((n, HEAD_DIM))
    head_c[:, :half] = cos; head_c[:, half:2 * half] = cos
    head_s1[:, :half] = -sin
    head_s2[:, half:2 * half] = sin
    att = [np.tile(t, (1, LANES // HEAD_DIM)) for t in (head_c, head_s1, head_s2)]
    rhalf = RET_DK // 2
    rinv = RET_THETA ** (-np.arange(rhalf, dtype=np.float64) / rhalf)
    rang = pos * rinv[None, :]
    rc = np.concatenate([np.cos(rang), np.cos(rang)], axis=1)
    rs = np.concatenate([-np.sin(rang), np.sin(rang)], axis=1)
    return np.stack(att + [rc, rs], axis=0).astype(np.float32)


def _log_gamma():
    lin = np.linspace(math.log(1.0 / 32), math.log(1.0 / 512), RET_HEADS)
    return np.log1p(-np.exp(lin))


def _decay_tables():
    lg = _log_gamma()
    idx = np.arange(CHUNK, dtype=np.float64)
    diff = idx[:, None] - idx[None, :]
    dmat = np.where(diff[None] >= 0, np.exp(lg[:, None, None] * np.maximum(diff, 0.0)[None]), 0.0)
    dq = np.exp(lg[:, None] * (idx[None, :] + 1.0))[:, :, None] * np.ones((1, 1, LANES))
    dk = np.exp(lg[:, None] * (CHUNK - 1.0 - idx[None, :]))[:, :, None] * np.ones((1, 1, LANES))
    return np.concatenate([dmat, dq, dk], axis=0).astype(np.float32)


def _band_bias():
    i = np.arange(CHUNK)[:, None]
    j = np.arange(2 * CHUNK)[None, :]
    valid = (j >= i) & (j <= i + WINDOW)
    neg = np.float32(-1e30)
    b0 = np.where(valid, 0.0, neg)
    b1 = np.where(valid & (j >= CHUNK), 0.0, neg)
    return np.stack([b0, b1], axis=0).astype(np.float32)


def _ada_kernel(cs_ref, cp_ref, w1_ref, w2_ref, b1_ref, b2_ref, o_ref):
    n_s = cs_ref.shape[0]
    a_s = _silu(cs_ref[...]).astype(BF16)
    a_p = _silu(cp_ref[...]).astype(BF16)

    def emit(w_ref, b_ref):
        w = w_ref[...].astype(BF16)
        o_ref[0:n_s, :] = _dot(a_s, w) + b_ref[...]
        o_ref[n_s:, :] = _dot(a_p, w) + b_ref[...]

    @pl.when(pl.program_id(0) < 3)
    def _():
        emit(w1_ref, b1_ref)

    @pl.when(pl.program_id(0) >= 3)
    def _():
        emit(w2_ref, b2_ref)


def _ada_mod(c_sample, c_prompt, w_mix, w_ffn, b_mix, b_ffn):
    rows = c_sample.shape[0] + c_prompt.shape[0]
    d = D_MODEL
    return pl.pallas_call(
        _ada_kernel,
        out_shape=jax.ShapeDtypeStruct((6, rows, d), F32),
        grid=(6,),
        in_specs=[
            pl.BlockSpec(c_sample.shape, lambda j: (0, 0)),
            pl.BlockSpec(c_prompt.shape, lambda j: (0, 0)),
            pl.BlockSpec((d, d), lambda j: (0, jnp.minimum(j, 2))),
            pl.BlockSpec((d, d), lambda j: (0, jnp.maximum(j - 3, 0))),
            pl.BlockSpec((1, d), lambda j: (0, jnp.minimum(j, 2))),
            pl.BlockSpec((1, d), lambda j: (0, jnp.maximum(j - 3, 0))),
        ],
        out_specs=pl.BlockSpec((None, rows, d), lambda j: (j, 0, 0)),
        compiler_params=pltpu.CompilerParams(
            dimension_semantics=("arbitrary",), vmem_limit_bytes=VMEM_LIMIT_BYTES),
        name="ada_mod",
    )(c_sample, c_prompt, w_mix, w_ffn, b_mix, b_ffn)


def _mixer_step(z_prev, z_cur, sinks_ref, x_ref, mods_ref, tab_ref, xp_ref, dec_ref, bias_ref,
                wq_ref, wrest_ref, wout_att_ref, wout_ret_ref, gnw_ref, lnw_ref, lnb_ref,
                y_ref, kout_ref, vout_ref, s_ref, kcarry, vcarry, mix_s,
                *, cur_b, prev_b, seq_start, tile, gl):
    n_chunks = tile // CHUNK
    x = x_ref[...]
    h = (x * (1.0 + mods_ref[1, pl.ds(cur_b, 1), :]) + mods_ref[0, pl.ds(cur_b, 1), :]).astype(BF16)
    n_fill = IN_WIDTH // FILL_WIDTH
    pending = list(range(n_fill))

    def fill(count):
        for _ in range(count):
            if pending:
                i = pending.pop(0)
                col = i * FILL_WIDTH
                w = (wq_ref[:, col:col + FILL_WIDTH] if col < ATT_WIDTH
                     else wrest_ref[:, col - ATT_WIDTH:col - ATT_WIDTH + FILL_WIDTH])
                z_cur[:, col:col + FILL_WIDTH] = _dot(h, w)

    gate = mods_ref[2, pl.ds(prev_b, 1), :]
    lo =jax.lax.broadcasted_iota(jnp.int32, (CHUNK, LANES), 1) < HEAD_DIM
    zero = jnp.zeros((CHUNK, LANES), F32)
    k_prev = [kcarry[g] for g in range(ATT_KV_HEADS)]
    v_prev = [vcarry[g] for g in range(ATT_KV_HEADS)]

    for c in range(n_chunks):
        rows = slice(c * CHUNK, (c + 1) * CHUNK)
        bias = bias_ref[seq_start] if c == 0 else bias_ref[0]
        att_c, att_s1, att_s2 = tab_ref[0, rows, :], tab_ref[1, rows, :], tab_ref[2, rows, :]
        ret_c, ret_s = tab_ref[3, rows, :], tab_ref[4, rows, :]

        def zs(col):
            return z_prev[rows, col:col + LANES]

        q_stack = jnp.concatenate(
            [(_rope_att(zs(OFF_Q + j * LANES), att_c, att_s1, att_s2) * (HEAD_DIM ** -0.5)).astype(BF16)
             for j in range(4)], axis=0)
        k = _rope_att(zs(OFF_KV), att_c, att_s1, att_s2)
        v = zs(OFF_KV + LANES)
        k_cur = [jnp.where(lo, k, zero).astype(BF16), jnp.where(lo, zero, k).astype(BF16)]
        v_cur = [jnp.where(lo, v, zero).astype(BF16), jnp.where(lo, zero, v).astype(BF16)]
        k_band = [jnp.concatenate([k_prev[g], k_cur[g]], axis=0) for g in range(ATT_KV_HEADS)]
        v_band = [jnp.concatenate([v_prev[g], v_cur[g]], axis=0) for g in range(ATT_KV_HEADS)]
        k_prev, v_prev = k_cur, v_cur
        if c == n_chunks - 1:
            kout_ref[...] = k
            vout_ref[...] = v
            for g in range(ATT_KV_HEADS):
                kcarry[g] = k_cur[g]
                vcarry[g] = v_cur[g]

        ret = []
        for hh in range(RET_HEADS):
            rq = _rope_ret(zs(OFF_RQ + hh * LANES), ret_c, ret_s)
            rk = _rope_ret(zs(OFF_RK + hh * LANES), ret_c, ret_s) * (RET_DK ** -0.5)
            ret.append(dict(
                q=rq.astype(BF16), qd=(rq * dec_ref[RET_HEADS + hh]).astype(BF16),
                k=rk.astype(BF16), kd=(rk * dec_ref[2 * RET_HEADS + hh]).astype(BF16),
                v=zs(OFF_RV + hh * LANES).astype(BF16)))

        sc_att = [_dot_nt(q_stack, k_band[g]) for g in range(ATT_KV_HEADS)]
        sc_ret = [_dot_nt(r["q"], r["k"]) for r in ret]
        fill(FILL_PLAN[c][0])

        att = None
        for g in range(ATT_KV_HEADS):
            ws = []
            for j in range(4):
                sj = sc_att[g][j * CHUNK:(j + 1) * CHUNK, :] + bias
                sink = sinks_ref[g * ATT_GROUP + j]
                mx = jnp.maximum(jnp.max(sj, axis=-1, keepdims=True), sink)
                p = jnp.exp(sj - mx)
                den = jnp.sum(p, axis=-1, keepdims=True) + jnp.exp(sink - mx)
                ws.append((p * (1.0 / den)).astype(BF16))
            o = _dot(jnp.concatenate(ws, axis=0), v_band[g])
            att = o if att is None else att + o
        for j in range(4):
            mix_s[rows, j * LANES:(j + 1) * LANES] = att[j * CHUNK:(j + 1) * CHUNK, :].astype(BF16)

        outs = []
        for hh, r in enumerate(ret):
            state = s_ref[hh]
            sc = sc_ret[hh] * dec_ref[hh]
            outs.append(_dot(sc.astype(BF16), r["v"]) + _dot(r["qd"], state.astype(BF16)))
            s_ref[hh] = gl[hh] * state + _dot_tn(r["kd"], r["v"])
        fill(FILL_PLAN[c][1])
        for hh in range(RET_HEADS):
            cols = slice(hh * LANES, (hh + 1) * LANES)
            o = outs[hh]
            mu = jnp.mean(o, axis=-1, keepdims=True)
            d = o - mu
            var = jnp.mean(d * d, axis=-1, keepdims=True)
            nrm = d * jax.lax.rsqrt(var + GN_EPS) * gnw_ref[:, cols]
            mix_s[rows, ATT_WIDTH + hh * LANES:ATT_WIDTH + (hh + 1) * LANES] = (
                _silu(zs(OFF_RG + hh * LANES)) * nrm).astype(BF16)

    y = _dot(mix_s[:, :ATT_WIDTH], wout_att_ref[...]) + _dot(mix_s[:, ATT_WIDTH:], wout_ret_ref[...])
    fill(n_fill)
    r = DEEPNORM_ALPHA * xp_ref[...] + (1.0 + gate) * y
    y_ref[...] = _layer_norm(r, lnw_ref[...], lnb_ref[...])


def _mixer_kernel(sinks_ref, x_ref, mods_ref, tab_ref, xp_ref, dec_ref, bias_ref,
                  wq_ref, wrest_ref, wout_att_ref, wout_ret_ref, gnw_ref, lnw_ref, lnb_ref,
                  y_ref, kout_ref, vout_ref, s_ref,
                  z_a, z_b, kcarry, vcarry, mix_s, *, tile, ns, nt, gl):
    t = pl.program_id(0)
    prev_tile = jnp.maximum(t - 1, 0)
    sa = jax.lax.rem(prev_tile, ns)
    seq_start = (sa == 0).astype(jnp.int32)
    cur_b = jax.lax.div(jnp.minimum(t, nt - 1), ns)
    prev_b = jax.lax.div(prev_tile, ns)
    parity = jax.lax.rem(t, 2)

    @pl.when(t == 0)
    def _():
        z_b[...] = jnp.zeros(z_b.shape, z_b.dtype)
        kcarry[...] = jnp.zeros(kcarry.shape, kcarry.dtype)
        vcarry[...] = jnp.zeros(vcarry.shape, vcarry.dtype)

    @pl.when(sa == 0)
    def _():
        s_ref[...] = jnp.zeros_like(s_ref)

    step = functools.partial(
        _mixer_step, sinks_ref=sinks_ref, x_ref=x_ref, mods_ref=mods_ref, tab_ref=tab_ref, xp_ref=xp_ref,
        dec_ref=dec_ref, bias_ref=bias_ref, wq_ref=wq_ref, wrest_ref=wrest_ref,
        wout_att_ref=wout_att_ref, wout_ret_ref=wout_ret_ref,
        gnw_ref=gnw_ref, lnw_ref=lnw_ref, lnb_ref=lnb_ref, y_ref=y_ref, kout_ref=kout_ref,
        vout_ref=vout_ref, s_ref=s_ref, kcarry=kcarry, vcarry=vcarry, mix_s=mix_s,
        cur_b=cur_b, prev_b=prev_b, seq_start=seq_start, tile=tile, gl=gl)

    @pl.when(parity == 0)
    def _():
        step(z_b, z_a)

    @pl.when(parity == 1)
    def _():
        step(z_a, z_b)


def _mixer_prompt(x, mods, mod_row0, sinks, tab, dec, bias, w_q, w_rest, w_out_att, w_out_ret, gn_w, ln_w, ln_b):
    batch, seq, _ = x.shape
    assert mod_row0 % batch == 0 and batch % 8 == 0
    tile = PROMPT_TILE
    ns = seq // tile
    gl = tuple(float(v) for v in np.exp(_log_gamma() * CHUNK).astype(np.float32))
    nt = batch * ns
    const = lambda shape: pl.BlockSpec(shape, lambda t, *_: (0,) * len(shape),
                                       pipeline_mode=pl.Buffered(1))
    cur_b = lambda t: jnp.minimum(t, nt - 1) // ns
    cur_s = lambda t: jnp.minimum(t, nt - 1) % ns
    prev_b = lambda t: jnp.maximum(t - 1, 0) // ns
    prev_s = lambda t: jnp.maximum(t - 1, 0) % ns
    grid_spec = pltpu.PrefetchScalarGridSpec(
        num_scalar_prefetch=1,
        grid=(nt + 1,),
        in_specs=[
            pl.BlockSpec((None, tile, D_MODEL), lambda t, *_: (cur_b(t), cur_s(t), 0)),
            pl.BlockSpec((3, batch, D_MODEL), lambda t, *_: (0, mod_row0 // batch, 0),
                         pipeline_mode=pl.Buffered(1)),
            pl.BlockSpec((5, tile, LANES), lambda t, *_: (0, prev_s(t), 0)),
            pl.BlockSpec((None, tile, D_MODEL), lambda t, *_: (prev_b(t), prev_s(t), 0)),
            const((3 * RET_HEADS, CHUNK, LANES)),
            const((2, CHUNK, 2 * CHUNK)),
            const((D_MODEL, ATT_WIDTH)),
            const((D_MODEL, IN_WIDTH - ATT_WIDTH)),
            const((ATT_WIDTH, D_MODEL)),
            const((RET_WIDTH, D_MODEL)),
            const((1, RET_WIDTH)),
            const((1, D_MODEL)),
            const((1, D_MODEL)),
        ],
        out_specs=[
            pl.BlockSpec((None, tile, D_MODEL), lambda t, *_: (prev_b(t), prev_s(t), 0)),
            pl.BlockSpec((None, WINDOW, LANES), lambda t, *_: (prev_b(t), 0, 0)),
            pl.BlockSpec((None, WINDOW, LANES), lambda t, *_: (prev_b(t), 0, 0)),
            pl.BlockSpec((None, RET_HEADS, RET_DK, LANES), lambda t, *_: (prev_b(t), 0, 0, 0)),
        ],
        scratch_shapes=[
            pltpu.VMEM((tile, IN_WIDTH), F32),
            pltpu.VMEM((tile, IN_WIDTH), F32),
            pltpu.VMEM((ATT_KV_HEADS, CHUNK, LANES), BF16),
            pltpu.VMEM((ATT_KV_HEADS, CHUNK, LANES), BF16),
            pltpu.VMEM((tile, D_MODEL), BF16),
        ],
    )
    return pl.pallas_call(
        functools.partial(_mixer_kernel, tile=tile, ns=ns, nt=nt, gl=gl),
        out_shape=[
            jax.ShapeDtypeStruct((batch, seq, D_MODEL), F32),
            jax.ShapeDtypeStruct((batch, WINDOW, LANES), F32),
            jax.ShapeDtypeStruct((batch, WINDOW, LANES), F32),
            jax.ShapeDtypeStruct((batch, RET_HEADS, RET_DK, LANES), F32),
        ],
        grid_spec=grid_spec,
        compiler_params=pltpu.CompilerParams(
            dimension_semantics=("arbitrary",), vmem_limit_bytes=VMEM_LIMIT_BYTES),
        name="mixer_prompt",
    )(sinks, x, mods, tab, x, dec, bias, w_q, w_rest, w_out_att, w_out_ret, gn_w, ln_w, ln_b)


def _ffn_kernel(y_ref, mods_ref, wup_ref, wdown_ref, lnw_ref, lnb_ref, o_ref, *, parts, tiles_per_mod_row):
    rows = y_ref.shape[0] // parts
    sl = [slice(p * rows, (p + 1) * rows) for p in range(parts)]
    if tiles_per_mod_row:
        seq = jax.lax.div(pl.program_id(0), tiles_per_mod_row)
        mod = lambda k, p: mods_ref[k, pl.ds(seq, 1), :]
    else:
        mod = lambda k, p: mods_ref[k, sl[p], :]
    ys = [y_ref[sl[p], :] for p in range(parts)]
    hs = [(ys[p] * (1.0 + mod(1, p)) + mod(0, p)).astype(BF16) for p in range(parts)]
    acts = [(_silu(_dot(h, wup_ref[:, :D_FF])) * _dot(h, wup_ref[:, D_FF:])).astype(BF16) for h in hs]
    fs = [_dot(a, wdown_ref[...]) for a in acts]
    for p in range(parts):
        r = DEEPNORM_ALPHA * ys[p] + (1.0 + mod(2, p)) * fs[p]
        o_ref[sl[p], :] = _layer_norm(r, lnw_ref[...], lnb_ref[...])


def _ffn(y, mods, mod_row0, mod_rows, tiles_per_mod_row, w_up, w_down, ln_w, ln_b, tile, parts):
    rows, _ = y.shape
    nt = rows // tile
    assert mod_row0 % mod_rows == 0 and (tiles_per_mod_row > 0 or mod_rows == tile)
    const = lambda shape: pl.BlockSpec(shape, lambda t: (0,) * len(shape), pipeline_mode=pl.Buffered(1))
    return pl.pallas_call(
        functools.partial(_ffn_kernel, parts=parts, tiles_per_mod_row=tiles_per_mod_row),
        out_shape=jax.ShapeDtypeStruct(y.shape, F32),
        grid=(nt,),
        in_specs=[
            pl.BlockSpec((tile, D_MODEL), lambda t: (t, 0)),
            pl.BlockSpec((3, mod_rows, D_MODEL), lambda t: (1, mod_row0 // mod_rows, 0),
                         pipeline_mode=pl.Buffered(1)),
            const((D_MODEL, 2 * D_FF)),
            const((D_FF, D_MODEL)),
            const((1, D_MODEL)),
            const((1, D_MODEL)),
        ],
        out_specs=pl.BlockSpec((tile, D_MODEL), lambda t: (t, 0)),
        compiler_params=pltpu.CompilerParams(
            dimension_semantics=("arbitrary",), vmem_limit_bytes=VMEM_LIMIT_BYTES),
        name="ffn",
    )(y, mods, w_up, w_down, ln_w, ln_b)


def _sample_kernel(sinks_ref, x_ref, mod_ref, tab_ref, wq_ref, wrest_ref, wout_att_ref, wout_ret_ref,
                   gnw_ref, lnw_ref, lnb_ref,
                   ck_ref, cv_ref, st_ref,
                   y_ref, ko_ref, vo_ref, so_ref,
                   z_s, mix_s, *, g1, n_blocks):
    i = pl.program_id(0)
    bb = SAMPLE_BLOCK

    @pl.when(i == 0)
    def _():
        m = mod_ref[...]
        h = (x_ref[...] * (1.0 + m[1]) + m[0]).astype(BF16)
        z = jnp.concatenate([_dot(h, wq_ref[...]), _dot(h, wrest_ref[...])], axis=1)
        att_c, att_s1, att_s2 = tab_ref[0], tab_ref[1], tab_ref[2]
        ret_c, ret_s = tab_ref[3], tab_ref[4]
        for j in range(4):
            col = OFF_Q + j * LANES
            z_s[:, col:col + LANES] = _rope_att(z[:, col:col + LANES], att_c, att_s1, att_s2) * (HEAD_DIM ** -0.5)
            col = OFF_RQ + j * LANES
            z_s[:, col:col + LANES] = _rope_ret(z[:, col:col + LANES], ret_c, ret_s)
            col = OFF_RK + j * LANES
            z_s[:, col:col + LANES] = _rope_ret(z[:, col:col + LANES], ret_c, ret_s) * (RET_DK ** -0.5)
        z_s[:, OFF_KV:OFF_KV + LANES] = _rope_att(z[:, OFF_KV:OFF_KV + LANES], att_c, att_s1, att_s2)
        z_s[:, OFF_KV + LANES:OFF_RQ] = z[:, OFF_KV + LANES:OFF_RQ]
        z_s[:, OFF_RV:OFF_RG] = z[:, OFF_RV:OFF_RG]
        z_s[:, OFF_RG:] = _silu(z[:, OFF_RG:])

    lane_w = jax.lax.broadcasted_iota(jnp.int32, (WINDOW, LANES), 1)
    row_w = jax.lax.broadcasted_iota(jnp.int32, (WINDOW, LANES), 0)
    lo = lane_w < HEAD_DIM
    lo_row = lo[0:1, :]
    last_row = row_w == WINDOW - 1
    r4 = jax.lax.broadcasted_iota(jnp.int32, (ATT_GROUP, 1), 0)
    head_blk = (jax.lax.broadcasted_iota(jnp.int32, (16, RET_WIDTH), 1) // LANES
                == jax.lax.broadcasted_iota(jnp.int32, (16, RET_WIDTH), 0))
    g1_col = jnp.where(r4 == 0, g1[0], jnp.where(r4 == 1, g1[1], jnp.where(r4 == 2, g1[2], g1[3])))

    def sink_col(base):
        s = [sinks_ref[base + j] for j in range(ATT_GROUP)]
        return jnp.where(r4 == 0, s[0], jnp.where(r4 == 1, s[1], jnp.where(r4 == 2, s[2], s[3])))

    sink_lo, sink_hi = sink_col(0), sink_col(ATT_GROUP)
    zero_w = jnp.zeros((WINDOW, LANES), F32)

    def rows16(a):
        return jnp.concatenate([a, jnp.zeros((16 - a.shape[0], a.shape[1]), F32)], axis=0)

    def heads(row, off):
        return jnp.concatenate([row[:, off + j * LANES:off + (j + 1) * LANES] for j in range(4)], axis=0)

    def block_rows(a4):
        wide = jnp.concatenate([rows16(a4)] * RET_HEADS, axis=1)
        return jnp.where(head_blk, wide, jnp.zeros_like(wide))

    r0 = pl.multiple_of(i * bb, bb)
    zblk = z_s[pl.ds(r0, bb), :]
    rows = [zblk[b:b + 1, :] for b in range(bb)]

    q4 = [heads(r, OFF_Q) for r in rows]
    kn = [r[:, OFF_KV:OFF_KV + LANES] for r in rows]
    vn = [r[:, OFF_KV + LANES:OFF_RQ] for r in rows]
    s01, oh = [], []
    for b in range(bb):
        ck = ck_ref[b]
        ko_ref[b] = jnp.where(last_row, kn[b], pltpu.roll(ck, WINDOW - 1, axis=0))
        kcat = jnp.concatenate([jnp.where(lo, ck, zero_w), jnp.where(lo, zero_w, ck)], axis=0).astype(BF16)
        s01.append(_dot_nt(rows16(q4[b]).astype(BF16), kcat)[0:ATT_GROUP])
    for b in range(bb):
        rq = heads(rows[b], OFF_RQ)
        rk = heads(rows[b], OFF_RK)
        rv = heads(rows[b], OFF_RV)
        state = st_ref[b].reshape(RET_WIDTH, LANES)
        o2 = _dot(block_rows(rq * g1_col).astype(BF16), state.astype(BF16))[0:RET_HEADS]
        outer = _dot_tn(block_rows(rk).astype(BF16), rows16(rv).astype(BF16))
        for hh in range(RET_HEADS):
            so_ref[b, hh] = g1[hh] * st_ref[b, hh] + outer[hh * RET_DK:(hh + 1) * RET_DK, :]
        o = jnp.sum(rq * rk, axis=-1, keepdims=True) * rv + o2
        mu = jnp.mean(o, axis=-1, keepdims=True)
        d = o - mu
        var = jnp.mean(d * d, axis=-1, keepdims=True)
        oh.append(heads(rows[b], OFF_RG) * (d * jax.lax.rsqrt(var + GN_EPS) * gnw_ref[...]))

    w01, wn = [], []
    for b in range(bb):
        qk = q4[b] * kn[b]
        zero4 = jnp.zeros_like(qk)
        sn_lo = jnp.sum(jnp.where(lo_row, qk, zero4), axis=-1, keepdims=True)
        sn_hi = jnp.sum(jnp.where(lo_row, zero4, qk), axis=-1, keepdims=True)
        ws, wb = [], []
        for s_half, sn, sink in ((s01[b][:, :WINDOW], sn_lo, sink_lo), (s01[b][:, WINDOW:], sn_hi, sink_hi)):
            mx = jnp.maximum(jnp.maximum(jnp.max(s_half, axis=-1, keepdims=True), sn), sink)
            p = jnp.exp(s_half - mx)
            pn = jnp.exp(sn - mx)
            den = jnp.sum(p, axis=-1, keepdims=True) + pn + jnp.exp(sink - mx)
            inv = 1.0 / den
            ws.append(rows16(p * inv).astype(BF16))
            wb.append(pn * inv)
        w01.append(jnp.concatenate(ws, axis=1))
        wn.append(wb)

    att = []
    for b in range(bb):
        cv = cv_ref[b]
        vo_ref[b] = jnp.where(last_row, vn[b], pltpu.roll(cv, WINDOW - 1, axis=0))
        vcat = jnp.concatenate([jnp.where(lo, cv, zero_w), jnp.where(lo, zero_w, cv)], axis=0).astype(BF16)
        vn_lo = jnp.where(lo_row, vn[b], jnp.zeros_like(vn[b]))
        vn_hi = vn[b] - vn_lo
        att.append(_dot(w01[b], vcat)[0:ATT_GROUP] + wn[b][0] * vn_lo + wn[b][1] * vn_hi)
    blk_rows = pl.ds(r0, bb)
    for j in range(4):
        mix_s[blk_rows, j * LANES:(j + 1) * LANES] = jnp.concatenate(
            [a[j:j + 1, :] for a in att], axis=0)
        mix_s[blk_rows, ATT_WIDTH + j * LANES:ATT_WIDTH + (j + 1) * LANES] = jnp.concatenate(
            [o[j:j + 1, :] for o in oh], axis=0)

    @pl.when(i == n_blocks - 1)
    def _():
        y = (_dot(mix_s[:, :ATT_WIDTH].astype(BF16), wout_att_ref[...])
             + _dot(mix_s[:, ATT_WIDTH:].astype(BF16), wout_ret_ref[...]))
        r = DEEPNORM_ALPHA * x_ref[...] + (1.0 + mod_ref[2]) * y
        y_ref[...] = _layer_norm(r, lnw_ref[...], lnb_ref[...])


def _sample_step(sinks, x, mods, tab, w_q, w_rest, w_out_att, w_out_ret, gn_w4, ln_w, ln_b,
                 cache_k, cache_v, state):
    n = x.shape[0]
    bb = SAMPLE_BLOCK
    n_blocks = n // bb
    g1 = tuple(float(v) for v in np.exp(_log_gamma()).astype(np.float32))
    full = lambda *shape: pl.BlockSpec(shape, lambda i, *_: (0,) * len(shape))
    blk = lambda *shape: pl.BlockSpec((bb,) + shape, lambda i, *_: (i,) + (0,) * len(shape))
    grid_spec = pltpu.PrefetchScalarGridSpec(
        num_scalar_prefetch=1,
        grid=(n_blocks,),
        in_specs=[
            full(n, D_MODEL), full(3, n, D_MODEL), full(5, 1, LANES),
            full(D_MODEL, ATT_WIDTH), full(D_MODEL, IN_WIDTH - ATT_WIDTH),
            full(ATT_WIDTH, D_MODEL), full(RET_WIDTH, D_MODEL),
            full(RET_HEADS, LANES), full(1, D_MODEL), full(1, D_MODEL),
            blk(WINDOW, LANES), blk(WINDOW, LANES), blk(RET_HEADS, RET_DK, LANES),
        ],
        out_specs=[full(n, D_MODEL), blk(WINDOW, LANES), blk(WINDOW, LANES),
                   blk(RET_HEADS, RET_DK, LANES)],
        scratch_shapes=[pltpu.VMEM((n, IN_WIDTH), F32), pltpu.VMEM((n, D_MODEL), F32)],
    )
    return pl.pallas_call(
        functools.partial(_sample_kernel, g1=g1, n_blocks=n_blocks),
        out_shape=[
            jax.ShapeDtypeStruct((n, D_MODEL), F32),
            jax.ShapeDtypeStruct((n, WINDOW, LANES), F32),
            jax.ShapeDtypeStruct((n, WINDOW, LANES), F32),
            jax.ShapeDtypeStruct((n, RET_HEADS, RET_DK, LANES), F32),
        ],
        grid_spec=grid_spec,
        compiler_params=pltpu.CompilerParams(
            dimension_semantics=("arbitrary",), vmem_limit_bytes=VMEM_LIMIT_BYTES),
        name="sample_step",
    )(sinks, x, mods, tab, w_q, w_rest, w_out_att, w_out_ret, gn_w4, ln_w, ln_b, cache_k, cache_v, state)


def _slab_order(w, axis):
    shape = w.shape
    pre, post = shape[:axis], shape[axis + 1:]
    w = w.reshape(pre + (ATT_KV_HEADS, ATT_GROUP, HEAD_DIM) + post)
    w = jnp.swapaxes(w, axis, axis + 1)
    return w.reshape(shape)


def kernel(x_prompt, x_sample, c_prompt, c_sample, cache_k, cache_v, state_ret, w_ada_mix, b_ada_mix, w_in, att_sinks, ret_gn_w, w_out, ln1_w, ln1_b, w_ada_ffn, b_ada_ffn, w_up, w_down, ln2_w, ln2_b):
    batch, seq, _ = x_prompt.shape
    n_s = x_sample.shape[0]
    l = 0
    assert w_in.shape[0] == DEPTH == 1

    w_in_f = w_in.reshape(D_MODEL, IN_WIDTH)
    w_out_f = w_out.reshape(D_MODEL, D_MODEL)
    w_q = _slab_order(w_in_f[:, :ATT_WIDTH].astype(BF16), 1)
    w_rest = w_in_f[:, ATT_WIDTH:].astype(BF16)
    w_out_att = _slab_order(w_out_f[:ATT_WIDTH].astype(BF16), 0)
    w_out_ret = w_out_f[ATT_WIDTH:].astype(BF16)
    w_up_b = w_up.reshape(D_MODEL, 2 * D_FF).astype(BF16)
    w_down_b = w_down.reshape(D_FF, D_MODEL).astype(BF16)
    sinks = att_sinks[l]
    gn_w = ret_gn_w[l].reshape(1, RET_WIDTH)
    ln1w, ln1b = ln1_w[l].reshape(1, D_MODEL), ln1_b[l].reshape(1, D_MODEL)
    ln2w, ln2b = ln2_w[l].reshape(1, D_MODEL), ln2_b[l].reshape(1, D_MODEL)

    tab_p = jnp.asarray(_rope_tables(np.arange(seq)))
    tab_s = jnp.asarray(_rope_tables(np.array([PAST_LEN])))
    dec = jnp.asarray(_decay_tables())
    bias = jnp.asarray(_band_bias())

    mods = _ada_mod(c_sample, c_prompt, w_ada_mix.reshape(D_MODEL, 3 * D_MODEL),
                    w_ada_ffn.reshape(D_MODEL, 3 * D_MODEL), b_ada_mix, b_ada_ffn)

    y1p, kp, vp, sp = _mixer_prompt(x_prompt, mods, n_s, sinks, tab_p, dec, bias, w_q, w_rest,
                                    w_out_att, w_out_ret, gn_w, ln1w, ln1b)
    yp = _ffn(y1p.reshape(batch * seq, D_MODEL), mods, n_s, batch, seq // FFN_TILE, w_up_b, w_down_b,
              ln2w, ln2b, FFN_TILE, FFN_PARTS).reshape(batch, seq, D_MODEL)

    xs = x_sample.reshape(n_s, D_MODEL)
    y1s, ks, vs, ss = _sample_step(
        sinks, xs, mods, tab_s, w_q, w_rest, w_out_att, w_out_ret, ret_gn_w.reshape(RET_HEADS, LANES),
        ln1w, ln1b, cache_k.reshape(n_s, WINDOW, LANES), cache_v.reshape(n_s, WINDOW, LANES),
        state_ret.reshape(n_s, RET_HEADS, RET_DK, LANES))
    ys = _ffn(y1s, mods, 0, n_s, 0, w_up_b, w_down_b, ln2w, ln2b, n_s, 1)

    kv_shape = (1, batch, WINDOW, ATT_KV_HEADS, HEAD_DIM)
    kvs_shape = (1, n_s, WINDOW, ATT_KV_HEADS, HEAD_DIM)
    return (yp, ys.reshape(n_s, 1, D_MODEL),
            kp.reshape(kv_shape), vp.reshape(kv_shape), sp[None],
            ks.reshape(kvs_shape), vs.reshape(kvs_shape), ss[None])
```

```python
import functools
import math

import numpy as np
import jax
import jax.numpy as jnp
from jax.experimental import pallas as pl
from jax.experimental.pallas import tpu as pltpu

D_MODEL = 1024
WINDOW = 128
CHUNK = 128
ATT_HEADS = 8
ATT_KV_HEADS = 2
ATT_GROUP = ATT_HEADS // ATT_KV_HEADS
HEAD_DIM = 64
ROPE_DIMS = HEAD_DIM // 4
ROPE_THETA = 500000.0
RET_HEADS = 4
RET_DK = 128
RET_THETA = 10000.0
ATT_WIDTH = ATT_HEADS * HEAD_DIM
KV_WIDTH = ATT_KV_HEADS * HEAD_DIM
RET_WIDTH = RET_HEADS * RET_DK
IN_WIDTH = ATT_WIDTH + 2 * KV_WIDTH + 4 * RET_WIDTH
D_FF = 2816
DEPTH = 1
DEEPNORM_ALPHA = (2 * DEPTH) ** 0.25
LN_EPS = 1e-5
GN_EPS = 1e-6
PAST_LEN = 16384

OFF_Q = 0
OFF_KV = ATT_WIDTH
OFF_RQ = OFF_KV + 2 * KV_WIDTH
OFF_RK = OFF_RQ + RET_WIDTH
OFF_RV = OFF_RK + RET_WIDTH
OFF_RG = OFF_RV + RET_WIDTH

LANES = 128
VMEM_LIMIT_BYTES = 56 * 1024 * 1024

PROMPT_TILE = 512
FILL_WIDTH = 256
FILL_PLAN = ((2, 1), (2, 1), (2, 1), (2, 0))
FFN_TILE = 1024
FFN_PARTS = 4
SAMPLE_BLOCK = 8

BF16 = jnp.bfloat16
F32 = jnp.float32


def _dot(a, b):
    return jnp.dot(a, b, preferred_element_type=F32)


def _dot_nt(a, b):
    return jax.lax.dot_general(a, b, (((1,), (1,)), ((), ())), preferred_element_type=F32)


def _dot_tn(a, b):
    return jax.lax.dot_general(a, b, (((0,), (0,)), ((), ())), preferred_element_type=F32)


def _sigmoid(x):
    return 1.0 / (1.0 + jnp.exp(-x))


def _silu(x):
    return x * _sigmoid(x)


def _layer_norm(r, w, b):
    mu = jnp.mean(r, axis=-1, keepdims=True)
    d = r - mu
    var = jnp.mean(d * d, axis=-1, keepdims=True)
    return d * jax.lax.rsqrt(var + LN_EPS) * w + b


def _rope_att(slab, c, s1, s2):
    return slab * c + pltpu.roll(slab, LANES - 8, axis=1) * s1 + pltpu.roll(slab, 8, axis=1) * s2


def _rope_ret(slab, c, s):
    return slab * c + pltpu.roll(slab, LANES // 2, axis=1) * s


def _rope_tables(pos):
    pos = np.asarray(pos, np.float64)[:, None]
    half = ROPE_DIMS // 2
    inv = ROPE_THETA ** (-np.arange(half, dtype=np.float64) / half)
    ang = pos * inv[None, :]
    cos, sin = np.cos(ang), np.sin(ang)
    n = pos.shape[0]
    head_c = np.ones((n, HEAD_DIM)); head_s1 = np.zeros((n, HEAD_DIM)); head_s2 = np.zeros((n, HEAD_DIM))
    head_c[:, :half] = cos; head_c[:, half:2 * half] = cos
    head_s1[:, :half] = -sin
    head_s2[:, half:2 * half] = sin
    att = [np.tile(t, (1, LANES // HEAD_DIM)) for t in (head_c, head_s1, head_s2)]
    rhalf = RET_DK // 2
    rinv = RET_THETA ** (-np.arange(rhalf, dtype=np.float64) / rhalf)
    rang = pos * rinv[None, :]
    rc = np.concatenate([np.cos(rang), np.cos(rang)], axis=1)
    rs = np.concatenate([-np.sin(rang), np.sin(rang)], axis=1)
    return np.stack(att + [rc, rs], axis=0).astype(np.float32)


def _log_gamma():
    lin = np.linspace(math.log(1.0 / 32), math.log(1.0 / 512), RET_HEADS)
    return np.log1p(-np.exp(lin))


def _decay_tables():
    lg = _log_gamma()
    idx = np.arange(CHUNK, dtype=np.float64)
    diff = idx[:, None] - idx[None, :]
    dmat = np.where(diff[None] >= 0, np.exp(lg[:, None, None] * np.maximum(diff, 0.0)[None]), 0.0)
    dq = np.exp(lg[:, None] * (idx[None, :] + 1.0))[:, :, None] * np.ones((1, 1, LANES))
    dk = np.exp(lg[:, None] * (CHUNK - 1.0 - idx[None, :]))[:, :, None] * np.ones((1, 1, LANES))
    return np.concatenate([dmat, dq, dk], axis=0).astype(np.float32)


def _band_bias():
    i = np.arange(CHUNK)[:, None]
    j = np.arange(2 * CHUNK)[None, :]
    valid = (j >= i) & (j <= i + WINDOW)
    neg = np.float32(-1e30)
    b0 = np.where(valid, 0.0, neg)
    b1 = np.where(valid & (j >= CHUNK), 0.0, neg)
    return np.stack([b0, b1], axis=0).astype(np.float32)


def _ada_kernel(cs_ref, cp_ref, w1_ref, w2_ref, b1_ref, b2_ref, o_ref):
    n_s = cs_ref.shape[0]
    a_s = _silu(cs_ref[...]).astype(BF16)
    a_p = _silu(cp_ref[...]).astype(BF16)

    def emit(w_ref, b_ref):
        w = w_ref[...].astype(BF16)
        o_ref[0:n_s, :] = _dot(a_s, w) + b_ref[...]
        o_ref[n_s:, :] = _dot(a_p, w) + b_ref[...]

    @pl.when(pl.program_id(0) < 3)
    def _():
        emit(w1_ref, b1_ref)

    @pl.when(pl.program_id(0) >= 3)
    def _():
        emit(w2_ref, b2_ref)


def _ada_mod(c_sample, c_prompt, w_mix, w_ffn, b_mix, b_ffn):
    rows = c_sample.shape[0] + c_prompt.shape[0]
    d = D_MODEL
    return pl.pallas_call(
        _ada_kernel,
        out_shape=jax.ShapeDtypeStruct((6, rows, d), F32),
        grid=(6,),
        in_specs=[
            pl.BlockSpec(c_sample.shape, lambda j: (0, 0)),
            pl.BlockSpec(c_prompt.shape, lambda j: (0, 0)),
            pl.BlockSpec((d, d), lambda j: (0, jnp.minimum(j, 2))),
            pl.BlockSpec((d, d), lambda j: (0, jnp.maximum(j - 3, 0))),
            pl.BlockSpec((1, d), lambda j: (0, jnp.minimum(j, 2))),
            pl.BlockSpec((1, d), lambda j: (0, jnp.maximum(j - 3, 0))),
        ],
        out_specs=pl.BlockSpec((None, rows, d), lambda j: (j, 0, 0)),
        compiler_params=pltpu.CompilerParams(
            dimension_semantics=("arbitrary",), vmem_limit_bytes=VMEM_LIMIT_BYTES),
        name="ada_mod",
    )(c_sample, c_prompt, w_mix, w_ffn, b_mix, b_ffn)


def _mixer_step(z_prev, z_cur, sinks_ref, x_ref, mods_ref, tab_ref, xp_ref, dec_ref, bias_ref,
                wq_ref, wrest_ref, wout_att_ref, wout_ret_ref, gnw_ref, lnw_ref, lnb_ref,
                y_ref, kout_ref, vout_ref, s_ref, kcarry, vcarry, mix_s,
                *, cur_b, prev_b, seq_start, tile, gl):
    n_chunks = tile // CHUNK
    x = x_ref[...]
    h = (x * (1.0 + mods_ref[1, pl.ds(cur_b, 1), :]) + mods_ref[0, pl.ds(cur_b, 1), :]).astype(BF16)
    n_fill = IN_WIDTH // FILL_WIDTH
    pending = list(range(n_fill))

    def fill(count):
        for _ in range(count):
            if pending:
                i = pending.pop(0)
                col = i * FILL_WIDTH
                w = (wq_ref[:, col:col + FILL_WIDTH] if col < ATT_WIDTH
                     else wrest_ref[:, col - ATT_WIDTH:col - ATT_WIDTH + FILL_WIDTH])
                z_cur[:, col:col + FILL_WIDTH] = _dot(h, w)

    gate = mods_ref[2, pl.ds(prev_b, 1), :]
    lo =jax.lax.broadcasted_iota(jnp.int32, (CHUNK, LANES), 1) < HEAD_DIM
    zero = jnp.zeros((CHUNK, LANES), F32)
    k_prev = [kcarry[g] for g in range(ATT_KV_HEADS)]
    v_prev = [vcarry[g] for g in range(ATT_KV_HEADS)]

    for c in range(n_chunks):
        rows = slice(c * CHUNK, (c + 1) * CHUNK)
        bias = bias_ref[seq_start] if c == 0 else bias_ref[0]
        att_c, att_s1, att_s2 = tab_ref[0, rows, :], tab_ref[1, rows, :], tab_ref[2, rows, :]
        ret_c, ret_s = tab_ref[3, rows, :], tab_ref[4, rows, :]

        def zs(col):
            return z_prev[rows, col:col + LANES]

        q_stack = jnp.concatenate(
            [(_rope_att(zs(OFF_Q + j * LANES), att_c, att_s1, att_s2) * (HEAD_DIM ** -0.5)).astype(BF16)
             for j in range(4)], axis=0)
        k = _rope_att(zs(OFF_KV), att_c, att_s1, att_s2)
        v = zs(OFF_KV + LANES)
        k_cur = [jnp.where(lo, k, zero).astype(BF16), jnp.where(lo, zero, k).astype(BF16)]
        v_cur = [jnp.where(lo, v, zero).astype(BF16), jnp.where(lo, zero, v).astype(BF16)]
        k_band = [jnp.concatenate([k_prev[g], k_cur[g]], axis=0) for g in range(ATT_KV_HEADS)]
        v_band = [jnp.concatenate([v_prev[g], v_cur[g]], axis=0) for g in range(ATT_KV_HEADS)]
        k_prev, v_prev = k_cur, v_cur
        if c == n_chunks - 1:
            kout_ref[...] = k
            vout_ref[...] = v
            for g in range(ATT_KV_HEADS):
                kcarry[g] = k_cur[g]
                vcarry[g] = v_cur[g]

        ret = []
        for hh in range(RET_HEADS):
            rq = _rope_ret(zs(OFF_RQ + hh * LANES), ret_c, ret_s)
            rk = _rope_ret(zs(OFF_RK + hh * LANES), ret_c, ret_s) * (RET_DK ** -0.5)
            ret.append(dict(
                q=rq.astype(BF16), qd=(rq * dec_ref[RET_HEADS + hh]).astype(BF16),
                k=rk.astype(BF16), kd=(rk * dec_ref[2 * RET_HEADS + hh]).astype(BF16),
                v=zs(OFF_RV + hh * LANES).astype(BF16)))

        sc_att = [_dot_nt(q_stack, k_band[g]) for g in range(ATT_KV_HEADS)]
        sc_ret = [_dot_nt(r["q"], r["k"]) for r in ret]
        fill(FILL_PLAN[c][0])

        att = None
        for g in range(ATT_KV_HEADS):
            ws = []
            for j in range(4):
                sj = sc_att[g][j * CHUNK:(j + 1) * CHUNK, :] + bias
                sink = sinks_ref[g * ATT_GROUP + j]
                mx = jnp.maximum(jnp.max(sj, axis=-1, keepdims=True), sink)
                p = jnp.exp(sj - mx)
                den = jnp.sum(p, axis=-1, keepdims=True) + jnp.exp(sink - mx)
                ws.append((p * (1.0 / den)).astype(BF16))
            o = _dot(jnp.concatenate(ws, axis=0), v_band[g])
            att = o if att is None else att + o
        for j in range(4):
            mix_s[rows, j * LANES:(j + 1) * LANES] = att[j * CHUNK:(j + 1) * CHUNK, :].astype(BF16)

        outs = []
        for hh, r in enumerate(ret):
            state = s_ref[hh]
            sc = sc_ret[hh] * dec_ref[hh]
            outs.append(_dot(sc.astype(BF16), r["v"]) + _dot(r["qd"], state.astype(BF16)))
            s_ref[hh] = gl[hh] * state + _dot_tn(r["kd"], r["v"])
        fill(FILL_PLAN[c][1])
        for hh in range(RET_HEADS):
            cols = slice(hh * LANES, (hh + 1) * LANES)
            o = outs[hh]
            mu = jnp.mean(o, axis=-1, keepdims=True)
            d = o - mu
            var = jnp.mean(d * d, axis=-1, keepdims=True)
            nrm = d * jax.lax.rsqrt(var + GN_EPS) * gnw_ref[:, cols]
            mix_s[rows, ATT_WIDTH + hh * LANES:ATT_WIDTH + (hh + 1) * LANES] = (
                _silu(zs(OFF_RG + hh * LANES)) * nrm).astype(BF16)

    y = _dot(mix_s[...], jnp.concatenate([wout_att_ref[...], wout_ret_ref[...]], axis=0))
    fill(n_fill)
    r = DEEPNORM_ALPHA * xp_ref[...] + (1.0 + gate) * y
    y_ref[...] = _layer_norm(r, lnw_ref[...], lnb_ref[...])


def _mixer_kernel(sinks_ref, x_ref, mods_ref, tab_ref, xp_ref, dec_ref, bias_ref,
                  wq_ref, wrest_ref, wout_att_ref, wout_ret_ref, gnw_ref, lnw_ref, lnb_ref,
                  y_ref, kout_ref, vout_ref, s_ref,
                  z_a, z_b, kcarry, vcarry, mix_s, *, tile, ns, nt, gl):
    t = pl.program_id(0)
    prev_tile = jnp.maximum(t - 1, 0)
    sa = jax.lax.rem(prev_tile, ns)
    seq_start = (sa == 0).astype(jnp.int32)
    cur_b = jax.lax.div(jnp.minimum(t, nt - 1), ns)
    prev_b = jax.lax.div(prev_tile, ns)
    parity = jax.lax.rem(t, 2)

    @pl.when(t == 0)
    def _():
        z_b[...] = jnp.zeros(z_b.shape, z_b.dtype)
        kcarry[...] = jnp.zeros(kcarry.shape, kcarry.dtype)
        vcarry[...] = jnp.zeros(vcarry.shape, vcarry.dtype)

    @pl.when(sa == 0)
    def _():
        s_ref[...] = jnp.zeros_like(s_ref)

    step = functools.partial(
        _mixer_step, sinks_ref=sinks_ref, x_ref=x_ref, mods_ref=mods_ref, tab_ref=tab_ref, xp_ref=xp_ref,
        dec_ref=dec_ref, bias_ref=bias_ref, wq_ref=wq_ref, wrest_ref=wrest_ref,
        wout_att_ref=wout_att_ref, wout_ret_ref=wout_ret_ref,
        gnw_ref=gnw_ref, lnw_ref=lnw_ref, lnb_ref=lnb_ref, y_ref=y_ref, kout_ref=kout_ref,
        vout_ref=vout_ref, s_ref=s_ref, kcarry=kcarry, vcarry=vcarry, mix_s=mix_s,
        cur_b=cur_b, prev_b=prev_b, seq_start=seq_start, tile=tile, gl=gl)

    @pl.when(parity == 0)
    def _():
        step(z_b, z_a)

    @pl.when(parity == 1)
    def _():
        step(z_a, z_b)


def _mixer_prompt(x, mods, mod_row0, sinks, tab, dec, bias, w_q, w_rest, w_out_att, w_out_ret, gn_w, ln_w, ln_b):
    batch, seq, _ = x.shape
    assert mod_row0 % batch == 0 and batch % 8 == 0
    tile = PROMPT_TILE
    ns = seq // tile
    gl = tuple(float(v) for v in np.exp(_log_gamma() * CHUNK).astype(np.float32))
    nt = batch * ns
    const = lambda shape: pl.BlockSpec(shape, lambda t, *_: (0,) * len(shape),
                                       pipeline_mode=pl.Buffered(1))
    cur_b = lambda t: jnp.minimum(t, nt - 1) // ns
    cur_s = lambda t: jnp.minimum(t, nt - 1) % ns
    prev_b = lambda t: jnp.maximum(t - 1, 0) // ns
    prev_s = lambda t: jnp.maximum(t - 1, 0) % ns
    grid_spec = pltpu.PrefetchScalarGridSpec(
        num_scalar_prefetch=1,
        grid=(nt + 1,),
        in_specs=[
            pl.BlockSpec((None, tile, D_MODEL), lambda t, *_: (cur_b(t), cur_s(t), 0)),
            pl.BlockSpec((3, batch, D_MODEL), lambda t, *_: (0, mod_row0 // batch, 0),
                         pipeline_mode=pl.Buffered(1)),
            pl.BlockSpec((5, tile, LANES), lambda t, *_: (0, prev_s(t), 0)),
            pl.BlockSpec((None, tile, D_MODEL), lambda t, *_: (prev_b(t), prev_s(t), 0)),
            const((3 * RET_HEADS, CHUNK, LANES)),
            const((2, CHUNK, 2 * CHUNK)),
            const((D_MODEL, ATT_WIDTH)),
            const((D_MODEL, IN_WIDTH - ATT_WIDTH)),
            const((ATT_WIDTH, D_MODEL)),
            const((RET_WIDTH, D_MODEL)),
            const((1, RET_WIDTH)),
            const((1, D_MODEL)),
            const((1, D_MODEL)),
        ],
        out_specs=[
            pl.BlockSpec((None, tile, D_MODEL), lambda t, *_: (prev_b(t), prev_s(t), 0)),
            pl.BlockSpec((None, WINDOW, LANES), lambda t, *_: (prev_b(t), 0, 0)),
            pl.BlockSpec((None, WINDOW, LANES), lambda t, *_: (prev_b(t), 0, 0)),
            pl.BlockSpec((None, RET_HEADS, RET_DK, LANES), lambda t, *_: (prev_b(t), 0, 0, 0)),
        ],
        scratch_shapes=[
            pltpu.VMEM((tile, IN_WIDTH), F32),
            pltpu.VMEM((tile, IN_WIDTH), F32),
            pltpu.VMEM((ATT_KV_HEADS, CHUNK, LANES), BF16),
            pltpu.VMEM((ATT_KV_HEADS, CHUNK, LANES), BF16),
            pltpu.VMEM((tile, D_MODEL), BF16),
        ],
    )
    return pl.pallas_call(
        functools.partial(_mixer_kernel, tile=tile, ns=ns, nt=nt, gl=gl),
        out_shape=[
            jax.ShapeDtypeStruct((batch, seq, D_MODEL), F32),
            jax.ShapeDtypeStruct((batch, WINDOW, LANES), F32),
            jax.ShapeDtypeStruct((batch, WINDOW, LANES), F32),
            jax.ShapeDtypeStruct((batch, RET_HEADS, RET_DK, LANES), F32),
        ],
        grid_spec=grid_spec,
        compiler_params=pltpu.CompilerParams(
            dimension_semantics=("arbitrary",), vmem_limit_bytes=VMEM_LIMIT_BYTES),
        name="mixer_prompt",
    )(sinks, x, mods, tab, x, dec, bias, w_q, w_rest, w_out_att, w_out_ret, gn_w, ln_w, ln_b)


def _ffn_kernel(y_ref, mods_ref, wup_ref, wdown_ref, lnw_ref, lnb_ref, o_ref, *, parts, tiles_per_mod_row):
    rows = y_ref.shape[0] // parts
    sl = [slice(p * rows, (p + 1) * rows) for p in range(parts)]
    if tiles_per_mod_row:
        seq = jax.lax.div(pl.program_id(0), tiles_per_mod_row)
        mod = lambda k, p: mods_ref[k, pl.ds(seq, 1), :]
    else:
        mod = lambda k, p: mods_ref[k, sl[p], :]
    ys = [y_ref[sl[p], :] for p in range(parts)]
    hs = [(ys[p] * (1.0 + mod(1, p)) + mod(0, p)).astype(BF16) for p in range(parts)]
    acts = [(_silu(_dot(h, wup_ref[:, :D_FF])) * _dot(h, wup_ref[:, D_FF:])).astype(BF16) for h in hs]
    fs = [_dot(a, wdown_ref[...]) for a in acts]
    for p in range(parts):
        r = DEEPNORM_ALPHA * ys[p] + (1.0 + mod(2, p)) * fs[p]
        o_ref[sl[p], :] = _layer_norm(r, lnw_ref[...], lnb_ref[...])


def _ffn(y, mods, mod_row0, mod_rows, tiles_per_mod_row, w_up, w_down, ln_w, ln_b, tile, parts):
    rows, _ = y.shape
    nt = rows // tile
    assert mod_row0 % mod_rows == 0 and (tiles_per_mod_row > 0 or mod_rows == tile)
    const = lambda shape: pl.BlockSpec(shape, lambda t: (0,) * len(shape), pipeline_mode=pl.Buffered(1))
    return pl.pallas_call(
        functools.partial(_ffn_kernel, parts=parts, tiles_per_mod_row=tiles_per_mod_row),
        out_shape=jax.ShapeDtypeStruct(y.shape, F32),
        grid=(nt,),
        in_specs=[
            pl.BlockSpec((tile, D_MODEL), lambda t: (t, 0)),
            pl.BlockSpec((3, mod_rows, D_MODEL), lambda t: (1, mod_row0 // mod_rows, 0),
                         pipeline_mode=pl.Buffered(1)),
            const((D_MODEL, 2 * D_FF)),
            const((D_FF, D_MODEL)),
            const((1, D_MODEL)),
            const((1, D_MODEL)),
        ],
        out_specs=pl.BlockSpec((tile, D_MODEL), lambda t: (t, 0)),
        compiler_params=pltpu.CompilerParams(
            dimension_semantics=("arbitrary",), vmem_limit_bytes=VMEM_LIMIT_BYTES),
        name="ffn",
    )(y, mods, w_up, w_down, ln_w, ln_b)


def _sample_kernel(sinks_ref, x_ref, mod_ref, tab_ref, wq_ref, wrest_ref, wout_att_ref, wout_ret_ref,
                   gnw_ref, lnw_ref, lnb_ref,
                   ck_ref, cv_ref, st_ref,
                   y_ref, ko_ref, vo_ref, so_ref,
                   z_s, mix_s, *, g1, n_blocks):
    i = pl.program_id(0)
    bb = SAMPLE_BLOCK

    @pl.when(i == 0)
    def _():
        m = mod_ref[...]
        h = (x_ref[...] * (1.0 + m[1]) + m[0]).astype(BF16)
        z = jnp.concatenate([_dot(h, wq_ref[...]), _dot(h, wrest_ref[...])], axis=1)
        att_c, att_s1, att_s2 = tab_ref[0], tab_ref[1], tab_ref[2]
        ret_c, ret_s = tab_ref[3], tab_ref[4]
        for j in range(4):
            col = OFF_Q + j * LANES
            z_s[:, col:col + LANES] = _rope_att(z[:, col:col + LANES], att_c, att_s1, att_s2) * (HEAD_DIM ** -0.5)
            col = OFF_RQ + j * LANES
            z_s[:, col:col + LANES] = _rope_ret(z[:, col:col + LANES], ret_c, ret_s)
            col = OFF_RK + j * LANES
            z_s[:, col:col + LANES] = _rope_ret(z[:, col:col + LANES], ret_c, ret_s) * (RET_DK ** -0.5)
        z_s[:, OFF_KV:OFF_KV + LANES] = _rope_att(z[:, OFF_KV:OFF_KV + LANES], att_c, att_s1, att_s2)
        z_s[:, OFF_KV + LANES:OFF_RQ] = z[:, OFF_KV + LANES:OFF_RQ]
        z_s[:, OFF_RV:OFF_RG] = z[:, OFF_RV:OFF_RG]
        z_s[:, OFF_RG:] = _silu(z[:, OFF_RG:])

    lane_w = jax.lax.broadcasted_iota(jnp.int32, (WINDOW, LANES), 1)
    row_w = jax.lax.broadcasted_iota(jnp.int32, (WINDOW, LANES), 0)
    lo = lane_w < HEAD_DIM
    lo_row = lo[0:1, :]
    last_row = row_w == WINDOW - 1
    r4 = jax.lax.broadcasted_iota(jnp.int32, (ATT_GROUP, 1), 0)
    head_blk = (jax.lax.broadcasted_iota(jnp.int32, (16, RET_WIDTH), 1) // LANES
                == jax.lax.broadcasted_iota(jnp.int32, (16, RET_WIDTH), 0))
    g1_col = jnp.where(r4 == 0, g1[0], jnp.where(r4 == 1, g1[1], jnp.where(r4 == 2, g1[2], g1[3])))

    def sink_col(base):
        s = [sinks_ref[base + j] for j in range(ATT_GROUP)]
        return jnp.where(r4 == 0, s[0], jnp.where(r4 == 1, s[1], jnp.where(r4 == 2, s[2], s[3])))

    sink_lo, sink_hi = sink_col(0), sink_col(ATT_GROUP)
    zero_w = jnp.zeros((WINDOW, LANES), F32)

    def rows16(a):
        return jnp.concatenate([a, jnp.zeros((16 - a.shape[0], a.shape[1]), F32)], axis=0)

    def heads(row, off):
        return jnp.concatenate([row[:, off + j * LANES:off + (j + 1) * LANES] for j in range(4)], axis=0)

    def block_rows(a4):
        wide = jnp.concatenate([rows16(a4)] * RET_HEADS, axis=1)
        return jnp.where(head_blk, wide, jnp.zeros_like(wide))

    r0 = pl.multiple_of(i * bb, bb)
    zblk = z_s[pl.ds(r0, bb), :]
    rows = [zblk[b:b + 1, :] for b in range(bb)]

    q4 = [heads(r, OFF_Q) for r in rows]
    kn = [r[:, OFF_KV:OFF_KV + LANES] for r in rows]
    vn = [r[:, OFF_KV + LANES:OFF_RQ] for r in rows]
    s01, oh = [], []
    for b in range(bb):
        ck = ck_ref[b]
        ko_ref[b] = jnp.where(last_row, kn[b], pltpu.roll(ck, WINDOW - 1, axis=0))
        kcat = jnp.concatenate([jnp.where(lo, ck, zero_w), jnp.where(lo, zero_w, ck)], axis=0).astype(BF16)
        s01.append(_dot_nt(rows16(q4[b]).astype(BF16), kcat)[0:ATT_GROUP])
    for b in range(bb):
        rq = heads(rows[b], OFF_RQ)
        rk = heads(rows[b], OFF_RK)
        rv = heads(rows[b], OFF_RV)
        state = st_ref[b].reshape(RET_WIDTH, LANES)
        o2 = _dot(block_rows(rq * g1_col).astype(BF16), state.astype(BF16))[0:RET_HEADS]
        outer = _dot_tn(block_rows(rk).astype(BF16), rows16(rv).astype(BF16))
        for hh in range(RET_HEADS):
            so_ref[b, hh] = g1[hh] * st_ref[b, hh] + outer[hh * RET_DK:(hh + 1) * RET_DK, :]
        o = jnp.sum(rq * rk, axis=-1, keepdims=True) * rv + o2
        mu = jnp.mean(o, axis=-1, keepdims=True)
        d = o - mu
        var = jnp.mean(d * d, axis=-1, keepdims=True)
        oh.append(heads(rows[b], OFF_RG) * (d * jax.lax.rsqrt(var + GN_EPS) * gnw_ref[...]))

    w01, wn = [], []
    for b in range(bb):
        qk = q4[b] * kn[b]
        zero4 = jnp.zeros_like(qk)
        sn_lo = jnp.sum(jnp.where(lo_row, qk, zero4), axis=-1, keepdims=True)
        sn_hi = jnp.sum(jnp.where(lo_row, zero4, qk), axis=-1, keepdims=True)
        ws, wb = [], []
        for s_half, sn, sink in ((s01[b][:, :WINDOW], sn_lo, sink_lo), (s01[b][:, WINDOW:], sn_hi, sink_hi)):
            mx = jnp.maximum(jnp.maximum(jnp.max(s_half, axis=-1, keepdims=True), sn), sink)
            p = jnp.exp(s_half - mx)
            pn = jnp.exp(sn - mx)
            den = jnp.sum(p, axis=-1, keepdims=True) + pn + jnp.exp(sink - mx)
            inv = 1.0 / den
            ws.append(rows16(p * inv).astype(BF16))
            wb.append(pn * inv)
        w01.append(jnp.concatenate(ws, axis=1))
        wn.append(wb)

    att = []
    for b in range(bb):
        cv = cv_ref[b]
        vo_ref[b] = jnp.where(last_row, vn[b], pltpu.roll(cv, WINDOW - 1, axis=0))
        vcat = jnp.concatenate([jnp.where(lo, cv, zero_w), jnp.where(lo, zero_w, cv)], axis=0).astype(BF16)
        vn_lo = jnp.where(lo_row, vn[b], jnp.zeros_like(vn[b]))
        vn_hi = vn[b] - vn_lo
        att.append(_dot(w01[b], vcat)[0:ATT_GROUP] + wn[b][0] * vn_lo + wn[b][1] * vn_hi)
    blk_rows = pl.ds(r0, bb)
    for j in range(4):
        mix_s[blk_rows, j * LANES:(j + 1) * LANES] = jnp.concatenate(
            [a[j:j + 1, :] for a in att], axis=0)
        mix_s[blk_rows, ATT_WIDTH + j * LANES:ATT_WIDTH + (j + 1) * LANES] = jnp.concatenate(
            [o[j:j + 1, :] for o in oh], axis=0)

    @pl.when(i == n_blocks - 1)
    def _():
        y = (_dot(mix_s[:, :ATT_WIDTH].astype(BF16), wout_att_ref[...])
             + _dot(mix_s[:, ATT_WIDTH:].astype(BF16), wout_ret_ref[...]))
        r = DEEPNORM_ALPHA * x_ref[...] + (1.0 + mod_ref[2]) * y
        y_ref[...] = _layer_norm(r, lnw_ref[...], lnb_ref[...])


def _sample_step(sinks, x, mods, tab, w_q, w_rest, w_out_att, w_out_ret, gn_w4, ln_w, ln_b,
                 cache_k, cache_v, state):
    n = x.shape[0]
    bb = SAMPLE_BLOCK
    n_blocks = n // bb
    g1 = tuple(float(v) for v in np.exp(_log_gamma()).astype(np.float32))
    full = lambda *shape: pl.BlockSpec(shape, lambda i, *_: (0,) * len(shape))
    blk = lambda *shape: pl.BlockSpec((bb,) + shape, lambda i, *_: (i,) + (0,) * len(shape))
    grid_spec = pltpu.PrefetchScalarGridSpec(
        num_scalar_prefetch=1,
        grid=(n_blocks,),
        in_specs=[
            full(n, D_MODEL), full(3, n, D_MODEL), full(5, 1, LANES),
            full(D_MODEL, ATT_WIDTH), full(D_MODEL, IN_WIDTH - ATT_WIDTH),
            full(ATT_WIDTH, D_MODEL), full(RET_WIDTH, D_MODEL),
            full(RET_HEADS, LANES), full(1, D_MODEL), full(1, D_MODEL),
            blk(WINDOW, LANES), blk(WINDOW, LANES), blk(RET_HEADS, RET_DK, LANES),
        ],
        out_specs=[full(n, D_MODEL), blk(WINDOW, LANES), blk(WINDOW, LANES),
                   blk(RET_HEADS, RET_DK, LANES)],
        scratch_shapes=[pltpu.VMEM((n, IN_WIDTH), F32), pltpu.VMEM((n, D_MODEL), F32)],
    )
    return pl.pallas_call(
        functools.partial(_sample_kernel, g1=g1, n_blocks=n_blocks),
        out_shape=[
            jax.ShapeDtypeStruct((n, D_MODEL), F32),
            jax.ShapeDtypeStruct((n, WINDOW, LANES), F32),
            jax.ShapeDtypeStruct((n, WINDOW, LANES), F32),
            jax.ShapeDtypeStruct((n, RET_HEADS, RET_DK, LANES), F32),
        ],
        grid_spec=grid_spec,
        compiler_params=pltpu.CompilerParams(
            dimension_semantics=("arbitrary",), vmem_limit_bytes=VMEM_LIMIT_BYTES),
        name="sample_step",
    )(sinks, x, mods, tab, w_q, w_rest, w_out_att, w_out_ret, gn_w4, ln_w, ln_b, cache_k, cache_v, state)


def _slab_order(w, axis):
    shape = w.shape
    pre, post = shape[:axis], shape[axis + 1:]
    w = w.reshape(pre + (ATT_KV_HEADS, ATT_GROUP, HEAD_DIM) + post)
    w = jnp.swapaxes(w, axis, axis + 1)
    return w.reshape(shape)


def kernel(x_prompt, x_sample, c_prompt, c_sample, cache_k, cache_v, state_ret, w_ada_mix, b_ada_mix, w_in, att_sinks, ret_gn_w, w_out, ln1_w, ln1_b, w_ada_ffn, b_ada_ffn, w_up, w_down, ln2_w, ln2_b):
    batch, seq, _ = x_prompt.shape
    n_s = x_sample.shape[0]
    l = 0
    assert w_in.shape[0] == DEPTH == 1

    w_in_f = w_in.reshape(D_MODEL, IN_WIDTH)
    w_out_f = w_out.reshape(D_MODEL, D_MODEL)
    w_q = _slab_order(w_in_f[:, :ATT_WIDTH].astype(BF16), 1)
    w_rest = w_in_f[:, ATT_WIDTH:].astype(BF16)
    w_out_att = _slab_order(w_out_f[:ATT_WIDTH].astype(BF16), 0)
    w_out_ret = w_out_f[ATT_WIDTH:].astype(BF16)
    w_up_b = w_up.reshape(D_MODEL, 2 * D_FF).astype(BF16)
    w_down_b = w_down.reshape(D_FF, D_MODEL).astype(BF16)
    sinks = att_sinks[l]
    gn_w = ret_gn_w[l].reshape(1, RET_WIDTH)
    ln1w, ln1b = ln1_w[l].reshape(1, D_MODEL), ln1_b[l].reshape(1, D_MODEL)
    ln2w, ln2b = ln2_w[l].reshape(1, D_MODEL), ln2_b[l].reshape(1, D_MODEL)

    tab_p = jnp.asarray(_rope_tables(np.arange(seq)))
    tab_s = jnp.asarray(_rope_tables(np.array([PAST_LEN])))
    dec = jnp.asarray(_decay_tables())
    bias = jnp.asarray(_band_bias())

    mods = _ada_mod(c_sample, c_prompt, w_ada_mix.reshape(D_MODEL, 3 * D_MODEL),
                    w_ada_ffn.reshape(D_MODEL, 3 * D_MODEL), b_ada_mix, b_ada_ffn)

    y1p, kp, vp, sp = _mixer_prompt(x_prompt, mods, n_s, sinks, tab_p, dec, bias, w_q, w_rest,
                                    w_out_att, w_out_ret, gn_w, ln1w, ln1b)
    yp = _ffn(y1p.reshape(batch * seq, D_MODEL), mods, n_s, batch, seq // FFN_TILE, w_up_b, w_down_b,
              ln2w, ln2b, FFN_TILE, FFN_PARTS).reshape(batch, seq, D_MODEL)

    xs = x_sample.reshape(n_s, D_MODEL)
    y1s, ks, vs, ss = _sample_step(
        sinks, xs, mods, tab_s, w_q, w_rest, w_out_att, w_out_ret, ret_gn_w.reshape(RET_HEADS, LANES),
        ln1w, ln1b, cache_k.reshape(n_s, WINDOW, LANES), cache_v.reshape(n_s, WINDOW, LANES),
        state_ret.reshape(n_s, RET_HEADS, RET_DK, LANES))
    ys = _ffn(y1s, mods, 0, n_s, 0, w_up_b, w_down_b, ln2w, ln2b, n_s, 1)

    kv_shape = (1, batch, WINDOW, ATT_KV_HEADS, HEAD_DIM)
    kvs_shape = (1, n_s, WINDOW, ATT_KV_HEADS, HEAD_DIM)
    return (yp, ys.reshape(n_s, 1, D_MODEL),
            kp.reshape(kv_shape), vp.reshape(kv_shape), sp[None],
            ks.reshape(kvs_shape), vs.reshape(kvs_shape), ss[None])
```

```python
import functools
import math

import numpy as np
import jax
import jax.numpy as jnp
from jax.experimental import pallas as pl
from jax.experimental.pallas import tpu as pltpu

D_MODEL = 1024
WINDOW = 128
CHUNK = 128
ATT_HEADS = 8
ATT_KV_HEADS = 2
ATT_GROUP = ATT_HEADS // ATT_KV_HEADS
HEAD_DIM = 64
ROPE_DIMS = HEAD_DIM // 4
ROPE_THETA = 500000.0
RET_HEADS = 4
RET_DK = 128
RET_THETA = 10000.0
ATT_WIDTH = ATT_HEADS * HEAD_DIM
KV_WIDTH = ATT_KV_HEADS * HEAD_DIM
RET_WIDTH = RET_HEADS * RET_DK
IN_WIDTH = ATT_WIDTH + 2 * KV_WIDTH + 4 * RET_WIDTH
D_FF = 2816
DEPTH = 1
DEEPNORM_ALPHA = (2 * DEPTH) ** 0.25
LN_EPS = 1e-5
GN_EPS = 1e-6
PAST_LEN = 16384

OFF_Q = 0
OFF_KV = ATT_WIDTH
OFF_RQ = OFF_KV + 2 * KV_WIDTH
OFF_RK = OFF_RQ + RET_WIDTH
OFF_RV = OFF_RK + RET_WIDTH
OFF_RG = OFF_RV + RET_WIDTH

LANES = 128
VMEM_LIMIT_BYTES = 56 * 1024 * 1024

PROMPT_TILE = 512
FILL_WIDTH = 256
FILL_PLAN = ((2, 1), (2, 1), (2, 1), (2, 0))
FFN_TILE = 1024
FFN_PARTS = 4
SAMPLE_BLOCK = 8

BF16 = jnp.bfloat16
F32 = jnp.float32


def _dot(a, b):
    return jnp.dot(a, b, preferred_element_type=F32)


def _dot_nt(a, b):
    return jax.lax.dot_general(a, b, (((1,), (1,)), ((), ())), preferred_element_type=F32)


def _dot_tn(a, b):
    return jax.lax.dot_general(a, b, (((0,), (0,)), ((), ())), preferred_element_type=F32)


def _sigmoid(x):
    return 1.0 / (1.0 + jnp.exp(-x))


def _silu(x):
    return x * _sigmoid(x)


def _layer_norm(r, w, b):
    mu = jnp.mean(r, axis=-1, keepdims=True)
    d = r - mu
    var = jnp.mean(d * d, axis=-1, keepdims=True)
    return d * jax.lax.rsqrt(var + LN_EPS) * w + b


def _rope_att(slab, c, s1, s2):
    return slab * c + pltpu.roll(slab, LANES - 8, axis=1) * s1 + pltpu.roll(slab, 8, axis=1) * s2


def _rope_ret(slab, c, s):
    return slab * c + pltpu.roll(slab, LANES // 2, axis=1) * s


def _rope_tables(pos):
    pos = np.asarray(pos, np.float64)[:, None]
    half = ROPE_DIMS // 2
    inv = ROPE_THETA ** (-np.arange(half, dtype=np.float64) / half)
    ang = pos * inv[None, :]
    cos, sin = np.cos(ang), np.sin(ang)
    n = pos.shape[0]
    head_c = np.ones((n, HEAD_DIM)); head_s1 = np.zeros((n, HEAD_DIM)); head_s2 = np.zeros((n, HEAD_DIM))
    head_c[:, :half] = cos; head_c[:, half:2 * half] = cos
    head_s1[:, :half] = -sin
    head_s2[:, half:2 * half] = sin
    att = [np.tile(t, (1, LANES // HEAD_DIM)) for t in (head_c, head_s1, head_s2)]
    rhalf = RET_DK // 2
    rinv = RET_THETA ** (-np.arange(rhalf, dtype=np.float64) / rhalf)
    rang = pos * rinv[None, :]
    rc = np.concatenate([np.cos(rang), np.cos(rang)], axis=1)
    rs = np.concatenate([-np.sin(rang), np.sin(rang)], axis=1)
    return np.stack(att + [rc, rs], axis=0).astype(np.float32)


def _log_gamma():
    lin = np.linspace(math.log(1.0 / 32), math.log(1.0 / 512), RET_HEADS)
    return np.log1p(-np.exp(lin))


def _decay_tables():
    lg = _log_gamma()
    idx = np.arange(CHUNK, dtype=np.float64)
    diff = idx[:, None] - idx[None, :]
    dmat = np.where(diff[None] >= 0, np.exp(lg[:, None, None] * np.maximum(diff, 0.0)[None]), 0.0)
    dq = np.exp(lg[:, None] * (idx[None, :] + 1.0))[:, :, None] * np.ones((1, 1, LANES))
    dk = np.exp(lg[:, None] * (CHUNK - 1.0 - idx[None, :]))[:, :, None] * np.ones((1, 1, LANES))
    return np.concatenate([dmat, dq, dk], axis=0).astype(np.float32)


def _band_bias():
    i = np.arange(CHUNK)[:, None]
    j = np.arange(2 * CHUNK)[None, :]
    valid = (j >= i) & (j <= i + WINDOW)
    neg = np.float32(-1e30)
    b0 = np.where(valid, 0.0, neg)
    b1 = np.where(valid & (j >= CHUNK), 0.0, neg)
    return np.stack([b0, b1], axis=0).astype(np.float32)


def _ada_kernel(cs_ref, cp_ref, w1_ref, w2_ref, b1_ref, b2_ref, o_ref):
    n_s = cs_ref.shape[0]
    a_s = _silu(cs_ref[...]).astype(BF16)
    a_p = _silu(cp_ref[...]).astype(BF16)

    def emit(w_ref, b_ref):
        w = w_ref[...].astype(BF16)
        o_ref[0:n_s, :] = _dot(a_s, w) + b_ref[...]
        o_ref[n_s:, :] = _dot(a_p, w) + b_ref[...]

    @pl.when(pl.program_id(0) < 3)
    def _():
        emit(w1_ref, b1_ref)

    @pl.when(pl.program_id(0) >= 3)
    def _():
        emit(w2_ref, b2_ref)


def _ada_mod(c_sample, c_prompt, w_mix, w_ffn, b_mix, b_ffn):
    rows = c_sample.shape[0] + c_prompt.shape[0]
    d = D_MODEL
    return pl.pallas_call(
        _ada_kernel,
        out_shape=jax.ShapeDtypeStruct((6, rows, d), F32),
        grid=(6,),
        in_specs=[
            pl.BlockSpec(c_sample.shape, lambda j: (0, 0)),
            pl.BlockSpec(c_prompt.shape, lambda j: (0, 0)),
            pl.BlockSpec((d, d), lambda j: (0, jnp.minimum(j, 2))),
            pl.BlockSpec((d, d), lambda j: (0, jnp.maximum(j - 3, 0))),
            pl.BlockSpec((1, d), lambda j: (0, jnp.minimum(j, 2))),
            pl.BlockSpec((1, d), lambda j: (0, jnp.maximum(j - 3, 0))),
        ],
        out_specs=pl.BlockSpec((None, rows, d), lambda j: (j, 0, 0)),
        compiler_params=pltpu.CompilerParams(
            dimension_semantics=("arbitrary",), vmem_limit_bytes=VMEM_LIMIT_BYTES),
        name="ada_mod",
    )(c_sample, c_prompt, w_mix, w_ffn, b_mix, b_ffn)


def _mixer_step(z_prev, z_cur, sinks_ref, x_ref, mods_ref, tab_ref, xp_ref, dec_ref, bias_ref,
                wq_ref, wrest_ref, wout_att_ref, wout_ret_ref, gnw_ref,
                y_ref, kout_ref, vout_ref, s_ref, kcarry, vcarry, mix_s,
                *, cur_b, prev_b, seq_start, tile, gl):
    n_chunks = tile // CHUNK
    x = x_ref[...]
    h = (x * (1.0 + mods_ref[1, pl.ds(cur_b, 1), :]) + mods_ref[0, pl.ds(cur_b, 1), :]).astype(BF16)
    n_fill = IN_WIDTH // FILL_WIDTH
    pending = list(range(n_fill))

    def fill(count):
        for _ in range(count):
            if pending:
                i = pending.pop(0)
                col = i * FILL_WIDTH
                w = (wq_ref[:, col:col + FILL_WIDTH] if col < ATT_WIDTH
                     else wrest_ref[:, col - ATT_WIDTH:col - ATT_WIDTH + FILL_WIDTH])
                z_cur[:, col:col + FILL_WIDTH] = _dot(h, w)

    gate = mods_ref[2, pl.ds(prev_b, 1), :]
    lo =jax.lax.broadcasted_iota(jnp.int32, (CHUNK, LANES), 1) < HEAD_DIM
    zero = jnp.zeros((CHUNK, LANES), F32)
    k_prev = [kcarry[g] for g in range(ATT_KV_HEADS)]
    v_prev = [vcarry[g] for g in range(ATT_KV_HEADS)]

    for c in range(n_chunks):
        rows = slice(c * CHUNK, (c + 1) * CHUNK)
        bias = bias_ref[seq_start] if c == 0 else bias_ref[0]
        att_c, att_s1, att_s2 = tab_ref[0, rows, :], tab_ref[1, rows, :], tab_ref[2, rows, :]
        ret_c, ret_s = tab_ref[3, rows, :], tab_ref[4, rows, :]

        def zs(col):
            return z_prev[rows, col:col + LANES]

        q_stack = jnp.concatenate(
            [(_rope_att(zs(OFF_Q + j * LANES), att_c, att_s1, att_s2) * (HEAD_DIM ** -0.5)).astype(BF16)
             for j in range(4)], axis=0)
        k = _rope_att(zs(OFF_KV), att_c, att_s1, att_s2)
        v = zs(OFF_KV + LANES)
        k_cur = [jnp.where(lo, k, zero).astype(BF16), jnp.where(lo, zero, k).astype(BF16)]
        v_cur = [jnp.where(lo, v, zero).astype(BF16), jnp.where(lo, zero, v).astype(BF16)]
        k_band = [jnp.concatenate([k_prev[g], k_cur[g]], axis=0) for g in range(ATT_KV_HEADS)]
        v_band = [jnp.concatenate([v_prev[g], v_cur[g]], axis=0) for g in range(ATT_KV_HEADS)]
        k_prev, v_prev = k_cur, v_cur
        if c == n_chunks - 1:
            kout_ref[...] = k
            vout_ref[...] = v
            for g in range(ATT_KV_HEADS):
                kcarry[g] = k_cur[g]
                vcarry[g] = v_cur[g]

        ret = []
        for hh in range(RET_HEADS):
            rq = _rope_ret(zs(OFF_RQ + hh * LANES), ret_c, ret_s)
            rk = _rope_ret(zs(OFF_RK + hh * LANES), ret_c, ret_s) * (RET_DK ** -0.5)
            ret.append(dict(
                q=rq.astype(BF16), qd=(rq * dec_ref[RET_HEADS + hh]).astype(BF16),
                k=rk.astype(BF16), kd=(rk * dec_ref[2 * RET_HEADS + hh]).astype(BF16),
                v=zs(OFF_RV + hh * LANES).astype(BF16)))

        sc_att = [_dot_nt(q_stack, k_band[g]) for g in range(ATT_KV_HEADS)]
        sc_ret = [_dot_nt(r["q"], r["k"]) for r in ret]
        fill(FILL_PLAN[c][0])

        att = None
        for g in range(ATT_KV_HEADS):
            ws = []
            for j in range(4):
                sj = sc_att[g][j * CHUNK:(j + 1) * CHUNK, :] + bias
                sink = sinks_ref[g * ATT_GROUP + j]
                mx = jnp.maximum(jnp.max(sj, axis=-1, keepdims=True), sink)
                p = jnp.exp(sj - mx)
                den = jnp.sum(p, axis=-1, keepdims=True) + jnp.exp(sink - mx)
                ws.append((p * (1.0 / den)).astype(BF16))
            o = _dot(jnp.concatenate(ws, axis=0), v_band[g])
            att = o if att is None else att + o
        for j in range(4):
            mix_s[rows, j * LANES:(j + 1) * LANES] = att[j * CHUNK:(j + 1) * CHUNK, :].astype(BF16)

        outs = []
        for hh, r in enumerate(ret):
            state = s_ref[hh]
            sc = sc_ret[hh] * dec_ref[hh]
            outs.append(_dot(sc.astype(BF16), r["v"]) + _dot(r["qd"], state.astype(BF16)))
            s_ref[hh] = gl[hh] * state + _dot_tn(r["kd"], r["v"])
        fill(FILL_PLAN[c][1])
        for hh in range(RET_HEADS):
            cols = slice(hh * LANES, (hh + 1) * LANES)
            o = outs[hh]
            mu = jnp.mean(o, axis=-1, keepdims=True)
            d = o - mu
            var = jnp.mean(d * d, axis=-1, keepdims=True)
            nrm = d * jax.lax.rsqrt(var + GN_EPS) * gnw_ref[:, cols]
            mix_s[rows, ATT_WIDTH + hh * LANES:ATT_WIDTH + (hh + 1) * LANES] = (
                _silu(zs(OFF_RG + hh * LANES)) * nrm).astype(BF16)

    y = _dot(mix_s[...], jnp.concatenate([wout_att_ref[...], wout_ret_ref[...]], axis=0))
    fill(n_fill)
    y_ref[...] = DEEPNORM_ALPHA * xp_ref[...] + (1.0 + gate) * y


def _mixer_kernel(sinks_ref, x_ref, mods_ref, tab_ref, xp_ref, dec_ref, bias_ref,
                  wq_ref, wrest_ref, wout_att_ref, wout_ret_ref, gnw_ref,
                  y_ref, kout_ref, vout_ref, s_ref,
                  z_a, z_b, kcarry, vcarry, mix_s, *, tile, ns, nt, gl):
    t = pl.program_id(0)
    prev_tile = jnp.maximum(t - 1, 0)
    sa = jax.lax.rem(prev_tile, ns)
    seq_start = (sa == 0).astype(jnp.int32)
    cur_b = jax.lax.div(jnp.minimum(t, nt - 1), ns)
    prev_b = jax.lax.div(prev_tile, ns)
    parity = jax.lax.rem(t, 2)

    @pl.when(t == 0)
    def _():
        z_b[...] = jnp.zeros(z_b.shape, z_b.dtype)
        kcarry[...] = jnp.zeros(kcarry.shape, kcarry.dtype)
        vcarry[...] = jnp.zeros(vcarry.shape, vcarry.dtype)

    @pl.when(sa == 0)
    def _():
        s_ref[...] = jnp.zeros_like(s_ref)

    step = functools.partial(
        _mixer_step, sinks_ref=sinks_ref, x_ref=x_ref, mods_ref=mods_ref, tab_ref=tab_ref, xp_ref=xp_ref,
        dec_ref=dec_ref, bias_ref=bias_ref, wq_ref=wq_ref, wrest_ref=wrest_ref,
        wout_att_ref=wout_att_ref, wout_ret_ref=wout_ret_ref,
        gnw_ref=gnw_ref, y_ref=y_ref, kout_ref=kout_ref,
        vout_ref=vout_ref, s_ref=s_ref, kcarry=kcarry, vcarry=vcarry, mix_s=mix_s,
        cur_b=cur_b, prev_b=prev_b, seq_start=seq_start, tile=tile, gl=gl)

    @pl.when(parity == 0)
    def _():
        step(z_b, z_a)

    @pl.when(parity == 1)
    def _():
        step(z_a, z_b)


def _mixer_prompt(x, mods, mod_row0, sinks, tab, dec, bias, w_q, w_rest, w_out_att, w_out_ret, gn_w):
    batch, seq, _ = x.shape
    assert mod_row0 % batch == 0 and batch % 8 == 0
    tile = PROMPT_TILE
    ns = seq // tile
    gl = tuple(float(v) for v in np.exp(_log_gamma() * CHUNK).astype(np.float32))
    nt = batch * ns
    const = lambda shape: pl.BlockSpec(shape, lambda t, *_: (0,) * len(shape),
                                       pipeline_mode=pl.Buffered(1))
    cur_b = lambda t: jnp.minimum(t, nt - 1) // ns
    cur_s = lambda t: jnp.minimum(t, nt - 1) % ns
    prev_b = lambda t: jnp.maximum(t - 1, 0) // ns
    prev_s = lambda t: jnp.maximum(t - 1, 0) % ns
    grid_spec = pltpu.PrefetchScalarGridSpec(
        num_scalar_prefetch=1,
        grid=(nt + 1,),
        in_specs=[
            pl.BlockSpec((None, tile, D_MODEL), lambda t, *_: (cur_b(t), cur_s(t), 0)),
            pl.BlockSpec((3, batch, D_MODEL), lambda t, *_: (0, mod_row0 // batch, 0),
                         pipeline_mode=pl.Buffered(1)),
            pl.BlockSpec((5, tile, LANES), lambda t, *_: (0, prev_s(t), 0)),
            pl.BlockSpec((None, tile, D_MODEL), lambda t, *_: (prev_b(t), prev_s(t), 0)),
            const((3 * RET_HEADS, CHUNK, LANES)),
            const((2, CHUNK, 2 * CHUNK)),
            const((D_MODEL, ATT_WIDTH)),
            const((D_MODEL, IN_WIDTH - ATT_WIDTH)),
            const((ATT_WIDTH, D_MODEL)),
            const((RET_WIDTH, D_MODEL)),
            const((1, RET_WIDTH)),
        ],
        out_specs=[
            pl.BlockSpec((None, tile, D_MODEL), lambda t, *_: (prev_b(t), prev_s(t), 0)),
            pl.BlockSpec((None, WINDOW, LANES), lambda t, *_: (prev_b(t), 0, 0)),
            pl.BlockSpec((None, WINDOW, LANES), lambda t, *_: (prev_b(t), 0, 0)),
            pl.BlockSpec((None, RET_HEADS, RET_DK, LANES), lambda t, *_: (prev_b(t), 0, 0, 0)),
        ],
        scratch_shapes=[
            pltpu.VMEM((tile, IN_WIDTH), F32),
            pltpu.VMEM((tile, IN_WIDTH), F32),
            pltpu.VMEM((ATT_KV_HEADS, CHUNK, LANES), BF16),
            pltpu.VMEM((ATT_KV_HEADS, CHUNK, LANES), BF16),
            pltpu.VMEM((tile, D_MODEL), BF16),
        ],
    )
    return pl.pallas_call(
        functools.partial(_mixer_kernel, tile=tile, ns=ns, nt=nt, gl=gl),
        out_shape=[
            jax.ShapeDtypeStruct((batch, seq, D_MODEL), F32),
            jax.ShapeDtypeStruct((batch, WINDOW, LANES), F32),
            jax.ShapeDtypeStruct((batch, WINDOW, LANES), F32),
            jax.ShapeDtypeStruct((batch, RET_HEADS, RET_DK, LANES), F32),
        ],
        grid_spec=grid_spec,
        compiler_params=pltpu.CompilerParams(
            dimension_semantics=("arbitrary",), vmem_limit_bytes=VMEM_LIMIT_BYTES),
        name="mixer_prompt",
    )(sinks, x, mods, tab, x, dec, bias, w_q, w_rest, w_out_att, w_out_ret, gn_w)


def _ffn_kernel(r_ref, mods_ref, wup_ref, wdown_ref, ln1w_ref, ln1b_ref, lnw_ref, lnb_ref, o_ref,
                *, parts, tiles_per_mod_row):
    rows = r_ref.shape[0] // parts
    sl = [slice(p * rows, (p + 1) * rows) for p in range(parts)]
    if tiles_per_mod_row:
        seq = jax.lax.div(pl.program_id(0), tiles_per_mod_row)
        mod = lambda k, p: mods_ref[k, pl.ds(seq, 1), :]
    else:
        mod = lambda k, p: mods_ref[k, sl[p], :]
    ys = [_layer_norm(r_ref[sl[p], :], ln1w_ref[...], ln1b_ref[...]) for p in range(parts)]
    hs = [(ys[p] * (1.0 + mod(1, p)) + mod(0, p)).astype(BF16) for p in range(parts)]
    acts = [(_silu(_dot(h, wup_ref[:, :D_FF])) * _dot(h, wup_ref[:, D_FF:])).astype(BF16) for h in hs]
    fs = [_dot(a, wdown_ref[...]) for a in acts]
    for p in range(parts):
        r = DEEPNORM_ALPHA * ys[p] + (1.0 + mod(2, p)) * fs[p]
        o_ref[sl[p], :] = _layer_norm(r, lnw_ref[...], lnb_ref[...])


def _ffn(y, mods, mod_row0, mod_rows, tiles_per_mod_row, w_up, w_down, ln1_w, ln1_b, ln_w, ln_b, tile, parts):
    rows, _ = y.shape
    nt = rows // tile
    assert mod_row0 % mod_rows == 0 and (tiles_per_mod_row > 0 or mod_rows == tile)
    const = lambda shape: pl.BlockSpec(shape, lambda t: (0,) * len(shape), pipeline_mode=pl.Buffered(1))
    return pl.pallas_call(
        functools.partial(_ffn_kernel, parts=parts, tiles_per_mod_row=tiles_per_mod_row),
        out_shape=jax.ShapeDtypeStruct(y.shape, F32),
        grid=(nt,),
        in_specs=[
            pl.BlockSpec((tile, D_MODEL), lambda t: (t, 0)),
            pl.BlockSpec((3, mod_rows, D_MODEL), lambda t: (1, mod_row0 // mod_rows, 0),
                         pipeline_mode=pl.Buffered(1)),
            const((D_MODEL, 2 * D_FF)),
            const((D_FF, D_MODEL)),
            const((1, D_MODEL)),
            const((1, D_MODEL)),
            const((1, D_MODEL)),
            const((1, D_MODEL)),
        ],
        out_specs=pl.BlockSpec((tile, D_MODEL), lambda t: (t, 0)),
        compiler_params=pltpu.CompilerParams(
            dimension_semantics=("arbitrary",), vmem_limit_bytes=VMEM_LIMIT_BYTES),
        name="ffn",
    )(y, mods, w_up, w_down, ln1_w, ln1_b, ln_w, ln_b)


def _sample_kernel(sinks_ref, x_ref, mod_ref, tab_ref, wq_ref, wrest_ref, wout_att_ref, wout_ret_ref,
                   gnw_ref,
                   ck_ref, cv_ref, st_ref,
                   y_ref, ko_ref, vo_ref, so_ref,
                   z_s, mix_s, *, g1, n_blocks):
    i = pl.program_id(0)
    bb = SAMPLE_BLOCK

    @pl.when(i == 0)
    def _():
        m = mod_ref[...]
        h = (x_ref[...] * (1.0 + m[1]) + m[0]).astype(BF16)
        z = jnp.concatenate([_dot(h, wq_ref[...]), _dot(h, wrest_ref[...])], axis=1)
        att_c, att_s1, att_s2 = tab_ref[0], tab_ref[1], tab_ref[2]
        ret_c, ret_s = tab_ref[3], tab_ref[4]
        for j in range(4):
            col = OFF_Q + j * LANES
            z_s[:, col:col + LANES] = _rope_att(z[:, col:col + LANES], att_c, att_s1, att_s2) * (HEAD_DIM ** -0.5)
            col = OFF_RQ + j * LANES
            z_s[:, col:col + LANES] = _rope_ret(z[:, col:col + LANES], ret_c, ret_s)
            col = OFF_RK + j * LANES
            z_s[:, col:col + LANES] = _rope_ret(z[:, col:col + LANES], ret_c, ret_s) * (RET_DK ** -0.5)
        z_s[:, OFF_KV:OFF_KV + LANES] = _rope_att(z[:, OFF_KV:OFF_KV + LANES], att_c, att_s1, att_s2)
        z_s[:, OFF_KV + LANES:OFF_RQ] = z[:, OFF_KV + LANES:OFF_RQ]
        z_s[:, OFF_RV:OFF_RG] = z[:, OFF_RV:OFF_RG]
        z_s[:, OFF_RG:] = _silu(z[:, OFF_RG:])

    lane_w = jax.lax.broadcasted_iota(jnp.int32, (WINDOW, LANES), 1)
    row_w = jax.lax.broadcasted_iota(jnp.int32, (WINDOW, LANES), 0)
    lo = lane_w < HEAD_DIM
    lo_row = lo[0:1, :]
    last_row = row_w == WINDOW - 1
    r4 = jax.lax.broadcasted_iota(jnp.int32, (ATT_GROUP, 1), 0)
    head_blk = (jax.lax.broadcasted_iota(jnp.int32, (16, RET_WIDTH), 1) // LANES
                == jax.lax.broadcasted_iota(jnp.int32, (16, RET_WIDTH), 0))
    g1_col = jnp.where(r4 == 0, g1[0], jnp.where(r4 == 1, g1[1], jnp.where(r4 == 2, g1[2], g1[3])))

    def sink_col(base):
        s = [sinks_ref[base + j] for j in range(ATT_GROUP)]
        return jnp.where(r4 == 0, s[0], jnp.where(r4 == 1, s[1], jnp.where(r4 == 2, s[2], s[3])))

    sink_lo, sink_hi = sink_col(0), sink_col(ATT_GROUP)
    zero_w = jnp.zeros((WINDOW, LANES), F32)

    def rows16(a):
        return jnp.concatenate([a, jnp.zeros((16 - a.shape[0], a.shape[1]), F32)], axis=0)

    def heads(row, off):
        return jnp.concatenate([row[:, off + j * LANES:off + (j + 1) * LANES] for j in range(4)], axis=0)

    def block_rows(a4):
        wide = jnp.concatenate([rows16(a4)] * RET_HEADS, axis=1)
        return jnp.where(head_blk, wide, jnp.zeros_like(wide))

    r0 = pl.multiple_of(i * bb, bb)
    zblk = z_s[pl.ds(r0, bb), :]
    rows = [zblk[b:b + 1, :] for b in range(bb)]

    q4 = [heads(r, OFF_Q) for r in rows]
    kn = [r[:, OFF_KV:OFF_KV + LANES] for r in rows]
    vn = [r[:, OFF_KV + LANES:OFF_RQ] for r in rows]
    s01, oh = [], []
    for b in range(bb):
        ck = ck_ref[b]
        ko_ref[b] = jnp.where(last_row, kn[b], pltpu.roll(ck, WINDOW - 1, axis=0))
        kcat = jnp.concatenate([jnp.where(lo, ck, zero_w), jnp.where(lo, zero_w, ck)], axis=0).astype(BF16)
        s01.append(_dot_nt(rows16(q4[b]).astype(BF16), kcat)[0:ATT_GROUP])
    for b in range(bb):
        rq = heads(rows[b], OFF_RQ)
        rk = heads(rows[b], OFF_RK)
        rv = heads(rows[b], OFF_RV)
        state = st_ref[b].reshape(RET_WIDTH, LANES)
        o2 = _dot(block_rows(rq * g1_col).astype(BF16), state.astype(BF16))[0:RET_HEADS]
        outer = _dot_tn(block_rows(rk).astype(BF16), rows16(rv).astype(BF16))
        for hh in range(RET_HEADS):
            so_ref[b, hh] = g1[hh] * st_ref[b, hh] + outer[hh * RET_DK:(hh + 1) * RET_DK, :]
        o = jnp.sum(rq * rk, axis=-1, keepdims=True) * rv + o2
        mu = jnp.mean(o, axis=-1, keepdims=True)
        d = o - mu
        var = jnp.mean(d * d, axis=-1, keepdims=True)
        oh.append(heads(rows[b], OFF_RG) * (d * jax.lax.rsqrt(var + GN_EPS) * gnw_ref[...]))

    w01, wn = [], []
    for b in range(bb):
        qk = q4[b] * kn[b]
        zero4 = jnp.zeros_like(qk)
        sn_lo = jnp.sum(jnp.where(lo_row, qk, zero4), axis=-1, keepdims=True)
        sn_hi = jnp.sum(jnp.where(lo_row, zero4, qk), axis=-1, keepdims=True)
        ws, wb = [], []
        for s_half, sn, sink in ((s01[b][:, :WINDOW], sn_lo, sink_lo), (s01[b][:, WINDOW:], sn_hi, sink_hi)):
            mx = jnp.maximum(jnp.maximum(jnp.max(s_half, axis=-1, keepdims=True), sn), sink)
            p = jnp.exp(s_half - mx)
            pn = jnp.exp(sn - mx)
            den = jnp.sum(p, axis=-1, keepdims=True) + pn + jnp.exp(sink - mx)
            inv = 1.0 / den
            ws.append(rows16(p * inv).astype(BF16))
            wb.append(pn * inv)
        w01.append(jnp.concatenate(ws, axis=1))
        wn.append(wb)

    att = []
    for b in range(bb):
        cv = cv_ref[b]
        vo_ref[b] = jnp.where(last_row, vn[b], pltpu.roll(cv, WINDOW - 1, axis=0))
        vcat = jnp.concatenate([jnp.where(lo, cv, zero_w), jnp.where(lo, zero_w, cv)], axis=0).astype(BF16)
        vn_lo = jnp.where(lo_row, vn[b], jnp.zeros_like(vn[b]))
        vn_hi = vn[b] - vn_lo
        att.append(_dot(w01[b], vcat)[0:ATT_GROUP] + wn[b][0] * vn_lo + wn[b][1] * vn_hi)
    blk_rows = pl.ds(r0, bb)
    for j in range(4):
        mix_s[blk_rows, j * LANES:(j + 1) * LANES] = jnp.concatenate(
            [a[j:j + 1, :] for a in att], axis=0)
        mix_s[blk_rows, ATT_WIDTH + j * LANES:ATT_WIDTH + (j + 1) * LANES] = jnp.concatenate(
            [o[j:j + 1, :] for o in oh], axis=0)

    @pl.when(i == n_blocks - 1)
    def _():
        y = (_dot(mix_s[:, :ATT_WIDTH].astype(BF16), wout_att_ref[...])
             + _dot(mix_s[:, ATT_WIDTH:].astype(BF16), wout_ret_ref[...]))
        y_ref[...] = DEEPNORM_ALPHA * x_ref[...] + (1.0 + mod_ref[2]) * y


def _sample_step(sinks, x, mods, tab, w_q, w_rest, w_out_att, w_out_ret, gn_w4,
                 cache_k, cache_v, state):
    n = x.shape[0]
    bb = SAMPLE_BLOCK
    n_blocks = n // bb
    g1 = tuple(float(v) for v in np.exp(_log_gamma()).astype(np.float32))
    full = lambda *shape: pl.BlockSpec(shape, lambda i, *_: (0,) * len(shape))
    blk = lambda *shape: pl.BlockSpec((bb,) + shape, lambda i, *_: (i,) + (0,) * len(shape))
    grid_spec = pltpu.PrefetchScalarGridSpec(
        num_scalar_prefetch=1,
        grid=(n_blocks,),
        in_specs=[
            full(n, D_MODEL), full(3, n, D_MODEL), full(5, 1, LANES),
            full(D_MODEL, ATT_WIDTH), full(D_MODEL, IN_WIDTH - ATT_WIDTH),
            full(ATT_WIDTH, D_MODEL), full(RET_WIDTH, D_MODEL),
            full(RET_HEADS, LANES),
            blk(WINDOW, LANES), blk(WINDOW, LANES), blk(RET_HEADS, RET_DK, LANES),
        ],
        out_specs=[full(n, D_MODEL), blk(WINDOW, LANES), blk(WINDOW, LANES),
                   blk(RET_HEADS, RET_DK, LANES)],
        scratch_shapes=[pltpu.VMEM((n, IN_WIDTH), F32), pltpu.VMEM((n, D_MODEL), F32)],
    )
    return pl.pallas_call(
        functools.partial(_sample_kernel, g1=g1, n_blocks=n_blocks),
        out_shape=[
            jax.ShapeDtypeStruct((n, D_MODEL), F32),
            jax.ShapeDtypeStruct((n, WINDOW, LANES), F32),
            jax.ShapeDtypeStruct((n, WINDOW, LANES), F32),
            jax.ShapeDtypeStruct((n, RET_HEADS, RET_DK, LANES), F32),
        ],
        grid_spec=grid_spec,
        compiler_params=pltpu.CompilerParams(
            dimension_semantics=("arbitrary",), vmem_limit_bytes=VMEM_LIMIT_BYTES),
        name="sample_step",
    )(sinks, x, mods, tab, w_q, w_rest, w_out_att, w_out_ret, gn_w4, cache_k, cache_v, state)


def _slab_order(w, axis):
    shape = w.shape
    pre, post = shape[:axis], shape[axis + 1:]
    w = w.reshape(pre + (ATT_KV_HEADS, ATT_GROUP, HEAD_DIM) + post)
    w = jnp.swapaxes(w, axis, axis + 1)
    return w.reshape(shape)


def kernel(x_prompt, x_sample, c_prompt, c_sample, cache_k, cache_v, state_ret, w_ada_mix, b_ada_mix, w_in, att_sinks, ret_gn_w, w_out, ln1_w, ln1_b, w_ada_ffn, b_ada_ffn, w_up, w_down, ln2_w, ln2_b):
    batch, seq, _ = x_prompt.shape
    n_s = x_sample.shape[0]
    l = 0
    assert w_in.shape[0] == DEPTH == 1

    w_in_f = w_in.reshape(D_MODEL, IN_WIDTH)
    w_out_f = w_out.reshape(D_MODEL, D_MODEL)
    w_q = _slab_order(w_in_f[:, :ATT_WIDTH].astype(BF16), 1)
    w_rest = w_in_f[:, ATT_WIDTH:].astype(BF16)
    w_out_att = _slab_order(w_out_f[:ATT_WIDTH].astype(BF16), 0)
    w_out_ret = w_out_f[ATT_WIDTH:].astype(BF16)
    w_up_b = w_up.reshape(D_MODEL, 2 * D_FF).astype(BF16)
    w_down_b = w_down.reshape(D_FF, D_MODEL).astype(BF16)
    sinks = att_sinks[l]
    gn_w = ret_gn_w[l].reshape(1, RET_WIDTH)
    ln1w, ln1b = ln1_w[l].reshape(1, D_MODEL), ln1_b[l].reshape(1, D_MODEL)
    ln2w, ln2b = ln2_w[l].reshape(1, D_MODEL), ln2_b[l].reshape(1, D_MODEL)

    tab_p = jnp.asarray(_rope_tables(np.arange(seq)))
    tab_s = jnp.asarray(_rope_tables(np.array([PAST_LEN])))
    dec = jnp.asarray(_decay_tables())
    bias = jnp.asarray(_band_bias())

    mods = _ada_mod(c_sample, c_prompt, w_ada_mix.reshape(D_MODEL, 3 * D_MODEL),
                    w_ada_ffn.reshape(D_MODEL, 3 * D_MODEL), b_ada_mix, b_ada_ffn)

    y1p, kp, vp, sp = _mixer_prompt(x_prompt, mods, n_s, sinks, tab_p, dec, bias, w_q, w_rest,
                                    w_out_att, w_out_ret, gn_w)
    yp = _ffn(y1p.reshape(batch * seq, D_MODEL), mods, n_s, batch, seq // FFN_TILE, w_up_b, w_down_b,
              ln1w, ln1b, ln2w, ln2b, FFN_TILE, FFN_PARTS).reshape(batch, seq, D_MODEL)

    xs = x_sample.reshape(n_s, D_MODEL)
    y1s, ks, vs, ss = _sample_step(
        sinks, xs, mods, tab_s, w_q, w_rest, w_out_att, w_out_ret, ret_gn_w.reshape(RET_HEADS, LANES),
        cache_k.reshape(n_s, WINDOW, LANES), cache_v.reshape(n_s, WINDOW, LANES),
        state_ret.reshape(n_s, RET_HEADS, RET_DK, LANES))
    ys = _ffn(y1s, mods, 0, n_s, 0, w_up_b, w_down_b, ln1w, ln1b, ln2w, ln2b, n_s, 1)

    kv_shape = (1, batch, WINDOW, ATT_KV_HEADS, HEAD_DIM)
    kvs_shape = (1, n_s, WINDOW, ATT_KV_HEADS, HEAD_DIM)
    return (yp, ys.reshape(n_s, 1, D_MODEL),
            kp.reshape(kv_shape), vp.reshape(kv_shape), sp[None],
            ks.reshape(kvs_shape), vs.reshape(kvs_shape), ss[None])
```

```python
import functools
import math

import numpy as np
import jax
import jax.numpy as jnp
from jax.experimental import pallas as pl
from jax.experimental.pallas import tpu as pltpu

D_MODEL = 1024
WINDOW = 128
CHUNK = 128
ATT_HEADS = 8
ATT_KV_HEADS = 2
ATT_GROUP = ATT_HEADS // ATT_KV_HEADS
HEAD_DIM = 64
ROPE_DIMS = HEAD_DIM // 4
ROPE_THETA = 500000.0
RET_HEADS = 4
RET_DK = 128
RET_THETA = 10000.0
ATT_WIDTH = ATT_HEADS * HEAD_DIM
KV_WIDTH = ATT_KV_HEADS * HEAD_DIM
RET_WIDTH = RET_HEADS * RET_DK
IN_WIDTH = ATT_WIDTH + 2 * KV_WIDTH + 4 * RET_WIDTH
D_FF = 2816
DEPTH = 1
DEEPNORM_ALPHA = (2 * DEPTH) ** 0.25
LN_EPS = 1e-5
GN_EPS = 1e-6
PAST_LEN = 16384

OFF_Q = 0
OFF_KV = ATT_WIDTH
OFF_RQ = OFF_KV + 2 * KV_WIDTH
OFF_RK = OFF_RQ + RET_WIDTH
OFF_RV = OFF_RK + RET_WIDTH
OFF_RG = OFF_RV + RET_WIDTH

LANES = 128
VMEM_LIMIT_BYTES = 56 * 1024 * 1024

PROMPT_TILE = 512
FILL_WIDTH = 256
FILL_PLAN = ((2, 1), (2, 1), (2, 1), (2, 0))
FFN_TILE = 1024
FFN_PARTS = 4
SAMPLE_BLOCK = 8

BF16 = jnp.bfloat16
F32 = jnp.float32


def _dot(a, b):
    return jnp.dot(a, b, preferred_element_type=F32)


def _dot_nt(a, b):
    return jax.lax.dot_general(a, b, (((1,), (1,)), ((), ())), preferred_element_type=F32)


def _dot_tn(a, b):
    return jax.lax.dot_general(a, b, (((0,), (0,)), ((), ())), preferred_element_type=F32)


def _sigmoid(x):
    return 1.0 / (1.0 + jnp.exp(-x))


def _silu(x):
    return x * _sigmoid(x)


def _layer_norm(r, w, b):
    mu = jnp.mean(r, axis=-1, keepdims=True)
    d = r - mu
    var = jnp.mean(d * d, axis=-1, keepdims=True)
    return d * jax.lax.rsqrt(var + LN_EPS) * w + b


def _rope_att(slab, c, s1, s2):
    return slab * c + pltpu.roll(slab, LANES - 8, axis=1) * s1 + pltpu.roll(slab, 8, axis=1) * s2


def _rope_ret(slab, c, s):
    return slab * c + pltpu.roll(slab, LANES // 2, axis=1) * s


def _rope_tables(pos):
    pos = np.asarray(pos, np.float64)[:, None]
    half = ROPE_DIMS // 2
    inv = ROPE_THETA ** (-np.arange(half, dtype=np.float64) / half)
    ang = pos * inv[None, :]
    cos, sin = np.cos(ang), np.sin(ang)
    n = pos.shape[0]
    head_c = np.ones((n, HEAD_DIM)); head_s1 = np.zeros((n, HEAD_DIM)); head_s2 = np.zeros((n, HEAD_DIM))
    head_c[:, :half] = cos; head_c[:, half:2 * half] = cos
    head_s1[:, :half] = -sin
    head_s2[:, half:2 * half] = sin
    att = [np.tile(t, (1, LANES // HEAD_DIM)) for t in (head_c, head_s1, head_s2)]
    rhalf = RET_DK // 2
    rinv = RET_THETA ** (-np.arange(rhalf, dtype=np.float64) / rhalf)
    rang = pos * rinv[None, :]
    rc = np.concatenate([np.cos(rang), np.cos(rang)], axis=1)
    rs = np.concatenate([-np.sin(rang), np.sin(rang)], axis=1)
    return np.stack(att + [rc, rs], axis=0).astype(np.float32)


def _log_gamma():
    lin = np.linspace(math.log(1.0 / 32), math.log(1.0 / 512), RET_HEADS)
    return np.log1p(-np.exp(lin))


def _decay_tables():
    lg = _log_gamma()
    idx = np.arange(CHUNK, dtype=np.float64)
    diff = idx[:, None] - idx[None, :]
    dmat = np.where(diff[None] >= 0, np.exp(lg[:, None, None] * np.maximum(diff, 0.0)[None]), 0.0)
    dq = np.exp(lg[:, None] * (idx[None, :] + 1.0))[:, :, None] * np.ones((1, 1, LANES))
    dk = np.exp(lg[:, None] * (CHUNK - 1.0 - idx[None, :]))[:, :, None] * np.ones((1, 1, LANES))
    return np.concatenate([dmat, dq, dk], axis=0).astype(np.float32)


def _band_bias():
    i = np.arange(CHUNK)[:, None]
    j = np.arange(2 * CHUNK)[None, :]
    valid = (j >= i) & (j <= i + WINDOW)
    neg = np.float32(-1e30)
    b0 = np.where(valid, 0.0, neg)
    b1 = np.where(valid & (j >= CHUNK), 0.0, neg)
    return np.stack([b0, b1], axis=0).astype(np.float32)


def _ada_kernel(cs_ref, cp_ref, w1_ref, w2_ref, b1_ref, b2_ref, o_ref):
    n_s = cs_ref.shape[0]
    a_s = _silu(cs_ref[...]).astype(BF16)
    a_p = _silu(cp_ref[...]).astype(BF16)

    def emit(w_ref, b_ref):
        w = w_ref[...].astype(BF16)
        o_ref[0:n_s, :] = _dot(a_s, w) + b_ref[...]
        o_ref[n_s:, :] = _dot(a_p, w) + b_ref[...]

    @pl.when(pl.program_id(0) < 3)
    def _():
        emit(w1_ref, b1_ref)

    @pl.when(pl.program_id(0) >= 3)
    def _():
        emit(w2_ref, b2_ref)


def _ada_mod(c_sample, c_prompt, w_mix, w_ffn, b_mix, b_ffn):
    rows = c_sample.shape[0] + c_prompt.shape[0]
    d = D_MODEL
    return pl.pallas_call(
        _ada_kernel,
        out_shape=jax.ShapeDtypeStruct((6, rows, d), F32),
        grid=(6,),
        in_specs=[
            pl.BlockSpec(c_sample.shape, lambda j: (0, 0)),
            pl.BlockSpec(c_prompt.shape, lambda j: (0, 0)),
            pl.BlockSpec((d, d), lambda j: (0, jnp.minimum(j, 2))),
            pl.BlockSpec((d, d), lambda j: (0, jnp.maximum(j - 3, 0))),
            pl.BlockSpec((1, d), lambda j: (0, jnp.minimum(j, 2))),
            pl.BlockSpec((1, d), lambda j: (0, jnp.maximum(j - 3, 0))),
        ],
        out_specs=pl.BlockSpec((None, rows, d), lambda j: (j, 0, 0)),
        compiler_params=pltpu.CompilerParams(
            dimension_semantics=("arbitrary",), vmem_limit_bytes=VMEM_LIMIT_BYTES),
        name="ada_mod",
    )(c_sample, c_prompt, w_mix, w_ffn, b_mix, b_ffn)


def _mixer_step(z_prev, z_cur, sinks_ref, x_ref, mods_ref, tab_ref, xp_ref, dec_ref, bias_ref,
                wq_ref, wrest_ref, wout_att_ref, wout_ret_ref, gnw_ref,
                y_ref, kout_ref, vout_ref, s_ref, kcarry, vcarry, mix_s,
                *, cur_b, prev_b, seq_start, tile, gl):
    n_chunks = tile // CHUNK
    x = x_ref[...]
    h = (x * (1.0 + mods_ref[1, pl.ds(cur_b, 1), :]) + mods_ref[0, pl.ds(cur_b, 1), :]).astype(BF16)
    n_fill = IN_WIDTH // FILL_WIDTH
    pending = list(range(n_fill))

    def fill(count):
        for _ in range(count):
            if pending:
                i = pending.pop(0)
                col = i * FILL_WIDTH
                w = (wq_ref[:, col:col + FILL_WIDTH] if col < ATT_WIDTH
                     else wrest_ref[:, col - ATT_WIDTH:col - ATT_WIDTH + FILL_WIDTH])
                z_cur[:, col:col + FILL_WIDTH] = _dot(h, w)

    gate = mods_ref[2, pl.ds(prev_b, 1), :]
    lo =jax.lax.broadcasted_iota(jnp.int32, (CHUNK, LANES), 1) < HEAD_DIM
    zero = jnp.zeros((CHUNK, LANES), F32)
    k_prev = [kcarry[g] for g in range(ATT_KV_HEADS)]
    v_prev = [vcarry[g] for g in range(ATT_KV_HEADS)]

    for c in range(n_chunks):
        rows = slice(c * CHUNK, (c + 1) * CHUNK)
        bias = bias_ref[seq_start] if c == 0 else bias_ref[0]
        att_c, att_s1, att_s2 = tab_ref[0, rows, :], tab_ref[1, rows, :], tab_ref[2, rows, :]
        ret_c, ret_s = tab_ref[3, rows, :], tab_ref[4, rows, :]

        def zs(col):
            return z_prev[rows, col:col + LANES]

        q_stack = jnp.concatenate(
            [(_rope_att(zs(OFF_Q + j * LANES), att_c, att_s1, att_s2) * (HEAD_DIM ** -0.5)).astype(BF16)
             for j in range(4)], axis=0)
        k = _rope_att(zs(OFF_KV), att_c, att_s1, att_s2)
        v = zs(OFF_KV + LANES)
        k_cur = [jnp.where(lo, k, zero).astype(BF16), jnp.where(lo, zero, k).astype(BF16)]
        v_cur = [jnp.where(lo, v, zero).astype(BF16), jnp.where(lo, zero, v).astype(BF16)]
        k_band = [jnp.concatenate([k_prev[g], k_cur[g]], axis=0) for g in range(ATT_KV_HEADS)]
        v_band = [jnp.concatenate([v_prev[g], v_cur[g]], axis=0) for g in range(ATT_KV_HEADS)]
        k_prev, v_prev = k_cur, v_cur
        if c == n_chunks - 1:
            kout_ref[...] = k
            vout_ref[...] = v
            for g in range(ATT_KV_HEADS):
                kcarry[g] = k_cur[g]
                vcarry[g] = v_cur[g]

        ret = []
        for hh in range(RET_HEADS):
            rq = _rope_ret(zs(OFF_RQ + hh * LANES), ret_c, ret_s)
            rk = _rope_ret(zs(OFF_RK + hh * LANES), ret_c, ret_s) * (RET_DK ** -0.5)
            ret.append(dict(
                q=rq.astype(BF16), qd=(rq * dec_ref[RET_HEADS + hh]).astype(BF16),
                k=rk.astype(BF16), kd=(rk * dec_ref[2 * RET_HEADS + hh]).astype(BF16),
                v=zs(OFF_RV + hh * LANES).astype(BF16)))

        sc_att = [_dot_nt(q_stack, k_band[g]) for g in range(ATT_KV_HEADS)]
        sc_ret = [_dot_nt(r["q"], r["k"]) for r in ret]
        fill(FILL_PLAN[c][0])

        att = None
        for g in range(ATT_KV_HEADS):
            ws = []
            for j in range(4):
                sj = sc_att[g][j * CHUNK:(j + 1) * CHUNK, :] + bias
                sink = sinks_ref[g * ATT_GROUP + j]
                mx = jnp.maximum(jnp.max(sj, axis=-1, keepdims=True), sink)
                p = jnp.exp(sj - mx)
                den = jnp.sum(p, axis=-1, keepdims=True) + jnp.exp(sink - mx)
                ws.append((p * (1.0 / den)).astype(BF16))
            o = _dot(jnp.concatenate(ws, axis=0), v_band[g])
            att = o if att is None else att + o
        for j in range(4):
            mix_s[rows, j * LANES:(j + 1) * LANES] = att[j * CHUNK:(j + 1) * CHUNK, :].astype(BF16)

        outs = []
        for hh, r in enumerate(ret):
            state = s_ref[hh]
            sc = sc_ret[hh] * dec_ref[hh]
            outs.append(_dot(sc.astype(BF16), r["v"]) + _dot(r["qd"], state.astype(BF16)))
            s_ref[hh] = gl[hh] * state + _dot_tn(r["kd"], r["v"])
        fill(FILL_PLAN[c][1])
        for hh in range(RET_HEADS):
            cols = slice(hh * LANES, (hh + 1) * LANES)
            o = outs[hh]
            mu = jnp.mean(o, axis=-1, keepdims=True)
            d = o - mu
            var = jnp.mean(d * d, axis=-1, keepdims=True)
            nrm = d * jax.lax.rsqrt(var + GN_EPS) * gnw_ref[:, cols]
            mix_s[rows, ATT_WIDTH + hh * LANES:ATT_WIDTH + (hh + 1) * LANES] = (
                _silu(zs(OFF_RG + hh * LANES)) * nrm).astype(BF16)

    y = _dot(mix_s[...], jnp.concatenate([wout_att_ref[...], wout_ret_ref[...]], axis=0))
    fill(n_fill)
    y_ref[...] = DEEPNORM_ALPHA * xp_ref[...] + (1.0 + gate) * y


def _mixer_kernel(sinks_ref, x_ref, mods_ref, tab_ref, xp_ref, dec_ref, bias_ref,
                  wq_ref, wrest_ref, wout_att_ref, wout_ret_ref, gnw_ref,
                  y_ref, kout_ref, vout_ref, s_ref,
                  z_a, z_b, kcarry, vcarry, mix_s, *, tile, ns, nt, gl):
    t = pl.program_id(0)
    prev_tile = jnp.maximum(t - 1, 0)
    sa = jax.lax.rem(prev_tile, ns)
    seq_start = (sa == 0).astype(jnp.int32)
    cur_b = jax.lax.div(jnp.minimum(t, nt - 1), ns)
    prev_b = jax.lax.div(prev_tile, ns)
    parity = jax.lax.rem(t, 2)

    @pl.when(t == 0)
    def _():
        z_b[...] = jnp.zeros(z_b.shape, z_b.dtype)
        kcarry[...] = jnp.zeros(kcarry.shape, kcarry.dtype)
        vcarry[...] = jnp.zeros(vcarry.shape, vcarry.dtype)

    @pl.when(sa == 0)
    def _():
        s_ref[...] = jnp.zeros_like(s_ref)

    step = functools.partial(
        _mixer_step, sinks_ref=sinks_ref, x_ref=x_ref, mods_ref=mods_ref, tab_ref=tab_ref, xp_ref=xp_ref,
        dec_ref=dec_ref, bias_ref=bias_ref, wq_ref=wq_ref, wrest_ref=wrest_ref,
        wout_att_ref=wout_att_ref, wout_ret_ref=wout_ret_ref,
        gnw_ref=gnw_ref, y_ref=y_ref, kout_ref=kout_ref,
        vout_ref=vout_ref, s_ref=s_ref, kcarry=kcarry, vcarry=vcarry, mix_s=mix_s,
        cur_b=cur_b, prev_b=prev_b, seq_start=seq_start, tile=tile, gl=gl)

    @pl.when(parity == 0)
    def _():
        step(z_b, z_a)

    @pl.when(parity == 1)
    def _():
        step(z_a, z_b)


def _mixer_prompt(x, mods, mod_row0, sinks, tab, dec, bias, w_q, w_rest, w_out_att, w_out_ret, gn_w):
    batch, seq, _ = x.shape
    assert mod_row0 % batch == 0 and batch % 8 == 0
    tile = PROMPT_TILE
    ns = seq // tile
    gl = tuple(float(v) for v in np.exp(_log_gamma() * CHUNK).astype(np.float32))
    nt = batch * ns
    const = lambda shape: pl.BlockSpec(shape, lambda t, *_: (0,) * len(shape),
                                       pipeline_mode=pl.Buffered(1))
    cur_b = lambda t: jnp.minimum(t, nt - 1) // ns
    cur_s = lambda t: jnp.minimum(t, nt - 1) % ns
    prev_b = lambda t: jnp.maximum(t - 1, 0) // ns
    prev_s = lambda t: jnp.maximum(t - 1, 0) % ns
    grid_spec = pltpu.PrefetchScalarGridSpec(
        num_scalar_prefetch=1,
        grid=(nt + 1,),
        in_specs=[
            pl.BlockSpec((None, tile, D_MODEL), lambda t, *_: (cur_b(t), cur_s(t), 0)),
            pl.BlockSpec((3, batch, D_MODEL), lambda t, *_: (0, mod_row0 // batch, 0),
                         pipeline_mode=pl.Buffered(1)),
            pl.BlockSpec((5, tile, LANES), lambda t, *_: (0, prev_s(t), 0)),
            pl.BlockSpec((None, tile, D_MODEL), lambda t, *_: (prev_b(t), prev_s(t), 0)),
            const((3 * RET_HEADS, CHUNK, LANES)),
            const((2, CHUNK, 2 * CHUNK)),
            const((D_MODEL, ATT_WIDTH)),
            const((D_MODEL, IN_WIDTH - ATT_WIDTH)),
            const((ATT_WIDTH, D_MODEL)),
            const((RET_WIDTH, D_MODEL)),
            const((1, RET_WIDTH)),
        ],
        out_specs=[
            pl.BlockSpec((None, tile, D_MODEL), lambda t, *_: (prev_b(t), prev_s(t), 0)),
            pl.BlockSpec((None, WINDOW, LANES), lambda t, *_: (prev_b(t), 0, 0)),
            pl.BlockSpec((None, WINDOW, LANES), lambda t, *_: (prev_b(t), 0, 0)),
            pl.BlockSpec((None, RET_HEADS, RET_DK, LANES), lambda t, *_: (prev_b(t), 0, 0, 0)),
        ],
        scratch_shapes=[
            pltpu.VMEM((tile, IN_WIDTH), F32),
            pltpu.VMEM((tile, IN_WIDTH), F32),
            pltpu.VMEM((ATT_KV_HEADS, CHUNK, LANES), BF16),
            pltpu.VMEM((ATT_KV_HEADS, CHUNK, LANES), BF16),
            pltpu.VMEM((tile, D_MODEL), BF16),
        ],
    )
    return pl.pallas_call(
        functools.partial(_mixer_kernel, tile=tile, ns=ns, nt=nt, gl=gl),
        out_shape=[
            jax.ShapeDtypeStruct((batch, seq, D_MODEL), F32),
            jax.ShapeDtypeStruct((batch, WINDOW, LANES), F32),
            jax.ShapeDtypeStruct((batch, WINDOW, LANES), F32),
            jax.ShapeDtypeStruct((batch, RET_HEADS, RET_DK, LANES), F32),
        ],
        grid_spec=grid_spec,
        compiler_params=pltpu.CompilerParams(
            dimension_semantics=("arbitrary",), vmem_limit_bytes=VMEM_LIMIT_BYTES),
        name="mixer_prompt",
    )(sinks, x, mods, tab, x, dec, bias, w_q, w_rest, w_out_att, w_out_ret, gn_w)


def _ffn_kernel(r_ref, mods_ref, wup_ref, wdown_ref, ln1w_ref, ln1b_ref, lnw_ref, lnb_ref, o_ref,
                *, parts, tiles_per_mod_row):
    rows = r_ref.shape[0] // parts
    sl = [slice(p * rows, (p + 1) * rows) for p in range(parts)]
    if tiles_per_mod_row:
        seq = jax.lax.div(pl.program_id(0), tiles_per_mod_row)
        mod = lambda k, p: mods_ref[k, pl.ds(seq, 1), :]
    else:
        mod = lambda k, p: mods_ref[k, sl[p], :]
    ys = [_layer_norm(r_ref[sl[p], :], ln1w_ref[...], ln1b_ref[...]) for p in range(parts)]
    hs = [(ys[p] * (1.0 + mod(1, p)) + mod(0, p)).astype(BF16) for p in range(parts)]
    acts = [(_silu(_dot(h, wup_ref[:, :D_FF])) * _dot(h, wup_ref[:, D_FF:])).astype(BF16) for h in hs]
    fs = [_dot(a, wdown_ref[...]) for a in acts]
    for p in range(parts):
        r = DEEPNORM_ALPHA * ys[p] + (1.0 + mod(2, p)) * fs[p]
        o_ref[sl[p], :] = _layer_norm(r, lnw_ref[...], lnb_ref[...])


def _ffn(y, mods, mod_row0, mod_rows, tiles_per_mod_row, w_up, w_down, ln1_w, ln1_b, ln_w, ln_b, tile, parts):
    rows, _ = y.shape
    nt = rows // tile
    assert mod_row0 % mod_rows == 0 and (tiles_per_mod_row > 0 or mod_rows == tile)
    const = lambda shape: pl.BlockSpec(shape, lambda t: (0,) * len(shape), pipeline_mode=pl.Buffered(1))
    return pl.pallas_call(
        functools.partial(_ffn_kernel, parts=parts, tiles_per_mod_row=tiles_per_mod_row),
        out_shape=jax.ShapeDtypeStruct(y.shape, F32),
        grid=(nt,),
        in_specs=[
            pl.BlockSpec((tile, D_MODEL), lambda t: (t, 0)),
            pl.BlockSpec((3, mod_rows, D_MODEL), lambda t: (1, mod_row0 // mod_rows, 0),
                         pipeline_mode=pl.Buffered(1)),
            const((D_MODEL, 2 * D_FF)),
            const((D_FF, D_MODEL)),
            const((1, D_MODEL)),
            const((1, D_MODEL)),
            const((1, D_MODEL)),
            const((1, D_MODEL)),
        ],
        out_specs=pl.BlockSpec((tile, D_MODEL), lambda t: (t, 0)),
        compiler_params=pltpu.CompilerParams(
            dimension_semantics=("arbitrary",), vmem_limit_bytes=VMEM_LIMIT_BYTES),
        name="ffn",
    )(y, mods, w_up, w_down, ln1_w, ln1_b, ln_w, ln_b)


def _sample_kernel(sinks_ref, x_ref, mod_ref, tab_ref, wq_ref, wrest_ref, wout_att_ref, wout_ret_ref,
                   gnw_ref,
                   ck_ref, cv_ref, st_ref,
                   y_ref, ko_ref, vo_ref, so_ref,
                   z_s, mix_s, *, g1, n_blocks):
    i = pl.program_id(0)
    bb = SAMPLE_BLOCK

    @pl.when(i == 0)
    def _():
        m = mod_ref[...]
        h = (x_ref[...] * (1.0 + m[1]) + m[0]).astype(BF16)
        z = jnp.concatenate([_dot(h, wq_ref[...]), _dot(h, wrest_ref[...])], axis=1)
        att_c, att_s1, att_s2 = tab_ref[0], tab_ref[1], tab_ref[2]
        ret_c, ret_s = tab_ref[3], tab_ref[4]
        for j in range(4):
            col = OFF_Q + j * LANES
            z_s[:, col:col + LANES] = _rope_att(z[:, col:col + LANES], att_c, att_s1, att_s2) * (HEAD_DIM ** -0.5)
            col = OFF_RQ + j * LANES
            z_s[:, col:col + LANES] = _rope_ret(z[:, col:col + LANES], ret_c, ret_s)
            col = OFF_RK + j * LANES
            z_s[:, col:col + LANES] = _rope_ret(z[:, col:col + LANES], ret_c, ret_s) * (RET_DK ** -0.5)
        z_s[:, OFF_KV:OFF_KV + LANES] = _rope_att(z[:, OFF_KV:OFF_KV + LANES], att_c, att_s1, att_s2)
        z_s[:, OFF_KV + LANES:OFF_RQ] = z[:, OFF_KV + LANES:OFF_RQ]
        z_s[:, OFF_RV:OFF_RG] = z[:, OFF_RV:OFF_RG]
        z_s[:, OFF_RG:] = _silu(z[:, OFF_RG:])

    lane_w = jax.lax.broadcasted_iota(jnp.int32, (LANES, WINDOW), 1)
    last_lane = lane_w == WINDOW - 1
    lo_row = lane_w[0:1, :] < HEAD_DIM
    r4 = jax.lax.broadcasted_iota(jnp.int32, (ATT_GROUP, 1), 0)
    r8 = jax.lax.broadcasted_iota(jnp.int32, (ATT_HEADS, 1), 0)
    sink8 = jnp.zeros((ATT_HEADS, 1), F32)
    for hd in range(ATT_HEADS):
        sink8 = jnp.where(r8 == hd, sinks_ref[hd], sink8)
    head_blk = (jax.lax.broadcasted_iota(jnp.int32, (16, RET_WIDTH), 1) // LANES
                == jax.lax.broadcasted_iota(jnp.int32, (16, RET_WIDTH), 0))
    g1_col = jnp.where(r4 == 0, g1[0], jnp.where(r4 == 1, g1[1], jnp.where(r4 == 2, g1[2], g1[3])))

    def rows16(a):
        return jnp.concatenate([a, jnp.zeros((16 - a.shape[0], a.shape[1]), F32)], axis=0)

    def heads(row, off):
        return jnp.concatenate([row[:, off + j * LANES:off + (j + 1) * LANES] for j in range(4)], axis=0)

    def block_rows(a4):
        wide = jnp.concatenate([rows16(a4)] * RET_HEADS, axis=1)
        return jnp.where(head_blk, wide, jnp.zeros_like(wide))

    r0 = pl.multiple_of(i * bb, bb)
    zblk = z_s[pl.ds(r0, bb), :]
    rows = [zblk[b:b + 1, :] for b in range(bb)]

    kn = [r[:, OFF_KV:OFF_KV + LANES] for r in rows]
    vn = [r[:, OFF_KV + LANES:OFF_RQ] for r in rows]
    kn_t = zblk[:, OFF_KV:OFF_KV + LANES].T
    vn_t = zblk[:, OFF_KV + LANES:OFF_RQ].T
    q8 = []
    for r in rows:
        q4 = heads(r, OFF_Q)
        zero4 = jnp.zeros_like(q4)
        q8.append(jnp.concatenate([jnp.where(lo_row, q4, zero4), jnp.where(lo_row, zero4, q4)], axis=0))
    sc, oh = [], []
    for b in range(bb):
        sc.append(_dot(rows16(q8[b]).astype(BF16), ck_ref[b].astype(BF16))[0:ATT_HEADS])
    for b in range(bb):
        rq = heads(rows[b], OFF_RQ)
        rk = heads(rows[b], OFF_RK)
        rv = heads(rows[b], OFF_RV)
        state = st_ref[b].reshape(RET_WIDTH, LANES)
        o2 = _dot(block_rows(rq * g1_col).astype(BF16), state.astype(BF16))[0:RET_HEADS]
        outer = _dot_tn(block_rows(rk).astype(BF16), rows16(rv).astype(BF16))
        for hh in range(RET_HEADS):
            so_ref[b, hh] = g1[hh] * st_ref[b, hh] + outer[hh * RET_DK:(hh + 1) * RET_DK, :]
        o = jnp.sum(rq * rk, axis=-1, keepdims=True) * rv + o2
        mu = jnp.mean(o, axis=-1, keepdims=True)
        d = o - mu
        var = jnp.mean(d * d, axis=-1, keepdims=True)
        oh.append(heads(rows[b], OFF_RG) * (d * jax.lax.rsqrt(var + GN_EPS) * gnw_ref[...]))

    ws, wn = [], []
    for b in range(bb):
        sn = jnp.sum(q8[b] * kn[b], axis=-1, keepdims=True)
        mx = jnp.maximum(jnp.maximum(jnp.max(sc[b], axis=-1, keepdims=True), sn), sink8)
        p = jnp.exp(sc[b] - mx)
        pn = jnp.exp(sn - mx)
        inv = 1.0 / (jnp.sum(p, axis=-1, keepdims=True) + pn + jnp.exp(sink8 - mx))
        ws.append(rows16(p * inv).astype(BF16))
        wn.append(pn * inv)

    att = []
    for b in range(bb):
        o8 = _dot_nt(ws[b], cv_ref[b].astype(BF16))[0:ATT_HEADS] + wn[b] * vn[b]
        att.append(jnp.where(lo_row, o8[0:ATT_GROUP], o8[ATT_GROUP:]))

    for b in range(bb):
        ko_ref[b] = jnp.where(last_lane, kn_t[:, b:b + 1], pltpu.roll(ck_ref[b], WINDOW - 1, axis=1))
        vo_ref[b] = jnp.where(last_lane, vn_t[:, b:b + 1], pltpu.roll(cv_ref[b], WINDOW - 1, axis=1))
    blk_rows = pl.ds(r0, bb)
    for j in range(4):
        mix_s[blk_rows, j * LANES:(j + 1) * LANES] = jnp.concatenate(
            [a[j:j + 1, :] for a in att], axis=0)
        mix_s[blk_rows, ATT_WIDTH + j * LANES:ATT_WIDTH + (j + 1) * LANES] = jnp.concatenate(
            [o[j:j + 1, :] for o in oh], axis=0)

    @pl.when(i == n_blocks - 1)
    def _():
        y = (_dot(mix_s[:, :ATT_WIDTH].astype(BF16), wout_att_ref[...])
             + _dot(mix_s[:, ATT_WIDTH:].astype(BF16), wout_ret_ref[...]))
        y_ref[...] = DEEPNORM_ALPHA * x_ref[...] + (1.0 + mod_ref[2]) * y


def _sample_step(sinks, x, mods, tab, w_q, w_rest, w_out_att, w_out_ret, gn_w4,
                 cache_k, cache_v, state):
    n = x.shape[0]
    bb = SAMPLE_BLOCK
    n_blocks = n // bb
    g1 = tuple(float(v) for v in np.exp(_log_gamma()).astype(np.float32))
    full = lambda *shape: pl.BlockSpec(shape, lambda i, *_: (0,) * len(shape))
    blk = lambda *shape: pl.BlockSpec((bb,) + shape, lambda i, *_: (i,) + (0,) * len(shape))
    grid_spec = pltpu.PrefetchScalarGridSpec(
        num_scalar_prefetch=1,
        grid=(n_blocks,),
        in_specs=[
            full(n, D_MODEL), full(3, n, D_MODEL), full(5, 1, LANES),
            full(D_MODEL, ATT_WIDTH), full(D_MODEL, IN_WIDTH - ATT_WIDTH),
            full(ATT_WIDTH, D_MODEL), full(RET_WIDTH, D_MODEL),
            full(RET_HEADS, LANES),
            blk(WINDOW, LANES), blk(WINDOW, LANES), blk(RET_HEADS, RET_DK, LANES),
        ],
        out_specs=[full(n, D_MODEL), blk(WINDOW, LANES), blk(WINDOW, LANES),
                   blk(RET_HEADS, RET_DK, LANES)],
        scratch_shapes=[pltpu.VMEM((n, IN_WIDTH), F32), pltpu.VMEM((n, D_MODEL), F32)],
    )
    return pl.pallas_call(
        functools.partial(_sample_kernel, g1=g1, n_blocks=n_blocks),
        out_shape=[
            jax.ShapeDtypeStruct((n, D_MODEL), F32),
            jax.ShapeDtypeStruct((n, WINDOW, LANES), F32),
            jax.ShapeDtypeStruct((n, WINDOW, LANES), F32),
            jax.ShapeDtypeStruct((n, RET_HEADS, RET_DK, LANES), F32),
        ],
        grid_spec=grid_spec,
        compiler_params=pltpu.CompilerParams(
            dimension_semantics=("arbitrary",), vmem_limit_bytes=VMEM_LIMIT_BYTES),
        name="sample_step",
    )(sinks, x, mods, tab, w_q, w_rest, w_out_att, w_out_ret, gn_w4, cache_k, cache_v, state)


def _slab_order(w, axis):
    shape = w.shape
    pre, post = shape[:axis], shape[axis + 1:]
    w = w.reshape(pre + (ATT_KV_HEADS, ATT_GROUP, HEAD_DIM) + post)
    w = jnp.swapaxes(w, axis, axis + 1)
    return w.reshape(shape)


def kernel(x_prompt, x_sample, c_prompt, c_sample, cache_k, cache_v, state_ret, w_ada_mix, b_ada_mix, w_in, att_sinks, ret_gn_w, w_out, ln1_w, ln1_b, w_ada_ffn, b_ada_ffn, w_up, w_down, ln2_w, ln2_b):
    batch, seq, _ = x_prompt.shape
    n_s = x_sample.shape[0]
    l = 0
    assert w_in.shape[0] == DEPTH == 1

    w_in_f = w_in.reshape(D_MODEL, IN_WIDTH)
    w_out_f = w_out.reshape(D_MODEL, D_MODEL)
    w_q = _slab_order(w_in_f[:, :ATT_WIDTH].astype(BF16), 1)
    w_rest = w_in_f[:, ATT_WIDTH:].astype(BF16)
    w_out_att = _slab_order(w_out_f[:ATT_WIDTH].astype(BF16), 0)
    w_out_ret = w_out_f[ATT_WIDTH:].astype(BF16)
    w_up_b = w_up.reshape(D_MODEL, 2 * D_FF).astype(BF16)
    w_down_b = w_down.reshape(D_FF, D_MODEL).astype(BF16)
    sinks = att_sinks[l]
    gn_w = ret_gn_w[l].reshape(1, RET_WIDTH)
    ln1w, ln1b = ln1_w[l].reshape(1, D_MODEL), ln1_b[l].reshape(1, D_MODEL)
    ln2w, ln2b = ln2_w[l].reshape(1, D_MODEL), ln2_b[l].reshape(1, D_MODEL)

    tab_p = jnp.asarray(_rope_tables(np.arange(seq)))
    tab_s = jnp.asarray(_rope_tables(np.array([PAST_LEN])))
    dec = jnp.asarray(_decay_tables())
    bias = jnp.asarray(_band_bias())

    mods = _ada_mod(c_sample, c_prompt, w_ada_mix.reshape(D_MODEL, 3 * D_MODEL),
                    w_ada_ffn.reshape(D_MODEL, 3 * D_MODEL), b_ada_mix, b_ada_ffn)

    y1p, kp, vp, sp = _mixer_prompt(x_prompt, mods, n_s, sinks, tab_p, dec, bias, w_q, w_rest,
                                    w_out_att, w_out_ret, gn_w)
    yp = _ffn(y1p.reshape(batch * seq, D_MODEL), mods, n_s, batch, seq // FFN_TILE, w_up_b, w_down_b,
              ln1w, ln1b, ln2w, ln2b, FFN_TILE, FFN_PARTS).reshape(batch, seq, D_MODEL)

    xs = x_sample.reshape(n_s, D_MODEL)
    y1s, ks, vs, ss = _sample_step(
        sinks, xs, mods, tab_s, w_q, w_rest, w_out_att, w_out_ret, ret_gn_w.reshape(RET_HEADS, LANES),
        jnp.swapaxes(cache_k.reshape(n_s, WINDOW, LANES), 1, 2),
        jnp.swapaxes(cache_v.reshape(n_s, WINDOW, LANES), 1, 2),
        state_ret.reshape(n_s, RET_HEADS, RET_DK, LANES))
    ks, vs = jnp.swapaxes(ks, 1, 2), jnp.swapaxes(vs, 1, 2)
    ys = _ffn(y1s, mods, 0, n_s, 0, w_up_b, w_down_b, ln1w, ln1b, ln2w, ln2b, n_s, 1)

    kv_shape = (1, batch, WINDOW, ATT_KV_HEADS, HEAD_DIM)
    kvs_shape = (1, n_s, WINDOW, ATT_KV_HEADS, HEAD_DIM)
    return (yp, ys.reshape(n_s, 1, D_MODEL),
            kp.reshape(kv_shape), vp.reshape(kv_shape), sp[None],
            ks.reshape(kvs_shape), vs.reshape(kvs_shape), ss[None])
```

```python
import functools
import math

import numpy as np
import jax
import jax.numpy as jnp
from jax.experimental import pallas as pl
from jax.experimental.pallas import tpu as pltpu

D_MODEL = 1024
WINDOW = 128
CHUNK = 128
ATT_HEADS = 8
ATT_KV_HEADS = 2
ATT_GROUP = ATT_HEADS // ATT_KV_HEADS
HEAD_DIM = 64
ROPE_DIMS = HEAD_DIM // 4
ROPE_THETA = 500000.0
RET_HEADS = 4
RET_DK = 128
RET_THETA = 10000.0
ATT_WIDTH = ATT_HEADS * HEAD_DIM
KV_WIDTH = ATT_KV_HEADS * HEAD_DIM
RET_WIDTH = RET_HEADS * RET_DK
IN_WIDTH = ATT_WIDTH + 2 * KV_WIDTH + 4 * RET_WIDTH
D_FF = 2816
DEPTH = 1
DEEPNORM_ALPHA = (2 * DEPTH) ** 0.25
LN_EPS = 1e-5
GN_EPS = 1e-6
PAST_LEN = 16384

OFF_Q = 0
OFF_KV = ATT_WIDTH
OFF_RQ = OFF_KV + 2 * KV_WIDTH
OFF_RK = OFF_RQ + RET_WIDTH
OFF_RV = OFF_RK + RET_WIDTH
OFF_RG = OFF_RV + RET_WIDTH

LANES = 128
VMEM_LIMIT_BYTES = 56 * 1024 * 1024

PROMPT_TILE = 512
FILL_WIDTH = 256
FILL_PLAN = ((2, 1), (2, 1), (2, 1), (2, 0))
FFN_TILE = 1024
FFN_PARTS = 4
SAMPLE_BLOCK = 8

BF16 = jnp.bfloat16
F32 = jnp.float32


def _dot(a, b):
    return jnp.dot(a, b, preferred_element_type=F32)


def _dot_nt(a, b):
    return jax.lax.dot_general(a, b, (((1,), (1,)), ((), ())), preferred_element_type=F32)


def _dot_tn(a, b):
    return jax.lax.dot_general(a, b, (((0,), (0,)), ((), ())), preferred_element_type=F32)


def _sigmoid(x):
    return 1.0 / (1.0 + jnp.exp(-x))


def _silu(x):
    return x * _sigmoid(x)


def _layer_norm(r, w, b):
    mu = jnp.mean(r, axis=-1, keepdims=True)
    d = r - mu
    var = jnp.mean(d * d, axis=-1, keepdims=True)
    return d * jax.lax.rsqrt(var + LN_EPS) * w + b


def _rope_att(slab, c, s1, s2):
    return slab * c + pltpu.roll(slab, LANES - 8, axis=1) * s1 + pltpu.roll(slab, 8, axis=1) * s2


def _rope_ret(slab, c, s):
    return slab * c + pltpu.roll(slab, LANES // 2, axis=1) * s


def _rope_tables(pos):
    pos = np.asarray(pos, np.float64)[:, None]
    half = ROPE_DIMS // 2
    inv = ROPE_THETA ** (-np.arange(half, dtype=np.float64) / half)
    ang = pos * inv[None, :]
    cos, sin = np.cos(ang), np.sin(ang)
    n = pos.shape[0]
    head_c = np.ones((n, HEAD_DIM)); head_s1 = np.zeros((n, HEAD_DIM)); head_s2 = np.zeros((n, HEAD_DIM))
    head_c[:, :half] = cos; head_c[:, half:2 * half] = cos
    head_s1[:, :half] = -sin
    head_s2[:, half:2 * half] = sin
    att = [np.tile(t, (1, LANES // HEAD_DIM)) for t in (head_c, head_s1, head_s2)]
    rhalf = RET_DK // 2
    rinv = RET_THETA ** (-np.arange(rhalf, dtype=np.float64) / rhalf)
    rang = pos * rinv[None, :]
    rc = np.concatenate([np.cos(rang), np.cos(rang)], axis=1)
    rs = np.concatenate([-np.sin(rang), np.sin(rang)], axis=1)
    return np.stack(att + [rc, rs], axis=0).astype(np.float32)


def _log_gamma():
    lin = np.linspace(math.log(1.0 / 32), math.log(1.0 / 512), RET_HEADS)
    return np.log1p(-np.exp(lin))


def _decay_tables():
    lg = _log_gamma()
    idx = np.arange(CHUNK, dtype=np.float64)
    diff = idx[:, None] - idx[None, :]
    dmat = np.where(diff[None] >= 0, np.exp(lg[:, None, None] * np.maximum(diff, 0.0)[None]), 0.0)
    dq = np.exp(lg[:, None] * (idx[None, :] + 1.0))[:, :, None] * np.ones((1, 1, LANES))
    dk = np.exp(lg[:, None] * (CHUNK - 1.0 - idx[None, :]))[:, :, None] * np.ones((1, 1, LANES))
    return np.concatenate([dmat, dq, dk], axis=0).astype(np.float32)


def _band_bias():
    i = np.arange(CHUNK)[:, None]
    j = np.arange(2 * CHUNK)[None, :]
    valid = (j >= i) & (j <= i + WINDOW)
    neg = np.float32(-1e30)
    b0 = np.where(valid, 0.0, neg)
    b1 = np.where(valid & (j >= CHUNK), 0.0, neg)
    return np.stack([b0, b1], axis=0).astype(np.float32)


def _ada_kernel(cs_ref, cp_ref, w1_ref, w2_ref, b1_ref, b2_ref, o_ref):
    rows = cs_ref.shape[0] + cp_ref.shape[0]
    pad = jnp.zeros((-rows % 16, D_MODEL), F32)
    a = _silu(jnp.concatenate([cs_ref[...], cp_ref[...], pad], axis=0)).astype(BF16)

    def emit(w_ref, b_ref):
        o_ref[...] = _dot(a, w_ref[...].astype(BF16))[0:rows] + b_ref[...]

    @pl.when(pl.program_id(0) < 3)
    def _():
        emit(w1_ref, b1_ref)

    @pl.when(pl.program_id(0) >= 3)
    def _():
        emit(w2_ref, b2_ref)


def _ada_mod(c_sample, c_prompt, w_mix, w_ffn, b_mix, b_ffn):
    rows = c_sample.shape[0] + c_prompt.shape[0]
    d = D_MODEL
    return pl.pallas_call(
        _ada_kernel,
        out_shape=jax.ShapeDtypeStruct((6, rows, d), F32),
        grid=(6,),
        in_specs=[
            pl.BlockSpec(c_sample.shape, lambda j: (0, 0)),
            pl.BlockSpec(c_prompt.shape, lambda j: (0, 0)),
            pl.BlockSpec((d, d), lambda j: (0, jnp.minimum(j, 2))),
            pl.BlockSpec((d, d), lambda j: (0, jnp.maximum(j - 3, 0))),
            pl.BlockSpec((1, d), lambda j: (0, jnp.minimum(j, 2))),
            pl.BlockSpec((1, d), lambda j: (0, jnp.maximum(j - 3, 0))),
        ],
        out_specs=pl.BlockSpec((None, rows, d), lambda j: (j, 0, 0)),
        compiler_params=pltpu.CompilerParams(
            dimension_semantics=("arbitrary",), vmem_limit_bytes=VMEM_LIMIT_BYTES),
        name="ada_mod",
    )(c_sample, c_prompt, w_mix, w_ffn, b_mix, b_ffn)


def _mixer_step(z_prev, z_cur, sinks_ref, x_ref, mods_ref, tab_ref, xp_ref, dec_ref, bias_ref,
                wq_ref, wrest_ref, wout_att_ref, wout_ret_ref, gnw_ref,
                y_ref, kout_ref, vout_ref, s_ref, kcarry, vcarry, mix_s,
                *, cur_b, prev_b, seq_start, tile, gl):
    n_chunks = tile // CHUNK
    x = x_ref[...]
    h = (x * (1.0 + mods_ref[1, pl.ds(cur_b, 1), :]) + mods_ref[0, pl.ds(cur_b, 1), :]).astype(BF16)
    n_fill = IN_WIDTH // FILL_WIDTH
    pending = list(range(n_fill))

    def fill(count):
        for _ in range(count):
            if pending:
                i = pending.pop(0)
                col = i * FILL_WIDTH
                w = (wq_ref[:, col:col + FILL_WIDTH] if col < ATT_WIDTH
                     else wrest_ref[:, col - ATT_WIDTH:col - ATT_WIDTH + FILL_WIDTH])
                z_cur[:, col:col + FILL_WIDTH] = _dot(h, w)

    gate = mods_ref[2, pl.ds(prev_b, 1), :]
    lo = jax.lax.broadcasted_iota(jnp.int32, (CHUNK, LANES), 1) < HEAD_DIM
    lo_rows = jax.lax.broadcasted_iota(jnp.int32, (LANES, CHUNK), 0) < HEAD_DIM
    zero = jnp.zeros((CHUNK, LANES), F32)
    k_prev = [kcarry[g] for g in range(ATT_KV_HEADS)]
    v_prev = [vcarry[g] for g in range(ATT_KV_HEADS)]

    for c in range(n_chunks):
        rows = slice(c * CHUNK, (c + 1) * CHUNK)
        bias = bias_ref[seq_start] if c == 0 else bias_ref[0]
        att_c, att_s1, att_s2 = tab_ref[0, rows, :], tab_ref[1, rows, :], tab_ref[2, rows, :]
        ret_c, ret_s = tab_ref[3, rows, :], tab_ref[4, rows, :]

        def zs(col):
            return z_prev[rows, col:col + LANES]

        q_stack = jnp.concatenate(
            [(_rope_att(zs(OFF_Q + j * LANES), att_c, att_s1, att_s2) * (HEAD_DIM ** -0.5)).astype(BF16)
             for j in range(4)], axis=0)
        k = _rope_att(zs(OFF_KV), att_c, att_s1, att_s2)
        v = zs(OFF_KV + LANES)
        k_t = k.T
        k_cur = [jnp.where(lo_rows, k_t, zero).astype(BF16), jnp.where(lo_rows, zero, k_t).astype(BF16)]
        v_cur = [jnp.where(lo, v, zero).astype(BF16), jnp.where(lo, zero, v).astype(BF16)]
        k_band = [jnp.concatenate([k_prev[g], k_cur[g]], axis=1) for g in range(ATT_KV_HEADS)]
        v_band = [jnp.concatenate([v_prev[g], v_cur[g]], axis=0) for g in range(ATT_KV_HEADS)]
        k_prev, v_prev = k_cur, v_cur
        if c == n_chunks - 1:
            kout_ref[...] = k
            vout_ref[...] = v
            for g in range(ATT_KV_HEADS):
                kcarry[g] = k_cur[g]
                vcarry[g] = v_cur[g]

        ret = []
        for hh in range(RET_HEADS):
            rq = _rope_ret(zs(OFF_RQ + hh * LANES), ret_c, ret_s)
            rk = _rope_ret(zs(OFF_RK + hh * LANES), ret_c, ret_s) * (RET_DK ** -0.5)
            ret.append(dict(
                q=rq.astype(BF16), qd=(rq * dec_ref[RET_HEADS + hh]).astype(BF16),
                k=rk.astype(BF16), kd=(rk * dec_ref[2 * RET_HEADS + hh]).astype(BF16),
                v=zs(OFF_RV + hh * LANES).astype(BF16)))

        sc_att = [_dot(q_stack, k_band[g]) for g in range(ATT_KV_HEADS)]
        sc_ret = [_dot_nt(r["q"], r["k"]) for r in ret]
        fill(FILL_PLAN[c][0])

        att = None
        for g in range(ATT_KV_HEADS):
            ws = []
            for j in range(4):
                sj = sc_att[g][j * CHUNK:(j + 1) * CHUNK, :] + bias
                sink = sinks_ref[g * ATT_GROUP + j]
                mx = jnp.maximum(jnp.max(sj, axis=-1, keepdims=True), sink)
                p = jnp.exp(sj - mx)
                den = jnp.sum(p, axis=-1, keepdims=True) + jnp.exp(sink - mx)
                ws.append((p * (1.0 / den)).astype(BF16))
            o = _dot(jnp.concatenate(ws, axis=0), v_band[g])
            att = o if att is None else att + o
        for j in range(4):
            mix_s[rows, j * LANES:(j + 1) * LANES] = att[j * CHUNK:(j + 1) * CHUNK, :].astype(BF16)

        outs = []
        for hh, r in enumerate(ret):
            state = s_ref[hh]
            sc = sc_ret[hh] * dec_ref[hh]
            outs.append(_dot(sc.astype(BF16), r["v"]) + _dot(r["qd"], state.astype(BF16)))
            s_ref[hh] = gl[hh] * state + _dot_tn(r["kd"], r["v"])
        fill(FILL_PLAN[c][1])
        for hh in range(RET_HEADS):
            cols = slice(hh * LANES, (hh + 1) * LANES)
            o = outs[hh]
            mu = jnp.mean(o, axis=-1, keepdims=True)
            d = o - mu
            var = jnp.mean(d * d, axis=-1, keepdims=True)
            nrm = d * jax.lax.rsqrt(var + GN_EPS) * gnw_ref[:, cols]
            mix_s[rows, ATT_WIDTH + hh * LANES:ATT_WIDTH + (hh + 1) * LANES] = (
                _silu(zs(OFF_RG + hh * LANES)) * nrm).astype(BF16)

    y = _dot(mix_s[...], jnp.concatenate([wout_att_ref[...], wout_ret_ref[...]], axis=0))
    fill(n_fill)
    y_ref[...] = DEEPNORM_ALPHA * xp_ref[...] + (1.0 + gate) * y


def _mixer_kernel(sinks_ref, x_ref, mods_ref, tab_ref, xp_ref, dec_ref, bias_ref,
                  wq_ref, wrest_ref, wout_att_ref, wout_ret_ref, gnw_ref,
                  y_ref, kout_ref, vout_ref, s_ref,
                  z_a, z_b, kcarry, vcarry, mix_s, *, tile, ns, nt, gl):
    t = pl.program_id(0)
    prev_tile = jnp.maximum(t - 1, 0)
    sa = jax.lax.rem(prev_tile, ns)
    seq_start = (sa == 0).astype(jnp.int32)
    cur_b = jax.lax.div(jnp.minimum(t, nt - 1), ns)
    prev_b = jax.lax.div(prev_tile, ns)
    parity = jax.lax.rem(t, 2)

    @pl.when(t == 0)
    def _():
        z_b[...] = jnp.zeros(z_b.shape, z_b.dtype)
        kcarry[...] = jnp.zeros(kcarry.shape, kcarry.dtype)
        vcarry[...] = jnp.zeros(vcarry.shape, vcarry.dtype)

    @pl.when(sa == 0)
    def _():
        s_ref[...] = jnp.zeros_like(s_ref)

    step = functools.partial(
        _mixer_step, sinks_ref=sinks_ref, x_ref=x_ref, mods_ref=mods_ref, tab_ref=tab_ref, xp_ref=xp_ref,
        dec_ref=dec_ref, bias_ref=bias_ref, wq_ref=wq_ref, wrest_ref=wrest_ref,
        wout_att_ref=wout_att_ref, wout_ret_ref=wout_ret_ref,
        gnw_ref=gnw_ref, y_ref=y_ref, kout_ref=kout_ref,
        vout_ref=vout_ref, s_ref=s_ref, kcarry=kcarry, vcarry=vcarry, mix_s=mix_s,
        cur_b=cur_b, prev_b=prev_b, seq_start=seq_start, tile=tile, gl=gl)

    @pl.when(parity == 0)
    def _():
        step(z_b, z_a)

    @pl.when(parity == 1)
    def _():
        step(z_a, z_b)


def _mixer_prompt(x, mods, mod_row0, sinks, tab, dec, bias, w_q, w_rest, w_out_att, w_out_ret, gn_w):
    batch, seq, _ = x.shape
    assert mod_row0 % batch == 0 and batch % 8 == 0
    tile = PROMPT_TILE
    ns = seq // tile
    gl = tuple(float(v) for v in np.exp(_log_gamma() * CHUNK).astype(np.float32))
    nt = batch * ns
    const = lambda shape: pl.BlockSpec(shape, lambda t, *_: (0,) * len(shape),
                                       pipeline_mode=pl.Buffered(1))
    cur_b = lambda t: jnp.minimum(t, nt - 1) // ns
    cur_s = lambda t: jnp.minimum(t, nt - 1) % ns
    prev_b = lambda t: jnp.maximum(t - 1, 0) // ns
    prev_s = lambda t: jnp.maximum(t - 1, 0) % ns
    grid_spec = pltpu.PrefetchScalarGridSpec(
        num_scalar_prefetch=1,
        grid=(nt + 1,),
        in_specs=[
            pl.BlockSpec((None, tile, D_MODEL), lambda t, *_: (cur_b(t), cur_s(t), 0)),
            pl.BlockSpec((3, batch, D_MODEL), lambda t, *_: (0, mod_row0 // batch, 0),
                         pipeline_mode=pl.Buffered(1)),
            pl.BlockSpec((5, tile, LANES), lambda t, *_: (0, prev_s(t), 0)),
            pl.BlockSpec((None, tile, D_MODEL), lambda t, *_: (prev_b(t), prev_s(t), 0)),
            const((3 * RET_HEADS, CHUNK, LANES)),
            const((2, CHUNK, 2 * CHUNK)),
            const((D_MODEL, ATT_WIDTH)),
            const((D_MODEL, IN_WIDTH - ATT_WIDTH)),
            const((ATT_WIDTH, D_MODEL)),
            const((RET_WIDTH, D_MODEL)),
            const((1, RET_WIDTH)),
        ],
        out_specs=[
            pl.BlockSpec((None, tile, D_MODEL), lambda t, *_: (prev_b(t), prev_s(t), 0)),
            pl.BlockSpec((None, WINDOW, LANES), lambda t, *_: (prev_b(t), 0, 0)),
            pl.BlockSpec((None, WINDOW, LANES), lambda t, *_: (prev_b(t), 0, 0)),
            pl.BlockSpec((None, RET_HEADS, RET_DK, LANES), lambda t, *_: (prev_b(t), 0, 0, 0)),
        ],
        scratch_shapes=[
            pltpu.VMEM((tile, IN_WIDTH), F32),
            pltpu.VMEM((tile, IN_WIDTH), F32),
            pltpu.VMEM((ATT_KV_HEADS, CHUNK, LANES), BF16),
            pltpu.VMEM((ATT_KV_HEADS, CHUNK, LANES), BF16),
            pltpu.VMEM((tile, D_MODEL), BF16),
        ],
    )
    return pl.pallas_call(
        functools.partial(_mixer_kernel, tile=tile, ns=ns, nt=nt, gl=gl),
        out_shape=[
            jax.ShapeDtypeStruct((batch, seq, D_MODEL), F32),
            jax.ShapeDtypeStruct((batch, WINDOW, LANES), F32),
            jax.ShapeDtypeStruct((batch, WINDOW, LANES), F32),
            jax.ShapeDtypeStruct((batch, RET_HEADS, RET_DK, LANES), F32),
        ],
        grid_spec=grid_spec,
        compiler_params=pltpu.CompilerParams(
            dimension_semantics=("arbitrary",), vmem_limit_bytes=VMEM_LIMIT_BYTES),
        name="mixer_prompt",
    )(sinks, x, mods, tab, x, dec, bias, w_q, w_rest, w_out_att, w_out_ret, gn_w)


def _ffn_kernel(r_ref, mods_ref, wup_ref, wdown_ref, ln1w_ref, ln1b_ref, lnw_ref, lnb_ref, o_ref,
                *, parts, tiles_per_mod_row):
    rows = r_ref.shape[0] // parts
    sl = [slice(p * rows, (p + 1) * rows) for p in range(parts)]
    if tiles_per_mod_row:
        seq = jax.lax.div(pl.program_id(0), tiles_per_mod_row)
        mod = lambda k, p: mods_ref[k, pl.ds(seq, 1), :]
    else:
        mod = lambda k, p: mods_ref[k, sl[p], :]
    ys = [_layer_norm(r_ref[sl[p], :], ln1w_ref[...], ln1b_ref[...]) for p in range(parts)]
    hs = [(ys[p] * (1.0 + mod(1, p)) + mod(0, p)).astype(BF16) for p in range(parts)]
    acts = [(_silu(_dot(h, wup_ref[:, :D_FF])) * _dot(h, wup_ref[:, D_FF:])).astype(BF16) for h in hs]
    fs = [_dot(a, wdown_ref[...]) for a in acts]
    for p in range(parts):
        r = DEEPNORM_ALPHA * ys[p] + (1.0 + mod(2, p)) * fs[p]
        o_ref[sl[p], :] = _layer_norm(r, lnw_ref[...], lnb_ref[...])


def _ffn(y, mods, mod_row0, mod_rows, tiles_per_mod_row, w_up, w_down, ln1_w, ln1_b, ln_w, ln_b, tile, parts):
    rows, _ = y.shape
    nt = rows // tile
    assert mod_row0 % mod_rows == 0 and (tiles_per_mod_row > 0 or mod_rows == tile)
    const = lambda shape: pl.BlockSpec(shape, lambda t: (0,) * len(shape), pipeline_mode=pl.Buffered(1))
    return pl.pallas_call(
        functools.partial(_ffn_kernel, parts=parts, tiles_per_mod_row=tiles_per_mod_row),
        out_shape=jax.ShapeDtypeStruct(y.shape, F32),
        grid=(nt,),
        in_specs=[
            pl.BlockSpec((tile, D_MODEL), lambda t: (t, 0)),
            pl.BlockSpec((3, mod_rows, D_MODEL), lambda t: (1, mod_row0 // mod_rows, 0),
                         pipeline_mode=pl.Buffered(1)),
            const((D_MODEL, 2 * D_FF)),
            const((D_FF, D_MODEL)),
            const((1, D_MODEL)),
            const((1, D_MODEL)),
            const((1, D_MODEL)),
            const((1, D_MODEL)),
        ],
        out_specs=pl.BlockSpec((tile, D_MODEL), lambda t: (t, 0)),
        compiler_params=pltpu.CompilerParams(
            dimension_semantics=("arbitrary",), vmem_limit_bytes=VMEM_LIMIT_BYTES),
        name="ffn",
    )(y, mods, w_up, w_down, ln1_w, ln1_b, ln_w, ln_b)


def _sample_kernel(sinks_ref, x_ref, mod_ref, tab_ref, wq_ref, wrest_ref, wout_att_ref, wout_ret_ref,
                   gnw_ref,
                   ck_ref, cv_ref, st_ref,
                   y_ref, ko_ref, vo_ref, so_ref,
                   z_s, mix_s, *, g1, n_blocks):
    i = pl.program_id(0)
    bb = SAMPLE_BLOCK

    @pl.when(i == 0)
    def _():
        m = mod_ref[...]
        h = (x_ref[...] * (1.0 + m[1]) + m[0]).astype(BF16)
        z = jnp.concatenate([_dot(h, wq_ref[...]), _dot(h, wrest_ref[...])], axis=1)
        att_c, att_s1, att_s2 = tab_ref[0], tab_ref[1], tab_ref[2]
        ret_c, ret_s = tab_ref[3], tab_ref[4]
        for j in range(4):
            col = OFF_Q + j * LANES
            z_s[:, col:col + LANES] = _rope_att(z[:, col:col + LANES], att_c, att_s1, att_s2) * (HEAD_DIM ** -0.5)
            col = OFF_RQ + j * LANES
            z_s[:, col:col + LANES] = _rope_ret(z[:, col:col + LANES], ret_c, ret_s)
            col = OFF_RK + j * LANES
            z_s[:, col:col + LANES] = _rope_ret(z[:, col:col + LANES], ret_c, ret_s) * (RET_DK ** -0.5)
        z_s[:, OFF_KV:OFF_KV + LANES] = _rope_att(z[:, OFF_KV:OFF_KV + LANES], att_c, att_s1, att_s2)
        z_s[:, OFF_KV + LANES:OFF_RQ] = z[:, OFF_KV + LANES:OFF_RQ]
        z_s[:, OFF_RV:OFF_RG] = z[:, OFF_RV:OFF_RG]
        z_s[:, OFF_RG:] = _silu(z[:, OFF_RG:])

    lane_w = jax.lax.broadcasted_iota(jnp.int32, (LANES, WINDOW), 1)
    last_lane = lane_w == WINDOW - 1
    lo_row = lane_w[0:1, :] < HEAD_DIM
    r4 = jax.lax.broadcasted_iota(jnp.int32, (ATT_GROUP, 1), 0)
    r8 = jax.lax.broadcasted_iota(jnp.int32, (ATT_HEADS, 1), 0)
    sink8 = jnp.zeros((ATT_HEADS, 1), F32)
    for hd in range(ATT_HEADS):
        sink8 = jnp.where(r8 == hd, sinks_ref[hd], sink8)
    head_blk = (jax.lax.broadcasted_iota(jnp.int32, (16, RET_WIDTH), 1) // LANES
                == jax.lax.broadcasted_iota(jnp.int32, (16, RET_WIDTH), 0))
    g1_col = jnp.where(r4 == 0, g1[0], jnp.where(r4 == 1, g1[1], jnp.where(r4 == 2, g1[2], g1[3])))

    def rows16(a):
        return jnp.concatenate([a, jnp.zeros((16 - a.shape[0], a.shape[1]), F32)], axis=0)

    def heads(row, off):
        return jnp.concatenate([row[:, off + j * LANES:off + (j + 1) * LANES] for j in range(4)], axis=0)

    def block_rows(a4):
        wide = jnp.concatenate([rows16(a4)] * RET_HEADS, axis=1)
        return jnp.where(head_blk, wide, jnp.zeros_like(wide))

    r0 = pl.multiple_of(i * bb, bb)
    zblk = z_s[pl.ds(r0, bb), :]
    rows = [zblk[b:b + 1, :] for b in range(bb)]

    kn = [r[:, OFF_KV:OFF_KV + LANES] for r in rows]
    vn = [r[:, OFF_KV + LANES:OFF_RQ] for r in rows]
    kn_t = zblk[:, OFF_KV:OFF_KV + LANES].T
    vn_t = zblk[:, OFF_KV + LANES:OFF_RQ].T
    q8 = []
    for r in rows:
        q4 = heads(r, OFF_Q)
        zero4 = jnp.zeros_like(q4)
        q8.append(jnp.concatenate([jnp.where(lo_row, q4, zero4), jnp.where(lo_row, zero4, q4)], axis=0))
    sc, oh = [], []
    for b in range(bb):
        sc.append(_dot(rows16(q8[b]).astype(BF16), ck_ref[b].astype(BF16))[0:ATT_HEADS])
    for b in range(bb):
        rq = heads(rows[b], OFF_RQ)
        rk = heads(rows[b], OFF_RK)
        rv = heads(rows[b], OFF_RV)
        state = st_ref[b].reshape(RET_WIDTH, LANES)
        o2 = _dot(block_rows(rq * g1_col).astype(BF16), state.astype(BF16))[0:RET_HEADS]
        outer = _dot_tn(block_rows(rk).astype(BF16), rows16(rv).astype(BF16))
        for hh in range(RET_HEADS):
            so_ref[b, hh] = g1[hh] * st_ref[b, hh] + outer[hh * RET_DK:(hh + 1) * RET_DK, :]
        o = jnp.sum(rq * rk, axis=-1, keepdims=True) * rv + o2
        mu = jnp.mean(o, axis=-1, keepdims=True)
        d = o - mu
        var = jnp.mean(d * d, axis=-1, keepdims=True)
        oh.append(heads(rows[b], OFF_RG) * (d * jax.lax.rsqrt(var + GN_EPS) * gnw_ref[...]))

    ws, wn = [], []
    for b in range(bb):
        sn = jnp.sum(q8[b] * kn[b], axis=-1, keepdims=True)
        mx = jnp.maximum(jnp.maximum(jnp.max(sc[b], axis=-1, keepdims=True), sn), sink8)
        p = jnp.exp(sc[b] - mx)
        pn = jnp.exp(sn - mx)
        inv = 1.0 / (jnp.sum(p, axis=-1, keepdims=True) + pn + jnp.exp(sink8 - mx))
        ws.append(rows16(p * inv).astype(BF16))
        wn.append(pn * inv)

    att = []
    for b in range(bb):
        o8 = _dot_nt(ws[b], cv_ref[b].astype(BF16))[0:ATT_HEADS] + wn[b] * vn[b]
        att.append(jnp.where(lo_row, o8[0:ATT_GROUP], o8[ATT_GROUP:]))

    for b in range(bb):
        ko_ref[b] = jnp.where(last_lane, kn_t[:, b:b + 1], pltpu.roll(ck_ref[b], WINDOW - 1, axis=1))
        vo_ref[b] = jnp.where(last_lane, vn_t[:, b:b + 1], pltpu.roll(cv_ref[b], WINDOW - 1, axis=1))
    blk_rows = pl.ds(r0, bb)
    for j in range(4):
        mix_s[blk_rows, j * LANES:(j + 1) * LANES] = jnp.concatenate(
            [a[j:j + 1, :] for a in att], axis=0)
        mix_s[blk_rows, ATT_WIDTH + j * LANES:ATT_WIDTH + (j + 1) * LANES] = jnp.concatenate(
            [o[j:j + 1, :] for o in oh], axis=0)

    @pl.when(i == n_blocks - 1)
    def _():
        y = (_dot(mix_s[:, :ATT_WIDTH].astype(BF16), wout_att_ref[...])
             + _dot(mix_s[:, ATT_WIDTH:].astype(BF16), wout_ret_ref[...]))
        y_ref[...] = DEEPNORM_ALPHA * x_ref[...] + (1.0 + mod_ref[2]) * y


def _sample_step(sinks, x, mods, tab, w_q, w_rest, w_out_att, w_out_ret, gn_w4,
                 cache_k, cache_v, state):
    n = x.shape[0]
    bb = SAMPLE_BLOCK
    n_blocks = n // bb
    g1 = tuple(float(v) for v in np.exp(_log_gamma()).astype(np.float32))
    full = lambda *shape: pl.BlockSpec(shape, lambda i, *_: (0,) * len(shape))
    blk = lambda *shape: pl.BlockSpec((bb,) + shape, lambda i, *_: (i,) + (0,) * len(shape))
    grid_spec = pltpu.PrefetchScalarGridSpec(
        num_scalar_prefetch=1,
        grid=(n_blocks,),
        in_specs=[
            full(n, D_MODEL), full(3, n, D_MODEL), full(5, 1, LANES),
            full(D_MODEL, ATT_WIDTH), full(D_MODEL, IN_WIDTH - ATT_WIDTH),
            full(ATT_WIDTH, D_MODEL), full(RET_WIDTH, D_MODEL),
            full(RET_HEADS, LANES),
            blk(WINDOW, LANES), blk(WINDOW, LANES), blk(RET_HEADS, RET_DK, LANES),
        ],
        out_specs=[full(n, D_MODEL), blk(WINDOW, LANES), blk(WINDOW, LANES),
                   blk(RET_HEADS, RET_DK, LANES)],
        scratch_shapes=[pltpu.VMEM((n, IN_WIDTH), F32), pltpu.VMEM((n, D_MODEL), F32)],
    )
    return pl.pallas_call(
        functools.partial(_sample_kernel, g1=g1, n_blocks=n_blocks),
        out_shape=[
            jax.ShapeDtypeStruct((n, D_MODEL), F32),
            jax.ShapeDtypeStruct((n, WINDOW, LANES), F32),
            jax.ShapeDtypeStruct((n, WINDOW, LANES), F32),
            jax.ShapeDtypeStruct((n, RET_HEADS, RET_DK, LANES), F32),
        ],
        grid_spec=grid_spec,
        compiler_params=pltpu.CompilerParams(
            dimension_semantics=("arbitrary",), vmem_limit_bytes=VMEM_LIMIT_BYTES),
        name="sample_step",
    )(sinks, x, mods, tab, w_q, w_rest, w_out_att, w_out_ret, gn_w4, cache_k, cache_v, state)


def _slab_order(w, axis):
    shape = w.shape
    pre, post = shape[:axis], shape[axis + 1:]
    w = w.reshape(pre + (ATT_KV_HEADS, ATT_GROUP, HEAD_DIM) + post)
    w = jnp.swapaxes(w, axis, axis + 1)
    return w.reshape(shape)


def kernel(x_prompt, x_sample, c_prompt, c_sample, cache_k, cache_v, state_ret, w_ada_mix, b_ada_mix, w_in, att_sinks, ret_gn_w, w_out, ln1_w, ln1_b, w_ada_ffn, b_ada_ffn, w_up, w_down, ln2_w, ln2_b):
    batch, seq, _ = x_prompt.shape
    n_s = x_sample.shape[0]
    l = 0
    assert w_in.shape[0] == DEPTH == 1

    w_in_f = w_in.reshape(D_MODEL, IN_WIDTH)
    w_out_f = w_out.reshape(D_MODEL, D_MODEL)
    w_q = _slab_order(w_in_f[:, :ATT_WIDTH].astype(BF16), 1)
    w_rest = w_in_f[:, ATT_WIDTH:].astype(BF16)
    w_out_att = _slab_order(w_out_f[:ATT_WIDTH].astype(BF16), 0)
    w_out_ret = w_out_f[ATT_WIDTH:].astype(BF16)
    w_up_b = w_up.reshape(D_MODEL, 2 * D_FF).astype(BF16)
    w_down_b = w_down.reshape(D_FF, D_MODEL).astype(BF16)
    sinks = att_sinks[l]
    gn_w = ret_gn_w[l].reshape(1, RET_WIDTH)
    ln1w, ln1b = ln1_w[l].reshape(1, D_MODEL), ln1_b[l].reshape(1, D_MODEL)
    ln2w, ln2b = ln2_w[l].reshape(1, D_MODEL), ln2_b[l].reshape(1, D_MODEL)

    tab_p = jnp.asarray(_rope_tables(np.arange(seq)))
    tab_s = jnp.asarray(_rope_tables(np.array([PAST_LEN])))
    dec = jnp.asarray(_decay_tables())
    bias = jnp.asarray(_band_bias())

    mods = _ada_mod(c_sample, c_prompt, w_ada_mix.reshape(D_MODEL, 3 * D_MODEL),
                    w_ada_ffn.reshape(D_MODEL, 3 * D_MODEL), b_ada_mix, b_ada_ffn)

    y1p, kp, vp, sp = _mixer_prompt(x_prompt, mods, n_s, sinks, tab_p, dec, bias, w_q, w_rest,
                                    w_out_att, w_out_ret, gn_w)
    yp = _ffn(y1p.reshape(batch * seq, D_MODEL), mods, n_s, batch, seq // FFN_TILE, w_up_b, w_down_b,
              ln1w, ln1b, ln2w, ln2b, FFN_TILE, FFN_PARTS).reshape(batch, seq, D_MODEL)

    xs = x_sample.reshape(n_s, D_MODEL)
    y1s, ks, vs, ss = _sample_step(
        sinks, xs, mods, tab_s, w_q, w_rest, w_out_att, w_out_ret, ret_gn_w.reshape(RET_HEADS, LANES),
        jnp.swapaxes(cache_k.reshape(n_s, WINDOW, LANES), 1, 2),
        jnp.swapaxes(cache_v.reshape(n_s, WINDOW, LANES), 1, 2),
        state_ret.reshape(n_s, RET_HEADS, RET_DK, LANES))
    ks, vs = jnp.swapaxes(ks, 1, 2), jnp.swapaxes(vs, 1, 2)
    ys = _ffn(y1s, mods, 0, n_s, 0, w_up_b, w_down_b, ln1w, ln1b, ln2w, ln2b, n_s, 1)

    kv_shape = (1, batch, WINDOW, ATT_KV_HEADS, HEAD_DIM)
    kvs_shape = (1, n_s, WINDOW, ATT_KV_HEADS, HEAD_DIM)
    return (yp, ys.reshape(n_s, 1, D_MODEL),
            kp.reshape(kv_shape), vp.reshape(kv_shape), sp[None],
            ks.reshape(kvs_shape), vs.reshape(kvs_shape), ss[None])
```

```python
import functools
import math

import numpy as np
import jax
import jax.numpy as jnp
from jax.experimental import pallas as pl
from jax.experimental.pallas import tpu as pltpu

D_MODEL = 1024
WINDOW = 128
CHUNK = 128
ATT_HEADS = 8
ATT_KV_HEADS = 2
ATT_GROUP = ATT_HEADS // ATT_KV_HEADS
HEAD_DIM = 64
ROPE_DIMS = HEAD_DIM // 4
ROPE_THETA = 500000.0
RET_HEADS = 4
RET_DK = 128
RET_THETA = 10000.0
ATT_WIDTH = ATT_HEADS * HEAD_DIM
KV_WIDTH = ATT_KV_HEADS * HEAD_DIM
RET_WIDTH = RET_HEADS * RET_DK
IN_WIDTH = ATT_WIDTH + 2 * KV_WIDTH + 4 * RET_WIDTH
D_FF = 2816
DEPTH = 1
DEEPNORM_ALPHA = (2 * DEPTH) ** 0.25
LN_EPS = 1e-5
GN_EPS = 1e-6
PAST_LEN = 16384

OFF_Q = 0
OFF_KV = ATT_WIDTH
OFF_RQ = OFF_KV + 2 * KV_WIDTH
OFF_RK = OFF_RQ + RET_WIDTH
OFF_RV = OFF_RK + RET_WIDTH
OFF_RG = OFF_RV + RET_WIDTH

LANES = 128
VMEM_LIMIT_BYTES = 56 * 1024 * 1024

PROMPT_TILE = 512
FILL_WIDTH = 256
FILL_PLAN = ((2, 1), (2, 1), (2, 1), (2, 0))
FFN_TILE = 1024
FFN_PARTS = 4
FFN_COLS = 256
SAMPLE_BLOCK = 8

BF16 = jnp.bfloat16
F32 = jnp.float32


def _dot(a, b):
    return jnp.dot(a, b, preferred_element_type=F32)


def _dot_nt(a, b):
    return jax.lax.dot_general(a, b, (((1,), (1,)), ((), ())), preferred_element_type=F32)


def _dot_tn(a, b):
    return jax.lax.dot_general(a, b, (((0,), (0,)), ((), ())), preferred_element_type=F32)


def _sigmoid(x):
    return 1.0 / (1.0 + jnp.exp(-x))


def _silu(x):
    return x * _sigmoid(x)


def _layer_norm(r, w, b):
    mu = jnp.mean(r, axis=-1, keepdims=True)
    d = r - mu
    var = jnp.mean(d * d, axis=-1, keepdims=True)
    return d * jax.lax.rsqrt(var + LN_EPS) * w + b


def _rope_att(slab, c, s1, s2):
    return slab * c + pltpu.roll(slab, LANES - 8, axis=1) * s1 + pltpu.roll(slab, 8, axis=1) * s2


def _rope_ret(slab, c, s):
    return slab * c + pltpu.roll(slab, LANES // 2, axis=1) * s


def _rope_tables(pos):
    pos = np.asarray(pos, np.float64)[:, None]
    half = ROPE_DIMS // 2
    inv = ROPE_THETA ** (-np.arange(half, dtype=np.float64) / half)
    ang = pos * inv[None, :]
    cos, sin = np.cos(ang), np.sin(ang)
    n = pos.shape[0]
    head_c = np.ones((n, HEAD_DIM)); head_s1 = np.zeros((n, HEAD_DIM)); head_s2 = np.zeros((n, HEAD_DIM))
    head_c[:, :half] = cos; head_c[:, half:2 * half] = cos
    head_s1[:, :half] = -sin
    head_s2[:, half:2 * half] = sin
    att = [np.tile(t, (1, LANES // HEAD_DIM)) for t in (head_c, head_s1, head_s2)]
    rhalf = RET_DK // 2
    rinv = RET_THETA ** (-np.arange(rhalf, dtype=np.float64) / rhalf)
    rang = pos * rinv[None, :]
    rc = np.concatenate([np.cos(rang), np.cos(rang)], axis=1)
    rs = np.concatenate([-np.sin(rang), np.sin(rang)], axis=1)
    return np.stack(att + [rc, rs], axis=0).astype(np.float32)


def _log_gamma():
    lin = np.linspace(math.log(1.0 / 32), math.log(1.0 / 512), RET_HEADS)
    return np.log1p(-np.exp(lin))


def _decay_tables():
    lg = _log_gamma()
    idx = np.arange(CHUNK, dtype=np.float64)
    diff = idx[:, None] - idx[None, :]
    dmat = np.where(diff[None] >= 0, np.exp(lg[:, None, None] * np.maximum(diff, 0.0)[None]), 0.0)
    dq = np.exp(lg[:, None] * (idx[None, :] + 1.0))[:, :, None] * np.ones((1, 1, LANES))
    dk = np.exp(lg[:, None] * (CHUNK - 1.0 - idx[None, :]))[:, :, None] * np.ones((1, 1, LANES))
    return np.concatenate([dmat, dq, dk], axis=0).astype(np.float32)


def _band_bias():
    i = np.arange(CHUNK)[:, None]
    j = np.arange(2 * CHUNK)[None, :]
    valid = (j >= i) & (j <= i + WINDOW)
    neg = np.float32(-1e30)
    b0 = np.where(valid, 0.0, neg)
    b1 = np.where(valid & (j >= CHUNK), 0.0, neg)
    return np.stack([b0, b1], axis=0).astype(np.float32)


def _ada_kernel(cs_ref, cp_ref, w1_ref, w2_ref, b1_ref, b2_ref, o_ref):
    rows = cs_ref.shape[0] + cp_ref.shape[0]
    pad = jnp.zeros((-rows % 16, D_MODEL), F32)
    a = _silu(jnp.concatenate([cs_ref[...], cp_ref[...], pad], axis=0)).astype(BF16)

    def emit(w_ref, b_ref):
        o_ref[...] = _dot(a, w_ref[...].astype(BF16))[0:rows] + b_ref[...]

    @pl.when(pl.program_id(0) < 3)
    def _():
        emit(w1_ref, b1_ref)

    @pl.when(pl.program_id(0) >= 3)
    def _():
        emit(w2_ref, b2_ref)


def _ada_mod(c_sample, c_prompt, w_mix, w_ffn, b_mix, b_ffn):
    rows = c_sample.shape[0] + c_prompt.shape[0]
    d = D_MODEL
    return pl.pallas_call(
        _ada_kernel,
        out_shape=jax.ShapeDtypeStruct((6, rows, d), F32),
        grid=(6,),
        in_specs=[
            pl.BlockSpec(c_sample.shape, lambda j: (0, 0)),
            pl.BlockSpec(c_prompt.shape, lambda j: (0, 0)),
            pl.BlockSpec((d, d), lambda j: (0, jnp.minimum(j, 2))),
            pl.BlockSpec((d, d), lambda j: (0, jnp.maximum(j - 3, 0))),
            pl.BlockSpec((1, d), lambda j: (0, jnp.minimum(j, 2))),
            pl.BlockSpec((1, d), lambda j: (0, jnp.maximum(j - 3, 0))),
        ],
        out_specs=pl.BlockSpec((None, rows, d), lambda j: (j, 0, 0)),
        compiler_params=pltpu.CompilerParams(
            dimension_semantics=("arbitrary",), vmem_limit_bytes=VMEM_LIMIT_BYTES),
        name="ada_mod",
    )(c_sample, c_prompt, w_mix, w_ffn, b_mix, b_ffn)


def _mixer_step(z_prev, z_cur, sinks_ref, x_ref, mods_ref, tab_ref, xp_ref, dec_ref, bias_ref,
                wq_ref, wrest_ref, wout_att_ref, wout_ret_ref, gnw_ref,
                y_ref, kout_ref, vout_ref, s_ref, kcarry, vcarry, mix_s,
                *, cur_b, prev_b, seq_start, tile, gl):
    n_chunks = tile // CHUNK
    x = x_ref[...]
    h = (x * (1.0 + mods_ref[1, pl.ds(cur_b, 1), :]) + mods_ref[0, pl.ds(cur_b, 1), :]).astype(BF16)
    n_fill = IN_WIDTH // FILL_WIDTH
    pending = list(range(n_fill))

    def fill(count):
        for _ in range(count):
            if pending:
                i = pending.pop(0)
                col = i * FILL_WIDTH
                w = (wq_ref[:, col:col + FILL_WIDTH] if col < ATT_WIDTH
                     else wrest_ref[:, col - ATT_WIDTH:col - ATT_WIDTH + FILL_WIDTH])
                z_cur[:, col:col + FILL_WIDTH] = _dot(h, w)

    gate = mods_ref[2, pl.ds(prev_b, 1), :]
    lo = jax.lax.broadcasted_iota(jnp.int32, (CHUNK, LANES), 1) < HEAD_DIM
    zero = jnp.zeros((CHUNK, LANES), F32)
    k_prev = [kcarry[g] for g in range(ATT_KV_HEADS)]
    v_prev = [vcarry[g] for g in range(ATT_KV_HEADS)]

    for c in range(n_chunks):
        rows = slice(c * CHUNK, (c + 1) * CHUNK)
        bias = bias_ref[seq_start] if c == 0 else bias_ref[0]
        att_c, att_s1, att_s2 = tab_ref[0, rows, :], tab_ref[1, rows, :], tab_ref[2, rows, :]
        ret_c, ret_s = tab_ref[3, rows, :], tab_ref[4, rows, :]

        def zs(col):
            return z_prev[rows, col:col + LANES]

        q_stack = jnp.concatenate(
            [(_rope_att(zs(OFF_Q + j * LANES), att_c, att_s1, att_s2) * (HEAD_DIM ** -0.5)).astype(BF16)
             for j in range(4)], axis=0)
        k = _rope_att(zs(OFF_KV), att_c, att_s1, att_s2)
        v = zs(OFF_KV + LANES)
        k_cur = [jnp.where(lo, k, zero).astype(BF16), jnp.where(lo, zero, k).astype(BF16)]
        v_cur = [jnp.where(lo, v, zero).astype(BF16), jnp.where(lo, zero, v).astype(BF16)]
        k_band = [jnp.concatenate([k_prev[g], k_cur[g]], axis=0) for g in range(ATT_KV_HEADS)]
        v_band = [jnp.concatenate([v_prev[g], v_cur[g]], axis=0) for g in range(ATT_KV_HEADS)]
        k_prev, v_prev = k_cur, v_cur
        if c == n_chunks - 1:
            kout_ref[...] = k
            vout_ref[...] = v
            for g in range(ATT_KV_HEADS):
                kcarry[g] = k_cur[g]
                vcarry[g] = v_cur[g]

        ret = []
        for hh in range(RET_HEADS):
            rq = _rope_ret(zs(OFF_RQ + hh * LANES), ret_c, ret_s)
            rk = _rope_ret(zs(OFF_RK + hh * LANES), ret_c, ret_s) * (RET_DK ** -0.5)
            ret.append(dict(
                q=rq.astype(BF16), qd=(rq * dec_ref[RET_HEADS + hh]).astype(BF16),
                k=rk.astype(BF16), kd=(rk * dec_ref[2 * RET_HEADS + hh]).astype(BF16),
                v=zs(OFF_RV + hh * LANES).astype(BF16)))

        sc_att = [_dot_nt(q_stack, k_band[g]) for g in range(ATT_KV_HEADS)]
        sc_ret = [_dot_nt(r["q"], r["k"]) for r in ret]
        fill(FILL_PLAN[c][0])

        att = None
        for g in range(ATT_KV_HEADS):
            ws = []
            for j in range(4):
                sj = sc_att[g][j * CHUNK:(j + 1) * CHUNK, :] + bias
                sink = sinks_ref[g * ATT_GROUP + j]
                mx = jnp.maximum(jnp.max(sj, axis=-1, keepdims=True), sink)
                p = jnp.exp(sj - mx)
                den = jnp.sum(p, axis=-1, keepdims=True) + jnp.exp(sink - mx)
                ws.append((p * (1.0 / den)).astype(BF16))
            o = _dot(jnp.concatenate(ws, axis=0), v_band[g])
            att = o if att is None else att + o
        for j in range(4):
            mix_s[rows, j * LANES:(j + 1) * LANES] = att[j * CHUNK:(j + 1) * CHUNK, :].astype(BF16)

        outs = []
        for hh, r in enumerate(ret):
            state = s_ref[hh]
            sc = sc_ret[hh] * dec_ref[hh]
            outs.append(_dot(sc.astype(BF16), r["v"]) + _dot(r["qd"], state.astype(BF16)))
            s_ref[hh] = gl[hh] * state + _dot_tn(r["kd"], r["v"])
        fill(FILL_PLAN[c][1])
        for hh in range(RET_HEADS):
            cols = slice(hh * LANES, (hh + 1) * LANES)
            o = outs[hh]
            mu = jnp.mean(o, axis=-1, keepdims=True)
            d = o - mu
            var = jnp.mean(d * d, axis=-1, keepdims=True)
            nrm = d * jax.lax.rsqrt(var + GN_EPS) * gnw_ref[:, cols]
            mix_s[rows, ATT_WIDTH + hh * LANES:ATT_WIDTH + (hh + 1) * LANES] = (
                _silu(zs(OFF_RG + hh * LANES)) * nrm).astype(BF16)

    y = _dot(mix_s[...], jnp.concatenate([wout_att_ref[...], wout_ret_ref[...]], axis=0))
    fill(n_fill)
    y_ref[...] = DEEPNORM_ALPHA * xp_ref[...] + (1.0 + gate) * y


def _mixer_kernel(sinks_ref, x_ref, mods_ref, tab_ref, xp_ref, dec_ref, bias_ref,
                  wq_ref, wrest_ref, wout_att_ref, wout_ret_ref, gnw_ref,
                  y_ref, kout_ref, vout_ref, s_ref,
                  z_a, z_b, kcarry, vcarry, mix_s, *, tile, ns, nt, gl):
    t = pl.program_id(0)
    prev_tile = jnp.maximum(t - 1, 0)
    sa = jax.lax.rem(prev_tile, ns)
    seq_start = (sa == 0).astype(jnp.int32)
    cur_b = jax.lax.div(jnp.minimum(t, nt - 1), ns)
    prev_b = jax.lax.div(prev_tile, ns)
    parity = jax.lax.rem(t, 2)

    @pl.when(t == 0)
    def _():
        z_b[...] = jnp.zeros(z_b.shape, z_b.dtype)
        kcarry[...] = jnp.zeros(kcarry.shape, kcarry.dtype)
        vcarry[...] = jnp.zeros(vcarry.shape, vcarry.dtype)

    @pl.when(sa == 0)
    def _():
        s_ref[...] = jnp.zeros_like(s_ref)

    step = functools.partial(
        _mixer_step, sinks_ref=sinks_ref, x_ref=x_ref, mods_ref=mods_ref, tab_ref=tab_ref, xp_ref=xp_ref,
        dec_ref=dec_ref, bias_ref=bias_ref, wq_ref=wq_ref, wrest_ref=wrest_ref,
        wout_att_ref=wout_att_ref, wout_ret_ref=wout_ret_ref,
        gnw_ref=gnw_ref, y_ref=y_ref, kout_ref=kout_ref,
        vout_ref=vout_ref, s_ref=s_ref, kcarry=kcarry, vcarry=vcarry, mix_s=mix_s,
        cur_b=cur_b, prev_b=prev_b, seq_start=seq_start, tile=tile, gl=gl)

    @pl.when(parity == 0)
    def _():
        step(z_b, z_a)

    @pl.when(parity == 1)
    def _():
        step(z_a, z_b)


def _mixer_prompt(x, mods, mod_row0, sinks, tab, dec, bias, w_q, w_rest, w_out_att, w_out_ret, gn_w):
    batch, seq, _ = x.shape
    assert mod_row0 % batch == 0 and batch % 8 == 0
    tile = PROMPT_TILE
    ns = seq // tile
    gl = tuple(float(v) for v in np.exp(_log_gamma() * CHUNK).astype(np.float32))
    nt = batch * ns
    const = lambda shape: pl.BlockSpec(shape, lambda t, *_: (0,) * len(shape),
                                       pipeline_mode=pl.Buffered(1))
    cur_b = lambda t: jnp.minimum(t, nt - 1) // ns
    cur_s = lambda t: jnp.minimum(t, nt - 1) % ns
    prev_b = lambda t: jnp.maximum(t - 1, 0) // ns
    prev_s = lambda t: jnp.maximum(t - 1, 0) % ns
    grid_spec = pltpu.PrefetchScalarGridSpec(
        num_scalar_prefetch=1,
        grid=(nt + 1,),
        in_specs=[
            pl.BlockSpec((None, tile, D_MODEL), lambda t, *_: (cur_b(t), cur_s(t), 0)),
            pl.BlockSpec((3, batch, D_MODEL), lambda t, *_: (0, mod_row0 // batch, 0),
                         pipeline_mode=pl.Buffered(1)),
            pl.BlockSpec((5, tile, LANES), lambda t, *_: (0, prev_s(t), 0)),
            pl.BlockSpec((None, tile, D_MODEL), lambda t, *_: (prev_b(t), prev_s(t), 0)),
            const((3 * RET_HEADS, CHUNK, LANES)),
            const((2, CHUNK, 2 * CHUNK)),
            const((D_MODEL, ATT_WIDTH)),
            const((D_MODEL, IN_WIDTH - ATT_WIDTH)),
            const((ATT_WIDTH, D_MODEL)),
            const((RET_WIDTH, D_MODEL)),
            const((1, RET_WIDTH)),
        ],
        out_specs=[
            pl.BlockSpec((None, tile, D_MODEL), lambda t, *_: (prev_b(t), prev_s(t), 0)),
            pl.BlockSpec((None, WINDOW, LANES), lambda t, *_: (prev_b(t), 0, 0)),
            pl.BlockSpec((None, WINDOW, LANES), lambda t, *_: (prev_b(t), 0, 0)),
            pl.BlockSpec((None, RET_HEADS, RET_DK, LANES), lambda t, *_: (prev_b(t), 0, 0, 0)),
        ],
        scratch_shapes=[
            pltpu.VMEM((tile, IN_WIDTH), F32),
            pltpu.VMEM((tile, IN_WIDTH), F32),
            pltpu.VMEM((ATT_KV_HEADS, CHUNK, LANES), BF16),
            pltpu.VMEM((ATT_KV_HEADS, CHUNK, LANES), BF16),
            pltpu.VMEM((tile, D_MODEL), BF16),
        ],
    )
    return pl.pallas_call(
        functools.partial(_mixer_kernel, tile=tile, ns=ns, nt=nt, gl=gl),
        out_shape=[
            jax.ShapeDtypeStruct((batch, seq, D_MODEL), F32),
            jax.ShapeDtypeStruct((batch, WINDOW, LANES), F32),
            jax.ShapeDtypeStruct((batch, WINDOW, LANES), F32),
            jax.ShapeDtypeStruct((batch, RET_HEADS, RET_DK, LANES), F32),
        ],
        grid_spec=grid_spec,
        compiler_params=pltpu.CompilerParams(
            dimension_semantics=("arbitrary",), vmem_limit_bytes=VMEM_LIMIT_BYTES),
        name="mixer_prompt",
    )(sinks, x, mods, tab, x, dec, bias, w_q, w_rest, w_out_att, w_out_ret, gn_w)


def _ffn_kernel(r_ref, mods_ref, wup_ref, wdown_ref, ln1w_ref, ln1b_ref, lnw_ref, lnb_ref, o_ref,
                *, parts, tiles_per_mod_row):
    rows = r_ref.shape[0] // parts
    sl = [slice(p * rows, (p + 1) * rows) for p in range(parts)]
    if tiles_per_mod_row:
        seq = jax.lax.div(pl.program_id(0), tiles_per_mod_row)
        mod = lambda k, p: mods_ref[k, pl.ds(seq, 1), :]
    else:
        mod = lambda k, p: mods_ref[k, sl[p], :]
    ys = [_layer_norm(r_ref[sl[p], :], ln1w_ref[...], ln1b_ref[...]) for p in range(parts)]
    hs = [(ys[p] * (1.0 + mod(1, p)) + mod(0, p)).astype(BF16) for p in range(parts)]
    def act(h):
        return jnp.concatenate(
            [(_silu(_dot(h, wup_ref[:, j:j + FFN_COLS]))
              * _dot(h, wup_ref[:, D_FF + j:D_FF + j + FFN_COLS])).astype(BF16)
             for j in range(0, D_FF, FFN_COLS)], axis=1)

    acts = [act(h) for h in hs]
    fs = [_dot(a, wdown_ref[...]) for a in acts]
    for p in range(parts):
        r = DEEPNORM_ALPHA * ys[p] + (1.0 + mod(2, p)) * fs[p]
        o_ref[sl[p], :] = _layer_norm(r, lnw_ref[...], lnb_ref[...])


def _ffn(y, mods, mod_row0, mod_rows, tiles_per_mod_row, w_up, w_down, ln1_w, ln1_b, ln_w, ln_b, tile, parts):
    rows, _ = y.shape
    nt = rows // tile
    assert mod_row0 % mod_rows == 0 and (tiles_per_mod_row > 0 or mod_rows == tile)
    const = lambda shape: pl.BlockSpec(shape, lambda t: (0,) * len(shape), pipeline_mode=pl.Buffered(1))
    return pl.pallas_call(
        functools.partial(_ffn_kernel, parts=parts, tiles_per_mod_row=tiles_per_mod_row),
        out_shape=jax.ShapeDtypeStruct(y.shape, F32),
        grid=(nt,),
        in_specs=[
            pl.BlockSpec((tile, D_MODEL), lambda t: (t, 0)),
            pl.BlockSpec((3, mod_rows, D_MODEL), lambda t: (1, mod_row0 // mod_rows, 0),
                         pipeline_mode=pl.Buffered(1)),
            const((D_MODEL, 2 * D_FF)),
            const((D_FF, D_MODEL)),
            const((1, D_MODEL)),
            const((1, D_MODEL)),
            const((1, D_MODEL)),
            const((1, D_MODEL)),
        ],
        out_specs=pl.BlockSpec((tile, D_MODEL), lambda t: (t, 0)),
        compiler_params=pltpu.CompilerParams(
            dimension_semantics=("arbitrary",), vmem_limit_bytes=VMEM_LIMIT_BYTES),
        name="ffn",
    )(y, mods, w_up, w_down, ln1_w, ln1_b, ln_w, ln_b)


def _sample_kernel(sinks_ref, x_ref, mod_ref, tab_ref, wq_ref, wrest_ref, wout_att_ref, wout_ret_ref,
                   gnw_ref,
                   ck_ref, cv_ref, st_ref,
                   y_ref, ko_ref, vo_ref, so_ref,
                   z_s, mix_s, *, g1, n_blocks):
    i = pl.program_id(0)
    bb = SAMPLE_BLOCK

    @pl.when(i == 0)
    def _():
        m = mod_ref[...]
        h = (x_ref[...] * (1.0 + m[1]) + m[0]).astype(BF16)
        z = jnp.concatenate([_dot(h, wq_ref[...]), _dot(h, wrest_ref[...])], axis=1)
        att_c, att_s1, att_s2 = tab_ref[0], tab_ref[1], tab_ref[2]
        ret_c, ret_s = tab_ref[3], tab_ref[4]
        for j in range(4):
            col = OFF_Q + j * LANES
            z_s[:, col:col + LANES] = _rope_att(z[:, col:col + LANES], att_c, att_s1, att_s2) * (HEAD_DIM ** -0.5)
            col = OFF_RQ + j * LANES
            z_s[:, col:col + LANES] = _rope_ret(z[:, col:col + LANES], ret_c, ret_s)
            col = OFF_RK + j * LANES
            z_s[:, col:col + LANES] = _rope_ret(z[:, col:col + LANES], ret_c, ret_s) * (RET_DK ** -0.5)
        z_s[:, OFF_KV:OFF_KV + LANES] = _rope_att(z[:, OFF_KV:OFF_KV + LANES], att_c, att_s1, att_s2)
        z_s[:, OFF_KV + LANES:OFF_RQ] = z[:, OFF_KV + LANES:OFF_RQ]
        z_s[:, OFF_RV:OFF_RG] = z[:, OFF_RV:OFF_RG]
        z_s[:, OFF_RG:] = _silu(z[:, OFF_RG:])

    lane_w = jax.lax.broadcasted_iota(jnp.int32, (LANES, WINDOW), 1)
    last_lane = lane_w == WINDOW - 1
    lo_row = lane_w[0:1, :] < HEAD_DIM
    r4 = jax.lax.broadcasted_iota(jnp.int32, (ATT_GROUP, 1), 0)
    r8 = jax.lax.broadcasted_iota(jnp.int32, (ATT_HEADS, 1), 0)
    sink8 = jnp.zeros((ATT_HEADS, 1), F32)
    for hd in range(ATT_HEADS):
        sink8 = jnp.where(r8 == hd, sinks_ref[hd], sink8)
    head_blk = (jax.lax.broadcasted_iota(jnp.int32, (16, RET_WIDTH), 1) // LANES
                == jax.lax.broadcasted_iota(jnp.int32, (16, RET_WIDTH), 0))
    g1_col = jnp.where(r4 == 0, g1[0], jnp.where(r4 == 1, g1[1], jnp.where(r4 == 2, g1[2], g1[3])))

    def rows16(a):
        return jnp.concatenate([a, jnp.zeros((16 - a.shape[0], a.shape[1]), F32)], axis=0)

    def heads(row, off):
        return jnp.concatenate([row[:, off + j * LANES:off + (j + 1) * LANES] for j in range(4)], axis=0)

    def block_rows(a4):
        wide = jnp.concatenate([rows16(a4)] * RET_HEADS, axis=1)
        return jnp.where(head_blk, wide, jnp.zeros_like(wide))

    r0 = pl.multiple_of(i * bb, bb)
    zblk = z_s[pl.ds(r0, bb), :]
    rows = [zblk[b:b + 1, :] for b in range(bb)]

    kn = [r[:, OFF_KV:OFF_KV + LANES] for r in rows]
    vn = [r[:, OFF_KV + LANES:OFF_RQ] for r in rows]
    kn_t = zblk[:, OFF_KV:OFF_KV + LANES].T
    vn_t = zblk[:, OFF_KV + LANES:OFF_RQ].T
    q8 = []
    for r in rows:
        q4 = heads(r, OFF_Q)
        zero4 = jnp.zeros_like(q4)
        q8.append(jnp.concatenate([jnp.where(lo_row, q4, zero4), jnp.where(lo_row, zero4, q4)], axis=0))
    sc, oh = [], []
    for b in range(bb):
        sc.append(_dot(rows16(q8[b]).astype(BF16), ck_ref[b].astype(BF16))[0:ATT_HEADS])
    for b in range(bb):
        rq = heads(rows[b], OFF_RQ)
        rk = heads(rows[b], OFF_RK)
        rv = heads(rows[b], OFF_RV)
        state = st_ref[b].reshape(RET_WIDTH, LANES)
        o2 = _dot(block_rows(rq * g1_col).astype(BF16), state.astype(BF16))[0:RET_HEADS]
        outer = _dot_tn(block_rows(rk).astype(BF16), rows16(rv).astype(BF16))
        for hh in range(RET_HEADS):
            so_ref[b, hh] = g1[hh] * st_ref[b, hh] + outer[hh * RET_DK:(hh + 1) * RET_DK, :]
        o = jnp.sum(rq * rk, axis=-1, keepdims=True) * rv + o2
        mu = jnp.mean(o, axis=-1, keepdims=True)
        d = o - mu
        var = jnp.mean(d * d, axis=-1, keepdims=True)
        oh.append(heads(rows[b], OFF_RG) * (d * jax.lax.rsqrt(var + GN_EPS) * gnw_ref[...]))

    ws, wn = [], []
    for b in range(bb):
        sn = jnp.sum(q8[b] * kn[b], axis=-1, keepdims=True)
        mx = jnp.maximum(jnp.maximum(jnp.max(sc[b], axis=-1, keepdims=True), sn), sink8)
        p = jnp.exp(sc[b] - mx)
        pn = jnp.exp(sn - mx)
        inv = 1.0 / (jnp.sum(p, axis=-1, keepdims=True) + pn + jnp.exp(sink8 - mx))
        ws.append(rows16(p * inv).astype(BF16))
        wn.append(pn * inv)

    att = []
    for b in range(bb):
        o8 = _dot_nt(ws[b], cv_ref[b].astype(BF16))[0:ATT_HEADS] + wn[b] * vn[b]
        att.append(jnp.where(lo_row, o8[0:ATT_GROUP], o8[ATT_GROUP:]))

    for b in range(bb):
        ko_ref[b] = jnp.where(last_lane, kn_t[:, b:b + 1], pltpu.roll(ck_ref[b], WINDOW - 1, axis=1))
        vo_ref[b] = jnp.where(last_lane, vn_t[:, b:b + 1], pltpu.roll(cv_ref[b], WINDOW - 1, axis=1))
    blk_rows = pl.ds(r0, bb)
    for j in range(4):
        mix_s[blk_rows, j * LANES:(j + 1) * LANES] = jnp.concatenate(
            [a[j:j + 1, :] for a in att], axis=0)
        mix_s[blk_rows, ATT_WIDTH + j * LANES:ATT_WIDTH + (j + 1) * LANES] = jnp.concatenate(
            [o[j:j + 1, :] for o in oh], axis=0)

    @pl.when(i == n_blocks - 1)
    def _():
        y = (_dot(mix_s[:, :ATT_WIDTH].astype(BF16), wout_att_ref[...])
             + _dot(mix_s[:, ATT_WIDTH:].astype(BF16), wout_ret_ref[...]))
        y_ref[...] = DEEPNORM_ALPHA * x_ref[...] + (1.0 + mod_ref[2]) * y


def _sample_step(sinks, x, mods, tab, w_q, w_rest, w_out_att, w_out_ret, gn_w4,
                 cache_k, cache_v, state):
    n = x.shape[0]
    bb = SAMPLE_BLOCK
    n_blocks = n // bb
    g1 = tuple(float(v) for v in np.exp(_log_gamma()).astype(np.float32))
    full = lambda *shape: pl.BlockSpec(shape, lambda i, *_: (0,) * len(shape))
    blk = lambda *shape: pl.BlockSpec((bb,) + shape, lambda i, *_: (i,) + (0,) * len(shape))
    grid_spec = pltpu.PrefetchScalarGridSpec(
        num_scalar_prefetch=1,
        grid=(n_blocks,),
        in_specs=[
            full(n, D_MODEL), full(3, n, D_MODEL), full(5, 1, LANES),
            full(D_MODEL, ATT_WIDTH), full(D_MODEL, IN_WIDTH - ATT_WIDTH),
            full(ATT_WIDTH, D_MODEL), full(RET_WIDTH, D_MODEL),
            full(RET_HEADS, LANES),
            blk(WINDOW, LANES), blk(WINDOW, LANES), blk(RET_HEADS, RET_DK, LANES),
        ],
        out_specs=[full(n, D_MODEL), blk(WINDOW, LANES), blk(WINDOW, LANES),
                   blk(RET_HEADS, RET_DK, LANES)],
        scratch_shapes=[pltpu.VMEM((n, IN_WIDTH), F32), pltpu.VMEM((n, D_MODEL), F32)],
    )
    return pl.pallas_call(
        functools.partial(_sample_kernel, g1=g1, n_blocks=n_blocks),
        out_shape=[
            jax.ShapeDtypeStruct((n, D_MODEL), F32),
            jax.ShapeDtypeStruct((n, WINDOW, LANES), F32),
            jax.ShapeDtypeStruct((n, WINDOW, LANES), F32),
            jax.ShapeDtypeStruct((n, RET_HEADS, RET_DK, LANES), F32),
        ],
        grid_spec=grid_spec,
        compiler_params=pltpu.CompilerParams(
            dimension_semantics=("arbitrary",), vmem_limit_bytes=VMEM_LIMIT_BYTES),
        name="sample_step",
    )(sinks, x, mods, tab, w_q, w_rest, w_out_att, w_out_ret, gn_w4, cache_k, cache_v, state)


def _slab_order(w, axis):
    shape = w.shape
    pre, post = shape[:axis], shape[axis + 1:]
    w = w.reshape(pre + (ATT_KV_HEADS, ATT_GROUP, HEAD_DIM) + post)
    w = jnp.swapaxes(w, axis, axis + 1)
    return w.reshape(shape)


def kernel(x_prompt, x_sample, c_prompt, c_sample, cache_k, cache_v, state_ret, w_ada_mix, b_ada_mix, w_in, att_sinks, ret_gn_w, w_out, ln1_w, ln1_b, w_ada_ffn, b_ada_ffn, w_up, w_down, ln2_w, ln2_b):
    batch, seq, _ = x_prompt.shape
    n_s = x_sample.shape[0]
    l = 0
    assert w_in.shape[0] == DEPTH == 1

    w_in_f = w_in.reshape(D_MODEL, IN_WIDTH)
    w_out_f = w_out.reshape(D_MODEL, D_MODEL)
    w_q = _slab_order(w_in_f[:, :ATT_WIDTH].astype(BF16), 1)
    w_rest = w_in_f[:, ATT_WIDTH:].astype(BF16)
    w_out_att = _slab_order(w_out_f[:ATT_WIDTH].astype(BF16), 0)
    w_out_ret = w_out_f[ATT_WIDTH:].astype(BF16)
    w_up_b = w_up.reshape(D_MODEL, 2 * D_FF).astype(BF16)
    w_down_b = w_down.reshape(D_FF, D_MODEL).astype(BF16)
    sinks = att_sinks[l]
    gn_w = ret_gn_w[l].reshape(1, RET_WIDTH)
    ln1w, ln1b = ln1_w[l].reshape(1, D_MODEL), ln1_b[l].reshape(1, D_MODEL)
    ln2w, ln2b = ln2_w[l].reshape(1, D_MODEL), ln2_b[l].reshape(1, D_MODEL)

    tab_p = jnp.asarray(_rope_tables(np.arange(seq)))
    tab_s = jnp.asarray(_rope_tables(np.array([PAST_LEN])))
    dec = jnp.asarray(_decay_tables())
    bias = jnp.asarray(_band_bias())

    mods = _ada_mod(c_sample, c_prompt, w_ada_mix.reshape(D_MODEL, 3 * D_MODEL),
                    w_ada_ffn.reshape(D_MODEL, 3 * D_MODEL), b_ada_mix, b_ada_ffn)

    y1p, kp, vp, sp = _mixer_prompt(x_prompt, mods, n_s, sinks, tab_p, dec, bias, w_q, w_rest,
                                    w_out_att, w_out_ret, gn_w)
    yp = _ffn(y1p.reshape(batch * seq, D_MODEL), mods, n_s, batch, seq // FFN_TILE, w_up_b, w_down_b,
              ln1w, ln1b, ln2w, ln2b, FFN_TILE, FFN_PARTS).reshape(batch, seq, D_MODEL)

    xs = x_sample.reshape(n_s, D_MODEL)
    y1s, ks, vs, ss = _sample_step(
        sinks, xs, mods, tab_s, w_q, w_rest, w_out_att, w_out_ret, ret_gn_w.reshape(RET_HEADS, LANES),
        jnp.swapaxes(cache_k.reshape(n_s, WINDOW, LANES), 1, 2),
        jnp.swapaxes(cache_v.reshape(n_s, WINDOW, LANES), 1, 2),
        state_ret.reshape(n_s, RET_HEADS, RET_DK, LANES))
    ks, vs = jnp.swapaxes(ks, 1, 2), jnp.swapaxes(vs, 1, 2)
    ys = _ffn(y1s, mods, 0, n_s, 0, w_up_b, w_down_b, ln1w, ln1b, ln2w, ln2b, n_s, 1)

    kv_shape = (1, batch, WINDOW, ATT_KV_HEADS, HEAD_DIM)
    kvs_shape = (1, n_s, WINDOW, ATT_KV_HEADS, HEAD_DIM)
    return (yp, ys.reshape(n_s, 1, D_MODEL),
            kp.reshape(kv_shape), vp.reshape(kv_shape), sp[None],
            ks.reshape(kvs_shape), vs.reshape(kvs_shape), ss[None])
```

```python
import functools
import math

import numpy as np
import jax
import jax.numpy as jnp
from jax.experimental import pallas as pl
from jax.experimental.pallas import tpu as pltpu

D_MODEL = 1024
WINDOW = 128
CHUNK = 128
ATT_HEADS = 8
ATT_KV_HEADS = 2
ATT_GROUP = ATT_HEADS // ATT_KV_HEADS
HEAD_DIM = 64
ROPE_DIMS = HEAD_DIM // 4
ROPE_THETA = 500000.0
RET_HEADS = 4
RET_DK = 128
RET_THETA = 10000.0
ATT_WIDTH = ATT_HEADS * HEAD_DIM
KV_WIDTH = ATT_KV_HEADS * HEAD_DIM
RET_WIDTH = RET_HEADS * RET_DK
IN_WIDTH = ATT_WIDTH + 2 * KV_WIDTH + 4 * RET_WIDTH
D_FF = 2816
DEPTH = 1
DEEPNORM_ALPHA = (2 * DEPTH) ** 0.25
LN_EPS = 1e-5
GN_EPS = 1e-6
PAST_LEN = 16384

OFF_Q = 0
OFF_KV = ATT_WIDTH
OFF_RQ = OFF_KV + 2 * KV_WIDTH
OFF_RK = OFF_RQ + RET_WIDTH
OFF_RV = OFF_RK + RET_WIDTH
OFF_RG = OFF_RV + RET_WIDTH

LANES = 128
VMEM_LIMIT_BYTES = 56 * 1024 * 1024

PROMPT_TILE = 512
FILL_WIDTH = 256
FILL_PLAN = ((2, 1), (2, 1), (2, 1), (2, 0))
FFN_TILE = 1024
FFN_PARTS = 4
FFN_COLS = 256
SAMPLE_BLOCK = 8

BF16 = jnp.bfloat16
F32 = jnp.float32


def _dot(a, b):
    return jnp.dot(a, b, preferred_element_type=F32)


def _dot_nt(a, b):
    return jax.lax.dot_general(a, b, (((1,), (1,)), ((), ())), preferred_element_type=F32)


def _dot_tn(a, b):
    return jax.lax.dot_general(a, b, (((0,), (0,)), ((), ())), preferred_element_type=F32)


def _sigmoid(x):
    return 1.0 / (1.0 + jnp.exp(-x))


def _silu(x):
    return x * _sigmoid(x)


def _layer_norm(r, w, b):
    mu = jnp.mean(r, axis=-1, keepdims=True)
    d = r - mu
    var = jnp.mean(d * d, axis=-1, keepdims=True)
    return d * jax.lax.rsqrt(var + LN_EPS) * w + b


def _rope_att(slab, c, s1, s2):
    return slab * c + pltpu.roll(slab, LANES - 8, axis=1) * s1 + pltpu.roll(slab, 8, axis=1) * s2


def _rope_ret(slab, c, s):
    return slab * c + pltpu.roll(slab, LANES // 2, axis=1) * s


def _rope_tables(pos):
    pos = np.asarray(pos, np.float64)[:, None]
    half = ROPE_DIMS // 2
    inv = ROPE_THETA ** (-np.arange(half, dtype=np.float64) / half)
    ang = pos * inv[None, :]
    cos, sin = np.cos(ang), np.sin(ang)
    n = pos.shape[0]
    head_c = np.ones((n, HEAD_DIM)); head_s1 = np.zeros((n, HEAD_DIM)); head_s2 = np.zeros((n, HEAD_DIM))
    head_c[:, :half] = cos; head_c[:, half:2 * half] = cos
    head_s1[:, :half] = -sin
    head_s2[:, half:2 * half] = sin
    att = [np.tile(t, (1, LANES // HEAD_DIM)) for t in (head_c, head_s1, head_s2)]
    rhalf = RET_DK // 2
    rinv = RET_THETA ** (-np.arange(rhalf, dtype=np.float64) / rhalf)
    rang = pos * rinv[None, :]
    rc = np.concatenate([np.cos(rang), np.cos(rang)], axis=1)
    rs = np.concatenate([-np.sin(rang), np.sin(rang)], axis=1)
    return np.stack(att + [rc, rs], axis=0).astype(np.float32)


def _log_gamma():
    lin = np.linspace(math.log(1.0 / 32), math.log(1.0 / 512), RET_HEADS)
    return np.log1p(-np.exp(lin))


def _decay_tables():
    lg = _log_gamma()
    idx = np.arange(CHUNK, dtype=np.float64)
    diff = idx[:, None] - idx[None, :]
    dmat = np.where(diff[None] >= 0, np.exp(lg[:, None, None] * np.maximum(diff, 0.0)[None]), 0.0)
    dq = np.exp(lg[:, None] * (idx[None, :] + 1.0))[:, :, None] * np.ones((1, 1, LANES))
    dk = np.exp(lg[:, None] * (CHUNK - 1.0 - idx[None, :]))[:, :, None] * np.ones((1, 1, LANES))
    return np.concatenate([dmat, dq, dk], axis=0).astype(np.float32)


def _band_bias():
    i = np.arange(CHUNK)[:, None]
    j = np.arange(2 * CHUNK)[None, :]
    valid = (j >= i) & (j <= i + WINDOW)
    neg = np.float32(-1e30)
    b0 = np.where(valid, 0.0, neg)
    b1 = np.where(valid & (j >= CHUNK), 0.0, neg)
    return np.stack([b0, b1], axis=0).astype(np.float32)


def _ada_kernel(cs_ref, cp_ref, w1_ref, w2_ref, b1_ref, b2_ref, o_ref):
    rows = cs_ref.shape[0] + cp_ref.shape[0]
    pad = jnp.zeros((-rows % 16, D_MODEL), F32)
    a = _silu(jnp.concatenate([cs_ref[...], cp_ref[...], pad], axis=0)).astype(BF16)

    def emit(w_ref, b_ref):
        o_ref[...] = _dot(a, w_ref[...].astype(BF16))[0:rows] + b_ref[...]

    @pl.when(pl.program_id(0) < 3)
    def _():
        emit(w1_ref, b1_ref)

    @pl.when(pl.program_id(0) >= 3)
    def _():
        emit(w2_ref, b2_ref)


def _ada_mod(c_sample, c_prompt, w_mix, w_ffn, b_mix, b_ffn):
    rows = c_sample.shape[0] + c_prompt.shape[0]
    d = D_MODEL
    return pl.pallas_call(
        _ada_kernel,
        out_shape=jax.ShapeDtypeStruct((6, rows, d), F32),
        grid=(6,),
        in_specs=[
            pl.BlockSpec(c_sample.shape, lambda j: (0, 0)),
            pl.BlockSpec(c_prompt.shape, lambda j: (0, 0)),
            pl.BlockSpec((d, d), lambda j: (0, jnp.minimum(j, 2))),
            pl.BlockSpec((d, d), lambda j: (0, jnp.maximum(j - 3, 0))),
            pl.BlockSpec((1, d), lambda j: (0, jnp.minimum(j, 2))),
            pl.BlockSpec((1, d), lambda j: (0, jnp.maximum(j - 3, 0))),
        ],
        out_specs=pl.BlockSpec((None, rows, d), lambda j: (j, 0, 0)),
        compiler_params=pltpu.CompilerParams(
            dimension_semantics=("arbitrary",), vmem_limit_bytes=VMEM_LIMIT_BYTES),
        name="ada_mod",
    )(c_sample, c_prompt, w_mix, w_ffn, b_mix, b_ffn)


def _mixer_step(z_prev, z_cur, sinks_ref, x_ref, mods_ref, tab_ref, xp_ref, dec_ref, bias_ref,
                wq_ref, wrest_ref, wout_att_ref, wout_ret_ref, gnw_ref,
                y_ref, kout_ref, vout_ref, s_ref, kcarry, vcarry, mix_s,
                *, cur_b, prev_b, tab_row0, seq_start, tile, gl):
    n_chunks = tile // CHUNK
    x = x_ref[...]
    h = (x * (1.0 + mods_ref[1, pl.ds(cur_b, 1), :]) + mods_ref[0, pl.ds(cur_b, 1), :]).astype(BF16)
    n_fill = IN_WIDTH // FILL_WIDTH
    pending = list(range(n_fill))

    def fill(count):
        for _ in range(count):
            if pending:
                i = pending.pop(0)
                col = i * FILL_WIDTH
                w = (wq_ref[:, col:col + FILL_WIDTH] if col < ATT_WIDTH
                     else wrest_ref[:, col - ATT_WIDTH:col - ATT_WIDTH + FILL_WIDTH])
                z_cur[:, col:col + FILL_WIDTH] = _dot(h, w)

    gate = mods_ref[2, pl.ds(prev_b, 1), :]
    lo = jax.lax.broadcasted_iota(jnp.int32, (CHUNK, LANES), 1) < HEAD_DIM
    zero = jnp.zeros((CHUNK, LANES), F32)
    k_prev = [kcarry[g] for g in range(ATT_KV_HEADS)]
    v_prev = [vcarry[g] for g in range(ATT_KV_HEADS)]

    for c in range(n_chunks):
        rows = slice(c * CHUNK, (c + 1) * CHUNK)
        bias = bias_ref[seq_start] if c == 0 else bias_ref[0]
        trows = pl.ds(pl.multiple_of(tab_row0 + c * CHUNK, CHUNK), CHUNK)
        att_c, att_s1, att_s2 = tab_ref[0, trows, :], tab_ref[1, trows, :], tab_ref[2, trows, :]
        ret_c, ret_s = tab_ref[3, trows, :], tab_ref[4, trows, :]

        def zs(col):
            return z_prev[rows, col:col + LANES]

        q_stack = jnp.concatenate(
            [(_rope_att(zs(OFF_Q + j * LANES), att_c, att_s1, att_s2) * (HEAD_DIM ** -0.5)).astype(BF16)
             for j in range(4)], axis=0)
        k = _rope_att(zs(OFF_KV), att_c, att_s1, att_s2)
        v = zs(OFF_KV + LANES)
        k_cur = [jnp.where(lo, k, zero).astype(BF16), jnp.where(lo, zero, k).astype(BF16)]
        v_cur = [jnp.where(lo, v, zero).astype(BF16), jnp.where(lo, zero, v).astype(BF16)]
        k_band = [jnp.concatenate([k_prev[g], k_cur[g]], axis=0) for g in range(ATT_KV_HEADS)]
        v_band = [jnp.concatenate([v_prev[g], v_cur[g]], axis=0) for g in range(ATT_KV_HEADS)]
        k_prev, v_prev = k_cur, v_cur
        if c == n_chunks - 1:
            kout_ref[...] = k
            vout_ref[...] = v
            for g in range(ATT_KV_HEADS):
                kcarry[g] = k_cur[g]
                vcarry[g] = v_cur[g]

        ret = []
        for hh in range(RET_HEADS):
            rq = _rope_ret(zs(OFF_RQ + hh * LANES), ret_c, ret_s)
            rk = _rope_ret(zs(OFF_RK + hh * LANES), ret_c, ret_s) * (RET_DK ** -0.5)
            ret.append(dict(
                q=rq.astype(BF16), qd=(rq * dec_ref[RET_HEADS + hh]).astype(BF16),
                k=rk.astype(BF16), kd=(rk * dec_ref[2 * RET_HEADS + hh]).astype(BF16),
                v=zs(OFF_RV + hh * LANES).astype(BF16)))

        sc_att = [_dot_nt(q_stack, k_band[g]) for g in range(ATT_KV_HEADS)]
        sc_ret = [_dot_nt(r["q"], r["k"]) for r in ret]
        fill(FILL_PLAN[c][0])

        att = None
        for g in range(ATT_KV_HEADS):
            ws = []
            for j in range(4):
                sj = sc_att[g][j * CHUNK:(j + 1) * CHUNK, :] + bias
                sink = sinks_ref[g * ATT_GROUP + j]
                mx = jnp.maximum(jnp.max(sj, axis=-1, keepdims=True), sink)
                p = jnp.exp(sj - mx)
                den = jnp.sum(p, axis=-1, keepdims=True) + jnp.exp(sink - mx)
                ws.append((p * (1.0 / den)).astype(BF16))
            o = _dot(jnp.concatenate(ws, axis=0), v_band[g])
            att = o if att is None else att + o
        for j in range(4):
            mix_s[rows, j * LANES:(j + 1) * LANES] = att[j * CHUNK:(j + 1) * CHUNK, :].astype(BF16)

        outs = []
        for hh, r in enumerate(ret):
            state = s_ref[hh]
            sc = sc_ret[hh] * dec_ref[hh]
            outs.append(_dot(sc.astype(BF16), r["v"]) + _dot(r["qd"], state.astype(BF16)))
            s_ref[hh] = gl[hh] * state + _dot_tn(r["kd"], r["v"])
        fill(FILL_PLAN[c][1])
        for hh in range(RET_HEADS):
            cols = slice(hh * LANES, (hh + 1) * LANES)
            o = outs[hh]
            mu = jnp.mean(o, axis=-1, keepdims=True)
            d = o - mu
            var = jnp.mean(d * d, axis=-1, keepdims=True)
            nrm = d * jax.lax.rsqrt(var + GN_EPS) * gnw_ref[:, cols]
            mix_s[rows, ATT_WIDTH + hh * LANES:ATT_WIDTH + (hh + 1) * LANES] = (
                _silu(zs(OFF_RG + hh * LANES)) * nrm).astype(BF16)

    y = _dot(mix_s[...], jnp.concatenate([wout_att_ref[...], wout_ret_ref[...]], axis=0))
    fill(n_fill)
    y_ref[...] = DEEPNORM_ALPHA * xp_ref[...] + (1.0 + gate) * y


def _mixer_kernel(sinks_ref, x_ref, mods_ref, tab_ref, xp_ref, dec_ref, bias_ref,
                  wq_ref, wrest_ref, wout_att_ref, wout_ret_ref, gnw_ref,
                  y_ref, kout_ref, vout_ref, s_ref,
                  z_a, z_b, kcarry, vcarry, mix_s, *, tile, ns, nt, gl):
    t = pl.program_id(0)
    prev_tile = jnp.maximum(t - 1, 0)
    sa = jax.lax.rem(prev_tile, ns)
    seq_start = (sa == 0).astype(jnp.int32)
    cur_b = jax.lax.div(jnp.minimum(t, nt - 1), ns)
    prev_b = jax.lax.div(prev_tile, ns)
    parity = jax.lax.rem(t, 2)

    @pl.when(t == 0)
    def _():
        z_b[...] = jnp.zeros(z_b.shape, z_b.dtype)
        kcarry[...] = jnp.zeros(kcarry.shape, kcarry.dtype)
        vcarry[...] = jnp.zeros(vcarry.shape, vcarry.dtype)

    @pl.when(sa == 0)
    def _():
        s_ref[...] = jnp.zeros_like(s_ref)

    step = functools.partial(
        _mixer_step, sinks_ref=sinks_ref, x_ref=x_ref, mods_ref=mods_ref, tab_ref=tab_ref, xp_ref=xp_ref,
        dec_ref=dec_ref, bias_ref=bias_ref, wq_ref=wq_ref, wrest_ref=wrest_ref,
        wout_att_ref=wout_att_ref, wout_ret_ref=wout_ret_ref,
        gnw_ref=gnw_ref, y_ref=y_ref, kout_ref=kout_ref,
        vout_ref=vout_ref, s_ref=s_ref, kcarry=kcarry, vcarry=vcarry, mix_s=mix_s,
        cur_b=cur_b, prev_b=prev_b, tab_row0=sa * tile, seq_start=seq_start, tile=tile, gl=gl)

    @pl.when(parity == 0)
    def _():
        step(z_b, z_a)

    @pl.when(parity == 1)
    def _():
        step(z_a, z_b)


def _mixer_prompt(x, mods, mod_row0, sinks, tab, dec, bias, w_q, w_rest, w_out_att, w_out_ret, gn_w):
    batch, seq, _ = x.shape
    assert mod_row0 % batch == 0 and batch % 8 == 0
    tile = PROMPT_TILE
    ns = seq // tile
    gl = tuple(float(v) for v in np.exp(_log_gamma() * CHUNK).astype(np.float32))
    nt = batch * ns
    const = lambda shape: pl.BlockSpec(shape, lambda t, *_: (0,) * len(shape),
                                       pipeline_mode=pl.Buffered(1))
    cur_b = lambda t: jnp.minimum(t, nt - 1) // ns
    cur_s = lambda t: jnp.minimum(t, nt - 1) % ns
    prev_b = lambda t: jnp.maximum(t - 1, 0) // ns
    prev_s = lambda t: jnp.maximum(t - 1, 0) % ns
    grid_spec = pltpu.PrefetchScalarGridSpec(
        num_scalar_prefetch=1,
        grid=(nt + 1,),
        in_specs=[
            pl.BlockSpec((None, tile, D_MODEL), lambda t, *_: (cur_b(t), cur_s(t), 0)),
            pl.BlockSpec((3, batch, D_MODEL), lambda t, *_: (0, mod_row0 // batch, 0),
                         pipeline_mode=pl.Buffered(1)),
            const((5, seq, LANES)),
            pl.BlockSpec((None, tile, D_MODEL), lambda t, *_: (prev_b(t), prev_s(t), 0)),
            const((3 * RET_HEADS, CHUNK, LANES)),
            const((2, CHUNK, 2 * CHUNK)),
            const((D_MODEL, ATT_WIDTH)),
            const((D_MODEL, IN_WIDTH - ATT_WIDTH)),
            const((ATT_WIDTH, D_MODEL)),
            const((RET_WIDTH, D_MODEL)),
            const((1, RET_WIDTH)),
        ],
        out_specs=[
            pl.BlockSpec((None, tile, D_MODEL), lambda t, *_: (prev_b(t), prev_s(t), 0)),
            pl.BlockSpec((None, WINDOW, LANES), lambda t, *_: (prev_b(t), 0, 0)),
            pl.BlockSpec((None, WINDOW, LANES), lambda t, *_: (prev_b(t), 0, 0)),
            pl.BlockSpec((None, RET_HEADS, RET_DK, LANES), lambda t, *_: (prev_b(t), 0, 0, 0)),
        ],
        scratch_shapes=[
            pltpu.VMEM((tile, IN_WIDTH), F32),
            pltpu.VMEM((tile, IN_WIDTH), F32),
            pltpu.VMEM((ATT_KV_HEADS, CHUNK, LANES), BF16),
            pltpu.VMEM((ATT_KV_HEADS, CHUNK, LANES), BF16),
            pltpu.VMEM((tile, D_MODEL), BF16),
        ],
    )
    return pl.pallas_call(
        functools.partial(_mixer_kernel, tile=tile, ns=ns, nt=nt, gl=gl),
        out_shape=[
            jax.ShapeDtypeStruct((batch, seq, D_MODEL), F32),
            jax.ShapeDtypeStruct((batch, WINDOW, LANES), F32),
            jax.ShapeDtypeStruct((batch, WINDOW, LANES), F32),
            jax.ShapeDtypeStruct((batch, RET_HEADS, RET_DK, LANES), F32),
        ],
        grid_spec=grid_spec,
        compiler_params=pltpu.CompilerParams(
            dimension_semantics=("arbitrary",), vmem_limit_bytes=VMEM_LIMIT_BYTES),
        name="mixer_prompt",
    )(sinks, x, mods, tab, x, dec, bias, w_q, w_rest, w_out_att, w_out_ret, gn_w)


def _ffn_kernel(r_ref, mods_ref, wup_ref, wdown_ref, ln1w_ref, ln1b_ref, lnw_ref, lnb_ref, o_ref,
                *, parts, tiles_per_mod_row):
    rows = r_ref.shape[0] // parts
    sl = [slice(p * rows, (p + 1) * rows) for p in range(parts)]
    if tiles_per_mod_row:
        seq = jax.lax.div(pl.program_id(0), tiles_per_mod_row)
        mod = lambda k, p: mods_ref[k, pl.ds(seq, 1), :]
    else:
        mod = lambda k, p: mods_ref[k, sl[p], :]
    ys = [_layer_norm(r_ref[sl[p], :], ln1w_ref[...], ln1b_ref[...]) for p in range(parts)]
    hs = [(ys[p] * (1.0 + mod(1, p)) + mod(0, p)).astype(BF16) for p in range(parts)]
    def act(h):
        return jnp.concatenate(
            [(_silu(_dot(h, wup_ref[:, j:j + FFN_COLS]))
              * _dot(h, wup_ref[:, D_FF + j:D_FF + j + FFN_COLS])).astype(BF16)
             for j in range(0, D_FF, FFN_COLS)], axis=1)

    acts = [act(h) for h in hs]
    fs = [_dot(a, wdown_ref[...]) for a in acts]
    for p in range(parts):
        r = DEEPNORM_ALPHA * ys[p] + (1.0 + mod(2, p)) * fs[p]
        o_ref[sl[p], :] = _layer_norm(r, lnw_ref[...], lnb_ref[...])


def _ffn(y, mods, mod_row0, mod_rows, tiles_per_mod_row, w_up, w_down, ln1_w, ln1_b, ln_w, ln_b, tile, parts):
    rows, _ = y.shape
    nt = rows // tile
    assert mod_row0 % mod_rows == 0 and (tiles_per_mod_row > 0 or mod_rows == tile)
    const = lambda shape: pl.BlockSpec(shape, lambda t: (0,) * len(shape), pipeline_mode=pl.Buffered(1))
    return pl.pallas_call(
        functools.partial(_ffn_kernel, parts=parts, tiles_per_mod_row=tiles_per_mod_row),
        out_shape=jax.ShapeDtypeStruct(y.shape, F32),
        grid=(nt,),
        in_specs=[
            pl.BlockSpec((tile, D_MODEL), lambda t: (t, 0)),
            pl.BlockSpec((3, mod_rows, D_MODEL), lambda t: (1, mod_row0 // mod_rows, 0),
                         pipeline_mode=pl.Buffered(1)),
            const((D_MODEL, 2 * D_FF)),
            const((D_FF, D_MODEL)),
            const((1, D_MODEL)),
            const((1, D_MODEL)),
            const((1, D_MODEL)),
            const((1, D_MODEL)),
        ],
        out_specs=pl.BlockSpec((tile, D_MODEL), lambda t: (t, 0)),
        compiler_params=pltpu.CompilerParams(
            dimension_semantics=("arbitrary",), vmem_limit_bytes=VMEM_LIMIT_BYTES),
        name="ffn",
    )(y, mods, w_up, w_down, ln1_w, ln1_b, ln_w, ln_b)


def _sample_kernel(sinks_ref, x_ref, mod_ref, tab_ref, wq_ref, wrest_ref, wout_att_ref, wout_ret_ref,
                   gnw_ref,
                   ck_ref, cv_ref, st_ref,
                   y_ref, ko_ref, vo_ref, so_ref,
                   z_s, mix_s, *, g1, n_blocks):
    i = pl.program_id(0)
    bb = SAMPLE_BLOCK

    @pl.when(i == 0)
    def _():
        m = mod_ref[...]
        h = (x_ref[...] * (1.0 + m[1]) + m[0]).astype(BF16)
        z = jnp.concatenate([_dot(h, wq_ref[...]), _dot(h, wrest_ref[...])], axis=1)
        att_c, att_s1, att_s2 = tab_ref[0], tab_ref[1], tab_ref[2]
        ret_c, ret_s = tab_ref[3], tab_ref[4]
        for j in range(4):
            col = OFF_Q + j * LANES
            z_s[:, col:col + LANES] = _rope_att(z[:, col:col + LANES], att_c, att_s1, att_s2) * (HEAD_DIM ** -0.5)
            col = OFF_RQ + j * LANES
            z_s[:, col:col + LANES] = _rope_ret(z[:, col:col + LANES], ret_c, ret_s)
            col = OFF_RK + j * LANES
            z_s[:, col:col + LANES] = _rope_ret(z[:, col:col + LANES], ret_c, ret_s) * (RET_DK ** -0.5)
        z_s[:, OFF_KV:OFF_KV + LANES] = _rope_att(z[:, OFF_KV:OFF_KV + LANES], att_c, att_s1, att_s2)
        z_s[:, OFF_KV + LANES:OFF_RQ] = z[:, OFF_KV + LANES:OFF_RQ]
        z_s[:, OFF_RV:OFF_RG] = z[:, OFF_RV:OFF_RG]
        z_s[:, OFF_RG:] = _silu(z[:, OFF_RG:])

    lane_w = jax.lax.broadcasted_iota(jnp.int32, (LANES, WINDOW), 1)
    last_lane = lane_w == WINDOW - 1
    lo_row = lane_w[0:1, :] < HEAD_DIM
    r4 = jax.lax.broadcasted_iota(jnp.int32, (ATT_GROUP, 1), 0)
    r8 = jax.lax.broadcasted_iota(jnp.int32, (ATT_HEADS, 1), 0)
    sink8 = jnp.zeros((ATT_HEADS, 1), F32)
    for hd in range(ATT_HEADS):
        sink8 = jnp.where(r8 == hd, sinks_ref[hd], sink8)
    head_blk = (jax.lax.broadcasted_iota(jnp.int32, (16, RET_WIDTH), 1) // LANES
                == jax.lax.broadcasted_iota(jnp.int32, (16, RET_WIDTH), 0))
    g1_col = jnp.where(r4 == 0, g1[0], jnp.where(r4 == 1, g1[1], jnp.where(r4 == 2, g1[2], g1[3])))

    def rows16(a):
        return jnp.concatenate([a, jnp.zeros((16 - a.shape[0], a.shape[1]), F32)], axis=0)

    def heads(row, off):
        return jnp.concatenate([row[:, off + j * LANES:off + (j + 1) * LANES] for j in range(4)], axis=0)

    def block_rows(a4):
        wide = jnp.concatenate([rows16(a4)] * RET_HEADS, axis=1)
        return jnp.where(head_blk, wide, jnp.zeros_like(wide))

    r0 = pl.multiple_of(i * bb, bb)
    zblk = z_s[pl.ds(r0, bb), :]
    rows = [zblk[b:b + 1, :] for b in range(bb)]

    kn = [r[:, OFF_KV:OFF_KV + LANES] for r in rows]
    vn = [r[:, OFF_KV + LANES:OFF_RQ] for r in rows]
    kn_t = zblk[:, OFF_KV:OFF_KV + LANES].T
    vn_t = zblk[:, OFF_KV + LANES:OFF_RQ].T
    q8 = []
    for r in rows:
        q4 = heads(r, OFF_Q)
        zero4 = jnp.zeros_like(q4)
        q8.append(jnp.concatenate([jnp.where(lo_row, q4, zero4), jnp.where(lo_row, zero4, q4)], axis=0))
    sc, oh = [], []
    for b in range(bb):
        sc.append(_dot(rows16(q8[b]).astype(BF16), ck_ref[b].astype(BF16))[0:ATT_HEADS])
    for b in range(bb):
        rq = heads(rows[b], OFF_RQ)
        rk = heads(rows[b], OFF_RK)
        rv = heads(rows[b], OFF_RV)
        state = st_ref[b].reshape(RET_WIDTH, LANES)
        o2 = _dot(block_rows(rq * g1_col).astype(BF16), state.astype(BF16))[0:RET_HEADS]
        outer = _dot_tn(block_rows(rk).astype(BF16), rows16(rv).astype(BF16))
        for hh in range(RET_HEADS):
            so_ref[b, hh] = g1[hh] * st_ref[b, hh] + outer[hh * RET_DK:(hh + 1) * RET_DK, :]
        o = jnp.sum(rq * rk, axis=-1, keepdims=True) * rv + o2
        mu = jnp.mean(o, axis=-1, keepdims=True)
        d = o - mu
        var = jnp.mean(d * d, axis=-1, keepdims=True)
        oh.append(heads(rows[b], OFF_RG) * (d * jax.lax.rsqrt(var + GN_EPS) * gnw_ref[...]))

    ws, wn = [], []
    for b in range(bb):
        sn = jnp.sum(q8[b] * kn[b], axis=-1, keepdims=True)
        mx = jnp.maximum(jnp.maximum(jnp.max(sc[b], axis=-1, keepdims=True), sn), sink8)
        p = jnp.exp(sc[b] - mx)
        pn = jnp.exp(sn - mx)
        inv = 1.0 / (jnp.sum(p, axis=-1, keepdims=True) + pn + jnp.exp(sink8 - mx))
        ws.append(rows16(p * inv).astype(BF16))
        wn.append(pn * inv)

    att = []
    for b in range(bb):
        o8 = _dot_nt(ws[b], cv_ref[b].astype(BF16))[0:ATT_HEADS] + wn[b] * vn[b]
        att.append(jnp.where(lo_row, o8[0:ATT_GROUP], o8[ATT_GROUP:]))

    for b in range(bb):
        ko_ref[b] = jnp.where(last_lane, kn_t[:, b:b + 1], pltpu.roll(ck_ref[b], WINDOW - 1, axis=1))
        vo_ref[b] = jnp.where(last_lane, vn_t[:, b:b + 1], pltpu.roll(cv_ref[b], WINDOW - 1, axis=1))
    blk_rows = pl.ds(r0, bb)
    for j in range(4):
        mix_s[blk_rows, j * LANES:(j + 1) * LANES] = jnp.concatenate(
            [a[j:j + 1, :] for a in att], axis=0)
        mix_s[blk_rows, ATT_WIDTH + j * LANES:ATT_WIDTH + (j + 1) * LANES] = jnp.concatenate(
            [o[j:j + 1, :] for o in oh], axis=0)

    @pl.when(i == n_blocks - 1)
    def _():
        y = (_dot(mix_s[:, :ATT_WIDTH].astype(BF16), wout_att_ref[...])
             + _dot(mix_s[:, ATT_WIDTH:].astype(BF16), wout_ret_ref[...]))
        y_ref[...] = DEEPNORM_ALPHA * x_ref[...] + (1.0 + mod_ref[2]) * y


def _sample_step(sinks, x, mods, tab, w_q, w_rest, w_out_att, w_out_ret, gn_w4,
                 cache_k, cache_v, state):
    n = x.shape[0]
    bb = SAMPLE_BLOCK
    n_blocks = n // bb
    g1 = tuple(float(v) for v in np.exp(_log_gamma()).astype(np.float32))
    full = lambda *shape: pl.BlockSpec(shape, lambda i, *_: (0,) * len(shape))
    blk = lambda *shape: pl.BlockSpec((bb,) + shape, lambda i, *_: (i,) + (0,) * len(shape))
    grid_spec = pltpu.PrefetchScalarGridSpec(
        num_scalar_prefetch=1,
        grid=(n_blocks,),
        in_specs=[
            full(n, D_MODEL), full(3, n, D_MODEL), full(5, 1, LANES),
            full(D_MODEL, ATT_WIDTH), full(D_MODEL, IN_WIDTH - ATT_WIDTH),
            full(ATT_WIDTH, D_MODEL), full(RET_WIDTH, D_MODEL),
            full(RET_HEADS, LANES),
            blk(WINDOW, LANES), blk(WINDOW, LANES), blk(RET_HEADS, RET_DK, LANES),
        ],
        out_specs=[full(n, D_MODEL), blk(WINDOW, LANES), blk(WINDOW, LANES),
                   blk(RET_HEADS, RET_DK, LANES)],
        scratch_shapes=[pltpu.VMEM((n, IN_WIDTH), F32), pltpu.VMEM((n, D_MODEL), F32)],
    )
    return pl.pallas_call(
        functools.partial(_sample_kernel, g1=g1, n_blocks=n_blocks),
        out_shape=[
            jax.ShapeDtypeStruct((n, D_MODEL), F32),
            jax.ShapeDtypeStruct((n, WINDOW, LANES), F32),
            jax.ShapeDtypeStruct((n, WINDOW, LANES), F32),
            jax.ShapeDtypeStruct((n, RET_HEADS, RET_DK, LANES), F32),
        ],
        grid_spec=grid_spec,
        compiler_params=pltpu.CompilerParams(
            dimension_semantics=("arbitrary",), vmem_limit_bytes=VMEM_LIMIT_BYTES),
        name="sample_step",
    )(sinks, x, mods, tab, w_q, w_rest, w_out_att, w_out_ret, gn_w4, cache_k, cache_v, state)


def _slab_order(w, axis):
    shape = w.shape
    pre, post = shape[:axis], shape[axis + 1:]
    w = w.reshape(pre + (ATT_KV_HEADS, ATT_GROUP, HEAD_DIM) + post)
    w = jnp.swapaxes(w, axis, axis + 1)
    return w.reshape(shape)


def kernel(x_prompt, x_sample, c_prompt, c_sample, cache_k, cache_v, state_ret, w_ada_mix, b_ada_mix, w_in, att_sinks, ret_gn_w, w_out, ln1_w, ln1_b, w_ada_ffn, b_ada_ffn, w_up, w_down, ln2_w, ln2_b):
    batch, seq, _ = x_prompt.shape
    n_s = x_sample.shape[0]
    l = 0
    assert w_in.shape[0] == DEPTH == 1

    w_in_f = w_in.reshape(D_MODEL, IN_WIDTH)
    w_out_f = w_out.reshape(D_MODEL, D_MODEL)
    w_q = _slab_order(w_in_f[:, :ATT_WIDTH].astype(BF16), 1)
    w_rest = w_in_f[:, ATT_WIDTH:].astype(BF16)
    w_out_att = _slab_order(w_out_f[:ATT_WIDTH].astype(BF16), 0)
    w_out_ret = w_out_f[ATT_WIDTH:].astype(BF16)
    w_up_b = w_up.reshape(D_MODEL, 2 * D_FF).astype(BF16)
    w_down_b = w_down.reshape(D_FF, D_MODEL).astype(BF16)
    sinks = att_sinks[l]
    gn_w = ret_gn_w[l].reshape(1, RET_WIDTH)
    ln1w, ln1b = ln1_w[l].reshape(1, D_MODEL), ln1_b[l].reshape(1, D_MODEL)
    ln2w, ln2b = ln2_w[l].reshape(1, D_MODEL), ln2_b[l].reshape(1, D_MODEL)

    tab_p = jnp.asarray(_rope_tables(np.arange(seq)))
    tab_s = jnp.asarray(_rope_tables(np.array([PAST_LEN])))
    dec = jnp.asarray(_decay_tables())
    bias = jnp.asarray(_band_bias())

    mods = _ada_mod(c_sample, c_prompt, w_ada_mix.reshape(D_MODEL, 3 * D_MODEL),
                    w_ada_ffn.reshape(D_MODEL, 3 * D_MODEL), b_ada_mix, b_ada_ffn)

    y1p, kp, vp, sp = _mixer_prompt(x_prompt, mods, n_s, sinks, tab_p, dec, bias, w_q, w_rest,
                                    w_out_att, w_out_ret, gn_w)
    yp = _ffn(y1p.reshape(batch * seq, D_MODEL), mods, n_s, batch, seq // FFN_TILE, w_up_b, w_down_b,
              ln1w, ln1b, ln2w, ln2b, FFN_TILE, FFN_PARTS).reshape(batch, seq, D_MODEL)

    xs = x_sample.reshape(n_s, D_MODEL)
    y1s, ks, vs, ss = _sample_step(
        sinks, xs, mods, tab_s, w_q, w_rest, w_out_att, w_out_ret, ret_gn_w.reshape(RET_HEADS, LANES),
        jnp.swapaxes(cache_k.reshape(n_s, WINDOW, LANES), 1, 2),
        jnp.swapaxes(cache_v.reshape(n_s, WINDOW, LANES), 1, 2),
        state_ret.reshape(n_s, RET_HEADS, RET_DK, LANES))
    ks, vs = jnp.swapaxes(ks, 1, 2), jnp.swapaxes(vs, 1, 2)
    ys = _ffn(y1s, mods, 0, n_s, 0, w_up_b, w_down_b, ln1w, ln1b, ln2w, ln2b, n_s, 1)

    kv_shape = (1, batch, WINDOW, ATT_KV_HEADS, HEAD_DIM)
    kvs_shape = (1, n_s, WINDOW, ATT_KV_HEADS, HEAD_DIM)
    return (yp, ys.reshape(n_s, 1, D_MODEL),
            kp.reshape(kv_shape), vp.reshape(kv_shape), sp[None],
            ks.reshape(kvs_shape), vs.reshape(kvs_shape), ss[None])
```

```python
import functools
import math

import numpy as np
import jax
import jax.numpy as jnp
from jax.experimental import pallas as pl
from jax.experimental.pallas import tpu as pltpu

D_MODEL = 1024
WINDOW = 128
CHUNK = 128
ATT_HEADS = 8
ATT_KV_HEADS = 2
ATT_GROUP = ATT_HEADS // ATT_KV_HEADS
HEAD_DIM = 64
ROPE_DIMS = HEAD_DIM // 4
ROPE_THETA = 500000.0
RET_HEADS = 4
RET_DK = 128
RET_THETA = 10000.0
ATT_WIDTH = ATT_HEADS * HEAD_DIM
KV_WIDTH = ATT_KV_HEADS * HEAD_DIM
RET_WIDTH = RET_HEADS * RET_DK
IN_WIDTH = ATT_WIDTH + 2 * KV_WIDTH + 4 * RET_WIDTH
D_FF = 2816
DEPTH = 1
DEEPNORM_ALPHA = (2 * DEPTH) ** 0.25
LN_EPS = 1e-5
GN_EPS = 1e-6
PAST_LEN = 16384

OFF_Q = 0
OFF_KV = ATT_WIDTH
OFF_RQ = OFF_KV + 2 * KV_WIDTH
OFF_RK = OFF_RQ + RET_WIDTH
OFF_RV = OFF_RK + RET_WIDTH
OFF_RG = OFF_RV + RET_WIDTH

LANES = 128
VMEM_LIMIT_BYTES = 56 * 1024 * 1024

PROMPT_TILE = 512
FILL_WIDTH = 256
FILL_PLAN = ((2, 1), (2, 1), (2, 1), (2, 0))
FFN_TILE = 1024
FFN_PARTS = 4
FFN_COLS = 256
SAMPLE_BLOCK = 8

BF16 = jnp.bfloat16
F32 = jnp.float32


def _dot(a, b):
    return jnp.dot(a, b, preferred_element_type=F32)


def _dot_nt(a, b):
    return jax.lax.dot_general(a, b, (((1,), (1,)), ((), ())), preferred_element_type=F32)


def _dot_tn(a, b):
    return jax.lax.dot_general(a, b, (((0,), (0,)), ((), ())), preferred_element_type=F32)


def _sigmoid(x):
    return 1.0 / (1.0 + jnp.exp(-x))


def _silu(x):
    return x * _sigmoid(x)


def _layer_norm(r, w, b):
    mu = jnp.mean(r, axis=-1, keepdims=True)
    d = r - mu
    var = jnp.mean(d * d, axis=-1, keepdims=True)
    return d * jax.lax.rsqrt(var + LN_EPS) * w + b


def _rope_att(slab, c, s1, s2):
    return slab * c + pltpu.roll(slab, LANES - 8, axis=1) * s1 + pltpu.roll(slab, 8, axis=1) * s2


def _rope_ret(slab, c, s):
    return slab * c + pltpu.roll(slab, LANES // 2, axis=1) * s


def _rope_tables(pos):
    pos = np.asarray(pos, np.float64)[:, None]
    half = ROPE_DIMS // 2
    inv = ROPE_THETA ** (-np.arange(half, dtype=np.float64) / half)
    ang = pos * inv[None, :]
    cos, sin = np.cos(ang), np.sin(ang)
    n = pos.shape[0]
    head_c = np.ones((n, HEAD_DIM)); head_s1 = np.zeros((n, HEAD_DIM)); head_s2 = np.zeros((n, HEAD_DIM))
    head_c[:, :half] = cos; head_c[:, half:2 * half] = cos
    head_s1[:, :half] = -sin
    head_s2[:, half:2 * half] = sin
    att = [np.tile(t, (1, LANES // HEAD_DIM)) for t in (head_c, head_s1, head_s2)]
    rhalf = RET_DK // 2
    rinv = RET_THETA ** (-np.arange(rhalf, dtype=np.float64) / rhalf)
    rang = pos * rinv[None, :]
    rc = np.concatenate([np.cos(rang), np.cos(rang)], axis=1)
    rs = np.concatenate([-np.sin(rang), np.sin(rang)], axis=1)
    return np.stack(att + [rc, rs], axis=0).astype(np.float32)


def _log_gamma():
    lin = np.linspace(math.log(1.0 / 32), math.log(1.0 / 512), RET_HEADS)
    return np.log1p(-np.exp(lin))


def _decay_tables():
    lg = _log_gamma()
    idx = np.arange(CHUNK, dtype=np.float64)
    diff = idx[:, None] - idx[None, :]
    dmat = np.where(diff[None] >= 0, np.exp(lg[:, None, None] * np.maximum(diff, 0.0)[None]), 0.0)
    dq = np.exp(lg[:, None] * (idx[None, :] + 1.0))[:, :, None] * np.ones((1, 1, LANES))
    dk = np.exp(lg[:, None] * (CHUNK - 1.0 - idx[None, :]))[:, :, None] * np.ones((1, 1, LANES))
    return np.concatenate([dmat, dq, dk], axis=0).astype(np.float32)


def _band_bias():
    i = np.arange(CHUNK)[:, None]
    j = np.arange(2 * CHUNK)[None, :]
    valid = (j >= i) & (j <= i + WINDOW)
    neg = np.float32(-1e30)
    b0 = np.where(valid, 0.0, neg)
    b1 = np.where(valid & (j >= CHUNK), 0.0, neg)
    return np.stack([b0, b1], axis=0).astype(np.float32)


def _ada_kernel(cs_ref, cp_ref, w1_ref, w2_ref, b1_ref, b2_ref, o_ref):
    rows = cs_ref.shape[0] + cp_ref.shape[0]
    pad = jnp.zeros((-rows % 16, D_MODEL), F32)
    a = _silu(jnp.concatenate([cs_ref[...], cp_ref[...], pad], axis=0)).astype(BF16)

    def emit(w_ref, b_ref):
        o_ref[...] = _dot(a, w_ref[...].astype(BF16))[0:rows] + b_ref[...]

    @pl.when(pl.program_id(0) < 3)
    def _():
        emit(w1_ref, b1_ref)

    @pl.when(pl.program_id(0) >= 3)
    def _():
        emit(w2_ref, b2_ref)


def _ada_mod(c_sample, c_prompt, w_mix, w_ffn, b_mix, b_ffn):
    rows = c_sample.shape[0] + c_prompt.shape[0]
    d = D_MODEL
    return pl.pallas_call(
        _ada_kernel,
        out_shape=jax.ShapeDtypeStruct((6, rows, d), F32),
        grid=(6,),
        in_specs=[
            pl.BlockSpec(c_sample.shape, lambda j: (0, 0)),
            pl.BlockSpec(c_prompt.shape, lambda j: (0, 0)),
            pl.BlockSpec((d, d), lambda j: (0, jnp.minimum(j, 2))),
            pl.BlockSpec((d, d), lambda j: (0, jnp.maximum(j - 3, 0))),
            pl.BlockSpec((1, d), lambda j: (0, jnp.minimum(j, 2))),
            pl.BlockSpec((1, d), lambda j: (0, jnp.maximum(j - 3, 0))),
        ],
        out_specs=pl.BlockSpec((None, rows, d), lambda j: (j, 0, 0)),
        compiler_params=pltpu.CompilerParams(
            dimension_semantics=("arbitrary",), vmem_limit_bytes=VMEM_LIMIT_BYTES),
        name="ada_mod",
    )(c_sample, c_prompt, w_mix, w_ffn, b_mix, b_ffn)


def _mixer_step(z_prev, z_cur, sinks_ref, x_ref, mods_ref, tab_ref, xp_ref, dec_ref, bias_ref,
                wq_ref, wrest_ref, wout_att_ref, wout_ret_ref, gnw_ref,
                y_ref, kout_ref, vout_ref, s_ref, kcarry, vcarry, mix_s,
                *, cur_b, prev_b, seq_start, tile, gl):
    n_chunks = tile // CHUNK
    x = x_ref[...]
    h = (x * (1.0 + mods_ref[1, pl.ds(cur_b, 1), :]) + mods_ref[0, pl.ds(cur_b, 1), :]).astype(BF16)
    n_fill = IN_WIDTH // FILL_WIDTH
    pending = list(range(n_fill))

    def fill(count):
        for _ in range(count):
            if pending:
                i = pending.pop(0)
                col = i * FILL_WIDTH
                w = (wq_ref[:, col:col + FILL_WIDTH] if col < ATT_WIDTH
                     else wrest_ref[:, col - ATT_WIDTH:col - ATT_WIDTH + FILL_WIDTH])
                z_cur[:, col:col + FILL_WIDTH] = _dot(h, w)

    gate = mods_ref[2, pl.ds(prev_b, 1), :]
    lo = jax.lax.broadcasted_iota(jnp.int32, (CHUNK, LANES), 1) < HEAD_DIM
    zero = jnp.zeros((CHUNK, LANES), F32)
    k_prev = [kcarry[g] for g in range(ATT_KV_HEADS)]
    v_prev = [vcarry[g] for g in range(ATT_KV_HEADS)]

    for c in range(n_chunks):
        rows = slice(c * CHUNK, (c + 1) * CHUNK)
        bias = bias_ref[seq_start] if c == 0 else bias_ref[0]
        att_c, att_s1, att_s2 = tab_ref[0, rows, :], tab_ref[1, rows, :], tab_ref[2, rows, :]
        ret_c, ret_s = tab_ref[3, rows, :], tab_ref[4, rows, :]

        def zs(col):
            return z_prev[rows, col:col + LANES]

        q_stack = jnp.concatenate(
            [(_rope_att(zs(OFF_Q + j * LANES), att_c, att_s1, att_s2) * (HEAD_DIM ** -0.5)).astype(BF16)
             for j in range(4)], axis=0)
        k = _rope_att(zs(OFF_KV), att_c, att_s1, att_s2)
        v = zs(OFF_KV + LANES)
        k_cur = [jnp.where(lo, k, zero).astype(BF16), jnp.where(lo, zero, k).astype(BF16)]
        v_cur = [jnp.where(lo, v, zero).astype(BF16), jnp.where(lo, zero, v).astype(BF16)]
        k_band = [jnp.concatenate([k_prev[g], k_cur[g]], axis=0) for g in range(ATT_KV_HEADS)]
        v_band = [jnp.concatenate([v_prev[g], v_cur[g]], axis=0) for g in range(ATT_KV_HEADS)]
        k_prev, v_prev = k_cur, v_cur
        if c == n_chunks - 1:
            kout_ref[...] = k
            vout_ref[...] = v
            for g in range(ATT_KV_HEADS):
                kcarry[g] = k_cur[g]
                vcarry[g] = v_cur[g]

        ret = []
        for hh in range(RET_HEADS):
            rq = _rope_ret(zs(OFF_RQ + hh * LANES), ret_c, ret_s)
            rk = _rope_ret(zs(OFF_RK + hh * LANES), ret_c, ret_s) * (RET_DK ** -0.5)
            ret.append(dict(
                q=rq.astype(BF16), qd=(rq * dec_ref[RET_HEADS + hh]).astype(BF16),
                k=rk.astype(BF16), kd=(rk * dec_ref[2 * RET_HEADS + hh]).astype(BF16),
                v=zs(OFF_RV + hh * LANES).astype(BF16)))

        sc_att = [_dot_nt(q_stack, k_band[g]) for g in range(ATT_KV_HEADS)]
        sc_ret = [_dot_nt(r["q"], r["k"]) for r in ret]
        fill(FILL_PLAN[c][0])

        att = None
        for g in range(ATT_KV_HEADS):
            ws = []
            for j in range(4):
                sj = sc_att[g][j * CHUNK:(j + 1) * CHUNK, :] + bias
                sink = sinks_ref[g * ATT_GROUP + j]
                mx = jnp.maximum(jnp.max(sj, axis=-1, keepdims=True), sink)
                p = jnp.exp(sj - mx)
                den = jnp.sum(p, axis=-1, keepdims=True) + jnp.exp(sink - mx)
                ws.append((p * (1.0 / den)).astype(BF16))
            o = _dot(jnp.concatenate(ws, axis=0), v_band[g])
            att = o if att is None else att + o
        for j in range(4):
            mix_s[rows, j * LANES:(j + 1) * LANES] = att[j * CHUNK:(j + 1) * CHUNK, :].astype(BF16)

        outs = []
        for hh, r in enumerate(ret):
            state = s_ref[hh]
            sc = sc_ret[hh] * dec_ref[hh]
            outs.append(_dot(sc.astype(BF16), r["v"]) + _dot(r["qd"], state.astype(BF16)))
            s_ref[hh] = gl[hh] * state + _dot_tn(r["kd"], r["v"])
        fill(FILL_PLAN[c][1])
        for hh in range(RET_HEADS):
            cols = slice(hh * LANES, (hh + 1) * LANES)
            o = outs[hh]
            mu = jnp.mean(o, axis=-1, keepdims=True)
            d = o - mu
            var = jnp.mean(d * d, axis=-1, keepdims=True)
            nrm = d * jax.lax.rsqrt(var + GN_EPS) * gnw_ref[:, cols]
            mix_s[rows, ATT_WIDTH + hh * LANES:ATT_WIDTH + (hh + 1) * LANES] = (
                _silu(zs(OFF_RG + hh * LANES)) * nrm).astype(BF16)

    y = _dot(mix_s[...], jnp.concatenate([wout_att_ref[...], wout_ret_ref[...]], axis=0))
    fill(n_fill)
    y_ref[...] = DEEPNORM_ALPHA * xp_ref[...] + (1.0 + gate) * y


def _mixer_kernel(sinks_ref, x_ref, mods_ref, tab_ref, xp_ref, dec_ref, bias_ref,
                  wq_ref, wrest_ref, wout_att_ref, wout_ret_ref, gnw_ref,
                  y_ref, kout_ref, vout_ref, s_ref,
                  z_a, z_b, kcarry, vcarry, mix_s, *, tile, ns, nt, gl):
    t = pl.program_id(0)
    prev_tile = jnp.maximum(t - 1, 0)
    sa = jax.lax.rem(prev_tile, ns)
    seq_start = (sa == 0).astype(jnp.int32)
    cur_b = jax.lax.div(jnp.minimum(t, nt - 1), ns)
    prev_b = jax.lax.div(prev_tile, ns)
    parity = jax.lax.rem(t, 2)

    @pl.when(t == 0)
    def _():
        z_b[...] = jnp.zeros(z_b.shape, z_b.dtype)
        kcarry[...] = jnp.zeros(kcarry.shape, kcarry.dtype)
        vcarry[...] = jnp.zeros(vcarry.shape, vcarry.dtype)

    @pl.when(sa == 0)
    def _():
        s_ref[...] = jnp.zeros_like(s_ref)

    step = functools.partial(
        _mixer_step, sinks_ref=sinks_ref, x_ref=x_ref, mods_ref=mods_ref, tab_ref=tab_ref, xp_ref=xp_ref,
        dec_ref=dec_ref, bias_ref=bias_ref, wq_ref=wq_ref, wrest_ref=wrest_ref,
        wout_att_ref=wout_att_ref, wout_ret_ref=wout_ret_ref,
        gnw_ref=gnw_ref, y_ref=y_ref, kout_ref=kout_ref,
        vout_ref=vout_ref, s_ref=s_ref, kcarry=kcarry, vcarry=vcarry, mix_s=mix_s,
        cur_b=cur_b, prev_b=prev_b, seq_start=seq_start, tile=tile, gl=gl)

    @pl.when(parity == 0)
    def _():
        step(z_b, z_a)

    @pl.when(parity == 1)
    def _():
        step(z_a, z_b)


def _mixer_prompt(x, mods, mod_row0, sinks, tab, dec, bias, w_q, w_rest, w_out_att, w_out_ret, gn_w):
    batch, seq, _ = x.shape
    assert mod_row0 % batch == 0 and batch % 8 == 0
    tile = PROMPT_TILE
    ns = seq // tile
    gl = tuple(float(v) for v in np.exp(_log_gamma() * CHUNK).astype(np.float32))
    nt = batch * ns
    const = lambda shape: pl.BlockSpec(shape, lambda t, *_: (0,) * len(shape),
                                       pipeline_mode=pl.Buffered(1))
    cur_b = lambda t: jnp.minimum(t, nt - 1) // ns
    cur_s = lambda t: jnp.minimum(t, nt - 1) % ns
    prev_b = lambda t: jnp.maximum(t - 1, 0) // ns
    prev_s = lambda t: jnp.maximum(t - 1, 0) % ns
    grid_spec = pltpu.PrefetchScalarGridSpec(
        num_scalar_prefetch=1,
        grid=(nt + 1,),
        in_specs=[
            pl.BlockSpec((None, tile, D_MODEL), lambda t, *_: (cur_b(t), cur_s(t), 0)),
            pl.BlockSpec((3, batch, D_MODEL), lambda t, *_: (0, mod_row0 // batch, 0),
                         pipeline_mode=pl.Buffered(1)),
            pl.BlockSpec((5, tile, LANES), lambda t, *_: (0, prev_s(t), 0)),
            pl.BlockSpec((None, tile, D_MODEL), lambda t, *_: (prev_b(t), prev_s(t), 0)),
            const((3 * RET_HEADS, CHUNK, LANES)),
            const((2, CHUNK, 2 * CHUNK)),
            const((D_MODEL, ATT_WIDTH)),
            const((D_MODEL, IN_WIDTH - ATT_WIDTH)),
            const((ATT_WIDTH, D_MODEL)),
            const((RET_WIDTH, D_MODEL)),
            const((1, RET_WIDTH)),
        ],
        out_specs=[
            pl.BlockSpec((None, tile, D_MODEL), lambda t, *_: (prev_b(t), prev_s(t), 0)),
            pl.BlockSpec((None, WINDOW, LANES), lambda t, *_: (prev_b(t), 0, 0)),
            pl.BlockSpec((None, WINDOW, LANES), lambda t, *_: (prev_b(t), 0, 0)),
            pl.BlockSpec((None, RET_HEADS, RET_DK, LANES), lambda t, *_: (prev_b(t), 0, 0, 0)),
        ],
        scratch_shapes=[
            pltpu.VMEM((tile, IN_WIDTH), F32),
            pltpu.VMEM((tile, IN_WIDTH), F32),
            pltpu.VMEM((ATT_KV_HEADS, CHUNK, LANES), BF16),
            pltpu.VMEM((ATT_KV_HEADS, CHUNK, LANES), BF16),
            pltpu.VMEM((tile, D_MODEL), BF16),
        ],
    )
    return pl.pallas_call(
        functools.partial(_mixer_kernel, tile=tile, ns=ns, nt=nt, gl=gl),
        out_shape=[
            jax.ShapeDtypeStruct((batch, seq, D_MODEL), F32),
            jax.ShapeDtypeStruct((batch, WINDOW, LANES), F32),
            jax.ShapeDtypeStruct((batch, WINDOW, LANES), F32),
            jax.ShapeDtypeStruct((batch, RET_HEADS, RET_DK, LANES), F32),
        ],
        grid_spec=grid_spec,
        compiler_params=pltpu.CompilerParams(
            dimension_semantics=("arbitrary",), vmem_limit_bytes=VMEM_LIMIT_BYTES),
        name="mixer_prompt",
    )(sinks, x, mods, tab, x, dec, bias, w_q, w_rest, w_out_att, w_out_ret, gn_w)


def _ffn_kernel(r_ref, mods_ref, wup_ref, wdown_ref, ln1w_ref, ln1b_ref, lnw_ref, lnb_ref, o_ref,
                *, parts, tiles_per_mod_row):
    rows = r_ref.shape[0] // parts
    sl = [slice(p * rows, (p + 1) * rows) for p in range(parts)]
    if tiles_per_mod_row:
        seq = jax.lax.div(pl.program_id(0), tiles_per_mod_row)
        mod = lambda k, p: mods_ref[k, pl.ds(seq, 1), :]
    else:
        mod = lambda k, p: mods_ref[k, sl[p], :]
    ys = [_layer_norm(r_ref[sl[p], :], ln1w_ref[...], ln1b_ref[...]) for p in range(parts)]
    hs = [(ys[p] * (1.0 + mod(1, p)) + mod(0, p)).astype(BF16) for p in range(parts)]
    def act(h):
        return jnp.concatenate(
            [(_silu(_dot(h, wup_ref[:, j:j + FFN_COLS]))
              * _dot(h, wup_ref[:, D_FF + j:D_FF + j + FFN_COLS])).astype(BF16)
             for j in range(0, D_FF, FFN_COLS)], axis=1)

    acts = [act(h) for h in hs]
    fs = [_dot(a, wdown_ref[...]) for a in acts]
    for p in range(parts):
        r = DEEPNORM_ALPHA * ys[p] + (1.0 + mod(2, p)) * fs[p]
        o_ref[sl[p], :] = _layer_norm(r, lnw_ref[...], lnb_ref[...])


def _ffn(y, mods, mod_row0, mod_rows, tiles_per_mod_row, w_up, w_down, ln1_w, ln1_b, ln_w, ln_b, tile, parts):
    rows, _ = y.shape
    nt = rows // tile
    assert mod_row0 % mod_rows == 0 and (tiles_per_mod_row > 0 or mod_rows == tile)
    const = lambda shape: pl.BlockSpec(shape, lambda t: (0,) * len(shape), pipeline_mode=pl.Buffered(1))
    return pl.pallas_call(
        functools.partial(_ffn_kernel, parts=parts, tiles_per_mod_row=tiles_per_mod_row),
        out_shape=jax.ShapeDtypeStruct(y.shape, F32),
        grid=(nt,),
        in_specs=[
            pl.BlockSpec((tile, D_MODEL), lambda t: (t, 0)),
            pl.BlockSpec((3, mod_rows, D_MODEL), lambda t: (1, mod_row0 // mod_rows, 0),
                         pipeline_mode=pl.Buffered(1)),
            const((D_MODEL, 2 * D_FF)),
            const((D_FF, D_MODEL)),
            const((1, D_MODEL)),
            const((1, D_MODEL)),
            const((1, D_MODEL)),
            const((1, D_MODEL)),
        ],
        out_specs=pl.BlockSpec((tile, D_MODEL), lambda t: (t, 0)),
        compiler_params=pltpu.CompilerParams(
            dimension_semantics=("arbitrary",), vmem_limit_bytes=VMEM_LIMIT_BYTES),
        name="ffn",
    )(y, mods, w_up, w_down, ln1_w, ln1_b, ln_w, ln_b)


def _sample_kernel(sinks_ref, x_ref, mod_ref, tab_ref, wq_ref, wrest_ref, wout_att_ref, wout_ret_ref,
                   gnw_ref,
                   ck_ref, cv_ref, st_ref,
                   y_ref, ko_ref, vo_ref, so_ref,
                   z_s, mix_s, *, g1, n_blocks):
    i = pl.program_id(0)
    bb = SAMPLE_BLOCK

    @pl.when(i == 0)
    def _():
        m = mod_ref[...]
        h = (x_ref[...] * (1.0 + m[1]) + m[0]).astype(BF16)
        z = jnp.concatenate([_dot(h, wq_ref[...]), _dot(h, wrest_ref[...])], axis=1)
        att_c, att_s1, att_s2 = tab_ref[0], tab_ref[1], tab_ref[2]
        ret_c, ret_s = tab_ref[3], tab_ref[4]
        for j in range(4):
            col = OFF_Q + j * LANES
            z_s[:, col:col + LANES] = _rope_att(z[:, col:col + LANES], att_c, att_s1, att_s2) * (HEAD_DIM ** -0.5)
            col = OFF_RQ + j * LANES
            z_s[:, col:col + LANES] = _rope_ret(z[:, col:col + LANES], ret_c, ret_s)
            col = OFF_RK + j * LANES
            z_s[:, col:col + LANES] = _rope_ret(z[:, col:col + LANES], ret_c, ret_s) * (RET_DK ** -0.5)
        z_s[:, OFF_KV:OFF_KV + LANES] = _rope_att(z[:, OFF_KV:OFF_KV + LANES], att_c, att_s1, att_s2)
        z_s[:, OFF_KV + LANES:OFF_RQ] = z[:, OFF_KV + LANES:OFF_RQ]
        z_s[:, OFF_RV:OFF_RG] = z[:, OFF_RV:OFF_RG]
        z_s[:, OFF_RG:] = _silu(z[:, OFF_RG:])

    lane_w = jax.lax.broadcasted_iota(jnp.int32, (LANES, WINDOW), 1)
    last_lane = lane_w == WINDOW - 1
    lo_row = lane_w[0:1, :] < HEAD_DIM
    r4 = jax.lax.broadcasted_iota(jnp.int32, (ATT_GROUP, 1), 0)
    r8 = jax.lax.broadcasted_iota(jnp.int32, (ATT_HEADS, 1), 0)
    sink8 = jnp.zeros((ATT_HEADS, 1), F32)
    for hd in range(ATT_HEADS):
        sink8 = jnp.where(r8 == hd, sinks_ref[hd], sink8)
    head_blk = (jax.lax.broadcasted_iota(jnp.int32, (16, RET_WIDTH), 1) // LANES
                == jax.lax.broadcasted_iota(jnp.int32, (16, RET_WIDTH), 0))
    g1_col = jnp.where(r4 == 0, g1[0], jnp.where(r4 == 1, g1[1], jnp.where(r4 == 2, g1[2], g1[3])))

    def rows16(a):
        return jnp.concatenate([a, jnp.zeros((16 - a.shape[0], a.shape[1]), F32)], axis=0)

    def heads(row, off):
        return jnp.concatenate([row[:, off + j * LANES:off + (j + 1) * LANES] for j in range(4)], axis=0)

    def block_rows(a4):
        wide = jnp.concatenate([rows16(a4)] * RET_HEADS, axis=1)
        return jnp.where(head_blk, wide, jnp.zeros_like(wide))

    r0 = pl.multiple_of(i * bb, bb)
    zblk = z_s[pl.ds(r0, bb), :]
    rows = [zblk[b:b + 1, :] for b in range(bb)]

    kn = [r[:, OFF_KV:OFF_KV + LANES] for r in rows]
    vn = [r[:, OFF_KV + LANES:OFF_RQ] for r in rows]
    kn_t = zblk[:, OFF_KV:OFF_KV + LANES].T
    vn_t = zblk[:, OFF_KV + LANES:OFF_RQ].T
    q8 = []
    for r in rows:
        q4 = heads(r, OFF_Q)
        zero4 = jnp.zeros_like(q4)
        q8.append(jnp.concatenate([jnp.where(lo_row, q4, zero4), jnp.where(lo_row, zero4, q4)], axis=0))
    sc, oh = [], []
    for b in range(bb):
        sc.append(_dot(rows16(q8[b]).astype(BF16), ck_ref[b].astype(BF16))[0:ATT_HEADS])
    for b in range(bb):
        rq = heads(rows[b], OFF_RQ)
        rk = heads(rows[b], OFF_RK)
        rv = heads(rows[b], OFF_RV)
        state = st_ref[b].reshape(RET_WIDTH, LANES)
        o2 = _dot(block_rows(rq * g1_col).astype(BF16), state.astype(BF16))[0:RET_HEADS]
        outer = _dot_tn(block_rows(rk).astype(BF16), rows16(rv).astype(BF16))
        for hh in range(RET_HEADS):
            so_ref[b, hh] = g1[hh] * st_ref[b, hh] + outer[hh * RET_DK:(hh + 1) * RET_DK, :]
        o = jnp.sum(rq * rk, axis=-1, keepdims=True) * rv + o2
        mu = jnp.mean(o, axis=-1, keepdims=True)
        d = o - mu
        var = jnp.mean(d * d, axis=-1, keepdims=True)
        oh.append(heads(rows[b], OFF_RG) * (d * jax.lax.rsqrt(var + GN_EPS) * gnw_ref[...]))

    ws, wn = [], []
    for b in range(bb):
        sn = jnp.sum(q8[b] * kn[b], axis=-1, keepdims=True)
        mx = jnp.maximum(jnp.maximum(jnp.max(sc[b], axis=-1, keepdims=True), sn), sink8)
        p = jnp.exp(sc[b] - mx)
        pn = jnp.exp(sn - mx)
        inv = 1.0 / (jnp.sum(p, axis=-1, keepdims=True) + pn + jnp.exp(sink8 - mx))
        ws.append(rows16(p * inv).astype(BF16))
        wn.append(pn * inv)

    att = []
    for b in range(bb):
        o8 = _dot_nt(ws[b], cv_ref[b].astype(BF16))[0:ATT_HEADS] + wn[b] * vn[b]
        att.append(jnp.where(lo_row, o8[0:ATT_GROUP], o8[ATT_GROUP:]))

    for b in range(bb):
        ko_ref[b] = jnp.where(last_lane, kn_t[:, b:b + 1], pltpu.roll(ck_ref[b], WINDOW - 1, axis=1))
        vo_ref[b] = jnp.where(last_lane, vn_t[:, b:b + 1], pltpu.roll(cv_ref[b], WINDOW - 1, axis=1))
    blk_rows = pl.ds(r0, bb)
    for j in range(4):
        mix_s[blk_rows, j * LANES:(j + 1) * LANES] = jnp.concatenate(
            [a[j:j + 1, :] for a in att], axis=0)
        mix_s[blk_rows, ATT_WIDTH + j * LANES:ATT_WIDTH + (j + 1) * LANES] = jnp.concatenate(
            [o[j:j + 1, :] for o in oh], axis=0)

    @pl.when(i == n_blocks - 1)
    def _():
        y = (_dot(mix_s[:, :ATT_WIDTH].astype(BF16), wout_att_ref[...])
             + _dot(mix_s[:, ATT_WIDTH:].astype(BF16), wout_ret_ref[...]))
        y_ref[...] = DEEPNORM_ALPHA * x_ref[...] + (1.0 + mod_ref[2]) * y


def _sample_step(sinks, x, mods, tab, w_q, w_rest, w_out_att, w_out_ret, gn_w4,
                 cache_k, cache_v, state):
    n = x.shape[0]
    bb = SAMPLE_BLOCK
    n_blocks = n // bb
    g1 = tuple(float(v) for v in np.exp(_log_gamma()).astype(np.float32))
    full = lambda *shape: pl.BlockSpec(shape, lambda i, *_: (0,) * len(shape))
    blk = lambda *shape: pl.BlockSpec((bb,) + shape, lambda i, *_: (i,) + (0,) * len(shape))
    grid_spec = pltpu.PrefetchScalarGridSpec(
        num_scalar_prefetch=1,
        grid=(n_blocks,),
        in_specs=[
            full(n, D_MODEL), full(3, n, D_MODEL), full(5, 1, LANES),
            full(D_MODEL, ATT_WIDTH), full(D_MODEL, IN_WIDTH - ATT_WIDTH),
            full(ATT_WIDTH, D_MODEL), full(RET_WIDTH, D_MODEL),
            full(RET_HEADS, LANES),
            blk(WINDOW, LANES), blk(WINDOW, LANES), blk(RET_HEADS, RET_DK, LANES),
        ],
        out_specs=[full(n, D_MODEL), blk(WINDOW, LANES), blk(WINDOW, LANES),
                   blk(RET_HEADS, RET_DK, LANES)],
        scratch_shapes=[pltpu.VMEM((n, IN_WIDTH), F32), pltpu.VMEM((n, D_MODEL), F32)],
    )
    return pl.pallas_call(
        functools.partial(_sample_kernel, g1=g1, n_blocks=n_blocks),
        out_shape=[
            jax.ShapeDtypeStruct((n, D_MODEL), F32),
            jax.ShapeDtypeStruct((n, WINDOW, LANES), F32),
            jax.ShapeDtypeStruct((n, WINDOW, LANES), F32),
            jax.ShapeDtypeStruct((n, RET_HEADS, RET_DK, LANES), F32),
        ],
        grid_spec=grid_spec,
        compiler_params=pltpu.CompilerParams(
            dimension_semantics=("arbitrary",), vmem_limit_bytes=VMEM_LIMIT_BYTES),
        name="sample_step",
    )(sinks, x, mods, tab, w_q, w_rest, w_out_att, w_out_ret, gn_w4, cache_k, cache_v, state)


def _slab_order(w, axis):
    shape = w.shape
    pre, post = shape[:axis], shape[axis + 1:]
    w = w.reshape(pre + (ATT_KV_HEADS, ATT_GROUP, HEAD_DIM) + post)
    w = jnp.swapaxes(w, axis, axis + 1)
    return w.reshape(shape)


def kernel(x_prompt, x_sample, c_prompt, c_sample, cache_k, cache_v, state_ret, w_ada_mix, b_ada_mix, w_in, att_sinks, ret_gn_w, w_out, ln1_w, ln1_b, w_ada_ffn, b_ada_ffn, w_up, w_down, ln2_w, ln2_b):
    batch, seq, _ = x_prompt.shape
    n_s = x_sample.shape[0]
    l = 0
    assert w_in.shape[0] == DEPTH == 1

    w_in_f = w_in.reshape(D_MODEL, IN_WIDTH)
    w_out_f = w_out.reshape(D_MODEL, D_MODEL)
    w_q = _slab_order(w_in_f[:, :ATT_WIDTH].astype(BF16), 1)
    w_rest = w_in_f[:, ATT_WIDTH:].astype(BF16)
    w_out_att = _slab_order(w_out_f[:ATT_WIDTH].astype(BF16), 0)
    w_out_ret = w_out_f[ATT_WIDTH:].astype(BF16)
    w_up_b = w_up.reshape(D_MODEL, 2 * D_FF).astype(BF16)
    w_down_b = w_down.reshape(D_FF, D_MODEL).astype(BF16)
    sinks = att_sinks[l]
    gn_w = ret_gn_w[l].reshape(1, RET_WIDTH)
    ln1w, ln1b = ln1_w[l].reshape(1, D_MODEL), ln1_b[l].reshape(1, D_MODEL)
    ln2w, ln2b = ln2_w[l].reshape(1, D_MODEL), ln2_b[l].reshape(1, D_MODEL)

    tab_p = jnp.asarray(_rope_tables(np.arange(seq)))
    tab_s = jnp.asarray(_rope_tables(np.array([PAST_LEN])))
    dec = jnp.asarray(_decay_tables())
    bias = jnp.asarray(_band_bias())

    mods = _ada_mod(c_sample, c_prompt, w_ada_mix.reshape(D_MODEL, 3 * D_MODEL),
                    w_ada_ffn.reshape(D_MODEL, 3 * D_MODEL), b_ada_mix, b_ada_ffn)

    y1p, kp, vp, sp = _mixer_prompt(x_prompt, mods, n_s, sinks, tab_p, dec, bias, w_q, w_rest,
                                    w_out_att, w_out_ret, gn_w)
    yp = _ffn(y1p.reshape(batch * seq, D_MODEL), mods, n_s, batch, seq // FFN_TILE, w_up_b, w_down_b,
              ln1w, ln1b, ln2w, ln2b, FFN_TILE, FFN_PARTS).reshape(batch, seq, D_MODEL)

    xs = x_sample.reshape(n_s, D_MODEL)
    y1s, ks, vs, ss = _sample_step(
        sinks, xs, mods, tab_s, w_q, w_rest, w_out_att, w_out_ret, ret_gn_w.reshape(RET_HEADS, LANES),
        jnp.swapaxes(cache_k.reshape(n_s, WINDOW, LANES), 1, 2),
        jnp.swapaxes(cache_v.reshape(n_s, WINDOW, LANES), 1, 2),
        state_ret.reshape(n_s, RET_HEADS, RET_DK, LANES))
    ks, vs = jnp.swapaxes(ks, 1, 2), jnp.swapaxes(vs, 1, 2)
    ys = _ffn(y1s, mods, 0, n_s, 0, w_up_b, w_down_b, ln1w, ln1b, ln2w, ln2b, n_s, 1)

    kv_shape = (1, batch, WINDOW, ATT_KV_HEADS, HEAD_DIM)
    kvs_shape = (1, n_s, WINDOW, ATT_KV_HEADS, HEAD_DIM)
    return (yp, ys.reshape(n_s, 1, D_MODEL),
            kp.reshape(kv_shape), vp.reshape(kv_shape), sp[None],
            ks.reshape(kvs_shape), vs.reshape(kvs_shape), ss[None])
```

```python
import functools
import math

import numpy as np
import jax
import jax.numpy as jnp
from jax.experimental import pallas as pl
from jax.experimental.pallas import tpu as pltpu

D_MODEL = 1024
WINDOW = 128
CHUNK = 128
ATT_HEADS = 8
ATT_KV_HEADS = 2
ATT_GROUP = ATT_HEADS // ATT_KV_HEADS
HEAD_DIM = 64
ROPE_DIMS = HEAD_DIM // 4
ROPE_THETA = 500000.0
RET_HEADS = 4
RET_DK = 128
RET_THETA = 10000.0
ATT_WIDTH = ATT_HEADS * HEAD_DIM
KV_WIDTH = ATT_KV_HEADS * HEAD_DIM
RET_WIDTH = RET_HEADS * RET_DK
IN_WIDTH = ATT_WIDTH + 2 * KV_WIDTH + 4 * RET_WIDTH
D_FF = 2816
DEPTH = 1
DEEPNORM_ALPHA = (2 * DEPTH) ** 0.25
LN_EPS = 1e-5
GN_EPS = 1e-6
PAST_LEN = 16384

OFF_Q = 0
OFF_KV = ATT_WIDTH
OFF_RQ = OFF_KV + 2 * KV_WIDTH
OFF_RK = OFF_RQ + RET_WIDTH
OFF_RV = OFF_RK + RET_WIDTH
OFF_RG = OFF_RV + RET_WIDTH

LANES = 128
VMEM_LIMIT_BYTES = 56 * 1024 * 1024

ADA_COLS = 512
PROMPT_TILE = 512
FILL_WIDTH = 256
FILL_PLAN = ((2, 1), (2, 1), (2, 1), (2, 0))
FFN_TILE = 1024
FFN_PARTS = 4
FFN_COLS = 256
SAMPLE_BLOCK = 8

BF16 = jnp.bfloat16
F32 = jnp.float32


def _dot(a, b):
    return jnp.dot(a, b, preferred_element_type=F32)


def _dot_nt(a, b):
    return jax.lax.dot_general(a, b, (((1,), (1,)), ((), ())), preferred_element_type=F32)


def _dot_tn(a, b):
    return jax.lax.dot_general(a, b, (((0,), (0,)), ((), ())), preferred_element_type=F32)


def _sigmoid(x):
    return 1.0 / (1.0 + jnp.exp(-x))


def _silu(x):
    return x * _sigmoid(x)


def _layer_norm(r, w, b):
    mu = jnp.mean(r, axis=-1, keepdims=True)
    d = r - mu
    var = jnp.mean(d * d, axis=-1, keepdims=True)
    return d * jax.lax.rsqrt(var + LN_EPS) * w + b


def _rope_att(slab, c, s1, s2):
    return slab * c + pltpu.roll(slab, LANES - 8, axis=1) * s1 + pltpu.roll(slab, 8, axis=1) * s2


def _rope_ret(slab, c, s):
    return slab * c + pltpu.roll(slab, LANES // 2, axis=1) * s


def _rope_tables(pos):
    pos = np.asarray(pos, np.float64)[:, None]
    half = ROPE_DIMS // 2
    inv = ROPE_THETA ** (-np.arange(half, dtype=np.float64) / half)
    ang = pos * inv[None, :]
    cos, sin = np.cos(ang), np.sin(ang)
    n = pos.shape[0]
    head_c = np.ones((n, HEAD_DIM)); head_s1 = np.zeros((n, HEAD_DIM)); head_s2 = np.zeros((n, HEAD_DIM))
    head_c[:, :half] = cos; head_c[:, half:2 * half] = cos
    head_s1[:, :half] = -sin
    head_s2[:, half:2 * half] = sin
    att = [np.tile(t, (1, LANES // HEAD_DIM)) for t in (head_c, head_s1, head_s2)]
    rhalf = RET_DK // 2
    rinv = RET_THETA ** (-np.arange(rhalf, dtype=np.float64) / rhalf)
    rang = pos * rinv[None, :]
    rc = np.concatenate([np.cos(rang), np.cos(rang)], axis=1)
    rs = np.concatenate([-np.sin(rang), np.sin(rang)], axis=1)
    return np.stack(att + [rc, rs], axis=0).astype(np.float32)


def _log_gamma():
    lin = np.linspace(math.log(1.0 / 32), math.log(1.0 / 512), RET_HEADS)
    return np.log1p(-np.exp(lin))


def _decay_tables():
    lg = _log_gamma()
    idx = np.arange(CHUNK, dtype=np.float64)
    diff = idx[:, None] - idx[None, :]
    dmat = np.where(diff[None] >= 0, np.exp(lg[:, None, None] * np.maximum(diff, 0.0)[None]), 0.0)
    dq = np.exp(lg[:, None] * (idx[None, :] + 1.0))[:, :, None] * np.ones((1, 1, LANES))
    dk = np.exp(lg[:, None] * (CHUNK - 1.0 - idx[None, :]))[:, :, None] * np.ones((1, 1, LANES))
    return np.concatenate([dmat, dq, dk], axis=0).astype(np.float32)


def _band_bias():
    i = np.arange(CHUNK)[:, None]
    j = np.arange(2 * CHUNK)[None, :]
    valid = (j >= i) & (j <= i + WINDOW)
    neg = np.float32(-1e30)
    b0 = np.where(valid, 0.0, neg)
    b1 = np.where(valid & (j >= CHUNK), 0.0, neg)
    return np.stack([b0, b1], axis=0).astype(np.float32)


def _ada_kernel(cs_ref, cp_ref, w1_ref, w2_ref, b1_ref, b2_ref, o_ref):
    rows = cs_ref.shape[0] + cp_ref.shape[0]
    pad = jnp.zeros((-rows % 16, D_MODEL), F32)
    a = _silu(jnp.concatenate([cs_ref[...], cp_ref[...], pad], axis=0)).astype(BF16)

    def emit(w_ref, b_ref):
        o_ref[...] = _dot(a, w_ref[...].astype(BF16))[0:rows] + b_ref[...]

    half = pl.num_programs(0) // 2

    @pl.when(pl.program_id(0) < half)
    def _():
        emit(w1_ref, b1_ref)

    @pl.when(pl.program_id(0) >= half)
    def _():
        emit(w2_ref, b2_ref)


def _ada_mod(c_sample, c_prompt, w_mix, w_ffn, b_mix, b_ffn):
    rows = c_sample.shape[0] + c_prompt.shape[0]
    d = D_MODEL
    cols = ADA_COLS
    per_mod = d // cols
    half = 3 * per_mod
    return pl.pallas_call(
        _ada_kernel,
        out_shape=jax.ShapeDtypeStruct((6, rows, d), F32),
        grid=(2 * half,),
        in_specs=[
            pl.BlockSpec(c_sample.shape, lambda j: (0, 0)),
            pl.BlockSpec(c_prompt.shape, lambda j: (0, 0)),
            pl.BlockSpec((d, cols), lambda j: (0, jnp.minimum(j, half - 1))),
            pl.BlockSpec((d, cols), lambda j: (0, jnp.maximum(j - half, 0))),
            pl.BlockSpec((1, cols), lambda j: (0, jnp.minimum(j, half - 1))),
            pl.BlockSpec((1, cols), lambda j: (0, jnp.maximum(j - half, 0))),
        ],
        out_specs=pl.BlockSpec((None, rows, cols), lambda j: (j // per_mod, 0, j % per_mod)),
        compiler_params=pltpu.CompilerParams(
            dimension_semantics=("arbitrary",), vmem_limit_bytes=VMEM_LIMIT_BYTES),
        name="ada_mod",
    )(c_sample, c_prompt, w_mix, w_ffn, b_mix, b_ffn)


def _mixer_step(z_prev, z_cur, sinks_ref, x_ref, mods_ref, tab_ref, xp_ref, dec_ref, bias_ref,
                wq_ref, wrest_ref, wout_att_ref, wout_ret_ref, gnw_ref,
                y_ref, kout_ref, vout_ref, s_ref, kcarry, vcarry, mix_s,
                *, cur_b, prev_b, seq_start, tile, gl):
    n_chunks = tile // CHUNK
    x = x_ref[...]
    h = (x * (1.0 + mods_ref[1, pl.ds(cur_b, 1), :]) + mods_ref[0, pl.ds(cur_b, 1), :]).astype(BF16)
    n_fill = IN_WIDTH // FILL_WIDTH
    pending = list(range(n_fill))

    def fill(count):
        for _ in range(count):
            if pending:
                i = pending.pop(0)
                col = i * FILL_WIDTH
                w = (wq_ref[:, col:col + FILL_WIDTH] if col < ATT_WIDTH
                     else wrest_ref[:, col - ATT_WIDTH:col - ATT_WIDTH + FILL_WIDTH])
                z_cur[:, col:col + FILL_WIDTH] = _dot(h, w)

    gate = mods_ref[2, pl.ds(prev_b, 1), :]
    lo = jax.lax.broadcasted_iota(jnp.int32, (CHUNK, LANES), 1) < HEAD_DIM
    zero = jnp.zeros((CHUNK, LANES), F32)
    k_prev = [kcarry[g] for g in range(ATT_KV_HEADS)]
    v_prev = [vcarry[g] for g in range(ATT_KV_HEADS)]

    for c in range(n_chunks):
        rows = slice(c * CHUNK, (c + 1) * CHUNK)
        bias = bias_ref[seq_start] if c == 0 else bias_ref[0]
        att_c, att_s1, att_s2 = tab_ref[0, rows, :], tab_ref[1, rows, :], tab_ref[2, rows, :]
        ret_c, ret_s = tab_ref[3, rows, :], tab_ref[4, rows, :]

        def zs(col):
            return z_prev[rows, col:col + LANES]

        q_stack = jnp.concatenate(
            [(_rope_att(zs(OFF_Q + j * LANES), att_c, att_s1, att_s2) * (HEAD_DIM ** -0.5)).astype(BF16)
             for j in range(4)], axis=0)
        k = _rope_att(zs(OFF_KV), att_c, att_s1, att_s2)
        v = zs(OFF_KV + LANES)
        k_cur = [jnp.where(lo, k, zero).astype(BF16), jnp.where(lo, zero, k).astype(BF16)]
        v_cur = [jnp.where(lo, v, zero).astype(BF16), jnp.where(lo, zero, v).astype(BF16)]
        k_band = [jnp.concatenate([k_prev[g], k_cur[g]], axis=0) for g in range(ATT_KV_HEADS)]
        v_band = [jnp.concatenate([v_prev[g], v_cur[g]], axis=0) for g in range(ATT_KV_HEADS)]
        k_prev, v_prev = k_cur, v_cur
        if c == n_chunks - 1:
            kout_ref[...] = k
            vout_ref[...] = v
            for g in range(ATT_KV_HEADS):
                kcarry[g] = k_cur[g]
                vcarry[g] = v_cur[g]

        ret = []
        for hh in range(RET_HEADS):
            rq = _rope_ret(zs(OFF_RQ + hh * LANES), ret_c, ret_s)
            rk = _rope_ret(zs(OFF_RK + hh * LANES), ret_c, ret_s) * (RET_DK ** -0.5)
            ret.append(dict(
                q=rq.astype(BF16), qd=(rq * dec_ref[RET_HEADS + hh]).astype(BF16),
                k=rk.astype(BF16), kd=(rk * dec_ref[2 * RET_HEADS + hh]).astype(BF16),
                v=zs(OFF_RV + hh * LANES).astype(BF16)))

        sc_att = [_dot_nt(q_stack, k_band[g]) for g in range(ATT_KV_HEADS)]
        sc_ret = [_dot_nt(r["q"], r["k"]) for r in ret]
        fill(FILL_PLAN[c][0])

        att = None
        for g in range(ATT_KV_HEADS):
            ws = []
            for j in range(4):
                sj = sc_att[g][j * CHUNK:(j + 1) * CHUNK, :] + bias
                sink = sinks_ref[g * ATT_GROUP + j]
                mx = jnp.maximum(jnp.max(sj, axis=-1, keepdims=True), sink)
                p = jnp.exp(sj - mx)
                den = jnp.sum(p, axis=-1, keepdims=True) + jnp.exp(sink - mx)
                ws.append((p * (1.0 / den)).astype(BF16))
            o = _dot(jnp.concatenate(ws, axis=0), v_band[g])
            att = o if att is None else att + o
        for j in range(4):
            mix_s[rows, j * LANES:(j + 1) * LANES] = att[j * CHUNK:(j + 1) * CHUNK, :].astype(BF16)

        outs = []
        for hh, r in enumerate(ret):
            state = s_ref[hh]
            sc = sc_ret[hh] * dec_ref[hh]
            outs.append(_dot(sc.astype(BF16), r["v"]) + _dot(r["qd"], state.astype(BF16)))
            s_ref[hh] = gl[hh] * state + _dot_tn(r["kd"], r["v"])
        fill(FILL_PLAN[c][1])
        for hh in range(RET_HEADS):
            cols = slice(hh * LANES, (hh + 1) * LANES)
            o = outs[hh]
            mu = jnp.mean(o, axis=-1, keepdims=True)
            d = o - mu
            var = jnp.mean(d * d, axis=-1, keepdims=True)
            nrm = d * jax.lax.rsqrt(var + GN_EPS) * gnw_ref[:, cols]
            mix_s[rows, ATT_WIDTH + hh * LANES:ATT_WIDTH + (hh + 1) * LANES] = (
                _silu(zs(OFF_RG + hh * LANES)) * nrm).astype(BF16)

    y = _dot(mix_s[...], jnp.concatenate([wout_att_ref[...], wout_ret_ref[...]], axis=0))
    fill(n_fill)
    y_ref[...] = DEEPNORM_ALPHA * xp_ref[...] + (1.0 + gate) * y


def _mixer_kernel(sinks_ref, x_ref, mods_ref, tab_ref, xp_ref, dec_ref, bias_ref,
                  wq_ref, wrest_ref, wout_att_ref, wout_ret_ref, gnw_ref,
                  y_ref, kout_ref, vout_ref, s_ref,
                  z_a, z_b, kcarry, vcarry, mix_s, *, tile, ns, nt, gl):
    t = pl.program_id(0)
    prev_tile = jnp.maximum(t - 1, 0)
    sa = jax.lax.rem(prev_tile, ns)
    seq_start = (sa == 0).astype(jnp.int32)
    cur_b = jax.lax.div(jnp.minimum(t, nt - 1), ns)
    prev_b = jax.lax.div(prev_tile, ns)
    parity = jax.lax.rem(t, 2)

    @pl.when(t == 0)
    def _():
        z_b[...] = jnp.zeros(z_b.shape, z_b.dtype)
        kcarry[...] = jnp.zeros(kcarry.shape, kcarry.dtype)
        vcarry[...] = jnp.zeros(vcarry.shape, vcarry.dtype)

    @pl.when(sa == 0)
    def _():
        s_ref[...] = jnp.zeros_like(s_ref)

    step = functools.partial(
        _mixer_step, sinks_ref=sinks_ref, x_ref=x_ref, mods_ref=mods_ref, tab_ref=tab_ref, xp_ref=xp_ref,
        dec_ref=dec_ref, bias_ref=bias_ref, wq_ref=wq_ref, wrest_ref=wrest_ref,
        wout_att_ref=wout_att_ref, wout_ret_ref=wout_ret_ref,
        gnw_ref=gnw_ref, y_ref=y_ref, kout_ref=kout_ref,
        vout_ref=vout_ref, s_ref=s_ref, kcarry=kcarry, vcarry=vcarry, mix_s=mix_s,
        cur_b=cur_b, prev_b=prev_b, seq_start=seq_start, tile=tile, gl=gl)

    @pl.when(parity == 0)
    def _():
        step(z_b, z_a)

    @pl.when(parity == 1)
    def _():
        step(z_a, z_b)


def _mixer_prompt(x, mods, mod_row0, sinks, tab, dec, bias, w_q, w_rest, w_out_att, w_out_ret, gn_w):
    batch, seq, _ = x.shape
    assert mod_row0 % batch == 0 and batch % 8 == 0
    tile = PROMPT_TILE
    ns = seq // tile
    gl = tuple(float(v) for v in np.exp(_log_gamma() * CHUNK).astype(np.float32))
    nt = batch * ns
    const = lambda shape: pl.BlockSpec(shape, lambda t, *_: (0,) * len(shape),
                                       pipeline_mode=pl.Buffered(1))
    cur_b = lambda t: jnp.minimum(t, nt - 1) // ns
    cur_s = lambda t: jnp.minimum(t, nt - 1) % ns
    prev_b = lambda t: jnp.maximum(t - 1, 0) // ns
    prev_s = lambda t: jnp.maximum(t - 1, 0) % ns
    grid_spec = pltpu.PrefetchScalarGridSpec(
        num_scalar_prefetch=1,
        grid=(nt + 1,),
        in_specs=[
            pl.BlockSpec((None, tile, D_MODEL), lambda t, *_: (cur_b(t), cur_s(t), 0)),
            pl.BlockSpec((3, batch, D_MODEL), lambda t, *_: (0, mod_row0 // batch, 0),
                         pipeline_mode=pl.Buffered(1)),
            pl.BlockSpec((5, tile, LANES), lambda t, *_: (0, prev_s(t), 0)),
            pl.BlockSpec((None, tile, D_MODEL), lambda t, *_: (prev_b(t), prev_s(t), 0)),
            const((3 * RET_HEADS, CHUNK, LANES)),
            const((2, CHUNK, 2 * CHUNK)),
            const((D_MODEL, ATT_WIDTH)),
            const((D_MODEL, IN_WIDTH - ATT_WIDTH)),
            const((ATT_WIDTH, D_MODEL)),
            const((RET_WIDTH, D_MODEL)),
            const((1, RET_WIDTH)),
        ],
        out_specs=[
            pl.BlockSpec((None, tile, D_MODEL), lambda t, *_: (prev_b(t), prev_s(t), 0)),
            pl.BlockSpec((None, WINDOW, LANES), lambda t, *_: (prev_b(t), 0, 0)),
            pl.BlockSpec((None, WINDOW, LANES), lambda t, *_: (prev_b(t), 0, 0)),
            pl.BlockSpec((None, RET_HEADS, RET_DK, LANES), lambda t, *_: (prev_b(t), 0, 0, 0)),
        ],
        scratch_shapes=[
            pltpu.VMEM((tile, IN_WIDTH), F32),
            pltpu.VMEM((tile, IN_WIDTH), F32),
            pltpu.VMEM((ATT_KV_HEADS, CHUNK, LANES), BF16),
            pltpu.VMEM((ATT_KV_HEADS, CHUNK, LANES), BF16),
            pltpu.VMEM((tile, D_MODEL), BF16),
        ],
    )
    return pl.pallas_call(
        functools.partial(_mixer_kernel, tile=tile, ns=ns, nt=nt, gl=gl),
        out_shape=[
            jax.ShapeDtypeStruct((batch, seq, D_MODEL), F32),
            jax.ShapeDtypeStruct((batch, WINDOW, LANES), F32),
            jax.ShapeDtypeStruct((batch, WINDOW, LANES), F32),
            jax.ShapeDtypeStruct((batch, RET_HEADS, RET_DK, LANES), F32),
        ],
        grid_spec=grid_spec,
        compiler_params=pltpu.CompilerParams(
            dimension_semantics=("arbitrary",), vmem_limit_bytes=VMEM_LIMIT_BYTES),
        name="mixer_prompt",
    )(sinks, x, mods, tab, x, dec, bias, w_q, w_rest, w_out_att, w_out_ret, gn_w)


def _ffn_kernel(r_ref, mods_ref, wup_ref, wdown_ref, ln1w_ref, ln1b_ref, lnw_ref, lnb_ref, o_ref,
                *, parts, tiles_per_mod_row):
    rows = r_ref.shape[0] // parts
    sl = [slice(p * rows, (p + 1) * rows) for p in range(parts)]
    if tiles_per_mod_row:
        seq = jax.lax.div(pl.program_id(0), tiles_per_mod_row)
        mod = lambda k, p: mods_ref[k, pl.ds(seq, 1), :]
    else:
        mod = lambda k, p: mods_ref[k, sl[p], :]
    ys = [_layer_norm(r_ref[sl[p], :], ln1w_ref[...], ln1b_ref[...]) for p in range(parts)]
    hs = [(ys[p] * (1.0 + mod(1, p)) + mod(0, p)).astype(BF16) for p in range(parts)]
    def act(h):
        return jnp.concatenate(
            [(_silu(_dot(h, wup_ref[:, j:j + FFN_COLS]))
              * _dot(h, wup_ref[:, D_FF + j:D_FF + j + FFN_COLS])).astype(BF16)
             for j in range(0, D_FF, FFN_COLS)], axis=1)

    acts = [act(h) for h in hs]
    fs = [_dot(a, wdown_ref[...]) for a in acts]
    for p in range(parts):
        r = DEEPNORM_ALPHA * ys[p] + (1.0 + mod(2, p)) * fs[p]
        o_ref[sl[p], :] = _layer_norm(r, lnw_ref[...], lnb_ref[...])


def _ffn(y, mods, mod_row0, mod_rows, tiles_per_mod_row, w_up, w_down, ln1_w, ln1_b, ln_w, ln_b, tile, parts):
    rows, _ = y.shape
    nt = rows // tile
    assert mod_row0 % mod_rows == 0 and (tiles_per_mod_row > 0 or mod_rows == tile)
    const = lambda shape: pl.BlockSpec(shape, lambda t: (0,) * len(shape), pipeline_mode=pl.Buffered(1))
    return pl.pallas_call(
        functools.partial(_ffn_kernel, parts=parts, tiles_per_mod_row=tiles_per_mod_row),
        out_shape=jax.ShapeDtypeStruct(y.shape, F32),
        grid=(nt,),
        in_specs=[
            pl.BlockSpec((tile, D_MODEL), lambda t: (t, 0)),
            pl.BlockSpec((3, mod_rows, D_MODEL), lambda t: (1, mod_row0 // mod_rows, 0),
                         pipeline_mode=pl.Buffered(1)),
            const((D_MODEL, 2 * D_FF)),
            const((D_FF, D_MODEL)),
            const((1, D_MODEL)),
            const((1, D_MODEL)),
            const((1, D_MODEL)),
            const((1, D_MODEL)),
        ],
        out_specs=pl.BlockSpec((tile, D_MODEL), lambda t: (t, 0)),
        compiler_params=pltpu.CompilerParams(
            dimension_semantics=("arbitrary",), vmem_limit_bytes=VMEM_LIMIT_BYTES),
        name="ffn",
    )(y, mods, w_up, w_down, ln1_w, ln1_b, ln_w, ln_b)


def _sample_kernel(sinks_ref, x_ref, mod_ref, tab_ref, wq_ref, wrest_ref, wout_att_ref, wout_ret_ref,
                   gnw_ref,
                   ck_ref, cv_ref, st_ref,
                   y_ref, ko_ref, vo_ref, so_ref,
                   z_s, mix_s, *, g1, n_blocks):
    i = pl.program_id(0)
    bb = SAMPLE_BLOCK

    @pl.when(i == 0)
    def _():
        m = mod_ref[...]
        h = (x_ref[...] * (1.0 + m[1]) + m[0]).astype(BF16)
        z = jnp.concatenate([_dot(h, wq_ref[...]), _dot(h, wrest_ref[...])], axis=1)
        att_c, att_s1, att_s2 = tab_ref[0], tab_ref[1], tab_ref[2]
        ret_c, ret_s = tab_ref[3], tab_ref[4]
        for j in range(4):
            col = OFF_Q + j * LANES
            z_s[:, col:col + LANES] = _rope_att(z[:, col:col + LANES], att_c, att_s1, att_s2) * (HEAD_DIM ** -0.5)
            col = OFF_RQ + j * LANES
            z_s[:, col:col + LANES] = _rope_ret(z[:, col:col + LANES], ret_c, ret_s)
            col = OFF_RK + j * LANES
            z_s[:, col:col + LANES] = _rope_ret(z[:, col:col + LANES], ret_c, ret_s) * (RET_DK ** -0.5)
        z_s[:, OFF_KV:OFF_KV + LANES] = _rope_att(z[:, OFF_KV:OFF_KV + LANES], att_c, att_s1, att_s2)
        z_s[:, OFF_KV + LANES:OFF_RQ] = z[:, OFF_KV + LANES:OFF_RQ]
        z_s[:, OFF_RV:OFF_RG] = z[:, OFF_RV:OFF_RG]
        z_s[:, OFF_RG:] = _silu(z[:, OFF_RG:])

    lane_w = jax.lax.broadcasted_iota(jnp.int32, (LANES, WINDOW), 1)
    last_lane = lane_w == WINDOW - 1
    lo_row = lane_w[0:1, :] < HEAD_DIM
    r4 = jax.lax.broadcasted_iota(jnp.int32, (ATT_GROUP, 1), 0)
    r8 = jax.lax.broadcasted_iota(jnp.int32, (ATT_HEADS, 1), 0)
    sink8 = jnp.zeros((ATT_HEADS, 1), F32)
    for hd in range(ATT_HEADS):
        sink8 = jnp.where(r8 == hd, sinks_ref[hd], sink8)
    head_blk = (jax.lax.broadcasted_iota(jnp.int32, (16, RET_WIDTH), 1) // LANES
                == jax.lax.broadcasted_iota(jnp.int32, (16, RET_WIDTH), 0))
    g1_col = jnp.where(r4 == 0, g1[0], jnp.where(r4 == 1, g1[1], jnp.where(r4 == 2, g1[2], g1[3])))

    def rows16(a):
        return jnp.concatenate([a, jnp.zeros((16 - a.shape[0], a.shape[1]), F32)], axis=0)

    def heads(row, off):
        return jnp.concatenate([row[:, off + j * LANES:off + (j + 1) * LANES] for j in range(4)], axis=0)

    def block_rows(a4):
        wide = jnp.concatenate([rows16(a4)] * RET_HEADS, axis=1)
        return jnp.where(head_blk, wide, jnp.zeros_like(wide))

    r0 = pl.multiple_of(i * bb, bb)
    zblk = z_s[pl.ds(r0, bb), :]
    rows = [zblk[b:b + 1, :] for b in range(bb)]

    kn = [r[:, OFF_KV:OFF_KV + LANES] for r in rows]
    vn = [r[:, OFF_KV + LANES:OFF_RQ] for r in rows]
    kn_t = zblk[:, OFF_KV:OFF_KV + LANES].T
    vn_t = zblk[:, OFF_KV + LANES:OFF_RQ].T
    q8 = []
    for r in rows:
        q4 = heads(r, OFF_Q)
        zero4 = jnp.zeros_like(q4)
        q8.append(jnp.concatenate([jnp.where(lo_row, q4, zero4), jnp.where(lo_row, zero4, q4)], axis=0))
    sc, oh = [], []
    for b in range(bb):
        sc.append(_dot(rows16(q8[b]).astype(BF16), ck_ref[b].astype(BF16))[0:ATT_HEADS])
    for b in range(bb):
        rq = heads(rows[b], OFF_RQ)
        rk = heads(rows[b], OFF_RK)
        rv = heads(rows[b], OFF_RV)
        state = st_ref[b].reshape(RET_WIDTH, LANES)
        o2 = _dot(block_rows(rq * g1_col).astype(BF16), state.astype(BF16))[0:RET_HEADS]
        outer = _dot_tn(block_rows(rk).astype(BF16), rows16(rv).astype(BF16))
        for hh in range(RET_HEADS):
            so_ref[b, hh] = g1[hh] * st_ref[b, hh] + outer[hh * RET_DK:(hh + 1) * RET_DK, :]
        o = jnp.sum(rq * rk, axis=-1, keepdims=True) * rv + o2
        mu = jnp.mean(o, axis=-1, keepdims=True)
        d = o - mu
        var = jnp.mean(d * d, axis=-1, keepdims=True)
        oh.append(heads(rows[b], OFF_RG) * (d * jax.lax.rsqrt(var + GN_EPS) * gnw_ref[...]))

    ws, wn = [], []
    for b in range(bb):
        sn = jnp.sum(q8[b] * kn[b], axis=-1, keepdims=True)
        mx = jnp.maximum(jnp.maximum(jnp.max(sc[b], axis=-1, keepdims=True), sn), sink8)
        p = jnp.exp(sc[b] - mx)
        pn = jnp.exp(sn - mx)
        inv = 1.0 / (jnp.sum(p, axis=-1, keepdims=True) + pn + jnp.exp(sink8 - mx))
        ws.append(rows16(p * inv).astype(BF16))
        wn.append(pn * inv)

    att = []
    for b in range(bb):
        o8 = _dot_nt(ws[b], cv_ref[b].astype(BF16))[0:ATT_HEADS] + wn[b] * vn[b]
        att.append(jnp.where(lo_row, o8[0:ATT_GROUP], o8[ATT_GROUP:]))

    for b in range(bb):
        ko_ref[b] = jnp.where(last_lane, kn_t[:, b:b + 1], pltpu.roll(ck_ref[b], WINDOW - 1, axis=1))
        vo_ref[b] = jnp.where(last_lane, vn_t[:, b:b + 1], pltpu.roll(cv_ref[b], WINDOW - 1, axis=1))
    blk_rows = pl.ds(r0, bb)
    for j in range(4):
        mix_s[blk_rows, j * LANES:(j + 1) * LANES] = jnp.concatenate(
            [a[j:j + 1, :] for a in att], axis=0)
        mix_s[blk_rows, ATT_WIDTH + j * LANES:ATT_WIDTH + (j + 1) * LANES] = jnp.concatenate(
            [o[j:j + 1, :] for o in oh], axis=0)

    @pl.when(i == n_blocks - 1)
    def _():
        y = (_dot(mix_s[:, :ATT_WIDTH].astype(BF16), wout_att_ref[...])
             + _dot(mix_s[:, ATT_WIDTH:].astype(BF16), wout_ret_ref[...]))
        y_ref[...] = DEEPNORM_ALPHA * x_ref[...] + (1.0 + mod_ref[2]) * y


def _sample_step(sinks, x, mods, tab, w_q, w_rest, w_out_att, w_out_ret, gn_w4,
                 cache_k, cache_v, state):
    n = x.shape[0]
    bb = SAMPLE_BLOCK
    n_blocks = n // bb
    g1 = tuple(float(v) for v in np.exp(_log_gamma()).astype(np.float32))
    full = lambda *shape: pl.BlockSpec(shape, lambda i, *_: (0,) * len(shape))
    blk = lambda *shape: pl.BlockSpec((bb,) + shape, lambda i, *_: (i,) + (0,) * len(shape))
    grid_spec = pltpu.PrefetchScalarGridSpec(
        num_scalar_prefetch=1,
        grid=(n_blocks,),
        in_specs=[
            full(n, D_MODEL), full(3, n, D_MODEL), full(5, 1, LANES),
            full(D_MODEL, ATT_WIDTH), full(D_MODEL, IN_WIDTH - ATT_WIDTH),
            full(ATT_WIDTH, D_MODEL), full(RET_WIDTH, D_MODEL),
            full(RET_HEADS, LANES),
            blk(WINDOW, LANES), blk(WINDOW, LANES), blk(RET_HEADS, RET_DK, LANES),
        ],
        out_specs=[full(n, D_MODEL), blk(WINDOW, LANES), blk(WINDOW, LANES),
                   blk(RET_HEADS, RET_DK, LANES)],
        scratch_shapes=[pltpu.VMEM((n, IN_WIDTH), F32), pltpu.VMEM((n, D_MODEL), F32)],
    )
    return pl.pallas_call(
        functools.partial(_sample_kernel, g1=g1, n_blocks=n_blocks),
        out_shape=[
            jax.ShapeDtypeStruct((n, D_MODEL), F32),
            jax.ShapeDtypeStruct((n, WINDOW, LANES), F32),
            jax.ShapeDtypeStruct((n, WINDOW, LANES), F32),
            jax.ShapeDtypeStruct((n, RET_HEADS, RET_DK, LANES), F32),
        ],
        grid_spec=grid_spec,
        compiler_params=pltpu.CompilerParams(
            dimension_semantics=("arbitrary",), vmem_limit_bytes=VMEM_LIMIT_BYTES),
        name="sample_step",
    )(sinks, x, mods, tab, w_q, w_rest, w_out_att, w_out_ret, gn_w4, cache_k, cache_v, state)


def _slab_order(w, axis):
    shape = w.shape
    pre, post = shape[:axis], shape[axis + 1:]
    w = w.reshape(pre + (ATT_KV_HEADS, ATT_GROUP, HEAD_DIM) + post)
    w = jnp.swapaxes(w, axis, axis + 1)
    return w.reshape(shape)


def kernel(x_prompt, x_sample, c_prompt, c_sample, cache_k, cache_v, state_ret, w_ada_mix, b_ada_mix, w_in, att_sinks, ret_gn_w, w_out, ln1_w, ln1_b, w_ada_ffn, b_ada_ffn, w_up, w_down, ln2_w, ln2_b):
    batch, seq, _ = x_prompt.shape
    n_s = x_sample.shape[0]
    l = 0
    assert w_in.shape[0] == DEPTH == 1

    w_in_f = w_in.reshape(D_MODEL, IN_WIDTH)
    w_out_f = w_out.reshape(D_MODEL, D_MODEL)
    w_q = _slab_order(w_in_f[:, :ATT_WIDTH].astype(BF16), 1)
    w_rest = w_in_f[:, ATT_WIDTH:].astype(BF16)
    w_out_att = _slab_order(w_out_f[:ATT_WIDTH].astype(BF16), 0)
    w_out_ret = w_out_f[ATT_WIDTH:].astype(BF16)
    w_up_b = w_up.reshape(D_MODEL, 2 * D_FF).astype(BF16)
    w_down_b = w_down.reshape(D_FF, D_MODEL).astype(BF16)
    sinks = att_sinks[l]
    gn_w = ret_gn_w[l].reshape(1, RET_WIDTH)
    ln1w, ln1b = ln1_w[l].reshape(1, D_MODEL), ln1_b[l].reshape(1, D_MODEL)
    ln2w, ln2b = ln2_w[l].reshape(1, D_MODEL), ln2_b[l].reshape(1, D_MODEL)

    tab_p = jnp.asarray(_rope_tables(np.arange(seq)))
    tab_s = jnp.asarray(_rope_tables(np.array([PAST_LEN])))
    dec = jnp.asarray(_decay_tables())
    bias = jnp.asarray(_band_bias())

    mods = _ada_mod(c_sample, c_prompt, w_ada_mix.reshape(D_MODEL, 3 * D_MODEL),
                    w_ada_ffn.reshape(D_MODEL, 3 * D_MODEL), b_ada_mix, b_ada_ffn)

    y1p, kp, vp, sp = _mixer_prompt(x_prompt, mods, n_s, sinks, tab_p, dec, bias, w_q, w_rest,
                                    w_out_att, w_out_ret, gn_w)
    yp = _ffn(y1p.reshape(batch * seq, D_MODEL), mods, n_s, batch, seq // FFN_TILE, w_up_b, w_down_b,
              ln1w, ln1b, ln2w, ln2b, FFN_TILE, FFN_PARTS).reshape(batch, seq, D_MODEL)

    xs = x_sample.reshape(n_s, D_MODEL)
    y1s, ks, vs, ss = _sample_step(
        sinks, xs, mods, tab_s, w_q, w_rest, w_out_att, w_out_ret, ret_gn_w.reshape(RET_HEADS, LANES),
        jnp.swapaxes(cache_k.reshape(n_s, WINDOW, LANES), 1, 2),
        jnp.swapaxes(cache_v.reshape(n_s, WINDOW, LANES), 1, 2),
        state_ret.reshape(n_s, RET_HEADS, RET_DK, LANES))
    ks, vs = jnp.swapaxes(ks, 1, 2), jnp.swapaxes(vs, 1, 2)
    ys = _ffn(y1s, mods, 0, n_s, 0, w_up_b, w_down_b, ln1w, ln1b, ln2w, ln2b, n_s, 1)

    kv_shape = (1, batch, WINDOW, ATT_KV_HEADS, HEAD_DIM)
    kvs_shape = (1, n_s, WINDOW, ATT_KV_HEADS, HEAD_DIM)
    return (yp, ys.reshape(n_s, 1, D_MODEL),
            kp.reshape(kv_shape), vp.reshape(kv_shape), sp[None],
            ks.reshape(kvs_shape), vs.reshape(kvs_shape), ss[None])
```

```python
import functools
import math

import numpy as np
import jax
import jax.numpy as jnp
from jax.experimental import pallas as pl
from jax.experimental.pallas import tpu as pltpu

D_MODEL = 1024
WINDOW = 128
CHUNK = 128
ATT_HEADS = 8
ATT_KV_HEADS = 2
ATT_GROUP = ATT_HEADS // ATT_KV_HEADS
HEAD_DIM = 64
ROPE_DIMS = HEAD_DIM // 4
ROPE_THETA = 500000.0
RET_HEADS = 4
RET_DK = 128
RET_THETA = 10000.0
ATT_WIDTH = ATT_HEADS * HEAD_DIM
KV_WIDTH = ATT_KV_HEADS * HEAD_DIM
RET_WIDTH = RET_HEADS * RET_DK
IN_WIDTH = ATT_WIDTH + 2 * KV_WIDTH + 4 * RET_WIDTH
D_FF = 2816
DEPTH = 1
DEEPNORM_ALPHA = (2 * DEPTH) ** 0.25
LN_EPS = 1e-5
GN_EPS = 1e-6
PAST_LEN = 16384

OFF_Q = 0
OFF_KV = ATT_WIDTH
OFF_RQ = OFF_KV + 2 * KV_WIDTH
OFF_RK = OFF_RQ + RET_WIDTH
OFF_RV = OFF_RK + RET_WIDTH
OFF_RG = OFF_RV + RET_WIDTH

LANES = 128
VMEM_LIMIT_BYTES = 56 * 1024 * 1024

PROMPT_TILE = 512
FILL_WIDTH = 256
FILL_PLAN = ((2, 1), (2, 1), (2, 1), (2, 0))
FFN_TILE = 1024
FFN_PARTS = 4
FFN_COLS = 256
SAMPLE_BLOCK = 8

BF16 = jnp.bfloat16
F32 = jnp.float32


def _dot(a, b):
    return jnp.dot(a, b, preferred_element_type=F32)


def _dot_nt(a, b):
    return jax.lax.dot_general(a, b, (((1,), (1,)), ((), ())), preferred_element_type=F32)


def _dot_tn(a, b):
    return jax.lax.dot_general(a, b, (((0,), (0,)), ((), ())), preferred_element_type=F32)


def _sigmoid(x):
    return 1.0 / (1.0 + jnp.exp(-x))


def _silu(x):
    return x * _sigmoid(x)


def _layer_norm(r, w, b):
    mu = jnp.mean(r, axis=-1, keepdims=True)
    d = r - mu
    var = jnp.mean(d * d, axis=-1, keepdims=True)
    return d * jax.lax.rsqrt(var + LN_EPS) * w + b


def _rope_att(slab, c, s1, s2):
    return slab * c + pltpu.roll(slab, LANES - 8, axis=1) * s1 + pltpu.roll(slab, 8, axis=1) * s2


def _rope_ret(slab, c, s):
    return slab * c + pltpu.roll(slab, LANES // 2, axis=1) * s


def _rope_tables(pos):
    pos = np.asarray(pos, np.float64)[:, None]
    half = ROPE_DIMS // 2
    inv = ROPE_THETA ** (-np.arange(half, dtype=np.float64) / half)
    ang = pos * inv[None, :]
    cos, sin = np.cos(ang), np.sin(ang)
    n = pos.shape[0]
    head_c = np.ones((n, HEAD_DIM)); head_s1 = np.zeros((n, HEAD_DIM)); head_s2 = np.zeros((n, HEAD_DIM))
    head_c[:, :half] = cos; head_c[:, half:2 * half] = cos
    head_s1[:, :half] = -sin
    head_s2[:, half:2 * half] = sin
    att = [np.tile(t, (1, LANES // HEAD_DIM)) for t in (head_c, head_s1, head_s2)]
    rhalf = RET_DK // 2
    rinv = RET_THETA ** (-np.arange(rhalf, dtype=np.float64) / rhalf)
    rang = pos * rinv[None, :]
    rc = np.concatenate([np.cos(rang), np.cos(rang)], axis=1)
    rs = np.concatenate([-np.sin(rang), np.sin(rang)], axis=1)
    return np.stack(att + [rc, rs], axis=0).astype(np.float32)


def _log_gamma():
    lin = np.linspace(math.log(1.0 / 32), math.log(1.0 / 512), RET_HEADS)
    return np.log1p(-np.exp(lin))


def _decay_tables():
    lg = _log_gamma()
    idx = np.arange(CHUNK, dtype=np.float64)
    diff = idx[:, None] - idx[None, :]
    dmat = np.where(diff[None] >= 0, np.exp(lg[:, None, None] * np.maximum(diff, 0.0)[None]), 0.0)
    dq = np.exp(lg[:, None] * (idx[None, :] + 1.0))[:, :, None] * np.ones((1, 1, LANES))
    dk = np.exp(lg[:, None] * (CHUNK - 1.0 - idx[None, :]))[:, :, None] * np.ones((1, 1, LANES))
    return np.concatenate([dmat, dq, dk], axis=0).astype(np.float32)


def _band_bias():
    i = np.arange(CHUNK)[:, None]
    j = np.arange(2 * CHUNK)[None, :]
    valid = (j >= i) & (j <= i + WINDOW)
    neg = np.float32(-1e30)
    b0 = np.where(valid, 0.0, neg)
    b1 = np.where(valid & (j >= CHUNK), 0.0, neg)
    return np.stack([b0, b1], axis=0).astype(np.float32)


def _ada_kernel(cs_ref, cp_ref, w1_ref, w2_ref, b1_ref, b2_ref, o_ref):
    rows = cs_ref.shape[0] + cp_ref.shape[0]
    pad = jnp.zeros((-rows % 16, D_MODEL), F32)
    a = _silu(jnp.concatenate([cs_ref[...], cp_ref[...], pad], axis=0)).astype(BF16)

    def emit(w_ref, b_ref):
        o_ref[...] = _dot(a, w_ref[...].astype(BF16))[0:rows] + b_ref[...]

    @pl.when(pl.program_id(0) < 3)
    def _():
        emit(w1_ref, b1_ref)

    @pl.when(pl.program_id(0) >= 3)
    def _():
        emit(w2_ref, b2_ref)


def _ada_mod(c_sample, c_prompt, w_mix, w_ffn, b_mix, b_ffn):
    rows = c_sample.shape[0] + c_prompt.shape[0]
    d = D_MODEL
    return pl.pallas_call(
        _ada_kernel,
        out_shape=jax.ShapeDtypeStruct((6, rows, d), F32),
        grid=(6,),
        in_specs=[
            pl.BlockSpec(c_sample.shape, lambda j: (0, 0)),
            pl.BlockSpec(c_prompt.shape, lambda j: (0, 0)),
            pl.BlockSpec((d, d), lambda j: (0, jnp.minimum(j, 2))),
            pl.BlockSpec((d, d), lambda j: (0, jnp.maximum(j - 3, 0))),
            pl.BlockSpec((1, d), lambda j: (0, jnp.minimum(j, 2))),
            pl.BlockSpec((1, d), lambda j: (0, jnp.maximum(j - 3, 0))),
        ],
        out_specs=pl.BlockSpec((None, rows, d), lambda j: (j, 0, 0)),
        compiler_params=pltpu.CompilerParams(
            dimension_semantics=("arbitrary",), vmem_limit_bytes=VMEM_LIMIT_BYTES),
        name="ada_mod",
    )(c_sample, c_prompt, w_mix, w_ffn, b_mix, b_ffn)


def _mixer_step(z_prev, z_cur, sinks_ref, x_ref, mods_ref, tab_ref, xp_ref, dec_ref, bias_ref,
                wq_ref, wrest_ref, wout_att_ref, wout_ret_ref, gnw_ref,
                y_ref, kout_ref, vout_ref, s_ref, kcarry, vcarry, mix_s,
                *, cur_b, prev_b, seq_start, tile, gl):
    n_chunks = tile // CHUNK
    x = x_ref[...]
    h = (x * (1.0 + mods_ref[1, pl.ds(cur_b, 1), :]) + mods_ref[0, pl.ds(cur_b, 1), :]).astype(BF16)
    n_fill = IN_WIDTH // FILL_WIDTH
    pending = list(range(n_fill))

    def fill(count):
        for _ in range(count):
            if pending:
                i = pending.pop(0)
                col = i * FILL_WIDTH
                w = (wq_ref[:, col:col + FILL_WIDTH] if col < ATT_WIDTH
                     else wrest_ref[:, col - ATT_WIDTH:col - ATT_WIDTH + FILL_WIDTH])
                z_cur[:, col:col + FILL_WIDTH] = _dot(h, w)

    gate = mods_ref[2, pl.ds(prev_b, 1), :]
    lo = jax.lax.broadcasted_iota(jnp.int32, (CHUNK, LANES), 1) < HEAD_DIM
    zero = jnp.zeros((CHUNK, LANES), F32)
    k_prev = [kcarry[g] for g in range(ATT_KV_HEADS)]
    v_prev = [vcarry[g] for g in range(ATT_KV_HEADS)]

    for c in range(n_chunks):
        rows = slice(c * CHUNK, (c + 1) * CHUNK)
        bias = bias_ref[seq_start] if c == 0 else bias_ref[0]
        att_c, att_s1, att_s2 = tab_ref[0, rows, :], tab_ref[1, rows, :], tab_ref[2, rows, :]
        ret_c, ret_s = tab_ref[3, rows, :], tab_ref[4, rows, :]

        def zs(col):
            return z_prev[rows, col:col + LANES]

        q_stack = jnp.concatenate(
            [(_rope_att(zs(OFF_Q + j * LANES), att_c, att_s1, att_s2) * (HEAD_DIM ** -0.5)).astype(BF16)
             for j in range(4)], axis=0)
        k = _rope_att(zs(OFF_KV), att_c, att_s1, att_s2)
        v = zs(OFF_KV + LANES)
        k_cur = [jnp.where(lo, k, zero).astype(BF16), jnp.where(lo, zero, k).astype(BF16)]
        v_cur = [jnp.where(lo, v, zero).astype(BF16), jnp.where(lo, zero, v).astype(BF16)]
        k_band = [jnp.concatenate([k_prev[g], k_cur[g]], axis=0) for g in range(ATT_KV_HEADS)]
        v_band = [jnp.concatenate([v_prev[g], v_cur[g]], axis=0) for g in range(ATT_KV_HEADS)]
        k_prev, v_prev = k_cur, v_cur
        if c == n_chunks - 1:
            kout_ref[...] = k
            vout_ref[...] = v
            for g in range(ATT_KV_HEADS):
                kcarry[g] = k_cur[g]
                vcarry[g] = v_cur[g]

        ret = []
        for hh in range(RET_HEADS):
            rq = _rope_ret(zs(OFF_RQ + hh * LANES), ret_c, ret_s)
            rk = _rope_ret(zs(OFF_RK + hh * LANES), ret_c, ret_s) * (RET_DK ** -0.5)
            ret.append(dict(
                q=rq.astype(BF16), qd=(rq * dec_ref[RET_HEADS + hh]).astype(BF16),
                k=rk.astype(BF16), kd=(rk * dec_ref[2 * RET_HEADS + hh]).astype(BF16),
                v=zs(OFF_RV + hh * LANES).astype(BF16)))

        sc_att = [_dot_nt(q_stack, k_band[g]) for g in range(ATT_KV_HEADS)]
        sc_ret = [_dot_nt(r["q"], r["k"]) for r in ret]
        fill(FILL_PLAN[c][0])

        att = None
        for g in range(ATT_KV_HEADS):
            ws = []
            for j in range(4):
                sj = sc_att[g][j * CHUNK:(j + 1) * CHUNK, :] + bias
                sink = sinks_ref[g * ATT_GROUP + j]
                mx = jnp.maximum(jnp.max(sj, axis=-1, keepdims=True), sink)
                p = jnp.exp(sj - mx)
                den = jnp.sum(p, axis=-1, keepdims=True) + jnp.exp(sink - mx)
                ws.append((p * (1.0 / den)).astype(BF16))
            o = _dot(jnp.concatenate(ws, axis=0), v_band[g])
            att = o if att is None else att + o
        for j in range(4):
            mix_s[rows, j * LANES:(j + 1) * LANES] = att[j * CHUNK:(j + 1) * CHUNK, :].astype(BF16)

        outs = []
        for hh, r in enumerate(ret):
            state = s_ref[hh]
            sc = sc_ret[hh] * dec_ref[hh]
            outs.append(_dot(sc.astype(BF16), r["v"]) + _dot(r["qd"], state.astype(BF16)))
            s_ref[hh] = gl[hh] * state + _dot_tn(r["kd"], r["v"])
        fill(FILL_PLAN[c][1])
        for hh in range(RET_HEADS):
            cols = slice(hh * LANES, (hh + 1) * LANES)
            o = outs[hh]
            mu = jnp.mean(o, axis=-1, keepdims=True)
            d = o - mu
            var = jnp.mean(d * d, axis=-1, keepdims=True)
            nrm = d * jax.lax.rsqrt(var + GN_EPS) * gnw_ref[:, cols]
            mix_s[rows, ATT_WIDTH + hh * LANES:ATT_WIDTH + (hh + 1) * LANES] = (
                _silu(zs(OFF_RG + hh * LANES)) * nrm).astype(BF16)

    y = _dot(mix_s[...], jnp.concatenate([wout_att_ref[...], wout_ret_ref[...]], axis=0))
    fill(n_fill)
    y_ref[...] = DEEPNORM_ALPHA * xp_ref[...] + (1.0 + gate) * y


def _mixer_kernel(sinks_ref, x_ref, mods_ref, tab_ref, xp_ref, dec_ref, bias_ref,
                  wq_ref, wrest_ref, wout_att_ref, wout_ret_ref, gnw_ref,
                  y_ref, kout_ref, vout_ref, s_ref,
                  z_a, z_b, kcarry, vcarry, mix_s, *, tile, ns, nt, gl):
    t = pl.program_id(0)
    prev_tile = jnp.maximum(t - 1, 0)
    sa = jax.lax.rem(prev_tile, ns)
    seq_start = (sa == 0).astype(jnp.int32)
    cur_b = jax.lax.div(jnp.minimum(t, nt - 1), ns)
    prev_b = jax.lax.div(prev_tile, ns)
    parity = jax.lax.rem(t, 2)

    @pl.when(t == 0)
    def _():
        z_b[...] = jnp.zeros(z_b.shape, z_b.dtype)
        kcarry[...] = jnp.zeros(kcarry.shape, kcarry.dtype)
        vcarry[...] = jnp.zeros(vcarry.shape, vcarry.dtype)

    @pl.when(sa == 0)
    def _():
        s_ref[...] = jnp.zeros_like(s_ref)

    step = functools.partial(
        _mixer_step, sinks_ref=sinks_ref, x_ref=x_ref, mods_ref=mods_ref, tab_ref=tab_ref, xp_ref=xp_ref,
        dec_ref=dec_ref, bias_ref=bias_ref, wq_ref=wq_ref, wrest_ref=wrest_ref,
        wout_att_ref=wout_att_ref, wout_ret_ref=wout_ret_ref,
        gnw_ref=gnw_ref, y_ref=y_ref, kout_ref=kout_ref,
        vout_ref=vout_ref, s_ref=s_ref, kcarry=kcarry, vcarry=vcarry, mix_s=mix_s,
        cur_b=cur_b, prev_b=prev_b, seq_start=seq_start, tile=tile, gl=gl)

    @pl.when(parity == 0)
    def _():
        step(z_b, z_a)

    @pl.when(parity == 1)
    def _():
        step(z_a, z_b)


def _mixer_prompt(x, mods, mod_row0, sinks, tab, dec, bias, w_q, w_rest, w_out_att, w_out_ret, gn_w):
    batch, seq, _ = x.shape
    assert mod_row0 % batch == 0 and batch % 8 == 0
    tile = PROMPT_TILE
    ns = seq // tile
    gl = tuple(float(v) for v in np.exp(_log_gamma() * CHUNK).astype(np.float32))
    nt = batch * ns
    const = lambda shape: pl.BlockSpec(shape, lambda t, *_: (0,) * len(shape),
                                       pipeline_mode=pl.Buffered(1))
    cur_b = lambda t: jnp.minimum(t, nt - 1) // ns
    cur_s = lambda t: jnp.minimum(t, nt - 1) % ns
    prev_b = lambda t: jnp.maximum(t - 1, 0) // ns
    prev_s = lambda t: jnp.maximum(t - 1, 0) % ns
    grid_spec = pltpu.PrefetchScalarGridSpec(
        num_scalar_prefetch=1,
        grid=(nt + 1,),
        in_specs=[
            pl.BlockSpec((None, tile, D_MODEL), lambda t, *_: (cur_b(t), cur_s(t), 0)),
            pl.BlockSpec((3, batch, D_MODEL), lambda t, *_: (0, mod_row0 // batch, 0),
                         pipeline_mode=pl.Buffered(1)),
            pl.BlockSpec((5, tile, LANES), lambda t, *_: (0, prev_s(t), 0)),
            pl.BlockSpec((None, tile, D_MODEL), lambda t, *_: (prev_b(t), prev_s(t), 0)),
            const((3 * RET_HEADS, CHUNK, LANES)),
            const((2, CHUNK, 2 * CHUNK)),
            const((D_MODEL, ATT_WIDTH)),
            const((D_MODEL, IN_WIDTH - ATT_WIDTH)),
            const((ATT_WIDTH, D_MODEL)),
            const((RET_WIDTH, D_MODEL)),
            const((1, RET_WIDTH)),
        ],
        out_specs=[
            pl.BlockSpec((None, tile, D_MODEL), lambda t, *_: (prev_b(t), prev_s(t), 0)),
            pl.BlockSpec((None, WINDOW, LANES), lambda t, *_: (prev_b(t), 0, 0)),
            pl.BlockSpec((None, WINDOW, LANES), lambda t, *_: (prev_b(t), 0, 0)),
            pl.BlockSpec((None, RET_HEADS, RET_DK, LANES), lambda t, *_: (prev_b(t), 0, 0, 0)),
        ],
        scratch_shapes=[
            pltpu.VMEM((tile, IN_WIDTH), F32),
            pltpu.VMEM((tile, IN_WIDTH), F32),
            pltpu.VMEM((ATT_KV_HEADS, CHUNK, LANES), BF16),
            pltpu.VMEM((ATT_KV_HEADS, CHUNK, LANES), BF16),
            pltpu.VMEM((tile, D_MODEL), BF16),
        ],
    )
    return pl.pallas_call(
        functools.partial(_mixer_kernel, tile=tile, ns=ns, nt=nt, gl=gl),
        out_shape=[
            jax.ShapeDtypeStruct((batch, seq, D_MODEL), F32),
            jax.ShapeDtypeStruct((batch, WINDOW, LANES), F32),
            jax.ShapeDtypeStruct((batch, WINDOW, LANES), F32),
            jax.ShapeDtypeStruct((batch, RET_HEADS, RET_DK, LANES), F32),
        ],
        grid_spec=grid_spec,
        compiler_params=pltpu.CompilerParams(
            dimension_semantics=("arbitrary",), vmem_limit_bytes=VMEM_LIMIT_BYTES),
        name="mixer_prompt",
    )(sinks, x, mods, tab, x, dec, bias, w_q, w_rest, w_out_att, w_out_ret, gn_w)


def _ffn_rows(r_ref, o_ref, mod, parts, wup_ref, wdown_ref, ln1w_ref, ln1b_ref, lnw_ref, lnb_ref):
    rows = r_ref.shape[0] // parts
    sl = [slice(p * rows, (p + 1) * rows) for p in range(parts)]
    ys = [_layer_norm(r_ref[sl[p], :], ln1w_ref[...], ln1b_ref[...]) for p in range(parts)]
    hs = [(ys[p] * (1.0 + mod(1, sl[p])) + mod(0, sl[p])).astype(BF16) for p in range(parts)]
    def act(h):
        return jnp.concatenate(
            [(_silu(_dot(h, wup_ref[:, j:j + FFN_COLS]))
              * _dot(h, wup_ref[:, D_FF + j:D_FF + j + FFN_COLS])).astype(BF16)
             for j in range(0, D_FF, FFN_COLS)], axis=1)

    acts = [act(h) for h in hs]
    fs = [_dot(a, wdown_ref[...]) for a in acts]
    for p in range(parts):
        r = DEEPNORM_ALPHA * ys[p] + (1.0 + mod(2, sl[p])) * fs[p]
        o_ref[sl[p], :] = _layer_norm(r, lnw_ref[...], lnb_ref[...])


def _ffn_kernel(r_ref, mods_ref, rs_ref, mods_s_ref, wup_ref, wdown_ref, ln1w_ref, ln1b_ref, lnw_ref,
                lnb_ref, o_ref, os_ref, *, parts, tiles_per_mod_row):
    weights = (wup_ref, wdown_ref, ln1w_ref, ln1b_ref, lnw_ref, lnb_ref)
    seq = jax.lax.div(pl.program_id(0), tiles_per_mod_row)
    _ffn_rows(r_ref, o_ref, lambda k, rows: mods_ref[k, pl.ds(seq, 1), :], parts, *weights)

    @pl.when(pl.program_id(0) == pl.num_programs(0) - 1)
    def _():
        _ffn_rows(rs_ref, os_ref, lambda k, rows: mods_s_ref[k, rows, :], 1, *weights)


def _ffn(y, y_s, mods, tiles_per_mod_row, w_up, w_down, ln1_w, ln1_b, ln_w, ln_b, tile, parts):
    rows, _ = y.shape
    n_s = y_s.shape[0]
    batch = mods.shape[1] - n_s
    nt = rows // tile
    assert n_s % batch == 0 and nt == batch * tiles_per_mod_row
    const = lambda shape, idx: pl.BlockSpec(shape, lambda t: idx, pipeline_mode=pl.Buffered(1))
    return pl.pallas_call(
        functools.partial(_ffn_kernel, parts=parts, tiles_per_mod_row=tiles_per_mod_row),
        out_shape=(jax.ShapeDtypeStruct(y.shape, F32), jax.ShapeDtypeStruct(y_s.shape, F32)),
        grid=(nt,),
        in_specs=[
            pl.BlockSpec((tile, D_MODEL), lambda t: (t, 0)),
            const((3, batch, D_MODEL), (1, n_s // batch, 0)),
            const((n_s, D_MODEL), (0, 0)),
            const((3, n_s, D_MODEL), (1, 0, 0)),
            const((D_MODEL, 2 * D_FF), (0, 0)),
            const((D_FF, D_MODEL), (0, 0)),
            const((1, D_MODEL), (0, 0)),
            const((1, D_MODEL), (0, 0)),
            const((1, D_MODEL), (0, 0)),
            const((1, D_MODEL), (0, 0)),
        ],
        out_specs=(pl.BlockSpec((tile, D_MODEL), lambda t: (t, 0)),
                   pl.BlockSpec((n_s, D_MODEL), lambda t: (0, 0))),
        compiler_params=pltpu.CompilerParams(
            dimension_semantics=("arbitrary",), vmem_limit_bytes=VMEM_LIMIT_BYTES),
        name="ffn",
    )(y, mods, y_s, mods, w_up, w_down, ln1_w, ln1_b, ln_w, ln_b)


def _sample_kernel(sinks_ref, x_ref, mod_ref, tab_ref, wq_ref, wrest_ref, wout_att_ref, wout_ret_ref,
                   gnw_ref,
                   ck_ref, cv_ref, st_ref,
                   y_ref, ko_ref, vo_ref, so_ref,
                   z_s, mix_s, *, g1, n_blocks):
    i = pl.program_id(0)
    bb = SAMPLE_BLOCK

    @pl.when(i == 0)
    def _():
        m = mod_ref[...]
        h = (x_ref[...] * (1.0 + m[1]) + m[0]).astype(BF16)
        z = jnp.concatenate([_dot(h, wq_ref[...]), _dot(h, wrest_ref[...])], axis=1)
        att_c, att_s1, att_s2 = tab_ref[0], tab_ref[1], tab_ref[2]
        ret_c, ret_s = tab_ref[3], tab_ref[4]
        for j in range(4):
            col = OFF_Q + j * LANES
            z_s[:, col:col + LANES] = _rope_att(z[:, col:col + LANES], att_c, att_s1, att_s2) * (HEAD_DIM ** -0.5)
            col = OFF_RQ + j * LANES
            z_s[:, col:col + LANES] = _rope_ret(z[:, col:col + LANES], ret_c, ret_s)
            col = OFF_RK + j * LANES
            z_s[:, col:col + LANES] = _rope_ret(z[:, col:col + LANES], ret_c, ret_s) * (RET_DK ** -0.5)
        z_s[:, OFF_KV:OFF_KV + LANES] = _rope_att(z[:, OFF_KV:OFF_KV + LANES], att_c, att_s1, att_s2)
        z_s[:, OFF_KV + LANES:OFF_RQ] = z[:, OFF_KV + LANES:OFF_RQ]
        z_s[:, OFF_RV:OFF_RG] = z[:, OFF_RV:OFF_RG]
        z_s[:, OFF_RG:] = _silu(z[:, OFF_RG:])

    lane_w = jax.lax.broadcasted_iota(jnp.int32, (LANES, WINDOW), 1)
    last_lane = lane_w == WINDOW - 1
    lo_row = lane_w[0:1, :] < HEAD_DIM
    r4 = jax.lax.broadcasted_iota(jnp.int32, (ATT_GROUP, 1), 0)
    r8 = jax.lax.broadcasted_iota(jnp.int32, (ATT_HEADS, 1), 0)
    sink8 = jnp.zeros((ATT_HEADS, 1), F32)
    for hd in range(ATT_HEADS):
        sink8 = jnp.where(r8 == hd, sinks_ref[hd], sink8)
    head_blk = (jax.lax.broadcasted_iota(jnp.int32, (16, RET_WIDTH), 1) // LANES
                == jax.lax.broadcasted_iota(jnp.int32, (16, RET_WIDTH), 0))
    g1_col = jnp.where(r4 == 0, g1[0], jnp.where(r4 == 1, g1[1], jnp.where(r4 == 2, g1[2], g1[3])))

    def rows16(a):
        return jnp.concatenate([a, jnp.zeros((16 - a.shape[0], a.shape[1]), F32)], axis=0)

    def heads(row, off):
        return jnp.concatenate([row[:, off + j * LANES:off + (j + 1) * LANES] for j in range(4)], axis=0)

    def block_rows(a4):
        wide = jnp.concatenate([rows16(a4)] * RET_HEADS, axis=1)
        return jnp.where(head_blk, wide, jnp.zeros_like(wide))

    r0 = pl.multiple_of(i * bb, bb)
    zblk = z_s[pl.ds(r0, bb), :]
    rows = [zblk[b:b + 1, :] for b in range(bb)]

    kn = [r[:, OFF_KV:OFF_KV + LANES] for r in rows]
    vn = [r[:, OFF_KV + LANES:OFF_RQ] for r in rows]
    kn_t = zblk[:, OFF_KV:OFF_KV + LANES].T
    vn_t = zblk[:, OFF_KV + LANES:OFF_RQ].T
    q8 = []
    for r in rows:
        q4 = heads(r, OFF_Q)
        zero4 = jnp.zeros_like(q4)
        q8.append(jnp.concatenate([jnp.where(lo_row, q4, zero4), jnp.where(lo_row, zero4, q4)], axis=0))
    sc, oh = [], []
    for b in range(bb):
        sc.append(_dot(rows16(q8[b]).astype(BF16), ck_ref[b].astype(BF16))[0:ATT_HEADS])
    for b in range(bb):
        rq = heads(rows[b], OFF_RQ)
        rk = heads(rows[b], OFF_RK)
        rv = heads(rows[b], OFF_RV)
        state = st_ref[b].reshape(RET_WIDTH, LANES)
        o2 = _dot(block_rows(rq * g1_col).astype(BF16), state.astype(BF16))[0:RET_HEADS]
        outer = _dot_tn(block_rows(rk).astype(BF16), rows16(rv).astype(BF16))
        for hh in range(RET_HEADS):
            so_ref[b, hh] = g1[hh] * st_ref[b, hh] + outer[hh * RET_DK:(hh + 1) * RET_DK, :]
        o = jnp.sum(rq * rk, axis=-1, keepdims=True) * rv + o2
        mu = jnp.mean(o, axis=-1, keepdims=True)
        d = o - mu
        var = jnp.mean(d * d, axis=-1, keepdims=True)
        oh.append(heads(rows[b], OFF_RG) * (d * jax.lax.rsqrt(var + GN_EPS) * gnw_ref[...]))

    ws, wn = [], []
    for b in range(bb):
        sn = jnp.sum(q8[b] * kn[b], axis=-1, keepdims=True)
        mx = jnp.maximum(jnp.maximum(jnp.max(sc[b], axis=-1, keepdims=True), sn), sink8)
        p = jnp.exp(sc[b] - mx)
        pn = jnp.exp(sn - mx)
        inv = 1.0 / (jnp.sum(p, axis=-1, keepdims=True) + pn + jnp.exp(sink8 - mx))
        ws.append(rows16(p * inv).astype(BF16))
        wn.append(pn * inv)

    att = []
    for b in range(bb):
        o8 = _dot_nt(ws[b], cv_ref[b].astype(BF16))[0:ATT_HEADS] + wn[b] * vn[b]
        att.append(jnp.where(lo_row, o8[0:ATT_GROUP], o8[ATT_GROUP:]))

    for b in range(bb):
        ko_ref[b] = jnp.where(last_lane, kn_t[:, b:b + 1], pltpu.roll(ck_ref[b], WINDOW - 1, axis=1))
        vo_ref[b] = jnp.where(last_lane, vn_t[:, b:b + 1], pltpu.roll(cv_ref[b], WINDOW - 1, axis=1))
    blk_rows = pl.ds(r0, bb)
    for j in range(4):
        mix_s[blk_rows, j * LANES:(j + 1) * LANES] = jnp.concatenate(
            [a[j:j + 1, :] for a in att], axis=0)
        mix_s[blk_rows, ATT_WIDTH + j * LANES:ATT_WIDTH + (j + 1) * LANES] = jnp.concatenate(
            [o[j:j + 1, :] for o in oh], axis=0)

    @pl.when(i == n_blocks - 1)
    def _():
        y = (_dot(mix_s[:, :ATT_WIDTH].astype(BF16), wout_att_ref[...])
             + _dot(mix_s[:, ATT_WIDTH:].astype(BF16), wout_ret_ref[...]))
        y_ref[...] = DEEPNORM_ALPHA * x_ref[...] + (1.0 + mod_ref[2]) * y


def _sample_step(sinks, x, mods, tab, w_q, w_rest, w_out_att, w_out_ret, gn_w4,
                 cache_k, cache_v, state):
    n = x.shape[0]
    bb = SAMPLE_BLOCK
    n_blocks = n // bb
    g1 = tuple(float(v) for v in np.exp(_log_gamma()).astype(np.float32))
    full = lambda *shape: pl.BlockSpec(shape, lambda i, *_: (0,) * len(shape))
    blk = lambda *shape: pl.BlockSpec((bb,) + shape, lambda i, *_: (i,) + (0,) * len(shape))
    grid_spec = pltpu.PrefetchScalarGridSpec(
        num_scalar_prefetch=1,
        grid=(n_blocks,),
        in_specs=[
            full(n, D_MODEL), full(3, n, D_MODEL), full(5, 1, LANES),
            full(D_MODEL, ATT_WIDTH), full(D_MODEL, IN_WIDTH - ATT_WIDTH),
            full(ATT_WIDTH, D_MODEL), full(RET_WIDTH, D_MODEL),
            full(RET_HEADS, LANES),
            blk(WINDOW, LANES), blk(WINDOW, LANES), blk(RET_HEADS, RET_DK, LANES),
        ],
        out_specs=[full(n, D_MODEL), blk(WINDOW, LANES), blk(WINDOW, LANES),
                   blk(RET_HEADS, RET_DK, LANES)],
        scratch_shapes=[pltpu.VMEM((n, IN_WIDTH), F32), pltpu.VMEM((n, D_MODEL), F32)],
    )
    return pl.pallas_call(
        functools.partial(_sample_kernel, g1=g1, n_blocks=n_blocks),
        out_shape=[
            jax.ShapeDtypeStruct((n, D_MODEL), F32),
            jax.ShapeDtypeStruct((n, WINDOW, LANES), F32),
            jax.ShapeDtypeStruct((n, WINDOW, LANES), F32),
            jax.ShapeDtypeStruct((n, RET_HEADS, RET_DK, LANES), F32),
        ],
        grid_spec=grid_spec,
        compiler_params=pltpu.CompilerParams(
            dimension_semantics=("arbitrary",), vmem_limit_bytes=VMEM_LIMIT_BYTES),
        name="sample_step",
    )(sinks, x, mods, tab, w_q, w_rest, w_out_att, w_out_ret, gn_w4, cache_k, cache_v, state)


def _slab_order(w, axis):
    shape = w.shape
    pre, post = shape[:axis], shape[axis + 1:]
    w = w.reshape(pre + (ATT_KV_HEADS, ATT_GROUP, HEAD_DIM) + post)
    w = jnp.swapaxes(w, axis, axis + 1)
    return w.reshape(shape)


def kernel(x_prompt, x_sample, c_prompt, c_sample, cache_k, cache_v, state_ret, w_ada_mix, b_ada_mix, w_in, att_sinks, ret_gn_w, w_out, ln1_w, ln1_b, w_ada_ffn, b_ada_ffn, w_up, w_down, ln2_w, ln2_b):
    batch, seq, _ = x_prompt.shape
    n_s = x_sample.shape[0]
    l = 0
    assert w_in.shape[0] == DEPTH == 1

    w_in_f = w_in.reshape(D_MODEL, IN_WIDTH)
    w_out_f = w_out.reshape(D_MODEL, D_MODEL)
    w_q = _slab_order(w_in_f[:, :ATT_WIDTH].astype(BF16), 1)
    w_rest = w_in_f[:, ATT_WIDTH:].astype(BF16)
    w_out_att = _slab_order(w_out_f[:ATT_WIDTH].astype(BF16), 0)
    w_out_ret = w_out_f[ATT_WIDTH:].astype(BF16)
    w_up_b = w_up.reshape(D_MODEL, 2 * D_FF).astype(BF16)
    w_down_b = w_down.reshape(D_FF, D_MODEL).astype(BF16)
    sinks = att_sinks[l]
    gn_w = ret_gn_w[l].reshape(1, RET_WIDTH)
    ln1w, ln1b = ln1_w[l].reshape(1, D_MODEL), ln1_b[l].reshape(1, D_MODEL)
    ln2w, ln2b = ln2_w[l].reshape(1, D_MODEL), ln2_b[l].reshape(1, D_MODEL)

    tab_p = jnp.asarray(_rope_tables(np.arange(seq)))
    tab_s = jnp.asarray(_rope_tables(np.array([PAST_LEN])))
    dec = jnp.asarray(_decay_tables())
    bias = jnp.asarray(_band_bias())

    mods = _ada_mod(c_sample, c_prompt, w_ada_mix.reshape(D_MODEL, 3 * D_MODEL),
                    w_ada_ffn.reshape(D_MODEL, 3 * D_MODEL), b_ada_mix, b_ada_ffn)

    y1p, kp, vp, sp = _mixer_prompt(x_prompt, mods, n_s, sinks, tab_p, dec, bias, w_q, w_rest,
                                    w_out_att, w_out_ret, gn_w)

    xs = x_sample.reshape(n_s, D_MODEL)
    y1s, ks, vs, ss = _sample_step(
        sinks, xs, mods, tab_s, w_q, w_rest, w_out_att, w_out_ret, ret_gn_w.reshape(RET_HEADS, LANES),
        jnp.swapaxes(cache_k.reshape(n_s, WINDOW, LANES), 1, 2),
        jnp.swapaxes(cache_v.reshape(n_s, WINDOW, LANES), 1, 2),
        state_ret.reshape(n_s, RET_HEADS, RET_DK, LANES))
    ks, vs = jnp.swapaxes(ks, 1, 2), jnp.swapaxes(vs, 1, 2)

    yp, ys = _ffn(y1p.reshape(batch * seq, D_MODEL), y1s, mods, seq // FFN_TILE, w_up_b, w_down_b,
                  ln1w, ln1b, ln2w, ln2b, FFN_TILE, FFN_PARTS)
    yp = yp.reshape(batch, seq, D_MODEL)

    kv_shape = (1, batch, WINDOW, ATT_KV_HEADS, HEAD_DIM)
    kvs_shape = (1, n_s, WINDOW, ATT_KV_HEADS, HEAD_DIM)
    return (yp, ys.reshape(n_s, 1, D_MODEL),
            kp.reshape(kv_shape), vp.reshape(kv_shape), sp[None],
            ks.reshape(kvs_shape), vs.reshape(kvs_shape), ss[None])
```

```python
import functools
import math

import numpy as np
import jax
import jax.numpy as jnp
from jax.experimental import pallas as pl
from jax.experimental.pallas import tpu as pltpu

D_MODEL = 1024
WINDOW = 128
CHUNK = 128
ATT_HEADS = 8
ATT_KV_HEADS = 2
ATT_GROUP = ATT_HEADS // ATT_KV_HEADS
HEAD_DIM = 64
ROPE_DIMS = HEAD_DIM // 4
ROPE_THETA = 500000.0
RET_HEADS = 4
RET_DK = 128
RET_THETA = 10000.0
ATT_WIDTH = ATT_HEADS * HEAD_DIM
KV_WIDTH = ATT_KV_HEADS * HEAD_DIM
RET_WIDTH = RET_HEADS * RET_DK
IN_WIDTH = ATT_WIDTH + 2 * KV_WIDTH + 4 * RET_WIDTH
D_FF = 2816
DEPTH = 1
DEEPNORM_ALPHA = (2 * DEPTH) ** 0.25
LN_EPS = 1e-5
GN_EPS = 1e-6
PAST_LEN = 16384

OFF_Q = 0
OFF_KV = ATT_WIDTH
OFF_RQ = OFF_KV + 2 * KV_WIDTH
OFF_RK = OFF_RQ + RET_WIDTH
OFF_RV = OFF_RK + RET_WIDTH
OFF_RG = OFF_RV + RET_WIDTH

LANES = 128
VMEM_LIMIT_BYTES = 56 * 1024 * 1024

PROMPT_TILE = 512
FILL_WIDTH = 256
FILL_PLAN = ((2, 1), (2, 1), (2, 1), (2, 0))
FFN_TILE = 1024
FFN_PARTS = 4
FFN_COLS = 256
SAMPLE_BLOCK = 8
SAMPLE_RING = 3

BF16 = jnp.bfloat16
F32 = jnp.float32


def _dot(a, b):
    return jnp.dot(a, b, preferred_element_type=F32)


def _dot_nt(a, b):
    return jax.lax.dot_general(a, b, (((1,), (1,)), ((), ())), preferred_element_type=F32)


def _dot_tn(a, b):
    return jax.lax.dot_general(a, b, (((0,), (0,)), ((), ())), preferred_element_type=F32)


def _sigmoid(x):
    return 1.0 / (1.0 + jnp.exp(-x))


def _silu(x):
    return x * _sigmoid(x)


def _layer_norm(r, w, b):
    mu = jnp.mean(r, axis=-1, keepdims=True)
    d = r - mu
    var = jnp.mean(d * d, axis=-1, keepdims=True)
    return d * jax.lax.rsqrt(var + LN_EPS) * w + b


def _rope_att(slab, c, s1, s2):
    return slab * c + pltpu.roll(slab, LANES - 8, axis=1) * s1 + pltpu.roll(slab, 8, axis=1) * s2


def _rope_ret(slab, c, s):
    return slab * c + pltpu.roll(slab, LANES // 2, axis=1) * s


def _rope_tables(pos):
    pos = np.asarray(pos, np.float64)[:, None]
    half = ROPE_DIMS // 2
    inv = ROPE_THETA ** (-np.arange(half, dtype=np.float64) / half)
    ang = pos * inv[None, :]
    cos, sin = np.cos(ang), np.sin(ang)
    n = pos.shape[0]
    head_c = np.ones((n, HEAD_DIM)); head_s1 = np.zeros((n, HEAD_DIM)); head_s2 = np.zeros((n, HEAD_DIM))
    head_c[:, :half] = cos; head_c[:, half:2 * half] = cos
    head_s1[:, :half] = -sin
    head_s2[:, half:2 * half] = sin
    att = [np.tile(t, (1, LANES // HEAD_DIM)) for t in (head_c, head_s1, head_s2)]
    rhalf = RET_DK // 2
    rinv = RET_THETA ** (-np.arange(rhalf, dtype=np.float64) / rhalf)
    rang = pos * rinv[None, :]
    rc = np.concatenate([np.cos(rang), np.cos(rang)], axis=1)
    rs = np.concatenate([-np.sin(rang), np.sin(rang)], axis=1)
    return np.stack(att + [rc, rs], axis=0).astype(np.float32)


def _log_gamma():
    lin = np.linspace(math.log(1.0 / 32), math.log(1.0 / 512), RET_HEADS)
    return np.log1p(-np.exp(lin))


def _decay_tables():
    lg = _log_gamma()
    idx = np.arange(CHUNK, dtype=np.float64)
    diff = idx[:, None] - idx[None, :]
    dmat = np.where(diff[None] >= 0, np.exp(lg[:, None, None] * np.maximum(diff, 0.0)[None]), 0.0)
    dq = np.exp(lg[:, None] * (idx[None, :] + 1.0))[:, :, None] * np.ones((1, 1, LANES))
    dk = np.exp(lg[:, None] * (CHUNK - 1.0 - idx[None, :]))[:, :, None] * np.ones((1, 1, LANES))
    return np.concatenate([dmat, dq, dk], axis=0).astype(np.float32)


def _band_bias():
    i = np.arange(CHUNK)[:, None]
    j = np.arange(2 * CHUNK)[None, :]
    valid = (j >= i) & (j <= i + WINDOW)
    neg = np.float32(-1e30)
    b0 = np.where(valid, 0.0, neg)
    b1 = np.where(valid & (j >= CHUNK), 0.0, neg)
    return np.stack([b0, b1], axis=0).astype(np.float32)


def _ada_kernel(cs_ref, cp_ref, w1_ref, w2_ref, b1_ref, b2_ref, o_ref):
    rows = cs_ref.shape[0] + cp_ref.shape[0]
    pad = jnp.zeros((-rows % 16, D_MODEL), F32)
    a = _silu(jnp.concatenate([cs_ref[...], cp_ref[...], pad], axis=0)).astype(BF16)

    def emit(w_ref, b_ref):
        o_ref[...] = _dot(a, w_ref[...].astype(BF16))[0:rows] + b_ref[...]

    @pl.when(pl.program_id(0) < 3)
    def _():
        emit(w1_ref, b1_ref)

    @pl.when(pl.program_id(0) >= 3)
    def _():
        emit(w2_ref, b2_ref)


def _ada_mod(c_sample, c_prompt, w_mix, w_ffn, b_mix, b_ffn):
    rows = c_sample.shape[0] + c_prompt.shape[0]
    d = D_MODEL
    return pl.pallas_call(
        _ada_kernel,
        out_shape=jax.ShapeDtypeStruct((6, rows, d), F32),
        grid=(6,),
        in_specs=[
            pl.BlockSpec(c_sample.shape, lambda j: (0, 0)),
            pl.BlockSpec(c_prompt.shape, lambda j: (0, 0)),
            pl.BlockSpec((d, d), lambda j: (0, jnp.minimum(j, 2))),
            pl.BlockSpec((d, d), lambda j: (0, jnp.maximum(j - 3, 0))),
            pl.BlockSpec((1, d), lambda j: (0, jnp.minimum(j, 2))),
            pl.BlockSpec((1, d), lambda j: (0, jnp.maximum(j - 3, 0))),
        ],
        out_specs=pl.BlockSpec((None, rows, d), lambda j: (j, 0, 0)),
        compiler_params=pltpu.CompilerParams(
            dimension_semantics=("arbitrary",), vmem_limit_bytes=VMEM_LIMIT_BYTES),
        name="ada_mod",
    )(c_sample, c_prompt, w_mix, w_ffn, b_mix, b_ffn)


def _mixer_step(z_prev, z_cur, sinks_ref, x_ref, mods_ref, tab_ref, xp_ref, dec_ref, bias_ref,
                wq_ref, wrest_ref, wout_att_ref, wout_ret_ref, gnw_ref,
                y_ref, kout_ref, vout_ref, s_ref, kcarry, vcarry, mix_s,
                *, cur_b, prev_b, seq_start, tile, gl):
    n_chunks = tile // CHUNK
    x = x_ref[...]
    h = (x * (1.0 + mods_ref[1, pl.ds(cur_b, 1), :]) + mods_ref[0, pl.ds(cur_b, 1), :]).astype(BF16)
    n_fill = IN_WIDTH // FILL_WIDTH
    pending = list(range(n_fill))

    def fill(count):
        for _ in range(count):
            if pending:
                i = pending.pop(0)
                col = i * FILL_WIDTH
                w = (wq_ref[:, col:col + FILL_WIDTH] if col < ATT_WIDTH
                     else wrest_ref[:, col - ATT_WIDTH:col - ATT_WIDTH + FILL_WIDTH])
                z_cur[:, col:col + FILL_WIDTH] = _dot(h, w)

    gate = mods_ref[2, pl.ds(prev_b, 1), :]
    lo = jax.lax.broadcasted_iota(jnp.int32, (CHUNK, LANES), 1) < HEAD_DIM
    zero = jnp.zeros((CHUNK, LANES), F32)
    k_prev = [kcarry[g] for g in range(ATT_KV_HEADS)]
    v_prev = [vcarry[g] for g in range(ATT_KV_HEADS)]

    for c in range(n_chunks):
        rows = slice(c * CHUNK, (c + 1) * CHUNK)
        bias = bias_ref[seq_start] if c == 0 else bias_ref[0]
        att_c, att_s1, att_s2 = tab_ref[0, rows, :], tab_ref[1, rows, :], tab_ref[2, rows, :]
        ret_c, ret_s = tab_ref[3, rows, :], tab_ref[4, rows, :]

        def zs(col):
            return z_prev[rows, col:col + LANES]

        q_stack = jnp.concatenate(
            [(_rope_att(zs(OFF_Q + j * LANES), att_c, att_s1, att_s2) * (HEAD_DIM ** -0.5)).astype(BF16)
             for j in range(4)], axis=0)
        k = _rope_att(zs(OFF_KV), att_c, att_s1, att_s2)
        v = zs(OFF_KV + LANES)
        k_cur = [jnp.where(lo, k, zero).astype(BF16), jnp.where(lo, zero, k).astype(BF16)]
        v_cur = [jnp.where(lo, v, zero).astype(BF16), jnp.where(lo, zero, v).astype(BF16)]
        k_band = [jnp.concatenate([k_prev[g], k_cur[g]], axis=0) for g in range(ATT_KV_HEADS)]
        v_band = [jnp.concatenate([v_prev[g], v_cur[g]], axis=0) for g in range(ATT_KV_HEADS)]
        k_prev, v_prev = k_cur, v_cur
        if c == n_chunks - 1:
            kout_ref[...] = k
            vout_ref[...] = v
            for g in range(ATT_KV_HEADS):
                kcarry[g] = k_cur[g]
                vcarry[g] = v_cur[g]

        ret = []
        for hh in range(RET_HEADS):
            rq = _rope_ret(zs(OFF_RQ + hh * LANES), ret_c, ret_s)
            rk = _rope_ret(zs(OFF_RK + hh * LANES), ret_c, ret_s) * (RET_DK ** -0.5)
            ret.append(dict(
                q=rq.astype(BF16), qd=(rq * dec_ref[RET_HEADS + hh]).astype(BF16),
                k=rk.astype(BF16), kd=(rk * dec_ref[2 * RET_HEADS + hh]).astype(BF16),
                v=zs(OFF_RV + hh * LANES).astype(BF16)))

        sc_att = [_dot_nt(q_stack, k_band[g]) for g in range(ATT_KV_HEADS)]
        sc_ret = [_dot_nt(r["q"], r["k"]) for r in ret]
        fill(FILL_PLAN[c][0])

        att = None
        for g in range(ATT_KV_HEADS):
            ws = []
            for j in range(4):
                sj = sc_att[g][j * CHUNK:(j + 1) * CHUNK, :] + bias
                sink = sinks_ref[g * ATT_GROUP + j]
                mx = jnp.maximum(jnp.max(sj, axis=-1, keepdims=True), sink)
                p = jnp.exp(sj - mx)
                den = jnp.sum(p, axis=-1, keepdims=True) + jnp.exp(sink - mx)
                ws.append((p * (1.0 / den)).astype(BF16))
            o = _dot(jnp.concatenate(ws, axis=0), v_band[g])
            att = o if att is None else att + o
        for j in range(4):
            mix_s[rows, j * LANES:(j + 1) * LANES] = att[j * CHUNK:(j + 1) * CHUNK, :].astype(BF16)

        outs = []
        for hh, r in enumerate(ret):
            state = s_ref[hh]
            sc = sc_ret[hh] * dec_ref[hh]
            outs.append(_dot(sc.astype(BF16), r["v"]) + _dot(r["qd"], state.astype(BF16)))
            s_ref[hh] = gl[hh] * state + _dot_tn(r["kd"], r["v"])
        fill(FILL_PLAN[c][1])
        for hh in range(RET_HEADS):
            cols = slice(hh * LANES, (hh + 1) * LANES)
            o = outs[hh]
            mu = jnp.mean(o, axis=-1, keepdims=True)
            d = o - mu
            var = jnp.mean(d * d, axis=-1, keepdims=True)
            nrm = d * jax.lax.rsqrt(var + GN_EPS) * gnw_ref[:, cols]
            mix_s[rows, ATT_WIDTH + hh * LANES:ATT_WIDTH + (hh + 1) * LANES] = (
                _silu(zs(OFF_RG + hh * LANES)) * nrm).astype(BF16)

    y = _dot(mix_s[...], jnp.concatenate([wout_att_ref[...], wout_ret_ref[...]], axis=0))
    fill(n_fill)
    y_ref[...] = DEEPNORM_ALPHA * xp_ref[...] + (1.0 + gate) * y


def _mixer_kernel(sinks_ref, x_ref, mods_ref, tab_ref, xp_ref, dec_ref, bias_ref,
                  wq_ref, wrest_ref, wout_att_ref, wout_ret_ref, gnw_ref,
                  y_ref, kout_ref, vout_ref, s_ref,
                  z_a, z_b, kcarry, vcarry, mix_s, *, tile, ns, nt, gl):
    t = pl.program_id(0)
    prev_tile = jnp.maximum(t - 1, 0)
    sa = jax.lax.rem(prev_tile, ns)
    seq_start = (sa == 0).astype(jnp.int32)
    cur_b = jax.lax.div(jnp.minimum(t, nt - 1), ns)
    prev_b = jax.lax.div(prev_tile, ns)
    parity = jax.lax.rem(t, 2)

    @pl.when(t == 0)
    def _():
        z_b[...] = jnp.zeros(z_b.shape, z_b.dtype)
        kcarry[...] = jnp.zeros(kcarry.shape, kcarry.dtype)
        vcarry[...] = jnp.zeros(vcarry.shape, vcarry.dtype)

    @pl.when(sa == 0)
    def _():
        s_ref[...] = jnp.zeros_like(s_ref)

    step = functools.partial(
        _mixer_step, sinks_ref=sinks_ref, x_ref=x_ref, mods_ref=mods_ref, tab_ref=tab_ref, xp_ref=xp_ref,
        dec_ref=dec_ref, bias_ref=bias_ref, wq_ref=wq_ref, wrest_ref=wrest_ref,
        wout_att_ref=wout_att_ref, wout_ret_ref=wout_ret_ref,
        gnw_ref=gnw_ref, y_ref=y_ref, kout_ref=kout_ref,
        vout_ref=vout_ref, s_ref=s_ref, kcarry=kcarry, vcarry=vcarry, mix_s=mix_s,
        cur_b=cur_b, prev_b=prev_b, seq_start=seq_start, tile=tile, gl=gl)

    @pl.when(parity == 0)
    def _():
        step(z_b, z_a)

    @pl.when(parity == 1)
    def _():
        step(z_a, z_b)


def _mixer_prompt(x, mods, mod_row0, sinks, tab, dec, bias, w_q, w_rest, w_out_att, w_out_ret, gn_w):
    batch, seq, _ = x.shape
    assert mod_row0 % batch == 0 and batch % 8 == 0
    tile = PROMPT_TILE
    ns = seq // tile
    gl = tuple(float(v) for v in np.exp(_log_gamma() * CHUNK).astype(np.float32))
    nt = batch * ns
    const = lambda shape: pl.BlockSpec(shape, lambda t, *_: (0,) * len(shape),
                                       pipeline_mode=pl.Buffered(1))
    cur_b = lambda t: jnp.minimum(t, nt - 1) // ns
    cur_s = lambda t: jnp.minimum(t, nt - 1) % ns
    prev_b = lambda t: jnp.maximum(t - 1, 0) // ns
    prev_s = lambda t: jnp.maximum(t - 1, 0) % ns
    grid_spec = pltpu.PrefetchScalarGridSpec(
        num_scalar_prefetch=1,
        grid=(nt + 1,),
        in_specs=[
            pl.BlockSpec((None, tile, D_MODEL), lambda t, *_: (cur_b(t), cur_s(t), 0)),
            pl.BlockSpec((3, batch, D_MODEL), lambda t, *_: (0, mod_row0 // batch, 0),
                         pipeline_mode=pl.Buffered(1)),
            pl.BlockSpec((5, tile, LANES), lambda t, *_: (0, prev_s(t), 0)),
            pl.BlockSpec((None, tile, D_MODEL), lambda t, *_: (prev_b(t), prev_s(t), 0)),
            const((3 * RET_HEADS, CHUNK, LANES)),
            const((2, CHUNK, 2 * CHUNK)),
            const((D_MODEL, ATT_WIDTH)),
            const((D_MODEL, IN_WIDTH - ATT_WIDTH)),
            const((ATT_WIDTH, D_MODEL)),
            const((RET_WIDTH, D_MODEL)),
            const((1, RET_WIDTH)),
        ],
        out_specs=[
            pl.BlockSpec((None, tile, D_MODEL), lambda t, *_: (prev_b(t), prev_s(t), 0)),
            pl.BlockSpec((None, WINDOW, LANES), lambda t, *_: (prev_b(t), 0, 0)),
            pl.BlockSpec((None, WINDOW, LANES), lambda t, *_: (prev_b(t), 0, 0)),
            pl.BlockSpec((None, RET_HEADS, RET_DK, LANES), lambda t, *_: (prev_b(t), 0, 0, 0)),
        ],
        scratch_shapes=[
            pltpu.VMEM((tile, IN_WIDTH), F32),
            pltpu.VMEM((tile, IN_WIDTH), F32),
            pltpu.VMEM((ATT_KV_HEADS, CHUNK, LANES), BF16),
            pltpu.VMEM((ATT_KV_HEADS, CHUNK, LANES), BF16),
            pltpu.VMEM((tile, D_MODEL), BF16),
        ],
    )
    return pl.pallas_call(
        functools.partial(_mixer_kernel, tile=tile, ns=ns, nt=nt, gl=gl),
        out_shape=[
            jax.ShapeDtypeStruct((batch, seq, D_MODEL), F32),
            jax.ShapeDtypeStruct((batch, WINDOW, LANES), F32),
            jax.ShapeDtypeStruct((batch, WINDOW, LANES), F32),
            jax.ShapeDtypeStruct((batch, RET_HEADS, RET_DK, LANES), F32),
        ],
        grid_spec=grid_spec,
        compiler_params=pltpu.CompilerParams(
            dimension_semantics=("arbitrary",), vmem_limit_bytes=VMEM_LIMIT_BYTES),
        name="mixer_prompt",
    )(sinks, x, mods, tab, x, dec, bias, w_q, w_rest, w_out_att, w_out_ret, gn_w)


def _ffn_rows(r_ref, o_ref, mod, parts, wup_ref, wdown_ref, ln1w_ref, ln1b_ref, lnw_ref, lnb_ref):
    rows = r_ref.shape[0] // parts
    sl = [slice(p * rows, (p + 1) * rows) for p in range(parts)]
    ys = [_layer_norm(r_ref[sl[p], :], ln1w_ref[...], ln1b_ref[...]) for p in range(parts)]
    hs = [(ys[p] * (1.0 + mod(1, sl[p])) + mod(0, sl[p])).astype(BF16) for p in range(parts)]
    def act(h):
        return jnp.concatenate(
            [(_silu(_dot(h, wup_ref[:, j:j + FFN_COLS]))
              * _dot(h, wup_ref[:, D_FF + j:D_FF + j + FFN_COLS])).astype(BF16)
             for j in range(0, D_FF, FFN_COLS)], axis=1)

    acts = [act(h) for h in hs]
    fs = [_dot(a, wdown_ref[...]) for a in acts]
    for p in range(parts):
        r = DEEPNORM_ALPHA * ys[p] + (1.0 + mod(2, sl[p])) * fs[p]
        o_ref[sl[p], :] = _layer_norm(r, lnw_ref[...], lnb_ref[...])


def _ffn_kernel(r_ref, mods_ref, rs_ref, mods_s_ref, wup_ref, wdown_ref, ln1w_ref, ln1b_ref, lnw_ref,
                lnb_ref, o_ref, os_ref, *, parts, tiles_per_mod_row):
    weights = (wup_ref, wdown_ref, ln1w_ref, ln1b_ref, lnw_ref, lnb_ref)
    seq = jax.lax.div(pl.program_id(0), tiles_per_mod_row)
    _ffn_rows(r_ref, o_ref, lambda k, rows: mods_ref[k, pl.ds(seq, 1), :], parts, *weights)

    @pl.when(pl.program_id(0) == pl.num_programs(0) - 1)
    def _():
        _ffn_rows(rs_ref, os_ref, lambda k, rows: mods_s_ref[k, rows, :], 1, *weights)


def _ffn(y, y_s, mods, tiles_per_mod_row, w_up, w_down, ln1_w, ln1_b, ln_w, ln_b, tile, parts):
    rows, _ = y.shape
    n_s = y_s.shape[0]
    batch = mods.shape[1] - n_s
    nt = rows // tile
    assert n_s % batch == 0 and nt == batch * tiles_per_mod_row
    const = lambda shape, idx: pl.BlockSpec(shape, lambda t: idx, pipeline_mode=pl.Buffered(1))
    return pl.pallas_call(
        functools.partial(_ffn_kernel, parts=parts, tiles_per_mod_row=tiles_per_mod_row),
        out_shape=(jax.ShapeDtypeStruct(y.shape, F32), jax.ShapeDtypeStruct(y_s.shape, F32)),
        grid=(nt,),
        in_specs=[
            pl.BlockSpec((tile, D_MODEL), lambda t: (t, 0)),
            const((3, batch, D_MODEL), (1, n_s // batch, 0)),
            const((n_s, D_MODEL), (0, 0)),
            const((3, n_s, D_MODEL), (1, 0, 0)),
            const((D_MODEL, 2 * D_FF), (0, 0)),
            const((D_FF, D_MODEL), (0, 0)),
            const((1, D_MODEL), (0, 0)),
            const((1, D_MODEL), (0, 0)),
            const((1, D_MODEL), (0, 0)),
            const((1, D_MODEL), (0, 0)),
        ],
        out_specs=(pl.BlockSpec((tile, D_MODEL), lambda t: (t, 0)),
                   pl.BlockSpec((n_s, D_MODEL), lambda t: (0, 0))),
        compiler_params=pltpu.CompilerParams(
            dimension_semantics=("arbitrary",), vmem_limit_bytes=VMEM_LIMIT_BYTES),
        name="ffn",
    )(y, mods, y_s, mods, w_up, w_down, ln1_w, ln1_b, ln_w, ln_b)


def _sample_kernel(sinks_ref, x_ref, mod_ref, tab_ref, wq_ref, wrest_ref, wout_att_ref, wout_ret_ref,
                   gnw_ref,
                   ck_ref, cv_ref, st_hbm,
                   y_ref, ko_ref, vo_ref, so_ref,
                   z_s, mix_s, st_buf, st_sem, *, g1, n_blocks):
    i = pl.program_id(0)
    bb = SAMPLE_BLOCK

    def st_copy(blk, slot):
        return pltpu.make_async_copy(st_hbm.at[pl.ds(blk * bb, bb)], st_buf.at[slot], st_sem.at[slot])

    @pl.when(i == 0)
    def _():
        for s in range(SAMPLE_RING - 1):
            st_copy(s, s).start()

    @pl.when(i + (SAMPLE_RING - 1) < n_blocks)
    def _():
        ahead = i + (SAMPLE_RING - 1)
        st_copy(ahead, jax.lax.rem(ahead, SAMPLE_RING)).start()

    @pl.when(i == 0)
    def _():
        m = mod_ref[...]
        h = (x_ref[...] * (1.0 + m[1]) + m[0]).astype(BF16)
        z = jnp.concatenate([_dot(h, wq_ref[...]), _dot(h, wrest_ref[...])], axis=1)
        att_c, att_s1, att_s2 = tab_ref[0], tab_ref[1], tab_ref[2]
        ret_c, ret_s = tab_ref[3], tab_ref[4]
        for j in range(4):
            col = OFF_Q + j * LANES
            z_s[:, col:col + LANES] = _rope_att(z[:, col:col + LANES], att_c, att_s1, att_s2) * (HEAD_DIM ** -0.5)
            col = OFF_RQ + j * LANES
            z_s[:, col:col + LANES] = _rope_ret(z[:, col:col + LANES], ret_c, ret_s)
            col = OFF_RK + j * LANES
            z_s[:, col:col + LANES] = _rope_ret(z[:, col:col + LANES], ret_c, ret_s) * (RET_DK ** -0.5)
        z_s[:, OFF_KV:OFF_KV + LANES] = _rope_att(z[:, OFF_KV:OFF_KV + LANES], att_c, att_s1, att_s2)
        z_s[:, OFF_KV + LANES:OFF_RQ] = z[:, OFF_KV + LANES:OFF_RQ]
        z_s[:, OFF_RV:OFF_RG] = z[:, OFF_RV:OFF_RG]
        z_s[:, OFF_RG:] = _silu(z[:, OFF_RG:])

    lane_w = jax.lax.broadcasted_iota(jnp.int32, (LANES, WINDOW), 1)
    last_lane = lane_w == WINDOW - 1
    lo_row = lane_w[0:1, :] < HEAD_DIM
    r4 = jax.lax.broadcasted_iota(jnp.int32, (ATT_GROUP, 1), 0)
    r8 = jax.lax.broadcasted_iota(jnp.int32, (ATT_HEADS, 1), 0)
    sink8 = jnp.zeros((ATT_HEADS, 1), F32)
    for hd in range(ATT_HEADS):
        sink8 = jnp.where(r8 == hd, sinks_ref[hd], sink8)
    head_blk = (jax.lax.broadcasted_iota(jnp.int32, (16, RET_WIDTH), 1) // LANES
                == jax.lax.broadcasted_iota(jnp.int32, (16, RET_WIDTH), 0))
    g1_col = jnp.where(r4 == 0, g1[0], jnp.where(r4 == 1, g1[1], jnp.where(r4 == 2, g1[2], g1[3])))

    def rows16(a):
        return jnp.concatenate([a, jnp.zeros((16 - a.shape[0], a.shape[1]), F32)], axis=0)

    def heads(row, off):
        return jnp.concatenate([row[:, off + j * LANES:off + (j + 1) * LANES] for j in range(4)], axis=0)

    def block_rows(a4):
        wide = jnp.concatenate([rows16(a4)] * RET_HEADS, axis=1)
        return jnp.where(head_blk, wide, jnp.zeros_like(wide))

    slot = jax.lax.rem(i, SAMPLE_RING)
    st_copy(i, slot).wait()
    st_ref = st_buf.at[slot]

    r0 = pl.multiple_of(i * bb, bb)
    zblk = z_s[pl.ds(r0, bb), :]
    rows = [zblk[b:b + 1, :] for b in range(bb)]

    kn = [r[:, OFF_KV:OFF_KV + LANES] for r in rows]
    vn = [r[:, OFF_KV + LANES:OFF_RQ] for r in rows]
    kn_t = zblk[:, OFF_KV:OFF_KV + LANES].T
    vn_t = zblk[:, OFF_KV + LANES:OFF_RQ].T
    q8 = []
    for r in rows:
        q4 = heads(r, OFF_Q)
        zero4 = jnp.zeros_like(q4)
        q8.append(jnp.concatenate([jnp.where(lo_row, q4, zero4), jnp.where(lo_row, zero4, q4)], axis=0))
    sc, oh = [], []
    for b in range(bb):
        sc.append(_dot(rows16(q8[b]).astype(BF16), ck_ref[b].astype(BF16))[0:ATT_HEADS])
    for b in range(bb):
        rq = heads(rows[b], OFF_RQ)
        rk = heads(rows[b], OFF_RK)
        rv = heads(rows[b], OFF_RV)
        state = st_ref[b].reshape(RET_WIDTH, LANES)
        o2 = _dot(block_rows(rq * g1_col).astype(BF16), state.astype(BF16))[0:RET_HEADS]
        outer = _dot_tn(block_rows(rk).astype(BF16), rows16(rv).astype(BF16))
        for hh in range(RET_HEADS):
            so_ref[b, hh] = g1[hh] * st_ref[b, hh] + outer[hh * RET_DK:(hh + 1) * RET_DK, :]
        o = jnp.sum(rq * rk, axis=-1, keepdims=True) * rv + o2
        mu = jnp.mean(o, axis=-1, keepdims=True)
        d = o - mu
        var = jnp.mean(d * d, axis=-1, keepdims=True)
        oh.append(heads(rows[b], OFF_RG) * (d * jax.lax.rsqrt(var + GN_EPS) * gnw_ref[...]))

    ws, wn = [], []
    for b in range(bb):
        sn = jnp.sum(q8[b] * kn[b], axis=-1, keepdims=True)
        mx = jnp.maximum(jnp.maximum(jnp.max(sc[b], axis=-1, keepdims=True), sn), sink8)
        p = jnp.exp(sc[b] - mx)
        pn = jnp.exp(sn - mx)
        inv = 1.0 / (jnp.sum(p, axis=-1, keepdims=True) + pn + jnp.exp(sink8 - mx))
        ws.append(rows16(p * inv).astype(BF16))
        wn.append(pn * inv)

    att = []
    for b in range(bb):
        o8 = _dot_nt(ws[b], cv_ref[b].astype(BF16))[0:ATT_HEADS] + wn[b] * vn[b]
        att.append(jnp.where(lo_row, o8[0:ATT_GROUP], o8[ATT_GROUP:]))

    for b in range(bb):
        ko_ref[b] = jnp.where(last_lane, kn_t[:, b:b + 1], pltpu.roll(ck_ref[b], WINDOW - 1, axis=1))
        vo_ref[b] = jnp.where(last_lane, vn_t[:, b:b + 1], pltpu.roll(cv_ref[b], WINDOW - 1, axis=1))
    blk_rows = pl.ds(r0, bb)
    for j in range(4):
        mix_s[blk_rows, j * LANES:(j + 1) * LANES] = jnp.concatenate(
            [a[j:j + 1, :] for a in att], axis=0)
        mix_s[blk_rows, ATT_WIDTH + j * LANES:ATT_WIDTH + (j + 1) * LANES] = jnp.concatenate(
            [o[j:j + 1, :] for o in oh], axis=0)

    @pl.when(i == n_blocks - 1)
    def _():
        y = (_dot(mix_s[:, :ATT_WIDTH].astype(BF16), wout_att_ref[...])
             + _dot(mix_s[:, ATT_WIDTH:].astype(BF16), wout_ret_ref[...]))
        y_ref[...] = DEEPNORM_ALPHA * x_ref[...] + (1.0 + mod_ref[2]) * y


def _sample_step(sinks, x, mods, tab, w_q, w_rest, w_out_att, w_out_ret, gn_w4,
                 cache_k, cache_v, state):
    n = x.shape[0]
    bb = SAMPLE_BLOCK
    n_blocks = n // bb
    assert n_blocks >= SAMPLE_RING - 1
    g1 = tuple(float(v) for v in np.exp(_log_gamma()).astype(np.float32))
    full = lambda *shape: pl.BlockSpec(shape, lambda i, *_: (0,) * len(shape))
    blk = lambda *shape: pl.BlockSpec((bb,) + shape, lambda i, *_: (i,) + (0,) * len(shape))
    grid_spec = pltpu.PrefetchScalarGridSpec(
        num_scalar_prefetch=1,
        grid=(n_blocks,),
        in_specs=[
            full(n, D_MODEL), full(3, n, D_MODEL), full(5, 1, LANES),
            full(D_MODEL, ATT_WIDTH), full(D_MODEL, IN_WIDTH - ATT_WIDTH),
            full(ATT_WIDTH, D_MODEL), full(RET_WIDTH, D_MODEL),
            full(RET_HEADS, LANES),
            blk(WINDOW, LANES), blk(WINDOW, LANES), pl.BlockSpec(memory_space=pl.ANY),
        ],
        out_specs=[full(n, D_MODEL), blk(WINDOW, LANES), blk(WINDOW, LANES),
                   blk(RET_HEADS, RET_DK, LANES)],
        scratch_shapes=[pltpu.VMEM((n, IN_WIDTH), F32), pltpu.VMEM((n, D_MODEL), F32),
                        pltpu.VMEM((SAMPLE_RING, bb, RET_HEADS, RET_DK, LANES), F32),
                        pltpu.SemaphoreType.DMA((SAMPLE_RING,))],
    )
    return pl.pallas_call(
        functools.partial(_sample_kernel, g1=g1, n_blocks=n_blocks),
        out_shape=[
            jax.ShapeDtypeStruct((n, D_MODEL), F32),
            jax.ShapeDtypeStruct((n, WINDOW, LANES), F32),
            jax.ShapeDtypeStruct((n, WINDOW, LANES), F32),
            jax.ShapeDtypeStruct((n, RET_HEADS, RET_DK, LANES), F32),
        ],
        grid_spec=grid_spec,
        compiler_params=pltpu.CompilerParams(
            dimension_semantics=("arbitrary",), vmem_limit_bytes=VMEM_LIMIT_BYTES),
        name="sample_step",
    )(sinks, x, mods, tab, w_q, w_rest, w_out_att, w_out_ret, gn_w4, cache_k, cache_v, state)


def _slab_order(w, axis):
    shape = w.shape
    pre, post = shape[:axis], shape[axis + 1:]
    w = w.reshape(pre + (ATT_KV_HEADS, ATT_GROUP, HEAD_DIM) + post)
    w = jnp.swapaxes(w, axis, axis + 1)
    return w.reshape(shape)


def kernel(x_prompt, x_sample, c_prompt, c_sample, cache_k, cache_v, state_ret, w_ada_mix, b_ada_mix, w_in, att_sinks, ret_gn_w, w_out, ln1_w, ln1_b, w_ada_ffn, b_ada_ffn, w_up, w_down, ln2_w, ln2_b):
    batch, seq, _ = x_prompt.shape
    n_s = x_sample.shape[0]
    l = 0
    assert w_in.shape[0] == DEPTH == 1

    w_in_f = w_in.reshape(D_MODEL, IN_WIDTH)
    w_out_f = w_out.reshape(D_MODEL, D_MODEL)
    w_q = _slab_order(w_in_f[:, :ATT_WIDTH].astype(BF16), 1)
    w_rest = w_in_f[:, ATT_WIDTH:].astype(BF16)
    w_out_att = _slab_order(w_out_f[:ATT_WIDTH].astype(BF16), 0)
    w_out_ret = w_out_f[ATT_WIDTH:].astype(BF16)
    w_up_b = w_up.reshape(D_MODEL, 2 * D_FF).astype(BF16)
    w_down_b = w_down.reshape(D_FF, D_MODEL).astype(BF16)
    sinks = att_sinks[l]
    gn_w = ret_gn_w[l].reshape(1, RET_WIDTH)
    ln1w, ln1b = ln1_w[l].reshape(1, D_MODEL), ln1_b[l].reshape(1, D_MODEL)
    ln2w, ln2b = ln2_w[l].reshape(1, D_MODEL), ln2_b[l].reshape(1, D_MODEL)

    tab_p = jnp.asarray(_rope_tables(np.arange(seq)))
    tab_s = jnp.asarray(_rope_tables(np.array([PAST_LEN])))
    dec = jnp.asarray(_decay_tables())
    bias = jnp.asarray(_band_bias())

    mods = _ada_mod(c_sample, c_prompt, w_ada_mix.reshape(D_MODEL, 3 * D_MODEL),
                    w_ada_ffn.reshape(D_MODEL, 3 * D_MODEL), b_ada_mix, b_ada_ffn)

    y1p, kp, vp, sp = _mixer_prompt(x_prompt, mods, n_s, sinks, tab_p, dec, bias, w_q, w_rest,
                                    w_out_att, w_out_ret, gn_w)

    xs = x_sample.reshape(n_s, D_MODEL)
    y1s, ks, vs, ss = _sample_step(
        sinks, xs, mods, tab_s, w_q, w_rest, w_out_att, w_out_ret, ret_gn_w.reshape(RET_HEADS, LANES),
        jnp.swapaxes(cache_k.reshape(n_s, WINDOW, LANES), 1, 2),
        jnp.swapaxes(cache_v.reshape(n_s, WINDOW, LANES), 1, 2),
        state_ret.reshape(n_s, RET_HEADS, RET_DK, LANES))
    ks, vs = jnp.swapaxes(ks, 1, 2), jnp.swapaxes(vs, 1, 2)

    yp, ys = _ffn(y1p.reshape(batch * seq, D_MODEL), y1s, mods, seq // FFN_TILE, w_up_b, w_down_b,
                  ln1w, ln1b, ln2w, ln2b, FFN_TILE, FFN_PARTS)
    yp = yp.reshape(batch, seq, D_MODEL)

    kv_shape = (1, batch, WINDOW, ATT_KV_HEADS, HEAD_DIM)
    kvs_shape = (1, n_s, WINDOW, ATT_KV_HEADS, HEAD_DIM)
    return (yp, ys.reshape(n_s, 1, D_MODEL),
            kp.reshape(kv_shape), vp.reshape(kv_shape), sp[None],
            ks.reshape(kvs_shape), vs.reshape(kvs_shape), ss[None])
```

```python
import functools
import math

import numpy as np
import jax
import jax.numpy as jnp
from jax.experimental import pallas as pl
from jax.experimental.pallas import tpu as pltpu

D_MODEL = 1024
WINDOW = 128
CHUNK = 128
ATT_HEADS = 8
ATT_KV_HEADS = 2
ATT_GROUP = ATT_HEADS // ATT_KV_HEADS
HEAD_DIM = 64
ROPE_DIMS = HEAD_DIM // 4
ROPE_THETA = 500000.0
RET_HEADS = 4
RET_DK = 128
RET_THETA = 10000.0
ATT_WIDTH = ATT_HEADS * HEAD_DIM
KV_WIDTH = ATT_KV_HEADS * HEAD_DIM
RET_WIDTH = RET_HEADS * RET_DK
IN_WIDTH = ATT_WIDTH + 2 * KV_WIDTH + 4 * RET_WIDTH
D_FF = 2816
DEPTH = 1
DEEPNORM_ALPHA = (2 * DEPTH) ** 0.25
LN_EPS = 1e-5
GN_EPS = 1e-6
PAST_LEN = 16384

OFF_Q = 0
OFF_KV = ATT_WIDTH
OFF_RQ = OFF_KV + 2 * KV_WIDTH
OFF_RK = OFF_RQ + RET_WIDTH
OFF_RV = OFF_RK + RET_WIDTH
OFF_RG = OFF_RV + RET_WIDTH

LANES = 128
VMEM_LIMIT_BYTES = 56 * 1024 * 1024

PROMPT_TILE = 512
FILL_WIDTH = 256
FILL_PLAN = ((2, 1), (2, 1), (2, 1), (2, 0))
FFN_TILE = 1024
FFN_PARTS = 4
FFN_COLS = 256
SAMPLE_BLOCK = 8
SAMPLE_RING = 3

BF16 = jnp.bfloat16
F32 = jnp.float32


def _dot(a, b):
    return jnp.dot(a, b, preferred_element_type=F32)


def _dot_nt(a, b):
    return jax.lax.dot_general(a, b, (((1,), (1,)), ((), ())), preferred_element_type=F32)


def _dot_tn(a, b):
    return jax.lax.dot_general(a, b, (((0,), (0,)), ((), ())), preferred_element_type=F32)


def _sigmoid(x):
    return 1.0 / (1.0 + jnp.exp(-x))


def _silu(x):
    return x * _sigmoid(x)


def _layer_norm(r, w, b):
    mu = jnp.mean(r, axis=-1, keepdims=True)
    d = r - mu
    var = jnp.mean(d * d, axis=-1, keepdims=True)
    return d * jax.lax.rsqrt(var + LN_EPS) * w + b


def _rope_att(slab, c, s1, s2):
    return slab * c + pltpu.roll(slab, LANES - 8, axis=1) * s1 + pltpu.roll(slab, 8, axis=1) * s2


def _rope_ret(slab, c, s):
    return slab * c + pltpu.roll(slab, LANES // 2, axis=1) * s


def _rope_tables(pos):
    pos = np.asarray(pos, np.float64)[:, None]
    half = ROPE_DIMS // 2
    inv = ROPE_THETA ** (-np.arange(half, dtype=np.float64) / half)
    ang = pos * inv[None, :]
    cos, sin = np.cos(ang), np.sin(ang)
    n = pos.shape[0]
    head_c = np.ones((n, HEAD_DIM)); head_s1 = np.zeros((n, HEAD_DIM)); head_s2 = np.zeros((n, HEAD_DIM))
    head_c[:, :half] = cos; head_c[:, half:2 * half] = cos
    head_s1[:, :half] = -sin
    head_s2[:, half:2 * half] = sin
    att = [np.tile(t, (1, LANES // HEAD_DIM)) for t in (head_c, head_s1, head_s2)]
    rhalf = RET_DK // 2
    rinv = RET_THETA ** (-np.arange(rhalf, dtype=np.float64) / rhalf)
    rang = pos * rinv[None, :]
    rc = np.concatenate([np.cos(rang), np.cos(rang)], axis=1)
    rs = np.concatenate([-np.sin(rang), np.sin(rang)], axis=1)
    return np.stack(att + [rc, rs], axis=0).astype(np.float32)


def _log_gamma():
    lin = np.linspace(math.log(1.0 / 32), math.log(1.0 / 512), RET_HEADS)
    return np.log1p(-np.exp(lin))


def _decay_tables():
    lg = _log_gamma()
    idx = np.arange(CHUNK, dtype=np.float64)
    diff = idx[:, None] - idx[None, :]
    dmat = np.where(diff[None] >= 0, np.exp(lg[:, None, None] * np.maximum(diff, 0.0)[None]), 0.0)
    dq = np.exp(lg[:, None] * (idx[None, :] + 1.0))[:, :, None] * np.ones((1, 1, LANES))
    dk = np.exp(lg[:, None] * (CHUNK - 1.0 - idx[None, :]))[:, :, None] * np.ones((1, 1, LANES))
    return np.concatenate([dmat, dq, dk], axis=0).astype(np.float32)


def _band_bias():
    i = np.arange(CHUNK)[:, None]
    j = np.arange(2 * CHUNK)[None, :]
    valid = (j >= i) & (j <= i + WINDOW)
    neg = np.float32(-1e30)
    b0 = np.where(valid, 0.0, neg)
    b1 = np.where(valid & (j >= CHUNK), 0.0, neg)
    return np.stack([b0, b1], axis=0).astype(np.float32)


def _ada_kernel(cs_ref, cp_ref, w1_ref, w2_ref, b1_ref, b2_ref, o_ref):
    rows = cs_ref.shape[0] + cp_ref.shape[0]
    pad = jnp.zeros((-rows % 16, D_MODEL), F32)
    a = _silu(jnp.concatenate([cs_ref[...], cp_ref[...], pad], axis=0)).astype(BF16)

    def emit(w_ref, b_ref):
        o_ref[...] = _dot(a, w_ref[...].astype(BF16))[0:rows] + b_ref[...]

    @pl.when(pl.program_id(0) < 3)
    def _():
        emit(w1_ref, b1_ref)

    @pl.when(pl.program_id(0) >= 3)
    def _():
        emit(w2_ref, b2_ref)


def _ada_mod(c_sample, c_prompt, w_mix, w_ffn, b_mix, b_ffn):
    rows = c_sample.shape[0] + c_prompt.shape[0]
    d = D_MODEL
    return pl.pallas_call(
        _ada_kernel,
        out_shape=jax.ShapeDtypeStruct((6, rows, d), F32),
        grid=(6,),
        in_specs=[
            pl.BlockSpec(c_sample.shape, lambda j: (0, 0)),
            pl.BlockSpec(c_prompt.shape, lambda j: (0, 0)),
            pl.BlockSpec((d, d), lambda j: (0, jnp.minimum(j, 2))),
            pl.BlockSpec((d, d), lambda j: (0, jnp.maximum(j - 3, 0))),
            pl.BlockSpec((1, d), lambda j: (0, jnp.minimum(j, 2))),
            pl.BlockSpec((1, d), lambda j: (0, jnp.maximum(j - 3, 0))),
        ],
        out_specs=pl.BlockSpec((None, rows, d), lambda j: (j, 0, 0)),
        compiler_params=pltpu.CompilerParams(
            dimension_semantics=("arbitrary",), vmem_limit_bytes=VMEM_LIMIT_BYTES),
        name="ada_mod",
    )(c_sample, c_prompt, w_mix, w_ffn, b_mix, b_ffn)


def _mixer_step(z_prev, z_cur, sinks_ref, x_ref, mods_ref, tab_ref, xp_ref, dec_ref, bias_ref,
                wq_ref, wrest_ref, wout_att_ref, wout_ret_ref, gnw_ref,
                y_ref, kout_ref, vout_ref, s_ref, kcarry, vcarry, mix_s,
                *, cur_b, prev_b, seq_start, tile, gl):
    n_chunks = tile // CHUNK
    x = x_ref[...]
    h = (x * (1.0 + mods_ref[1, pl.ds(cur_b, 1), :]) + mods_ref[0, pl.ds(cur_b, 1), :]).astype(BF16)
    n_fill = IN_WIDTH // FILL_WIDTH
    pending = list(range(n_fill))

    def fill(count):
        for _ in range(count):
            if pending:
                i = pending.pop(0)
                col = i * FILL_WIDTH
                w = (wq_ref[:, col:col + FILL_WIDTH] if col < ATT_WIDTH
                     else wrest_ref[:, col - ATT_WIDTH:col - ATT_WIDTH + FILL_WIDTH])
                z_cur[:, col:col + FILL_WIDTH] = _dot(h, w)

    gate = mods_ref[2, pl.ds(prev_b, 1), :]
    lo = jax.lax.broadcasted_iota(jnp.int32, (CHUNK, LANES), 1) < HEAD_DIM
    zero = jnp.zeros((CHUNK, LANES), F32)
    k_prev = [kcarry[g] for g in range(ATT_KV_HEADS)]
    v_prev = [vcarry[g] for g in range(ATT_KV_HEADS)]

    for c in range(n_chunks):
        rows = slice(c * CHUNK, (c + 1) * CHUNK)
        bias = bias_ref[seq_start] if c == 0 else bias_ref[0]
        att_c, att_s1, att_s2 = tab_ref[0, rows, :], tab_ref[1, rows, :], tab_ref[2, rows, :]
        ret_c, ret_s = tab_ref[3, rows, :], tab_ref[4, rows, :]

        def zs(col):
            return z_prev[rows, col:col + LANES]

        q_stack = jnp.concatenate(
            [(_rope_att(zs(OFF_Q + j * LANES), att_c, att_s1, att_s2) * (HEAD_DIM ** -0.5)).astype(BF16)
             for j in range(4)], axis=0)
        k = _rope_att(zs(OFF_KV), att_c, att_s1, att_s2)
        v = zs(OFF_KV + LANES)
        k_cur = [jnp.where(lo, k, zero).astype(BF16), jnp.where(lo, zero, k).astype(BF16)]
        v_cur = [jnp.where(lo, v, zero).astype(BF16), jnp.where(lo, zero, v).astype(BF16)]
        k_band = [jnp.concatenate([k_prev[g], k_cur[g]], axis=0) for g in range(ATT_KV_HEADS)]
        v_band = [jnp.concatenate([v_prev[g], v_cur[g]], axis=0) for g in range(ATT_KV_HEADS)]
        k_prev, v_prev = k_cur, v_cur
        if c == n_chunks - 1:
            kout_ref[...] = k
            vout_ref[...] = v
            for g in range(ATT_KV_HEADS):
                kcarry[g] = k_cur[g]
                vcarry[g] = v_cur[g]

        ret = []
        for hh in range(RET_HEADS):
            rq = _rope_ret(zs(OFF_RQ + hh * LANES), ret_c, ret_s)
            rk = _rope_ret(zs(OFF_RK + hh * LANES), ret_c, ret_s) * (RET_DK ** -0.5)
            ret.append(dict(
                q=rq.astype(BF16), qd=(rq * dec_ref[RET_HEADS + hh]).astype(BF16),
                k=rk.astype(BF16), kd=(rk * dec_ref[2 * RET_HEADS + hh]).astype(BF16),
                v=zs(OFF_RV + hh * LANES).astype(BF16)))

        sc_att = [_dot_nt(q_stack, k_band[g]) for g in range(ATT_KV_HEADS)]
        sc_ret = [_dot_nt(r["q"], r["k"]) for r in ret]
        fill(FILL_PLAN[c][0])

        att = None
        for g in range(ATT_KV_HEADS):
            ws = []
            for j in range(4):
                sj = sc_att[g][j * CHUNK:(j + 1) * CHUNK, :] + bias
                sink = sinks_ref[g * ATT_GROUP + j]
                mx = jnp.maximum(jnp.max(sj, axis=-1, keepdims=True), sink)
                p = jnp.exp(sj - mx)
                den = jnp.sum(p, axis=-1, keepdims=True) + jnp.exp(sink - mx)
                ws.append((p * (1.0 / den)).astype(BF16))
            o = _dot(jnp.concatenate(ws, axis=0), v_band[g])
            att = o if att is None else att + o
        for j in range(4):
            mix_s[rows, j * LANES:(j + 1) * LANES] = att[j * CHUNK:(j + 1) * CHUNK, :].astype(BF16)

        outs = []
        for hh, r in enumerate(ret):
            state = s_ref[hh]
            sc = sc_ret[hh] * dec_ref[hh]
            outs.append(_dot(sc.astype(BF16), r["v"]) + _dot(r["qd"], state.astype(BF16)))
            s_ref[hh] = gl[hh] * state + _dot_tn(r["kd"], r["v"])
        fill(FILL_PLAN[c][1])
        for hh in range(RET_HEADS):
            cols = slice(hh * LANES, (hh + 1) * LANES)
            o = outs[hh]
            mu = jnp.mean(o, axis=-1, keepdims=True)
            d = o - mu
            var = jnp.mean(d * d, axis=-1, keepdims=True)
            nrm = d * jax.lax.rsqrt(var + GN_EPS) * gnw_ref[:, cols]
            mix_s[rows, ATT_WIDTH + hh * LANES:ATT_WIDTH + (hh + 1) * LANES] = (
                _silu(zs(OFF_RG + hh * LANES)) * nrm).astype(BF16)

    y = _dot(mix_s[...], jnp.concatenate([wout_att_ref[...], wout_ret_ref[...]], axis=0))
    fill(n_fill)
    y_ref[...] = DEEPNORM_ALPHA * xp_ref[...] + (1.0 + gate) * y


def _mixer_kernel(sinks_ref, x_ref, mods_ref, tab_ref, xp_ref, dec_ref, bias_ref,
                  wq_ref, wrest_ref, wout_att_ref, wout_ret_ref, gnw_ref,
                  y_ref, kout_ref, vout_ref, s_ref,
                  z_a, z_b, kcarry, vcarry, mix_s, *, tile, ns, nt, gl):
    t = pl.program_id(0)
    prev_tile = jnp.maximum(t - 1, 0)
    sa = jax.lax.rem(prev_tile, ns)
    seq_start = (sa == 0).astype(jnp.int32)
    cur_b = jax.lax.div(jnp.minimum(t, nt - 1), ns)
    prev_b = jax.lax.div(prev_tile, ns)
    parity = jax.lax.rem(t, 2)

    @pl.when(t == 0)
    def _():
        z_b[...] = jnp.zeros(z_b.shape, z_b.dtype)
        kcarry[...] = jnp.zeros(kcarry.shape, kcarry.dtype)
        vcarry[...] = jnp.zeros(vcarry.shape, vcarry.dtype)

    @pl.when(sa == 0)
    def _():
        s_ref[...] = jnp.zeros_like(s_ref)

    step = functools.partial(
        _mixer_step, sinks_ref=sinks_ref, x_ref=x_ref, mods_ref=mods_ref, tab_ref=tab_ref, xp_ref=xp_ref,
        dec_ref=dec_ref, bias_ref=bias_ref, wq_ref=wq_ref, wrest_ref=wrest_ref,
        wout_att_ref=wout_att_ref, wout_ret_ref=wout_ret_ref,
        gnw_ref=gnw_ref, y_ref=y_ref, kout_ref=kout_ref,
        vout_ref=vout_ref, s_ref=s_ref, kcarry=kcarry, vcarry=vcarry, mix_s=mix_s,
        cur_b=cur_b, prev_b=prev_b, seq_start=seq_start, tile=tile, gl=gl)

    @pl.when(parity == 0)
    def _():
        step(z_b, z_a)

    @pl.when(parity == 1)
    def _():
        step(z_a, z_b)


def _mixer_prompt(x, mods, mod_row0, sinks, tab, dec, bias, w_q, w_rest, w_out_att, w_out_ret, gn_w):
    batch, seq, _ = x.shape
    assert mod_row0 % batch == 0 and batch % 8 == 0
    tile = PROMPT_TILE
    ns = seq // tile
    gl = tuple(float(v) for v in np.exp(_log_gamma() * CHUNK).astype(np.float32))
    nt = batch * ns
    const = lambda shape: pl.BlockSpec(shape, lambda t, *_: (0,) * len(shape),
                                       pipeline_mode=pl.Buffered(1))
    cur_b = lambda t: jnp.minimum(t, nt - 1) // ns
    cur_s = lambda t: jnp.minimum(t, nt - 1) % ns
    prev_b = lambda t: jnp.maximum(t - 1, 0) // ns
    prev_s = lambda t: jnp.maximum(t - 1, 0) % ns
    grid_spec = pltpu.PrefetchScalarGridSpec(
        num_scalar_prefetch=1,
        grid=(nt + 1,),
        in_specs=[
            pl.BlockSpec((None, tile, D_MODEL), lambda t, *_: (cur_b(t), cur_s(t), 0)),
            pl.BlockSpec((3, batch, D_MODEL), lambda t, *_: (0, mod_row0 // batch, 0),
                         pipeline_mode=pl.Buffered(1)),
            pl.BlockSpec((5, tile, LANES), lambda t, *_: (0, prev_s(t), 0)),
            pl.BlockSpec((None, tile, D_MODEL), lambda t, *_: (prev_b(t), prev_s(t), 0)),
            const((3 * RET_HEADS, CHUNK, LANES)),
            const((2, CHUNK, 2 * CHUNK)),
            const((D_MODEL, ATT_WIDTH)),
            const((D_MODEL, IN_WIDTH - ATT_WIDTH)),
            const((ATT_WIDTH, D_MODEL)),
            const((RET_WIDTH, D_MODEL)),
            const((1, RET_WIDTH)),
        ],
        out_specs=[
            pl.BlockSpec((None, tile, D_MODEL), lambda t, *_: (prev_b(t), prev_s(t), 0)),
            pl.BlockSpec((None, WINDOW, LANES), lambda t, *_: (prev_b(t), 0, 0)),
            pl.BlockSpec((None, WINDOW, LANES), lambda t, *_: (prev_b(t), 0, 0)),
            pl.BlockSpec((None, RET_HEADS, RET_DK, LANES), lambda t, *_: (prev_b(t), 0, 0, 0)),
        ],
        scratch_shapes=[
            pltpu.VMEM((tile, IN_WIDTH), F32),
            pltpu.VMEM((tile, IN_WIDTH), F32),
            pltpu.VMEM((ATT_KV_HEADS, CHUNK, LANES), BF16),
            pltpu.VMEM((ATT_KV_HEADS, CHUNK, LANES), BF16),
            pltpu.VMEM((tile, D_MODEL), BF16),
        ],
    )
    return pl.pallas_call(
        functools.partial(_mixer_kernel, tile=tile, ns=ns, nt=nt, gl=gl),
        out_shape=[
            jax.ShapeDtypeStruct((batch, seq, D_MODEL), F32),
            jax.ShapeDtypeStruct((batch, WINDOW, LANES), F32),
            jax.ShapeDtypeStruct((batch, WINDOW, LANES), F32),
            jax.ShapeDtypeStruct((batch, RET_HEADS, RET_DK, LANES), F32),
        ],
        grid_spec=grid_spec,
        compiler_params=pltpu.CompilerParams(
            dimension_semantics=("arbitrary",), vmem_limit_bytes=VMEM_LIMIT_BYTES),
        name="mixer_prompt",
    )(sinks, x, mods, tab, x, dec, bias, w_q, w_rest, w_out_att, w_out_ret, gn_w)


def _ffn_rows(r_ref, o_ref, mod, parts, wup_ref, wdown_ref, ln1w_ref, ln1b_ref, lnw_ref, lnb_ref):
    rows = r_ref.shape[0] // parts
    sl = [slice(p * rows, (p + 1) * rows) for p in range(parts)]
    ys = [_layer_norm(r_ref[sl[p], :], ln1w_ref[...], ln1b_ref[...]) for p in range(parts)]
    hs = [(ys[p] * (1.0 + mod(1, sl[p])) + mod(0, sl[p])).astype(BF16) for p in range(parts)]
    def act(h):
        return jnp.concatenate(
            [(_silu(_dot(h, wup_ref[:, j:j + FFN_COLS]))
              * _dot(h, wup_ref[:, D_FF + j:D_FF + j + FFN_COLS])).astype(BF16)
             for j in range(0, D_FF, FFN_COLS)], axis=1)

    acts = [act(h) for h in hs]
    fs = [_dot(a, wdown_ref[...]) for a in acts]
    for p in range(parts):
        r = DEEPNORM_ALPHA * ys[p] + (1.0 + mod(2, sl[p])) * fs[p]
        o_ref[sl[p], :] = _layer_norm(r, lnw_ref[...], lnb_ref[...])


def _ffn_kernel(r_ref, mods_ref, rs_ref, mods_s_ref, wup_ref, wdown_ref, ln1w_ref, ln1b_ref, lnw_ref,
                lnb_ref, o_ref, os_ref, *, parts, tiles_per_mod_row):
    weights = (wup_ref, wdown_ref, ln1w_ref, ln1b_ref, lnw_ref, lnb_ref)
    seq = jax.lax.div(pl.program_id(0), tiles_per_mod_row)
    _ffn_rows(r_ref, o_ref, lambda k, rows: mods_ref[k, pl.ds(seq, 1), :], parts, *weights)

    @pl.when(pl.program_id(0) == pl.num_programs(0) - 1)
    def _():
        _ffn_rows(rs_ref, os_ref, lambda k, rows: mods_s_ref[k, rows, :], 1, *weights)


def _ffn(y, y_s, mods, tiles_per_mod_row, w_up, w_down, ln1_w, ln1_b, ln_w, ln_b, tile, parts):
    rows, _ = y.shape
    n_s = y_s.shape[0]
    batch = mods.shape[1] - n_s
    nt = rows // tile
    assert n_s % batch == 0 and nt == batch * tiles_per_mod_row
    const = lambda shape, idx: pl.BlockSpec(shape, lambda t: idx, pipeline_mode=pl.Buffered(1))
    return pl.pallas_call(
        functools.partial(_ffn_kernel, parts=parts, tiles_per_mod_row=tiles_per_mod_row),
        out_shape=(jax.ShapeDtypeStruct(y.shape, F32), jax.ShapeDtypeStruct(y_s.shape, F32)),
        grid=(nt,),
        in_specs=[
            pl.BlockSpec((tile, D_MODEL), lambda t: (t, 0)),
            const((3, batch, D_MODEL), (1, n_s // batch, 0)),
            const((n_s, D_MODEL), (0, 0)),
            const((3, n_s, D_MODEL), (1, 0, 0)),
            const((D_MODEL, 2 * D_FF), (0, 0)),
            const((D_FF, D_MODEL), (0, 0)),
            const((1, D_MODEL), (0, 0)),
            const((1, D_MODEL), (0, 0)),
            const((1, D_MODEL), (0, 0)),
            const((1, D_MODEL), (0, 0)),
        ],
        out_specs=(pl.BlockSpec((tile, D_MODEL), lambda t: (t, 0)),
                   pl.BlockSpec((n_s, D_MODEL), lambda t: (0, 0))),
        compiler_params=pltpu.CompilerParams(
            dimension_semantics=("arbitrary",), vmem_limit_bytes=VMEM_LIMIT_BYTES),
        name="ffn",
    )(y, mods, y_s, mods, w_up, w_down, ln1_w, ln1_b, ln_w, ln_b)


def _sample_kernel(sinks_ref, x_ref, mod_ref, tab_ref, wq_ref, wrest_ref, wout_att_ref, wout_ret_ref,
                   gnw_ref,
                   ck_hbm, cv_hbm, st_hbm,
                   y_ref, ko_ref, vo_ref, so_ref,
                   z_s, mix_s, ck_buf, cv_buf, st_buf, ring_sem, *, g1, n_blocks):
    i = pl.program_id(0)
    bb = SAMPLE_BLOCK
    streams = ((ck_hbm, ck_buf), (cv_hbm, cv_buf), (st_hbm, st_buf))

    def ring_copies(blk, slot):
        return [pltpu.make_async_copy(hbm.at[pl.ds(blk * bb, bb)], buf.at[slot], ring_sem.at[k, slot])
                for k, (hbm, buf) in enumerate(streams)]

    @pl.when(i == 0)
    def _():
        for s in range(SAMPLE_RING - 1):
            for c in ring_copies(s, s):
                c.start()

    @pl.when(i + (SAMPLE_RING - 1) < n_blocks)
    def _():
        ahead = i + (SAMPLE_RING - 1)
        for c in ring_copies(ahead, jax.lax.rem(ahead, SAMPLE_RING)):
            c.start()

    @pl.when(i == 0)
    def _():
        m = mod_ref[...]
        h = (x_ref[...] * (1.0 + m[1]) + m[0]).astype(BF16)
        z = jnp.concatenate([_dot(h, wq_ref[...]), _dot(h, wrest_ref[...])], axis=1)
        att_c, att_s1, att_s2 = tab_ref[0], tab_ref[1], tab_ref[2]
        ret_c, ret_s = tab_ref[3], tab_ref[4]
        for j in range(4):
            col = OFF_Q + j * LANES
            z_s[:, col:col + LANES] = _rope_att(z[:, col:col + LANES], att_c, att_s1, att_s2) * (HEAD_DIM ** -0.5)
            col = OFF_RQ + j * LANES
            z_s[:, col:col + LANES] = _rope_ret(z[:, col:col + LANES], ret_c, ret_s)
            col = OFF_RK + j * LANES
            z_s[:, col:col + LANES] = _rope_ret(z[:, col:col + LANES], ret_c, ret_s) * (RET_DK ** -0.5)
        z_s[:, OFF_KV:OFF_KV + LANES] = _rope_att(z[:, OFF_KV:OFF_KV + LANES], att_c, att_s1, att_s2)
        z_s[:, OFF_KV + LANES:OFF_RQ] = z[:, OFF_KV + LANES:OFF_RQ]
        z_s[:, OFF_RV:OFF_RG] = z[:, OFF_RV:OFF_RG]
        z_s[:, OFF_RG:] = _silu(z[:, OFF_RG:])

    lane_w = jax.lax.broadcasted_iota(jnp.int32, (LANES, WINDOW), 1)
    last_lane = lane_w == WINDOW - 1
    lo_row = lane_w[0:1, :] < HEAD_DIM
    r4 = jax.lax.broadcasted_iota(jnp.int32, (ATT_GROUP, 1), 0)
    r8 = jax.lax.broadcasted_iota(jnp.int32, (ATT_HEADS, 1), 0)
    sink8 = jnp.zeros((ATT_HEADS, 1), F32)
    for hd in range(ATT_HEADS):
        sink8 = jnp.where(r8 == hd, sinks_ref[hd], sink8)
    head_blk = (jax.lax.broadcasted_iota(jnp.int32, (16, RET_WIDTH), 1) // LANES
                == jax.lax.broadcasted_iota(jnp.int32, (16, RET_WIDTH), 0))
    g1_col = jnp.where(r4 == 0, g1[0], jnp.where(r4 == 1, g1[1], jnp.where(r4 == 2, g1[2], g1[3])))

    def rows16(a):
        return jnp.concatenate([a, jnp.zeros((16 - a.shape[0], a.shape[1]), F32)], axis=0)

    def heads(row, off):
        return jnp.concatenate([row[:, off + j * LANES:off + (j + 1) * LANES] for j in range(4)], axis=0)

    def block_rows(a4):
        wide = jnp.concatenate([rows16(a4)] * RET_HEADS, axis=1)
        return jnp.where(head_blk, wide, jnp.zeros_like(wide))

    slot = jax.lax.rem(i, SAMPLE_RING)
    for c in ring_copies(i, slot):
        c.wait()
    ck_ref, cv_ref, st_ref = ck_buf.at[slot], cv_buf.at[slot], st_buf.at[slot]

    r0 = pl.multiple_of(i * bb, bb)
    zblk = z_s[pl.ds(r0, bb), :]
    rows = [zblk[b:b + 1, :] for b in range(bb)]

    kn = [r[:, OFF_KV:OFF_KV + LANES] for r in rows]
    vn = [r[:, OFF_KV + LANES:OFF_RQ] for r in rows]
    kn_t = zblk[:, OFF_KV:OFF_KV + LANES].T
    vn_t = zblk[:, OFF_KV + LANES:OFF_RQ].T
    q8 = []
    for r in rows:
        q4 = heads(r, OFF_Q)
        zero4 = jnp.zeros_like(q4)
        q8.append(jnp.concatenate([jnp.where(lo_row, q4, zero4), jnp.where(lo_row, zero4, q4)], axis=0))
    sc, oh = [], []
    for b in range(bb):
        sc.append(_dot(rows16(q8[b]).astype(BF16), ck_ref[b].astype(BF16))[0:ATT_HEADS])
    for b in range(bb):
        rq = heads(rows[b], OFF_RQ)
        rk = heads(rows[b], OFF_RK)
        rv = heads(rows[b], OFF_RV)
        state = st_ref[b].reshape(RET_WIDTH, LANES)
        o2 = _dot(block_rows(rq * g1_col).astype(BF16), state.astype(BF16))[0:RET_HEADS]
        outer = _dot_tn(block_rows(rk).astype(BF16), rows16(rv).astype(BF16))
        for hh in range(RET_HEADS):
            so_ref[b, hh] = g1[hh] * st_ref[b, hh] + outer[hh * RET_DK:(hh + 1) * RET_DK, :]
        o = jnp.sum(rq * rk, axis=-1, keepdims=True) * rv + o2
        mu = jnp.mean(o, axis=-1, keepdims=True)
        d = o - mu
        var = jnp.mean(d * d, axis=-1, keepdims=True)
        oh.append(heads(rows[b], OFF_RG) * (d * jax.lax.rsqrt(var + GN_EPS) * gnw_ref[...]))

    ws, wn = [], []
    for b in range(bb):
        sn = jnp.sum(q8[b] * kn[b], axis=-1, keepdims=True)
        mx = jnp.maximum(jnp.maximum(jnp.max(sc[b], axis=-1, keepdims=True), sn), sink8)
        p = jnp.exp(sc[b] - mx)
        pn = jnp.exp(sn - mx)
        inv = 1.0 / (jnp.sum(p, axis=-1, keepdims=True) + pn + jnp.exp(sink8 - mx))
        ws.append(rows16(p * inv).astype(BF16))
        wn.append(pn * inv)

    att = []
    for b in range(bb):
        o8 = _dot_nt(ws[b], cv_ref[b].astype(BF16))[0:ATT_HEADS] + wn[b] * vn[b]
        att.append(jnp.where(lo_row, o8[0:ATT_GROUP], o8[ATT_GROUP:]))

    for b in range(bb):
        ko_ref[b] = jnp.where(last_lane, kn_t[:, b:b + 1], pltpu.roll(ck_ref[b], WINDOW - 1, axis=1))
        vo_ref[b] = jnp.where(last_lane, vn_t[:, b:b + 1], pltpu.roll(cv_ref[b], WINDOW - 1, axis=1))
    blk_rows = pl.ds(r0, bb)
    for j in range(4):
        mix_s[blk_rows, j * LANES:(j + 1) * LANES] = jnp.concatenate(
            [a[j:j + 1, :] for a in att], axis=0)
        mix_s[blk_rows, ATT_WIDTH + j * LANES:ATT_WIDTH + (j + 1) * LANES] = jnp.concatenate(
            [o[j:j + 1, :] for o in oh], axis=0)

    @pl.when(i == n_blocks - 1)
    def _():
        y = (_dot(mix_s[:, :ATT_WIDTH].astype(BF16), wout_att_ref[...])
             + _dot(mix_s[:, ATT_WIDTH:].astype(BF16), wout_ret_ref[...]))
        y_ref[...] = DEEPNORM_ALPHA * x_ref[...] + (1.0 + mod_ref[2]) * y


def _sample_step(sinks, x, mods, tab, w_q, w_rest, w_out_att, w_out_ret, gn_w4,
                 cache_k, cache_v, state):
    n = x.shape[0]
    bb = SAMPLE_BLOCK
    n_blocks = n // bb
    assert n_blocks >= SAMPLE_RING - 1
    g1 = tuple(float(v) for v in np.exp(_log_gamma()).astype(np.float32))
    full = lambda *shape: pl.BlockSpec(shape, lambda i, *_: (0,) * len(shape))
    blk = lambda *shape: pl.BlockSpec((bb,) + shape, lambda i, *_: (i,) + (0,) * len(shape))
    grid_spec = pltpu.PrefetchScalarGridSpec(
        num_scalar_prefetch=1,
        grid=(n_blocks,),
        in_specs=[
            full(n, D_MODEL), full(3, n, D_MODEL), full(5, 1, LANES),
            full(D_MODEL, ATT_WIDTH), full(D_MODEL, IN_WIDTH - ATT_WIDTH),
            full(ATT_WIDTH, D_MODEL), full(RET_WIDTH, D_MODEL),
            full(RET_HEADS, LANES),
            pl.BlockSpec(memory_space=pl.ANY), pl.BlockSpec(memory_space=pl.ANY),
            pl.BlockSpec(memory_space=pl.ANY),
        ],
        out_specs=[full(n, D_MODEL), blk(WINDOW, LANES), blk(WINDOW, LANES),
                   blk(RET_HEADS, RET_DK, LANES)],
        scratch_shapes=[pltpu.VMEM((n, IN_WIDTH), F32), pltpu.VMEM((n, D_MODEL), F32),
                        pltpu.VMEM((SAMPLE_RING, bb, WINDOW, LANES), F32),
                        pltpu.VMEM((SAMPLE_RING, bb, WINDOW, LANES), F32),
                        pltpu.VMEM((SAMPLE_RING, bb, RET_HEADS, RET_DK, LANES), F32),
                        pltpu.SemaphoreType.DMA((3, SAMPLE_RING))],
    )
    return pl.pallas_call(
        functools.partial(_sample_kernel, g1=g1, n_blocks=n_blocks),
        out_shape=[
            jax.ShapeDtypeStruct((n, D_MODEL), F32),
            jax.ShapeDtypeStruct((n, WINDOW, LANES), F32),
            jax.ShapeDtypeStruct((n, WINDOW, LANES), F32),
            jax.ShapeDtypeStruct((n, RET_HEADS, RET_DK, LANES), F32),
        ],
        grid_spec=grid_spec,
        compiler_params=pltpu.CompilerParams(
            dimension_semantics=("arbitrary",), vmem_limit_bytes=VMEM_LIMIT_BYTES),
        name="sample_step",
    )(sinks, x, mods, tab, w_q, w_rest, w_out_att, w_out_ret, gn_w4, cache_k, cache_v, state)


def _slab_order(w, axis):
    shape = w.shape
    pre, post = shape[:axis], shape[axis + 1:]
    w = w.reshape(pre + (ATT_KV_HEADS, ATT_GROUP, HEAD_DIM) + post)
    w = jnp.swapaxes(w, axis, axis + 1)
    return w.reshape(shape)


def kernel(x_prompt, x_sample, c_prompt, c_sample, cache_k, cache_v, state_ret, w_ada_mix, b_ada_mix, w_in, att_sinks, ret_gn_w, w_out, ln1_w, ln1_b, w_ada_ffn, b_ada_ffn, w_up, w_down, ln2_w, ln2_b):
    batch, seq, _ = x_prompt.shape
    n_s = x_sample.shape[0]
    l = 0
    assert w_in.shape[0] == DEPTH == 1

    w_in_f = w_in.reshape(D_MODEL, IN_WIDTH)
    w_out_f = w_out.reshape(D_MODEL, D_MODEL)
    w_q = _slab_order(w_in_f[:, :ATT_WIDTH].astype(BF16), 1)
    w_rest = w_in_f[:, ATT_WIDTH:].astype(BF16)
    w_out_att = _slab_order(w_out_f[:ATT_WIDTH].astype(BF16), 0)
    w_out_ret = w_out_f[ATT_WIDTH:].astype(BF16)
    w_up_b = w_up.reshape(D_MODEL, 2 * D_FF).astype(BF16)
    w_down_b = w_down.reshape(D_FF, D_MODEL).astype(BF16)
    sinks = att_sinks[l]
    gn_w = ret_gn_w[l].reshape(1, RET_WIDTH)
    ln1w, ln1b = ln1_w[l].reshape(1, D_MODEL), ln1_b[l].reshape(1, D_MODEL)
    ln2w, ln2b = ln2_w[l].reshape(1, D_MODEL), ln2_b[l].reshape(1, D_MODEL)

    tab_p = jnp.asarray(_rope_tables(np.arange(seq)))
    tab_s = jnp.asarray(_rope_tables(np.array([PAST_LEN])))
    dec = jnp.asarray(_decay_tables())
    bias = jnp.asarray(_band_bias())

    mods = _ada_mod(c_sample, c_prompt, w_ada_mix.reshape(D_MODEL, 3 * D_MODEL),
                    w_ada_ffn.reshape(D_MODEL, 3 * D_MODEL), b_ada_mix, b_ada_ffn)

    y1p, kp, vp, sp = _mixer_prompt(x_prompt, mods, n_s, sinks, tab_p, dec, bias, w_q, w_rest,
                                    w_out_att, w_out_ret, gn_w)

    xs = x_sample.reshape(n_s, D_MODEL)
    y1s, ks, vs, ss = _sample_step(
        sinks, xs, mods, tab_s, w_q, w_rest, w_out_att, w_out_ret, ret_gn_w.reshape(RET_HEADS, LANES),
        jnp.swapaxes(cache_k.reshape(n_s, WINDOW, LANES), 1, 2),
        jnp.swapaxes(cache_v.reshape(n_s, WINDOW, LANES), 1, 2),
        state_ret.reshape(n_s, RET_HEADS, RET_DK, LANES))
    ks, vs = jnp.swapaxes(ks, 1, 2), jnp.swapaxes(vs, 1, 2)

    yp, ys = _ffn(y1p.reshape(batch * seq, D_MODEL), y1s, mods, seq // FFN_TILE, w_up_b, w_down_b,
                  ln1w, ln1b, ln2w, ln2b, FFN_TILE, FFN_PARTS)
    yp = yp.reshape(batch, seq, D_MODEL)

    kv_shape = (1, batch, WINDOW, ATT_KV_HEADS, HEAD_DIM)
    kvs_shape = (1, n_s, WINDOW, ATT_KV_HEADS, HEAD_DIM)
    return (yp, ys.reshape(n_s, 1, D_MODEL),
            kp.reshape(kv_shape), vp.reshape(kv_shape), sp[None],
            ks.reshape(kvs_shape), vs.reshape(kvs_shape), ss[None])
```

```python
import functools
import math

import numpy as np
import jax
import jax.numpy as jnp
from jax.experimental import pallas as pl
from jax.experimental.pallas import tpu as pltpu

D_MODEL = 1024
WINDOW = 128
CHUNK = 128
ATT_HEADS = 8
ATT_KV_HEADS = 2
ATT_GROUP = ATT_HEADS // ATT_KV_HEADS
HEAD_DIM = 64
ROPE_DIMS = HEAD_DIM // 4
ROPE_THETA = 500000.0
RET_HEADS = 4
RET_DK = 128
RET_THETA = 10000.0
ATT_WIDTH = ATT_HEADS * HEAD_DIM
KV_WIDTH = ATT_KV_HEADS * HEAD_DIM
RET_WIDTH = RET_HEADS * RET_DK
IN_WIDTH = ATT_WIDTH + 2 * KV_WIDTH + 4 * RET_WIDTH
D_FF = 2816
DEPTH = 1
DEEPNORM_ALPHA = (2 * DEPTH) ** 0.25
LN_EPS = 1e-5
GN_EPS = 1e-6
PAST_LEN = 16384

OFF_Q = 0
OFF_KV = ATT_WIDTH
OFF_RQ = OFF_KV + 2 * KV_WIDTH
OFF_RK = OFF_RQ + RET_WIDTH
OFF_RV = OFF_RK + RET_WIDTH
OFF_RG = OFF_RV + RET_WIDTH

LANES = 128
VMEM_LIMIT_BYTES = 56 * 1024 * 1024

PROMPT_TILE = 512
FILL_WIDTH = 256
FILL_PLAN = ((2, 1), (2, 1), (2, 1), (2, 0))
FFN_TILE = 1024
FFN_PARTS = 4
FFN_COLS = 256
SAMPLE_BLOCK = 16
SAMPLE_RING = 3

BF16 = jnp.bfloat16
F32 = jnp.float32


def _dot(a, b):
    return jnp.dot(a, b, preferred_element_type=F32)


def _dot_nt(a, b):
    return jax.lax.dot_general(a, b, (((1,), (1,)), ((), ())), preferred_element_type=F32)


def _dot_tn(a, b):
    return jax.lax.dot_general(a, b, (((0,), (0,)), ((), ())), preferred_element_type=F32)


def _sigmoid(x):
    return 1.0 / (1.0 + jnp.exp(-x))


def _silu(x):
    return x * _sigmoid(x)


def _layer_norm(r, w, b):
    mu = jnp.mean(r, axis=-1, keepdims=True)
    d = r - mu
    var = jnp.mean(d * d, axis=-1, keepdims=True)
    return d * jax.lax.rsqrt(var + LN_EPS) * w + b


def _rope_att(slab, c, s1, s2):
    return slab * c + pltpu.roll(slab, LANES - 8, axis=1) * s1 + pltpu.roll(slab, 8, axis=1) * s2


def _rope_ret(slab, c, s):
    return slab * c + pltpu.roll(slab, LANES // 2, axis=1) * s


def _rope_tables(pos):
    pos = np.asarray(pos, np.float64)[:, None]
    half = ROPE_DIMS // 2
    inv = ROPE_THETA ** (-np.arange(half, dtype=np.float64) / half)
    ang = pos * inv[None, :]
    cos, sin = np.cos(ang), np.sin(ang)
    n = pos.shape[0]
    head_c = np.ones((n, HEAD_DIM)); head_s1 = np.zeros((n, HEAD_DIM)); head_s2 = np.zeros((n, HEAD_DIM))
    head_c[:, :half] = cos; head_c[:, half:2 * half] = cos
    head_s1[:, :half] = -sin
    head_s2[:, half:2 * half] = sin
    att = [np.tile(t, (1, LANES // HEAD_DIM)) for t in (head_c, head_s1, head_s2)]
    rhalf = RET_DK // 2
    rinv = RET_THETA ** (-np.arange(rhalf, dtype=np.float64) / rhalf)
    rang = pos * rinv[None, :]
    rc = np.concatenate([np.cos(rang), np.cos(rang)], axis=1)
    rs = np.concatenate([-np.sin(rang), np.sin(rang)], axis=1)
    return np.stack(att + [rc, rs], axis=0).astype(np.float32)


def _log_gamma():
    lin = np.linspace(math.log(1.0 / 32), math.log(1.0 / 512), RET_HEADS)
    return np.log1p(-np.exp(lin))


def _decay_tables():
    lg = _log_gamma()
    idx = np.arange(CHUNK, dtype=np.float64)
    diff = idx[:, None] - idx[None, :]
    dmat = np.where(diff[None] >= 0, np.exp(lg[:, None, None] * np.maximum(diff, 0.0)[None]), 0.0)
    dq = np.exp(lg[:, None] * (idx[None, :] + 1.0))[:, :, None] * np.ones((1, 1, LANES))
    dk = np.exp(lg[:, None] * (CHUNK - 1.0 - idx[None, :]))[:, :, None] * np.ones((1, 1, LANES))
    return np.concatenate([dmat, dq, dk], axis=0).astype(np.float32)


def _band_bias():
    i = np.arange(CHUNK)[:, None]
    j = np.arange(2 * CHUNK)[None, :]
    valid = (j >= i) & (j <= i + WINDOW)
    neg = np.float32(-1e30)
    b0 = np.where(valid, 0.0, neg)
    b1 = np.where(valid & (j >= CHUNK), 0.0, neg)
    return np.stack([b0, b1], axis=0).astype(np.float32)


def _ada_kernel(cs_ref, cp_ref, w1_ref, w2_ref, b1_ref, b2_ref, o_ref):
    rows = cs_ref.shape[0] + cp_ref.shape[0]
    pad = jnp.zeros((-rows % 16, D_MODEL), F32)
    a = _silu(jnp.concatenate([cs_ref[...], cp_ref[...], pad], axis=0)).astype(BF16)

    def emit(w_ref, b_ref):
        o_ref[...] = _dot(a, w_ref[...].astype(BF16))[0:rows] + b_ref[...]

    @pl.when(pl.program_id(0) < 3)
    def _():
        emit(w1_ref, b1_ref)

    @pl.when(pl.program_id(0) >= 3)
    def _():
        emit(w2_ref, b2_ref)


def _ada_mod(c_sample, c_prompt, w_mix, w_ffn, b_mix, b_ffn):
    rows = c_sample.shape[0] + c_prompt.shape[0]
    d = D_MODEL
    return pl.pallas_call(
        _ada_kernel,
        out_shape=jax.ShapeDtypeStruct((6, rows, d), F32),
        grid=(6,),
        in_specs=[
            pl.BlockSpec(c_sample.shape, lambda j: (0, 0)),
            pl.BlockSpec(c_prompt.shape, lambda j: (0, 0)),
            pl.BlockSpec((d, d), lambda j: (0, jnp.minimum(j, 2))),
            pl.BlockSpec((d, d), lambda j: (0, jnp.maximum(j - 3, 0))),
            pl.BlockSpec((1, d), lambda j: (0, jnp.minimum(j, 2))),
            pl.BlockSpec((1, d), lambda j: (0, jnp.maximum(j - 3, 0))),
        ],
        out_specs=pl.BlockSpec((None, rows, d), lambda j: (j, 0, 0)),
        compiler_params=pltpu.CompilerParams(
            dimension_semantics=("arbitrary",), vmem_limit_bytes=VMEM_LIMIT_BYTES),
        name="ada_mod",
    )(c_sample, c_prompt, w_mix, w_ffn, b_mix, b_ffn)


def _mixer_step(z_prev, z_cur, sinks_ref, x_ref, mods_ref, tab_ref, xp_ref, dec_ref, bias_ref,
                wq_ref, wrest_ref, wout_att_ref, wout_ret_ref, gnw_ref,
                y_ref, kout_ref, vout_ref, s_ref, kcarry, vcarry, mix_s,
                *, cur_b, prev_b, seq_start, tile, gl):
    n_chunks = tile // CHUNK
    x = x_ref[...]
    h = (x * (1.0 + mods_ref[1, pl.ds(cur_b, 1), :]) + mods_ref[0, pl.ds(cur_b, 1), :]).astype(BF16)
    n_fill = IN_WIDTH // FILL_WIDTH
    pending = list(range(n_fill))

    def fill(count):
        for _ in range(count):
            if pending:
                i = pending.pop(0)
                col = i * FILL_WIDTH
                w = (wq_ref[:, col:col + FILL_WIDTH] if col < ATT_WIDTH
                     else wrest_ref[:, col - ATT_WIDTH:col - ATT_WIDTH + FILL_WIDTH])
                z_cur[:, col:col + FILL_WIDTH] = _dot(h, w)

    gate = mods_ref[2, pl.ds(prev_b, 1), :]
    lo = jax.lax.broadcasted_iota(jnp.int32, (CHUNK, LANES), 1) < HEAD_DIM
    zero = jnp.zeros((CHUNK, LANES), F32)
    k_prev = [kcarry[g] for g in range(ATT_KV_HEADS)]
    v_prev = [vcarry[g] for g in range(ATT_KV_HEADS)]

    for c in range(n_chunks):
        rows = slice(c * CHUNK, (c + 1) * CHUNK)
        bias = bias_ref[seq_start] if c == 0 else bias_ref[0]
        att_c, att_s1, att_s2 = tab_ref[0, rows, :], tab_ref[1, rows, :], tab_ref[2, rows, :]
        ret_c, ret_s = tab_ref[3, rows, :], tab_ref[4, rows, :]

        def zs(col):
            return z_prev[rows, col:col + LANES]

        q_stack = jnp.concatenate(
            [(_rope_att(zs(OFF_Q + j * LANES), att_c, att_s1, att_s2) * (HEAD_DIM ** -0.5)).astype(BF16)
             for j in range(4)], axis=0)
        k = _rope_att(zs(OFF_KV), att_c, att_s1, att_s2)
        v = zs(OFF_KV + LANES)
        k_cur = [jnp.where(lo, k, zero).astype(BF16), jnp.where(lo, zero, k).astype(BF16)]
        v_cur = [jnp.where(lo, v, zero).astype(BF16), jnp.where(lo, zero, v).astype(BF16)]
        k_band = [jnp.concatenate([k_prev[g], k_cur[g]], axis=0) for g in range(ATT_KV_HEADS)]
        v_band = [jnp.concatenate([v_prev[g], v_cur[g]], axis=0) for g in range(ATT_KV_HEADS)]
        k_prev, v_prev = k_cur, v_cur
        if c == n_chunks - 1:
            kout_ref[...] = k
            vout_ref[...] = v
            for g in range(ATT_KV_HEADS):
                kcarry[g] = k_cur[g]
                vcarry[g] = v_cur[g]

        ret = []
        for hh in range(RET_HEADS):
            rq = _rope_ret(zs(OFF_RQ + hh * LANES), ret_c, ret_s)
            rk = _rope_ret(zs(OFF_RK + hh * LANES), ret_c, ret_s) * (RET_DK ** -0.5)
            ret.append(dict(
                q=rq.astype(BF16), qd=(rq * dec_ref[RET_HEADS + hh]).astype(BF16),
                k=rk.astype(BF16), kd=(rk * dec_ref[2 * RET_HEADS + hh]).astype(BF16),
                v=zs(OFF_RV + hh * LANES).astype(BF16)))

        sc_att = [_dot_nt(q_stack, k_band[g]) for g in range(ATT_KV_HEADS)]
        sc_ret = [_dot_nt(r["q"], r["k"]) for r in ret]
        fill(FILL_PLAN[c][0])

        att = None
        for g in range(ATT_KV_HEADS):
            ws = []
            for j in range(4):
                sj = sc_att[g][j * CHUNK:(j + 1) * CHUNK, :] + bias
                sink = sinks_ref[g * ATT_GROUP + j]
                mx = jnp.maximum(jnp.max(sj, axis=-1, keepdims=True), sink)
                p = jnp.exp(sj - mx)
                den = jnp.sum(p, axis=-1, keepdims=True) + jnp.exp(sink - mx)
                ws.append((p * (1.0 / den)).astype(BF16))
            o = _dot(jnp.concatenate(ws, axis=0), v_band[g])
            att = o if att is None else att + o
        for j in range(4):
            mix_s[rows, j * LANES:(j + 1) * LANES] = att[j * CHUNK:(j + 1) * CHUNK, :].astype(BF16)

        outs = []
        for hh, r in enumerate(ret):
            state = s_ref[hh]
            sc = sc_ret[hh] * dec_ref[hh]
            outs.append(_dot(sc.astype(BF16), r["v"]) + _dot(r["qd"], state.astype(BF16)))
            s_ref[hh] = gl[hh] * state + _dot_tn(r["kd"], r["v"])
        fill(FILL_PLAN[c][1])
        for hh in range(RET_HEADS):
            cols = slice(hh * LANES, (hh + 1) * LANES)
            o = outs[hh]
            mu = jnp.mean(o, axis=-1, keepdims=True)
            d = o - mu
            var = jnp.mean(d * d, axis=-1, keepdims=True)
            nrm = d * jax.lax.rsqrt(var + GN_EPS) * gnw_ref[:, cols]
            mix_s[rows, ATT_WIDTH + hh * LANES:ATT_WIDTH + (hh + 1) * LANES] = (
                _silu(zs(OFF_RG + hh * LANES)) * nrm).astype(BF16)

    y = _dot(mix_s[...], jnp.concatenate([wout_att_ref[...], wout_ret_ref[...]], axis=0))
    fill(n_fill)
    y_ref[...] = DEEPNORM_ALPHA * xp_ref[...] + (1.0 + gate) * y


def _mixer_kernel(sinks_ref, x_ref, mods_ref, tab_ref, xp_ref, dec_ref, bias_ref,
                  wq_ref, wrest_ref, wout_att_ref, wout_ret_ref, gnw_ref,
                  y_ref, kout_ref, vout_ref, s_ref,
                  z_a, z_b, kcarry, vcarry, mix_s, *, tile, ns, nt, gl):
    t = pl.program_id(0)
    prev_tile = jnp.maximum(t - 1, 0)
    sa = jax.lax.rem(prev_tile, ns)
    seq_start = (sa == 0).astype(jnp.int32)
    cur_b = jax.lax.div(jnp.minimum(t, nt - 1), ns)
    prev_b = jax.lax.div(prev_tile, ns)
    parity = jax.lax.rem(t, 2)

    @pl.when(t == 0)
    def _():
        z_b[...] = jnp.zeros(z_b.shape, z_b.dtype)
        kcarry[...] = jnp.zeros(kcarry.shape, kcarry.dtype)
        vcarry[...] = jnp.zeros(vcarry.shape, vcarry.dtype)

    @pl.when(sa == 0)
    def _():
        s_ref[...] = jnp.zeros_like(s_ref)

    step = functools.partial(
        _mixer_step, sinks_ref=sinks_ref, x_ref=x_ref, mods_ref=mods_ref, tab_ref=tab_ref, xp_ref=xp_ref,
        dec_ref=dec_ref, bias_ref=bias_ref, wq_ref=wq_ref, wrest_ref=wrest_ref,
        wout_att_ref=wout_att_ref, wout_ret_ref=wout_ret_ref,
        gnw_ref=gnw_ref, y_ref=y_ref, kout_ref=kout_ref,
        vout_ref=vout_ref, s_ref=s_ref, kcarry=kcarry, vcarry=vcarry, mix_s=mix_s,
        cur_b=cur_b, prev_b=prev_b, seq_start=seq_start, tile=tile, gl=gl)

    @pl.when(parity == 0)
    def _():
        step(z_b, z_a)

    @pl.when(parity == 1)
    def _():
        step(z_a, z_b)


def _mixer_prompt(x, mods, mod_row0, sinks, tab, dec, bias, w_q, w_rest, w_out_att, w_out_ret, gn_w):
    batch, seq, _ = x.shape
    assert mod_row0 % batch == 0 and batch % 8 == 0
    tile = PROMPT_TILE
    ns = seq // tile
    gl = tuple(float(v) for v in np.exp(_log_gamma() * CHUNK).astype(np.float32))
    nt = batch * ns
    const = lambda shape: pl.BlockSpec(shape, lambda t, *_: (0,) * len(shape),
                                       pipeline_mode=pl.Buffered(1))
    cur_b = lambda t: jnp.minimum(t, nt - 1) // ns
    cur_s = lambda t: jnp.minimum(t, nt - 1) % ns
    prev_b = lambda t: jnp.maximum(t - 1, 0) // ns
    prev_s = lambda t: jnp.maximum(t - 1, 0) % ns
    grid_spec = pltpu.PrefetchScalarGridSpec(
        num_scalar_prefetch=1,
        grid=(nt + 1,),
        in_specs=[
            pl.BlockSpec((None, tile, D_MODEL), lambda t, *_: (cur_b(t), cur_s(t), 0)),
            pl.BlockSpec((3, batch, D_MODEL), lambda t, *_: (0, mod_row0 // batch, 0),
                         pipeline_mode=pl.Buffered(1)),
            pl.BlockSpec((5, tile, LANES), lambda t, *_: (0, prev_s(t), 0)),
            pl.BlockSpec((None, tile, D_MODEL), lambda t, *_: (prev_b(t), prev_s(t), 0)),
            const((3 * RET_HEADS, CHUNK, LANES)),
            const((2, CHUNK, 2 * CHUNK)),
            const((D_MODEL, ATT_WIDTH)),
            const((D_MODEL, IN_WIDTH - ATT_WIDTH)),
            const((ATT_WIDTH, D_MODEL)),
            const((RET_WIDTH, D_MODEL)),
            const((1, RET_WIDTH)),
        ],
        out_specs=[
            pl.BlockSpec((None, tile, D_MODEL), lambda t, *_: (prev_b(t), prev_s(t), 0)),
            pl.BlockSpec((None, WINDOW, LANES), lambda t, *_: (prev_b(t), 0, 0)),
            pl.BlockSpec((None, WINDOW, LANES), lambda t, *_: (prev_b(t), 0, 0)),
            pl.BlockSpec((None, RET_HEADS, RET_DK, LANES), lambda t, *_: (prev_b(t), 0, 0, 0)),
        ],
        scratch_shapes=[
            pltpu.VMEM((tile, IN_WIDTH), F32),
            pltpu.VMEM((tile, IN_WIDTH), F32),
            pltpu.VMEM((ATT_KV_HEADS, CHUNK, LANES), BF16),
            pltpu.VMEM((ATT_KV_HEADS, CHUNK, LANES), BF16),
            pltpu.VMEM((tile, D_MODEL), BF16),
        ],
    )
    return pl.pallas_call(
        functools.partial(_mixer_kernel, tile=tile, ns=ns, nt=nt, gl=gl),
        out_shape=[
            jax.ShapeDtypeStruct((batch, seq, D_MODEL), F32),
            jax.ShapeDtypeStruct((batch, WINDOW, LANES), F32),
            jax.ShapeDtypeStruct((batch, WINDOW, LANES), F32),
            jax.ShapeDtypeStruct((batch, RET_HEADS, RET_DK, LANES), F32),
        ],
        grid_spec=grid_spec,
        compiler_params=pltpu.CompilerParams(
            dimension_semantics=("arbitrary",), vmem_limit_bytes=VMEM_LIMIT_BYTES),
        name="mixer_prompt",
    )(sinks, x, mods, tab, x, dec, bias, w_q, w_rest, w_out_att, w_out_ret, gn_w)


def _ffn_rows(r_ref, o_ref, mod, parts, wup_ref, wdown_ref, ln1w_ref, ln1b_ref, lnw_ref, lnb_ref):
    rows = r_ref.shape[0] // parts
    sl = [slice(p * rows, (p + 1) * rows) for p in range(parts)]
    ys = [_layer_norm(r_ref[sl[p], :], ln1w_ref[...], ln1b_ref[...]) for p in range(parts)]
    hs = [(ys[p] * (1.0 + mod(1, sl[p])) + mod(0, sl[p])).astype(BF16) for p in range(parts)]
    def act(h):
        return jnp.concatenate(
            [(_silu(_dot(h, wup_ref[:, j:j + FFN_COLS]))
              * _dot(h, wup_ref[:, D_FF + j:D_FF + j + FFN_COLS])).astype(BF16)
             for j in range(0, D_FF, FFN_COLS)], axis=1)

    acts = [act(h) for h in hs]
    fs = [_dot(a, wdown_ref[...]) for a in acts]
    for p in range(parts):
        r = DEEPNORM_ALPHA * ys[p] + (1.0 + mod(2, sl[p])) * fs[p]
        o_ref[sl[p], :] = _layer_norm(r, lnw_ref[...], lnb_ref[...])


def _ffn_kernel(r_ref, mods_ref, rs_ref, mods_s_ref, wup_ref, wdown_ref, ln1w_ref, ln1b_ref, lnw_ref,
                lnb_ref, o_ref, os_ref, *, parts, tiles_per_mod_row):
    weights = (wup_ref, wdown_ref, ln1w_ref, ln1b_ref, lnw_ref, lnb_ref)
    seq = jax.lax.div(pl.program_id(0), tiles_per_mod_row)
    _ffn_rows(r_ref, o_ref, lambda k, rows: mods_ref[k, pl.ds(seq, 1), :], parts, *weights)

    @pl.when(pl.program_id(0) == pl.num_programs(0) - 1)
    def _():
        _ffn_rows(rs_ref, os_ref, lambda k, rows: mods_s_ref[k, rows, :], 1, *weights)


def _ffn(y, y_s, mods, tiles_per_mod_row, w_up, w_down, ln1_w, ln1_b, ln_w, ln_b, tile, parts):
    rows, _ = y.shape
    n_s = y_s.shape[0]
    batch = mods.shape[1] - n_s
    nt = rows // tile
    assert n_s % batch == 0 and nt == batch * tiles_per_mod_row
    const = lambda shape, idx: pl.BlockSpec(shape, lambda t: idx, pipeline_mode=pl.Buffered(1))
    return pl.pallas_call(
        functools.partial(_ffn_kernel, parts=parts, tiles_per_mod_row=tiles_per_mod_row),
        out_shape=(jax.ShapeDtypeStruct(y.shape, F32), jax.ShapeDtypeStruct(y_s.shape, F32)),
        grid=(nt,),
        in_specs=[
            pl.BlockSpec((tile, D_MODEL), lambda t: (t, 0)),
            const((3, batch, D_MODEL), (1, n_s // batch, 0)),
            const((n_s, D_MODEL), (0, 0)),
            const((3, n_s, D_MODEL), (1, 0, 0)),
            const((D_MODEL, 2 * D_FF), (0, 0)),
            const((D_FF, D_MODEL), (0, 0)),
            const((1, D_MODEL), (0, 0)),
            const((1, D_MODEL), (0, 0)),
            const((1, D_MODEL), (0, 0)),
            const((1, D_MODEL), (0, 0)),
        ],
        out_specs=(pl.BlockSpec((tile, D_MODEL), lambda t: (t, 0)),
                   pl.BlockSpec((n_s, D_MODEL), lambda t: (0, 0))),
        compiler_params=pltpu.CompilerParams(
            dimension_semantics=("arbitrary",), vmem_limit_bytes=VMEM_LIMIT_BYTES),
        name="ffn",
    )(y, mods, y_s, mods, w_up, w_down, ln1_w, ln1_b, ln_w, ln_b)


def _sample_kernel(sinks_ref, x_ref, mod_ref, tab_ref, wq_ref, wrest_ref, wout_att_ref, wout_ret_ref,
                   gnw_ref,
                   ck_hbm, cv_hbm, st_hbm,
                   y_ref, ko_ref, vo_ref, so_ref,
                   z_s, mix_s, ck_buf, cv_buf, st_buf, ring_sem, *, g1, n_blocks):
    i = pl.program_id(0)
    bb = SAMPLE_BLOCK
    streams = ((ck_hbm, ck_buf), (cv_hbm, cv_buf), (st_hbm, st_buf))

    def ring_copies(blk, slot):
        return [pltpu.make_async_copy(hbm.at[pl.ds(blk * bb, bb)], buf.at[slot], ring_sem.at[k, slot])
                for k, (hbm, buf) in enumerate(streams)]

    @pl.when(i == 0)
    def _():
        for s in range(SAMPLE_RING - 1):
            for c in ring_copies(s, s):
                c.start()

    @pl.when(i + (SAMPLE_RING - 1) < n_blocks)
    def _():
        ahead = i + (SAMPLE_RING - 1)
        for c in ring_copies(ahead, jax.lax.rem(ahead, SAMPLE_RING)):
            c.start()

    @pl.when(i == 0)
    def _():
        m = mod_ref[...]
        h = (x_ref[...] * (1.0 + m[1]) + m[0]).astype(BF16)
        z = jnp.concatenate([_dot(h, wq_ref[...]), _dot(h, wrest_ref[...])], axis=1)
        att_c, att_s1, att_s2 = tab_ref[0], tab_ref[1], tab_ref[2]
        ret_c, ret_s = tab_ref[3], tab_ref[4]
        for j in range(4):
            col = OFF_Q + j * LANES
            z_s[:, col:col + LANES] = _rope_att(z[:, col:col + LANES], att_c, att_s1, att_s2) * (HEAD_DIM ** -0.5)
            col = OFF_RQ + j * LANES
            z_s[:, col:col + LANES] = _rope_ret(z[:, col:col + LANES], ret_c, ret_s)
            col = OFF_RK + j * LANES
            z_s[:, col:col + LANES] = _rope_ret(z[:, col:col + LANES], ret_c, ret_s) * (RET_DK ** -0.5)
        z_s[:, OFF_KV:OFF_KV + LANES] = _rope_att(z[:, OFF_KV:OFF_KV + LANES], att_c, att_s1, att_s2)
        z_s[:, OFF_KV + LANES:OFF_RQ] = z[:, OFF_KV + LANES:OFF_RQ]
        z_s[:, OFF_RV:OFF_RG] = z[:, OFF_RV:OFF_RG]
        z_s[:, OFF_RG:] = _silu(z[:, OFF_RG:])

    lane_w = jax.lax.broadcasted_iota(jnp.int32, (LANES, WINDOW), 1)
    last_lane = lane_w == WINDOW - 1
    lo_row = lane_w[0:1, :] < HEAD_DIM
    r4 = jax.lax.broadcasted_iota(jnp.int32, (ATT_GROUP, 1), 0)
    r8 = jax.lax.broadcasted_iota(jnp.int32, (ATT_HEADS, 1), 0)
    sink8 = jnp.zeros((ATT_HEADS, 1), F32)
    for hd in range(ATT_HEADS):
        sink8 = jnp.where(r8 == hd, sinks_ref[hd], sink8)
    head_blk = (jax.lax.broadcasted_iota(jnp.int32, (16, RET_WIDTH), 1) // LANES
                == jax.lax.broadcasted_iota(jnp.int32, (16, RET_WIDTH), 0))
    g1_col = jnp.where(r4 == 0, g1[0], jnp.where(r4 == 1, g1[1], jnp.where(r4 == 2, g1[2], g1[3])))

    def rows16(a):
        return jnp.concatenate([a, jnp.zeros((16 - a.shape[0], a.shape[1]), F32)], axis=0)

    def heads(row, off):
        return jnp.concatenate([row[:, off + j * LANES:off + (j + 1) * LANES] for j in range(4)], axis=0)

    def block_rows(a4):
        wide = jnp.concatenate([rows16(a4)] * RET_HEADS, axis=1)
        return jnp.where(head_blk, wide, jnp.zeros_like(wide))

    slot = jax.lax.rem(i, SAMPLE_RING)
    for c in ring_copies(i, slot):
        c.wait()
    ck_ref, cv_ref, st_ref = ck_buf.at[slot], cv_buf.at[slot], st_buf.at[slot]

    r0 = pl.multiple_of(i * bb, bb)
    zblk = z_s[pl.ds(r0, bb), :]
    rows = [zblk[b:b + 1, :] for b in range(bb)]

    kn = [r[:, OFF_KV:OFF_KV + LANES] for r in rows]
    vn = [r[:, OFF_KV + LANES:OFF_RQ] for r in rows]
    kn_t = zblk[:, OFF_KV:OFF_KV + LANES].T
    vn_t = zblk[:, OFF_KV + LANES:OFF_RQ].T
    q8 = []
    for r in rows:
        q4 = heads(r, OFF_Q)
        zero4 = jnp.zeros_like(q4)
        q8.append(jnp.concatenate([jnp.where(lo_row, q4, zero4), jnp.where(lo_row, zero4, q4)], axis=0))
    sc, oh = [], []
    for b in range(bb):
        sc.append(_dot(rows16(q8[b]).astype(BF16), ck_ref[b].astype(BF16))[0:ATT_HEADS])
    for b in range(bb):
        rq = heads(rows[b], OFF_RQ)
        rk = heads(rows[b], OFF_RK)
        rv = heads(rows[b], OFF_RV)
        state = st_ref[b].reshape(RET_WIDTH, LANES)
        o2 = _dot(block_rows(rq * g1_col).astype(BF16), state.astype(BF16))[0:RET_HEADS]
        outer = _dot_tn(block_rows(rk).astype(BF16), rows16(rv).astype(BF16))
        for hh in range(RET_HEADS):
            so_ref[b, hh] = g1[hh] * st_ref[b, hh] + outer[hh * RET_DK:(hh + 1) * RET_DK, :]
        o = jnp.sum(rq * rk, axis=-1, keepdims=True) * rv + o2
        mu = jnp.mean(o, axis=-1, keepdims=True)
        d = o - mu
        var = jnp.mean(d * d, axis=-1, keepdims=True)
        oh.append(heads(rows[b], OFF_RG) * (d * jax.lax.rsqrt(var + GN_EPS) * gnw_ref[...]))

    ws, wn = [], []
    for b in range(bb):
        sn = jnp.sum(q8[b] * kn[b], axis=-1, keepdims=True)
        mx = jnp.maximum(jnp.maximum(jnp.max(sc[b], axis=-1, keepdims=True), sn), sink8)
        p = jnp.exp(sc[b] - mx)
        pn = jnp.exp(sn - mx)
        inv = 1.0 / (jnp.sum(p, axis=-1, keepdims=True) + pn + jnp.exp(sink8 - mx))
        ws.append(rows16(p * inv).astype(BF16))
        wn.append(pn * inv)

    att = []
    for b in range(bb):
        o8 = _dot_nt(ws[b], cv_ref[b].astype(BF16))[0:ATT_HEADS] + wn[b] * vn[b]
        att.append(jnp.where(lo_row, o8[0:ATT_GROUP], o8[ATT_GROUP:]))

    for b in range(bb):
        ko_ref[b] = jnp.where(last_lane, kn_t[:, b:b + 1], pltpu.roll(ck_ref[b], WINDOW - 1, axis=1))
        vo_ref[b] = jnp.where(last_lane, vn_t[:, b:b + 1], pltpu.roll(cv_ref[b], WINDOW - 1, axis=1))
    blk_rows = pl.ds(r0, bb)
    for j in range(4):
        mix_s[blk_rows, j * LANES:(j + 1) * LANES] = jnp.concatenate(
            [a[j:j + 1, :] for a in att], axis=0)
        mix_s[blk_rows, ATT_WIDTH + j * LANES:ATT_WIDTH + (j + 1) * LANES] = jnp.concatenate(
            [o[j:j + 1, :] for o in oh], axis=0)

    @pl.when(i == n_blocks - 1)
    def _():
        y = (_dot(mix_s[:, :ATT_WIDTH].astype(BF16), wout_att_ref[...])
             + _dot(mix_s[:, ATT_WIDTH:].astype(BF16), wout_ret_ref[...]))
        y_ref[...] = DEEPNORM_ALPHA * x_ref[...] + (1.0 + mod_ref[2]) * y


def _sample_step(sinks, x, mods, tab, w_q, w_rest, w_out_att, w_out_ret, gn_w4,
                 cache_k, cache_v, state):
    n = x.shape[0]
    bb = SAMPLE_BLOCK
    n_blocks = n // bb
    assert n_blocks >= SAMPLE_RING - 1
    g1 = tuple(float(v) for v in np.exp(_log_gamma()).astype(np.float32))
    full = lambda *shape: pl.BlockSpec(shape, lambda i, *_: (0,) * len(shape))
    blk = lambda *shape: pl.BlockSpec((bb,) + shape, lambda i, *_: (i,) + (0,) * len(shape))
    grid_spec = pltpu.PrefetchScalarGridSpec(
        num_scalar_prefetch=1,
        grid=(n_blocks,),
        in_specs=[
            full(n, D_MODEL), full(3, n, D_MODEL), full(5, 1, LANES),
            full(D_MODEL, ATT_WIDTH), full(D_MODEL, IN_WIDTH - ATT_WIDTH),
            full(ATT_WIDTH, D_MODEL), full(RET_WIDTH, D_MODEL),
            full(RET_HEADS, LANES),
            pl.BlockSpec(memory_space=pl.ANY), pl.BlockSpec(memory_space=pl.ANY),
            pl.BlockSpec(memory_space=pl.ANY),
        ],
        out_specs=[full(n, D_MODEL), blk(WINDOW, LANES), blk(WINDOW, LANES),
                   blk(RET_HEADS, RET_DK, LANES)],
        scratch_shapes=[pltpu.VMEM((n, IN_WIDTH), F32), pltpu.VMEM((n, D_MODEL), F32),
                        pltpu.VMEM((SAMPLE_RING, bb, WINDOW, LANES), F32),
                        pltpu.VMEM((SAMPLE_RING, bb, WINDOW, LANES), F32),
                        pltpu.VMEM((SAMPLE_RING, bb, RET_HEADS, RET_DK, LANES), F32),
                        pltpu.SemaphoreType.DMA((3, SAMPLE_RING))],
    )
    return pl.pallas_call(
        functools.partial(_sample_kernel, g1=g1, n_blocks=n_blocks),
        out_shape=[
            jax.ShapeDtypeStruct((n, D_MODEL), F32),
            jax.ShapeDtypeStruct((n, WINDOW, LANES), F32),
            jax.ShapeDtypeStruct((n, WINDOW, LANES), F32),
            jax.ShapeDtypeStruct((n, RET_HEADS, RET_DK, LANES), F32),
        ],
        grid_spec=grid_spec,
        compiler_params=pltpu.CompilerParams(
            dimension_semantics=("arbitrary",), vmem_limit_bytes=VMEM_LIMIT_BYTES),
        name="sample_step",
    )(sinks, x, mods, tab, w_q, w_rest, w_out_att, w_out_ret, gn_w4, cache_k, cache_v, state)


def _slab_order(w, axis):
    shape = w.shape
    pre, post = shape[:axis], shape[axis + 1:]
    w = w.reshape(pre + (ATT_KV_HEADS, ATT_GROUP, HEAD_DIM) + post)
    w = jnp.swapaxes(w, axis, axis + 1)
    return w.reshape(shape)


def kernel(x_prompt, x_sample, c_prompt, c_sample, cache_k, cache_v, state_ret, w_ada_mix, b_ada_mix, w_in, att_sinks, ret_gn_w, w_out, ln1_w, ln1_b, w_ada_ffn, b_ada_ffn, w_up, w_down, ln2_w, ln2_b):
    batch, seq, _ = x_prompt.shape
    n_s = x_sample.shape[0]
    l = 0
    assert w_in.shape[0] == DEPTH == 1

    w_in_f = w_in.reshape(D_MODEL, IN_WIDTH)
    w_out_f = w_out.reshape(D_MODEL, D_MODEL)
    w_q = _slab_order(w_in_f[:, :ATT_WIDTH].astype(BF16), 1)
    w_rest = w_in_f[:, ATT_WIDTH:].astype(BF16)
    w_out_att = _slab_order(w_out_f[:ATT_WIDTH].astype(BF16), 0)
    w_out_ret = w_out_f[ATT_WIDTH:].astype(BF16)
    w_up_b = w_up.reshape(D_MODEL, 2 * D_FF).astype(BF16)
    w_down_b = w_down.reshape(D_FF, D_MODEL).astype(BF16)
    sinks = att_sinks[l]
    gn_w = ret_gn_w[l].reshape(1, RET_WIDTH)
    ln1w, ln1b = ln1_w[l].reshape(1, D_MODEL), ln1_b[l].reshape(1, D_MODEL)
    ln2w, ln2b = ln2_w[l].reshape(1, D_MODEL), ln2_b[l].reshape(1, D_MODEL)

    tab_p = jnp.asarray(_rope_tables(np.arange(seq)))
    tab_s = jnp.asarray(_rope_tables(np.array([PAST_LEN])))
    dec = jnp.asarray(_decay_tables())
    bias = jnp.asarray(_band_bias())

    mods = _ada_mod(c_sample, c_prompt, w_ada_mix.reshape(D_MODEL, 3 * D_MODEL),
                    w_ada_ffn.reshape(D_MODEL, 3 * D_MODEL), b_ada_mix, b_ada_ffn)

    y1p, kp, vp, sp = _mixer_prompt(x_prompt, mods, n_s, sinks, tab_p, dec, bias, w_q, w_rest,
                                    w_out_att, w_out_ret, gn_w)

    xs = x_sample.reshape(n_s, D_MODEL)
    y1s, ks, vs, ss = _sample_step(
        sinks, xs, mods, tab_s, w_q, w_rest, w_out_att, w_out_ret, ret_gn_w.reshape(RET_HEADS, LANES),
        jnp.swapaxes(cache_k.reshape(n_s, WINDOW, LANES), 1, 2),
        jnp.swapaxes(cache_v.reshape(n_s, WINDOW, LANES), 1, 2),
        state_ret.reshape(n_s, RET_HEADS, RET_DK, LANES))
    ks, vs = jnp.swapaxes(ks, 1, 2), jnp.swapaxes(vs, 1, 2)

    yp, ys = _ffn(y1p.reshape(batch * seq, D_MODEL), y1s, mods, seq // FFN_TILE, w_up_b, w_down_b,
                  ln1w, ln1b, ln2w, ln2b, FFN_TILE, FFN_PARTS)
    yp = yp.reshape(batch, seq, D_MODEL)

    kv_shape = (1, batch, WINDOW, ATT_KV_HEADS, HEAD_DIM)
    kvs_shape = (1, n_s, WINDOW, ATT_KV_HEADS, HEAD_DIM)
    return (yp, ys.reshape(n_s, 1, D_MODEL),
            kp.reshape(kv_shape), vp.reshape(kv_shape), sp[None],
            ks.reshape(kvs_shape), vs.reshape(kvs_shape), ss[None])
```

```python
import functools
import math

import numpy as np
import jax
import jax.numpy as jnp
from jax.experimental import pallas as pl
from jax.experimental.pallas import tpu as pltpu

D_MODEL = 1024
WINDOW = 128
CHUNK = 128
ATT_HEADS = 8
ATT_KV_HEADS = 2
ATT_GROUP = ATT_HEADS // ATT_KV_HEADS
HEAD_DIM = 64
ROPE_DIMS = HEAD_DIM // 4
ROPE_THETA = 500000.0
RET_HEADS = 4
RET_DK = 128
RET_THETA = 10000.0
ATT_WIDTH = ATT_HEADS * HEAD_DIM
KV_WIDTH = ATT_KV_HEADS * HEAD_DIM
RET_WIDTH = RET_HEADS * RET_DK
IN_WIDTH = ATT_WIDTH + 2 * KV_WIDTH + 4 * RET_WIDTH
D_FF = 2816
DEPTH = 1
DEEPNORM_ALPHA = (2 * DEPTH) ** 0.25
LN_EPS = 1e-5
GN_EPS = 1e-6
PAST_LEN = 16384

OFF_Q = 0
OFF_KV = ATT_WIDTH
OFF_RQ = OFF_KV + 2 * KV_WIDTH
OFF_RK = OFF_RQ + RET_WIDTH
OFF_RV = OFF_RK + RET_WIDTH
OFF_RG = OFF_RV + RET_WIDTH

LANES = 128
VMEM_LIMIT_BYTES = 56 * 1024 * 1024

PROMPT_TILE = 512
FILL_WIDTH = 256
FILL_PLAN = ((2, 1), (2, 1), (2, 1), (2, 0))
FFN_TILE = 1024
FFN_PARTS = 4
FFN_COLS = 256
SAMPLE_BLOCK = 8
SAMPLE_RING = 3

BF16 = jnp.bfloat16
F32 = jnp.float32


def _dot(a, b):
    return jnp.dot(a, b, preferred_element_type=F32)


def _dot_nt(a, b):
    return jax.lax.dot_general(a, b, (((1,), (1,)), ((), ())), preferred_element_type=F32)


def _dot_tn(a, b):
    return jax.lax.dot_general(a, b, (((0,), (0,)), ((), ())), preferred_element_type=F32)


def _sigmoid(x):
    return 1.0 / (1.0 + jnp.exp(-x))


def _silu(x):
    return x * _sigmoid(x)


def _layer_norm(r, w, b):
    mu = jnp.mean(r, axis=-1, keepdims=True)
    d = r - mu
    var = jnp.mean(d * d, axis=-1, keepdims=True)
    return d * jax.lax.rsqrt(var + LN_EPS) * w + b


def _rope_att(slab, c, s1, s2):
    return slab * c + pltpu.roll(slab, LANES - 8, axis=1) * s1 + pltpu.roll(slab, 8, axis=1) * s2


def _rope_ret(slab, c, s):
    return slab * c + pltpu.roll(slab, LANES // 2, axis=1) * s


def _rope_tables(pos):
    pos = np.asarray(pos, np.float64)[:, None]
    half = ROPE_DIMS // 2
    inv = ROPE_THETA ** (-np.arange(half, dtype=np.float64) / half)
    ang = pos * inv[None, :]
    cos, sin = np.cos(ang), np.sin(ang)
    n = pos.shape[0]
    head_c = np.ones((n, HEAD_DIM)); head_s1 = np.zeros((n, HEAD_DIM)); head_s2 = np.zeros((n, HEAD_DIM))
    head_c[:, :half] = cos; head_c[:, half:2 * half] = cos
    head_s1[:, :half] = -sin
    head_s2[:, half:2 * half] = sin
    att = [np.tile(t, (1, LANES // HEAD_DIM)) for t in (head_c, head_s1, head_s2)]
    rhalf = RET_DK // 2
    rinv = RET_THETA ** (-np.arange(rhalf, dtype=np.float64) / rhalf)
    rang = pos * rinv[None, :]
    rc = np.concatenate([np.cos(rang), np.cos(rang)], axis=1)
    rs = np.concatenate([-np.sin(rang), np.sin(rang)], axis=1)
    return np.stack(att + [rc, rs], axis=0).astype(np.float32)


def _log_gamma():
    lin = np.linspace(math.log(1.0 / 32), math.log(1.0 / 512), RET_HEADS)
    return np.log1p(-np.exp(lin))


def _decay_tables():
    lg = _log_gamma()
    idx = np.arange(CHUNK, dtype=np.float64)
    diff = idx[:, None] - idx[None, :]
    dmat = np.where(diff[None] >= 0, np.exp(lg[:, None, None] * np.maximum(diff, 0.0)[None]), 0.0)
    dq = np.exp(lg[:, None] * (idx[None, :] + 1.0))[:, :, None] * np.ones((1, 1, LANES))
    dk = np.exp(lg[:, None] * (CHUNK - 1.0 - idx[None, :]))[:, :, None] * np.ones((1, 1, LANES))
    return np.concatenate([dmat, dq, dk], axis=0).astype(np.float32)


def _band_bias():
    i = np.arange(CHUNK)[:, None]
    j = np.arange(2 * CHUNK)[None, :]
    valid = (j >= i) & (j <= i + WINDOW)
    neg = np.float32(-1e30)
    b0 = np.where(valid, 0.0, neg)
    b1 = np.where(valid & (j >= CHUNK), 0.0, neg)
    return np.stack([b0, b1], axis=0).astype(np.float32)


def _ada_kernel(cs_ref, cp_ref, w1_ref, w2_ref, b1_ref, b2_ref, o_ref):
    rows = cs_ref.shape[0] + cp_ref.shape[0]
    pad = jnp.zeros((-rows % 16, D_MODEL), F32)
    a = _silu(jnp.concatenate([cs_ref[...], cp_ref[...], pad], axis=0)).astype(BF16)

    def emit(w_ref, b_ref):
        o_ref[...] = _dot(a, w_ref[...].astype(BF16))[0:rows] + b_ref[...]

    @pl.when(pl.program_id(0) < 3)
    def _():
        emit(w1_ref, b1_ref)

    @pl.when(pl.program_id(0) >= 3)
    def _():
        emit(w2_ref, b2_ref)


def _ada_mod(c_sample, c_prompt, w_mix, w_ffn, b_mix, b_ffn):
    rows = c_sample.shape[0] + c_prompt.shape[0]
    d = D_MODEL
    return pl.pallas_call(
        _ada_kernel,
        out_shape=jax.ShapeDtypeStruct((6, rows, d), F32),
        grid=(6,),
        in_specs=[
            pl.BlockSpec(c_sample.shape, lambda j: (0, 0)),
            pl.BlockSpec(c_prompt.shape, lambda j: (0, 0)),
            pl.BlockSpec((d, d), lambda j: (0, jnp.minimum(j, 2))),
            pl.BlockSpec((d, d), lambda j: (0, jnp.maximum(j - 3, 0))),
            pl.BlockSpec((1, d), lambda j: (0, jnp.minimum(j, 2))),
            pl.BlockSpec((1, d), lambda j: (0, jnp.maximum(j - 3, 0))),
        ],
        out_specs=pl.BlockSpec((None, rows, d), lambda j: (j, 0, 0)),
        compiler_params=pltpu.CompilerParams(
            dimension_semantics=("arbitrary",), vmem_limit_bytes=VMEM_LIMIT_BYTES),
        name="ada_mod",
    )(c_sample, c_prompt, w_mix, w_ffn, b_mix, b_ffn)


def _mixer_step(z_prev, z_cur, sinks_ref, x_ref, mods_ref, tab_ref, xp_ref, dec_ref, bias_ref,
                wq_ref, wrest_ref, wout_att_ref, wout_ret_ref, gnw_ref,
                y_ref, kout_ref, vout_ref, s_ref, kcarry, vcarry, mix_s,
                *, cur_b, prev_b, seq_start, tile, gl):
    n_chunks = tile // CHUNK
    x = x_ref[...]
    h = (x * (1.0 + mods_ref[1, pl.ds(cur_b, 1), :]) + mods_ref[0, pl.ds(cur_b, 1), :]).astype(BF16)
    n_fill = IN_WIDTH // FILL_WIDTH
    pending = list(range(n_fill))

    def fill(count):
        for _ in range(count):
            if pending:
                i = pending.pop(0)
                col = i * FILL_WIDTH
                w = (wq_ref[:, col:col + FILL_WIDTH] if col < ATT_WIDTH
                     else wrest_ref[:, col - ATT_WIDTH:col - ATT_WIDTH + FILL_WIDTH])
                z_cur[:, col:col + FILL_WIDTH] = _dot(h, w)

    gate = mods_ref[2, pl.ds(prev_b, 1), :]
    lo = jax.lax.broadcasted_iota(jnp.int32, (CHUNK, LANES), 1) < HEAD_DIM
    zero = jnp.zeros((CHUNK, LANES), F32)
    k_prev = [kcarry[g] for g in range(ATT_KV_HEADS)]
    v_prev = [vcarry[g] for g in range(ATT_KV_HEADS)]

    for c in range(n_chunks):
        rows = slice(c * CHUNK, (c + 1) * CHUNK)
        bias = bias_ref[seq_start] if c == 0 else bias_ref[0]
        att_c, att_s1, att_s2 = tab_ref[0, rows, :], tab_ref[1, rows, :], tab_ref[2, rows, :]
        ret_c, ret_s = tab_ref[3, rows, :], tab_ref[4, rows, :]

        def zs(col):
            return z_prev[rows, col:col + LANES]

        q_stack = jnp.concatenate(
            [(_rope_att(zs(OFF_Q + j * LANES), att_c, att_s1, att_s2) * (HEAD_DIM ** -0.5)).astype(BF16)
             for j in range(4)], axis=0)
        k = _rope_att(zs(OFF_KV), att_c, att_s1, att_s2)
        v = zs(OFF_KV + LANES)
        k_cur = [jnp.where(lo, k, zero).astype(BF16), jnp.where(lo, zero, k).astype(BF16)]
        v_cur = [jnp.where(lo, v, zero).astype(BF16), jnp.where(lo, zero, v).astype(BF16)]
        k_band = [jnp.concatenate([k_prev[g], k_cur[g]], axis=0) for g in range(ATT_KV_HEADS)]
        v_band = [jnp.concatenate([v_prev[g], v_cur[g]], axis=0) for g in range(ATT_KV_HEADS)]
        k_prev, v_prev = k_cur, v_cur
        if c == n_chunks - 1:
            kout_ref[...] = k
            vout_ref[...] = v
            for g in range(ATT_KV_HEADS):
                kcarry[g] = k_cur[g]
                vcarry[g] = v_cur[g]

        ret = []
        for hh in range(RET_HEADS):
            rq = _rope_ret(zs(OFF_RQ + hh * LANES), ret_c, ret_s)
            rk = _rope_ret(zs(OFF_RK + hh * LANES), ret_c, ret_s) * (RET_DK ** -0.5)
            ret.append(dict(
                q=rq.astype(BF16), qd=(rq * dec_ref[RET_HEADS + hh]).astype(BF16),
                k=rk.astype(BF16), kd=(rk * dec_ref[2 * RET_HEADS + hh]).astype(BF16),
                v=zs(OFF_RV + hh * LANES).astype(BF16)))

        sc_att = [_dot_nt(q_stack, k_band[g]) for g in range(ATT_KV_HEADS)]
        sc_ret = [_dot_nt(r["q"], r["k"]) for r in ret]
        fill(FILL_PLAN[c][0])

        att = None
        for g in range(ATT_KV_HEADS):
            ws = []
            for j in range(4):
                sj = sc_att[g][j * CHUNK:(j + 1) * CHUNK, :] + bias
                sink = sinks_ref[g * ATT_GROUP + j]
                mx = jnp.maximum(jnp.max(sj, axis=-1, keepdims=True), sink)
                p = jnp.exp(sj - mx)
                den = jnp.sum(p, axis=-1, keepdims=True) + jnp.exp(sink - mx)
                ws.append((p * (1.0 / den)).astype(BF16))
            o = _dot(jnp.concatenate(ws, axis=0), v_band[g])
            att = o if att is None else att + o
        for j in range(4):
            mix_s[rows, j * LANES:(j + 1) * LANES] = att[j * CHUNK:(j + 1) * CHUNK, :].astype(BF16)

        outs = []
        for hh, r in enumerate(ret):
            state = s_ref[hh]
            sc = sc_ret[hh] * dec_ref[hh]
            outs.append(_dot(sc.astype(BF16), r["v"]) + _dot(r["qd"], state.astype(BF16)))
            s_ref[hh] = gl[hh] * state + _dot_tn(r["kd"], r["v"])
        fill(FILL_PLAN[c][1])
        for hh in range(RET_HEADS):
            cols = slice(hh * LANES, (hh + 1) * LANES)
            o = outs[hh]
            mu = jnp.mean(o, axis=-1, keepdims=True)
            d = o - mu
            var = jnp.mean(d * d, axis=-1, keepdims=True)
            nrm = d * jax.lax.rsqrt(var + GN_EPS) * gnw_ref[:, cols]
            mix_s[rows, ATT_WIDTH + hh * LANES:ATT_WIDTH + (hh + 1) * LANES] = (
                _silu(zs(OFF_RG + hh * LANES)) * nrm).astype(BF16)

    y = _dot(mix_s[...], jnp.concatenate([wout_att_ref[...], wout_ret_ref[...]], axis=0))
    fill(n_fill)
    y_ref[...] = DEEPNORM_ALPHA * xp_ref[...] + (1.0 + gate) * y


def _mixer_kernel(sinks_ref, x_ref, mods_ref, tab_ref, xp_ref, dec_ref, bias_ref,
                  wq_ref, wrest_ref, wout_att_ref, wout_ret_ref, gnw_ref,
                  y_ref, kout_ref, vout_ref, s_ref,
                  z_a, z_b, kcarry, vcarry, mix_s, *, tile, ns, nt, gl):
    t = pl.program_id(0)
    prev_tile = jnp.maximum(t - 1, 0)
    sa = jax.lax.rem(prev_tile, ns)
    seq_start = (sa == 0).astype(jnp.int32)
    cur_b = jax.lax.div(jnp.minimum(t, nt - 1), ns)
    prev_b = jax.lax.div(prev_tile, ns)
    parity = jax.lax.rem(t, 2)

    @pl.when(t == 0)
    def _():
        z_b[...] = jnp.zeros(z_b.shape, z_b.dtype)
        kcarry[...] = jnp.zeros(kcarry.shape, kcarry.dtype)
        vcarry[...] = jnp.zeros(vcarry.shape, vcarry.dtype)

    @pl.when(sa == 0)
    def _():
        s_ref[...] = jnp.zeros_like(s_ref)

    step = functools.partial(
        _mixer_step, sinks_ref=sinks_ref, x_ref=x_ref, mods_ref=mods_ref, tab_ref=tab_ref, xp_ref=xp_ref,
        dec_ref=dec_ref, bias_ref=bias_ref, wq_ref=wq_ref, wrest_ref=wrest_ref,
        wout_att_ref=wout_att_ref, wout_ret_ref=wout_ret_ref,
        gnw_ref=gnw_ref, y_ref=y_ref, kout_ref=kout_ref,
        vout_ref=vout_ref, s_ref=s_ref, kcarry=kcarry, vcarry=vcarry, mix_s=mix_s,
        cur_b=cur_b, prev_b=prev_b, seq_start=seq_start, tile=tile, gl=gl)

    @pl.when(parity == 0)
    def _():
        step(z_b, z_a)

    @pl.when(parity == 1)
    def _():
        step(z_a, z_b)


def _mixer_prompt(x, mods, mod_row0, sinks, tab, dec, bias, w_q, w_rest, w_out_att, w_out_ret, gn_w):
    batch, seq, _ = x.shape
    assert mod_row0 % batch == 0 and batch % 8 == 0
    tile = PROMPT_TILE
    ns = seq // tile
    gl = tuple(float(v) for v in np.exp(_log_gamma() * CHUNK).astype(np.float32))
    nt = batch * ns
    const = lambda shape: pl.BlockSpec(shape, lambda t, *_: (0,) * len(shape),
                                       pipeline_mode=pl.Buffered(1))
    cur_b = lambda t: jnp.minimum(t, nt - 1) // ns
    cur_s = lambda t: jnp.minimum(t, nt - 1) % ns
    prev_b = lambda t: jnp.maximum(t - 1, 0) // ns
    prev_s = lambda t: jnp.maximum(t - 1, 0) % ns
    grid_spec = pltpu.PrefetchScalarGridSpec(
        num_scalar_prefetch=1,
        grid=(nt + 1,),
        in_specs=[
            pl.BlockSpec((None, tile, D_MODEL), lambda t, *_: (cur_b(t), cur_s(t), 0)),
            pl.BlockSpec((3, batch, D_MODEL), lambda t, *_: (0, mod_row0 // batch, 0),
                         pipeline_mode=pl.Buffered(1)),
            pl.BlockSpec((5, tile, LANES), lambda t, *_: (0, prev_s(t), 0)),
            pl.BlockSpec((None, tile, D_MODEL), lambda t, *_: (prev_b(t), prev_s(t), 0)),
            const((3 * RET_HEADS, CHUNK, LANES)),
            const((2, CHUNK, 2 * CHUNK)),
            const((D_MODEL, ATT_WIDTH)),
            const((D_MODEL, IN_WIDTH - ATT_WIDTH)),
            const((ATT_WIDTH, D_MODEL)),
            const((RET_WIDTH, D_MODEL)),
            const((1, RET_WIDTH)),
        ],
        out_specs=[
            pl.BlockSpec((None, tile, D_MODEL), lambda t, *_: (prev_b(t), prev_s(t), 0)),
            pl.BlockSpec((None, WINDOW, LANES), lambda t, *_: (prev_b(t), 0, 0)),
            pl.BlockSpec((None, WINDOW, LANES), lambda t, *_: (prev_b(t), 0, 0)),
            pl.BlockSpec((None, RET_HEADS, RET_DK, LANES), lambda t, *_: (prev_b(t), 0, 0, 0)),
        ],
        scratch_shapes=[
            pltpu.VMEM((tile, IN_WIDTH), F32),
            pltpu.VMEM((tile, IN_WIDTH), F32),
            pltpu.VMEM((ATT_KV_HEADS, CHUNK, LANES), BF16),
            pltpu.VMEM((ATT_KV_HEADS, CHUNK, LANES), BF16),
            pltpu.VMEM((tile, D_MODEL), BF16),
        ],
    )
    return pl.pallas_call(
        functools.partial(_mixer_kernel, tile=tile, ns=ns, nt=nt, gl=gl),
        out_shape=[
            jax.ShapeDtypeStruct((batch, seq, D_MODEL), F32),
            jax.ShapeDtypeStruct((batch, WINDOW, LANES), F32),
            jax.ShapeDtypeStruct((batch, WINDOW, LANES), F32),
            jax.ShapeDtypeStruct((batch, RET_HEADS, RET_DK, LANES), F32),
        ],
        grid_spec=grid_spec,
        compiler_params=pltpu.CompilerParams(
            dimension_semantics=("arbitrary",), vmem_limit_bytes=VMEM_LIMIT_BYTES),
        name="mixer_prompt",
    )(sinks, x, mods, tab, x, dec, bias, w_q, w_rest, w_out_att, w_out_ret, gn_w)


def _ffn_rows(r_ref, o_ref, mod, parts, wup_ref, wdown_ref, ln1w_ref, ln1b_ref, lnw_ref, lnb_ref):
    rows = r_ref.shape[0] // parts
    sl = [slice(p * rows, (p + 1) * rows) for p in range(parts)]
    ys = [_layer_norm(r_ref[sl[p], :], ln1w_ref[...], ln1b_ref[...]) for p in range(parts)]
    hs = [(ys[p] * (1.0 + mod(1, sl[p])) + mod(0, sl[p])).astype(BF16) for p in range(parts)]
    def act(h):
        return jnp.concatenate(
            [(_silu(_dot(h, wup_ref[:, j:j + FFN_COLS]))
              * _dot(h, wup_ref[:, D_FF + j:D_FF + j + FFN_COLS])).astype(BF16)
             for j in range(0, D_FF, FFN_COLS)], axis=1)

    acts = [act(h) for h in hs]
    fs = [_dot(a, wdown_ref[...]) for a in acts]
    for p in range(parts):
        r = DEEPNORM_ALPHA * ys[p] + (1.0 + mod(2, sl[p])) * fs[p]
        o_ref[sl[p], :] = _layer_norm(r, lnw_ref[...], lnb_ref[...])


def _ffn_kernel(r_ref, mods_ref, rs_ref, mods_s_ref, wup_ref, wdown_ref, ln1w_ref, ln1b_ref, lnw_ref,
                lnb_ref, o_ref, os_ref, *, parts, tiles_per_mod_row):
    weights = (wup_ref, wdown_ref, ln1w_ref, ln1b_ref, lnw_ref, lnb_ref)
    seq = jax.lax.div(pl.program_id(0), tiles_per_mod_row)
    _ffn_rows(r_ref, o_ref, lambda k, rows: mods_ref[k, pl.ds(seq, 1), :], parts, *weights)

    @pl.when(pl.program_id(0) == pl.num_programs(0) - 1)
    def _():
        _ffn_rows(rs_ref, os_ref, lambda k, rows: mods_s_ref[k, rows, :], 1, *weights)


def _ffn(y, y_s, mods, tiles_per_mod_row, w_up, w_down, ln1_w, ln1_b, ln_w, ln_b, tile, parts):
    rows, _ = y.shape
    n_s = y_s.shape[0]
    batch = mods.shape[1] - n_s
    nt = rows // tile
    assert n_s % batch == 0 and nt == batch * tiles_per_mod_row
    const = lambda shape, idx: pl.BlockSpec(shape, lambda t: idx, pipeline_mode=pl.Buffered(1))
    return pl.pallas_call(
        functools.partial(_ffn_kernel, parts=parts, tiles_per_mod_row=tiles_per_mod_row),
        out_shape=(jax.ShapeDtypeStruct(y.shape, F32), jax.ShapeDtypeStruct(y_s.shape, F32)),
        grid=(nt,),
        in_specs=[
            pl.BlockSpec((tile, D_MODEL), lambda t: (t, 0)),
            const((3, batch, D_MODEL), (1, n_s // batch, 0)),
            const((n_s, D_MODEL), (0, 0)),
            const((3, n_s, D_MODEL), (1, 0, 0)),
            const((D_MODEL, 2 * D_FF), (0, 0)),
            const((D_FF, D_MODEL), (0, 0)),
            const((1, D_MODEL), (0, 0)),
            const((1, D_MODEL), (0, 0)),
            const((1, D_MODEL), (0, 0)),
            const((1, D_MODEL), (0, 0)),
        ],
        out_specs=(pl.BlockSpec((tile, D_MODEL), lambda t: (t, 0)),
                   pl.BlockSpec((n_s, D_MODEL), lambda t: (0, 0))),
        compiler_params=pltpu.CompilerParams(
            dimension_semantics=("arbitrary",), vmem_limit_bytes=VMEM_LIMIT_BYTES),
        name="ffn",
    )(y, mods, y_s, mods, w_up, w_down, ln1_w, ln1_b, ln_w, ln_b)


def _sample_kernel(sinks_ref, x_ref, mod_ref, tab_ref, wq_ref, wrest_ref, wout_att_hbm, wout_ret_hbm,
                   gnw_ref,
                   ck_hbm, cv_hbm, st_hbm,
                   y_ref, ko_ref, vo_ref, so_ref,
                   z_s, mix_s, ck_buf, cv_buf, st_buf, ring_sem, wout_att_ref, wout_ret_ref, wout_sem,
                   *, g1, n_blocks):
    i = pl.program_id(0)
    bb = SAMPLE_BLOCK
    streams = ((ck_hbm, ck_buf), (cv_hbm, cv_buf), (st_hbm, st_buf))

    def ring_copies(blk, slot):
        return [pltpu.make_async_copy(hbm.at[pl.ds(blk * bb, bb)], buf.at[slot], ring_sem.at[k, slot])
                for k, (hbm, buf) in enumerate(streams)]

    wout_copies = [pltpu.make_async_copy(wout_att_hbm, wout_att_ref, wout_sem.at[0]),
                   pltpu.make_async_copy(wout_ret_hbm, wout_ret_ref, wout_sem.at[1])]

    @pl.when(i == 0)
    def _():
        for s in range(SAMPLE_RING - 1):
            for c in ring_copies(s, s):
                c.start()
        for c in wout_copies:
            c.start()

    @pl.when(i + (SAMPLE_RING - 1) < n_blocks)
    def _():
        ahead = i + (SAMPLE_RING - 1)
        for c in ring_copies(ahead, jax.lax.rem(ahead, SAMPLE_RING)):
            c.start()

    @pl.when(i == 0)
    def _():
        m = mod_ref[...]
        h = (x_ref[...] * (1.0 + m[1]) + m[0]).astype(BF16)
        z = jnp.concatenate([_dot(h, wq_ref[...]), _dot(h, wrest_ref[...])], axis=1)
        att_c, att_s1, att_s2 = tab_ref[0], tab_ref[1], tab_ref[2]
        ret_c, ret_s = tab_ref[3], tab_ref[4]
        for j in range(4):
            col = OFF_Q + j * LANES
            z_s[:, col:col + LANES] = _rope_att(z[:, col:col + LANES], att_c, att_s1, att_s2) * (HEAD_DIM ** -0.5)
            col = OFF_RQ + j * LANES
            z_s[:, col:col + LANES] = _rope_ret(z[:, col:col + LANES], ret_c, ret_s)
            col = OFF_RK + j * LANES
            z_s[:, col:col + LANES] = _rope_ret(z[:, col:col + LANES], ret_c, ret_s) * (RET_DK ** -0.5)
        z_s[:, OFF_KV:OFF_KV + LANES] = _rope_att(z[:, OFF_KV:OFF_KV + LANES], att_c, att_s1, att_s2)
        z_s[:, OFF_KV + LANES:OFF_RQ] = z[:, OFF_KV + LANES:OFF_RQ]
        z_s[:, OFF_RV:OFF_RG] = z[:, OFF_RV:OFF_RG]
        z_s[:, OFF_RG:] = _silu(z[:, OFF_RG:])

    lane_w = jax.lax.broadcasted_iota(jnp.int32, (LANES, WINDOW), 1)
    last_lane = lane_w == WINDOW - 1
    lo_row = lane_w[0:1, :] < HEAD_DIM
    r4 = jax.lax.broadcasted_iota(jnp.int32, (ATT_GROUP, 1), 0)
    r8 = jax.lax.broadcasted_iota(jnp.int32, (ATT_HEADS, 1), 0)
    sink8 = jnp.zeros((ATT_HEADS, 1), F32)
    for hd in range(ATT_HEADS):
        sink8 = jnp.where(r8 == hd, sinks_ref[hd], sink8)
    head_blk = (jax.lax.broadcasted_iota(jnp.int32, (16, RET_WIDTH), 1) // LANES
                == jax.lax.broadcasted_iota(jnp.int32, (16, RET_WIDTH), 0))
    g1_col = jnp.where(r4 == 0, g1[0], jnp.where(r4 == 1, g1[1], jnp.where(r4 == 2, g1[2], g1[3])))

    def rows16(a):
        return jnp.concatenate([a, jnp.zeros((16 - a.shape[0], a.shape[1]), F32)], axis=0)

    def heads(row, off):
        return jnp.concatenate([row[:, off + j * LANES:off + (j + 1) * LANES] for j in range(4)], axis=0)

    def block_rows(a4):
        wide = jnp.concatenate([rows16(a4)] * RET_HEADS, axis=1)
        return jnp.where(head_blk, wide, jnp.zeros_like(wide))

    slot = jax.lax.rem(i, SAMPLE_RING)
    for c in ring_copies(i, slot):
        c.wait()
    ck_ref, cv_ref, st_ref = ck_buf.at[slot], cv_buf.at[slot], st_buf.at[slot]

    r0 = pl.multiple_of(i * bb, bb)
    zblk = z_s[pl.ds(r0, bb), :]
    rows = [zblk[b:b + 1, :] for b in range(bb)]

    kn = [r[:, OFF_KV:OFF_KV + LANES] for r in rows]
    vn = [r[:, OFF_KV + LANES:OFF_RQ] for r in rows]
    kn_t = zblk[:, OFF_KV:OFF_KV + LANES].T
    vn_t = zblk[:, OFF_KV + LANES:OFF_RQ].T
    q8 = []
    for r in rows:
        q4 = heads(r, OFF_Q)
        zero4 = jnp.zeros_like(q4)
        q8.append(jnp.concatenate([jnp.where(lo_row, q4, zero4), jnp.where(lo_row, zero4, q4)], axis=0))
    sc, oh = [], []
    for b in range(bb):
        sc.append(_dot(rows16(q8[b]).astype(BF16), ck_ref[b].astype(BF16))[0:ATT_HEADS])
    for b in range(bb):
        rq = heads(rows[b], OFF_RQ)
        rk = heads(rows[b], OFF_RK)
        rv = heads(rows[b], OFF_RV)
        state = st_ref[b].reshape(RET_WIDTH, LANES)
        o2 = _dot(block_rows(rq * g1_col).astype(BF16), state.astype(BF16))[0:RET_HEADS]
        outer = _dot_tn(block_rows(rk).astype(BF16), rows16(rv).astype(BF16))
        for hh in range(RET_HEADS):
            so_ref[b, hh] = g1[hh] * st_ref[b, hh] + outer[hh * RET_DK:(hh + 1) * RET_DK, :]
        o = jnp.sum(rq * rk, axis=-1, keepdims=True) * rv + o2
        mu = jnp.mean(o, axis=-1, keepdims=True)
        d = o - mu
        var = jnp.mean(d * d, axis=-1, keepdims=True)
        oh.append(heads(rows[b], OFF_RG) * (d * jax.lax.rsqrt(var + GN_EPS) * gnw_ref[...]))

    ws, wn = [], []
    for b in range(bb):
        sn = jnp.sum(q8[b] * kn[b], axis=-1, keepdims=True)
        mx = jnp.maximum(jnp.maximum(jnp.max(sc[b], axis=-1, keepdims=True), sn), sink8)
        p = jnp.exp(sc[b] - mx)
        pn = jnp.exp(sn - mx)
        inv = 1.0 / (jnp.sum(p, axis=-1, keepdims=True) + pn + jnp.exp(sink8 - mx))
        ws.append(rows16(p * inv).astype(BF16))
        wn.append(pn * inv)

    att = []
    for b in range(bb):
        o8 = _dot_nt(ws[b], cv_ref[b].astype(BF16))[0:ATT_HEADS] + wn[b] * vn[b]
        att.append(jnp.where(lo_row, o8[0:ATT_GROUP], o8[ATT_GROUP:]))

    for b in range(bb):
        ko_ref[b] = jnp.where(last_lane, kn_t[:, b:b + 1], pltpu.roll(ck_ref[b], WINDOW - 1, axis=1))
        vo_ref[b] = jnp.where(last_lane, vn_t[:, b:b + 1], pltpu.roll(cv_ref[b], WINDOW - 1, axis=1))
    blk_rows = pl.ds(r0, bb)
    for j in range(4):
        mix_s[blk_rows, j * LANES:(j + 1) * LANES] = jnp.concatenate(
            [a[j:j + 1, :] for a in att], axis=0)
        mix_s[blk_rows, ATT_WIDTH + j * LANES:ATT_WIDTH + (j + 1) * LANES] = jnp.concatenate(
            [o[j:j + 1, :] for o in oh], axis=0)

    @pl.when(i == n_blocks - 1)
    def _():
        for c in wout_copies:
            c.wait()
        y = (_dot(mix_s[:, :ATT_WIDTH].astype(BF16), wout_att_ref[...])
             + _dot(mix_s[:, ATT_WIDTH:].astype(BF16), wout_ret_ref[...]))
        y_ref[...] = DEEPNORM_ALPHA * x_ref[...] + (1.0 + mod_ref[2]) * y


def _sample_step(sinks, x, mods, tab, w_q, w_rest, w_out_att, w_out_ret, gn_w4,
                 cache_k, cache_v, state):
    n = x.shape[0]
    bb = SAMPLE_BLOCK
    n_blocks = n // bb
    assert n_blocks >= SAMPLE_RING - 1
    g1 = tuple(float(v) for v in np.exp(_log_gamma()).astype(np.float32))
    full = lambda *shape: pl.BlockSpec(shape, lambda i, *_: (0,) * len(shape))
    blk = lambda *shape: pl.BlockSpec((bb,) + shape, lambda i, *_: (i,) + (0,) * len(shape))
    grid_spec = pltpu.PrefetchScalarGridSpec(
        num_scalar_prefetch=1,
        grid=(n_blocks,),
        in_specs=[
            full(n, D_MODEL), full(3, n, D_MODEL), full(5, 1, LANES),
            full(D_MODEL, ATT_WIDTH), full(D_MODEL, IN_WIDTH - ATT_WIDTH),
            pl.BlockSpec(memory_space=pl.ANY), pl.BlockSpec(memory_space=pl.ANY),
            full(RET_HEADS, LANES),
            pl.BlockSpec(memory_space=pl.ANY), pl.BlockSpec(memory_space=pl.ANY),
            pl.BlockSpec(memory_space=pl.ANY),
        ],
        out_specs=[full(n, D_MODEL), blk(WINDOW, LANES), blk(WINDOW, LANES),
                   blk(RET_HEADS, RET_DK, LANES)],
        scratch_shapes=[pltpu.VMEM((n, IN_WIDTH), F32), pltpu.VMEM((n, D_MODEL), F32),
                        pltpu.VMEM((SAMPLE_RING, bb, WINDOW, LANES), F32),
                        pltpu.VMEM((SAMPLE_RING, bb, WINDOW, LANES), F32),
                        pltpu.VMEM((SAMPLE_RING, bb, RET_HEADS, RET_DK, LANES), F32),
                        pltpu.SemaphoreType.DMA((3, SAMPLE_RING)),
                        pltpu.VMEM((ATT_WIDTH, D_MODEL), BF16), pltpu.VMEM((RET_WIDTH, D_MODEL), BF16),
                        pltpu.SemaphoreType.DMA((2,))],
    )
    return pl.pallas_call(
        functools.partial(_sample_kernel, g1=g1, n_blocks=n_blocks),
        out_shape=[
            jax.ShapeDtypeStruct((n, D_MODEL), F32),
            jax.ShapeDtypeStruct((n, WINDOW, LANES), F32),
            jax.ShapeDtypeStruct((n, WINDOW, LANES), F32),
            jax.ShapeDtypeStruct((n, RET_HEADS, RET_DK, LANES), F32),
        ],
        grid_spec=grid_spec,
        compiler_params=pltpu.CompilerParams(
            dimension_semantics=("arbitrary",), vmem_limit_bytes=VMEM_LIMIT_BYTES),
        name="sample_step",
    )(sinks, x, mods, tab, w_q, w_rest, w_out_att, w_out_ret, gn_w4, cache_k, cache_v, state)


def _slab_order(w, axis):
    shape = w.shape
    pre, post = shape[:axis], shape[axis + 1:]
    w = w.reshape(pre + (ATT_KV_HEADS, ATT_GROUP, HEAD_DIM) + post)
    w = jnp.swapaxes(w, axis, axis + 1)
    return w.reshape(shape)


def kernel(x_prompt, x_sample, c_prompt, c_sample, cache_k, cache_v, state_ret, w_ada_mix, b_ada_mix, w_in, att_sinks, ret_gn_w, w_out, ln1_w, ln1_b, w_ada_ffn, b_ada_ffn, w_up, w_down, ln2_w, ln2_b):
    batch, seq, _ = x_prompt.shape
    n_s = x_sample.shape[0]
    l = 0
    assert w_in.shape[0] == DEPTH == 1

    w_in_f = w_in.reshape(D_MODEL, IN_WIDTH)
    w_out_f = w_out.reshape(D_MODEL, D_MODEL)
    w_q = _slab_order(w_in_f[:, :ATT_WIDTH].astype(BF16), 1)
    w_rest = w_in_f[:, ATT_WIDTH:].astype(BF16)
    w_out_att = _slab_order(w_out_f[:ATT_WIDTH].astype(BF16), 0)
    w_out_ret = w_out_f[ATT_WIDTH:].astype(BF16)
    w_up_b = w_up.reshape(D_MODEL, 2 * D_FF).astype(BF16)
    w_down_b = w_down.reshape(D_FF, D_MODEL).astype(BF16)
    sinks = att_sinks[l]
    gn_w = ret_gn_w[l].reshape(1, RET_WIDTH)
    ln1w, ln1b = ln1_w[l].reshape(1, D_MODEL), ln1_b[l].reshape(1, D_MODEL)
    ln2w, ln2b = ln2_w[l].reshape(1, D_MODEL), ln2_b[l].reshape(1, D_MODEL)

    tab_p = jnp.asarray(_rope_tables(np.arange(seq)))
    tab_s = jnp.asarray(_rope_tables(np.array([PAST_LEN])))
    dec = jnp.asarray(_decay_tables())
    bias = jnp.asarray(_band_bias())

    mods = _ada_mod(c_sample, c_prompt, w_ada_mix.reshape(D_MODEL, 3 * D_MODEL),
                    w_ada_ffn.reshape(D_MODEL, 3 * D_MODEL), b_ada_mix, b_ada_ffn)

    y1p, kp, vp, sp = _mixer_prompt(x_prompt, mods, n_s, sinks, tab_p, dec, bias, w_q, w_rest,
                                    w_out_att, w_out_ret, gn_w)

    xs = x_sample.reshape(n_s, D_MODEL)
    y1s, ks, vs, ss = _sample_step(
        sinks, xs, mods, tab_s, w_q, w_rest, w_out_att, w_out_ret, ret_gn_w.reshape(RET_HEADS, LANES),
        jnp.swapaxes(cache_k.reshape(n_s, WINDOW, LANES), 1, 2),
        jnp.swapaxes(cache_v.reshape(n_s, WINDOW, LANES), 1, 2),
        state_ret.reshape(n_s, RET_HEADS, RET_DK, LANES))
    ks, vs = jnp.swapaxes(ks, 1, 2), jnp.swapaxes(vs, 1, 2)

    yp, ys = _ffn(y1p.reshape(batch * seq, D_MODEL), y1s, mods, seq // FFN_TILE, w_up_b, w_down_b,
                  ln1w, ln1b, ln2w, ln2b, FFN_TILE, FFN_PARTS)
    yp = yp.reshape(batch, seq, D_MODEL)

    kv_shape = (1, batch, WINDOW, ATT_KV_HEADS, HEAD_DIM)
    kvs_shape = (1, n_s, WINDOW, ATT_KV_HEADS, HEAD_DIM)
    return (yp, ys.reshape(n_s, 1, D_MODEL),
            kp.reshape(kv_shape), vp.reshape(kv_shape), sp[None],
            ks.reshape(kvs_shape), vs.reshape(kvs_shape), ss[None])
```
